```python
import jax
import jax.numpy as jnp
from jax import lax
import numpy as np

D_MODEL = 2048
BATCH = 4
SEQ = 4096
DEPTH = 2

GRID_W = 64
CTX_LEN = 256
N_BRANCH = 4
BRANCH_W = D_MODEL // N_BRANCH
RW_HEAD = 64
RW_HEADS = BRANCH_W // RW_HEAD
RW_DECAY_LORA = 64
RW_A_LORA = 64
RW_GATE_LORA = 128
RW_LN_EPS = 64e-5
HG_DK = 128
HG_HEADS = BRANCH_W // HG_DK
GLA_HEADS = 4
GLA_DK = BRANCH_W // (2 * GLA_HEADS)
GLA_GATE_LORA = 16
GLA_GATE_NORM = 16.0
LA_CHUNK = 16
SGU_CHUNK = 2 * GRID_W
SGU_GROUPS = 4
SGU_GW = BRANCH_W // SGU_GROUPS
MLP_HIDDEN = 4 * D_MODEL
EPS = 1e-6

IN_WIDTHS = (
    3 * BRANCH_W,
    2 * RW_DECAY_LORA,
    2 * RW_A_LORA,
    RW_GATE_LORA,
    HG_HEADS * HG_DK,
    2 * HG_HEADS * HG_DK,
    BRANCH_W,
    BRANCH_W,
    2 * GLA_HEADS * GLA_DK,
    BRANCH_W,
    2 * GLA_GATE_LORA,
    BRANCH_W,
    2 * BRANCH_W,
    N_BRANCH * D_MODEL,
)
N_IN = sum(IN_WIDTHS)

kernel_name = 'hybrid_rwkv7_hgrn2_gla_sgu_dit_block'


def _rmsnorm(x, g):
    xf = x.astype(jnp.float32)
    y = xf * lax.rsqrt(jnp.mean(jnp.square(xf), axis=-1, keepdims=True) + EPS)
    return (y * g.astype(jnp.float32)).astype(x.dtype)


def _layernorm(x, w, b):
    xf = x.astype(jnp.float32)
    xc = xf - jnp.mean(xf, axis=-1, keepdims=True)
    y = xc * lax.rsqrt(jnp.mean(jnp.square(xc), axis=-1, keepdims=True) + EPS)
    return (y * w.astype(jnp.float32) + b.astype(jnp.float32)).astype(x.dtype)


def _modulate(h, shift, scale):
    return h * (1.0 + scale) + shift


def _heads(t, n_heads):
    return t.reshape(t.shape[:-1] + (n_heads, t.shape[-1] // n_heads))


def _head_rmsnorm(o, g):
    o = o * lax.rsqrt(jnp.mean(jnp.square(o), axis=-1, keepdims=True) + EPS)
    return o.reshape(o.shape[:-2] + (-1,)) * g.astype(jnp.float32)


def _split_cols(p):
    cuts, acc = [], 0
    for w in IN_WIDTHS[:-1]:
        acc += w
        cuts.append(acc)
    return jnp.split(p, cuts, axis=-1)


def _conv3(x, w):
    xp = jnp.pad(x, ((0, 0), (1, 1), (0, 0)))
    return w[0] * xp[:, :-2] + w[1] * xp[:, 1:-1] + w[2] * xp[:, 2:]


def _rwkv7_scan(r, w, a, b, k, v, s0, reverse):
    def step(s, inp):
        r_t, w_t, a_t, b_t, k_t, v_t = inp
        sa = jnp.einsum('bhvk,bhk->bhv', s, a_t)
        s = s * w_t[:, :, None, :] + sa[..., None] * b_t[:, :, None, :] + v_t[..., None] * k_t[:, :, None, :]
        return s, jnp.einsum('bhvk,bhk->bhv', s, r_t)
    xs = tuple(jnp.swapaxes(t, 0, 1) for t in (r, w, a, b, k, v))
    s_fin, y = lax.scan(step, s0, xs, reverse=reverse)
    return jnp.swapaxes(y, 0, 1), s_fin


def _rwkv7_stream(rkv, wl, al, gl, conv_w, w0, w2, a0, a2, g2, k_k, k_a):
    f32 = jnp.float32
    bsz, t_len, _ = rkv.shape
    r, k, v = jnp.split(_conv3(rkv, conv_w).astype(f32), 3, axis=-1)
    wl = wl.astype(f32).reshape(bsz, t_len, 2, RW_DECAY_LORA)
    al = al.astype(f32).reshape(bsz, t_len, 2, RW_A_LORA)
    w_pre = w0.astype(f32) + jnp.einsum('btdr,drc->btdc', jnp.tanh(wl), w2.astype(f32))
    decay = jnp.exp(-jnp.exp(-jax.nn.softplus(-w_pre) - 0.5))
    a = jax.nn.sigmoid(a0.astype(f32) + jnp.einsum('btdr,drc->btdc', al, a2.astype(f32)))
    g = jax.nn.sigmoid(gl.astype(f32)) @ g2.astype(f32)
    kk = _heads(k * k_k.astype(f32), RW_HEADS)
    kk = kk * lax.rsqrt(jnp.sum(jnp.square(kk), axis=-1, keepdims=True) + 1e-12)
    k_mod = _heads(k[:, :, None, :] * (1.0 + (a - 1.0) * k_a.astype(f32)), RW_HEADS)
    a = _heads(a, RW_HEADS)
    return dict(r=_heads(r, RW_HEADS), k=_heads(k, RW_HEADS), v=_heads(v, RW_HEADS), kk=kk,
                decay=_heads(decay, RW_HEADS), kka=kk[:, :, None] * a, k_mod=k_mod, g=g)


def _rwkv7_out(s, y, r_k, ln_w, ln_b):
    f32 = jnp.float32
    yc = y - jnp.mean(y, axis=-1, keepdims=True)
    y = yc * lax.rsqrt(jnp.mean(jnp.square(yc), axis=-1, keepdims=True) + RW_LN_EPS)
    bonus = jnp.sum(s['r'] * s['k'] * r_k.astype(f32), axis=-1, keepdims=True) * s['v']
    shape = y.shape[:-2] + (-1,)
    y = y.reshape(shape) * ln_w.astype(f32) + ln_b.astype(f32) + bonus.reshape(shape)
    return y * s['g']


def _rwkv7_branch(pc, pl, prm, r_k, ln_w, ln_b):
    sc = _rwkv7_stream(*pc, *prm)
    sl = _rwkv7_stream(*pl, *prm)
    yc, yl = 0.0, 0.0
    for d in range(2):
        s0 = jnp.zeros(sc['r'].shape[:1] + (RW_HEADS, RW_HEAD, RW_HEAD), jnp.float32)
        y_c, s_ctx = _rwkv7_scan(sc['r'], sc['decay'][:, :, d], -sc['kk'], sc['kka'][:, :, d],
                                 sc['k_mod'][:, :, d], sc['v'], s0, d == 1)
        y_l, _ = _rwkv7_scan(sl['r'], sl['decay'][:, :, d], -sl['kk'], sl['kka'][:, :, d],
                             sl['k_mod'][:, :, d], sl['v'], s_ctx, d == 1)
        yc, yl = yc + y_c, yl + y_l
    return (_rwkv7_out(sc, yc, r_k, ln_w, ln_b), _rwkv7_out(sl, yl, r_k, ln_w, ln_b))


def _chunk_la(q, k, v, log_g, s0):
    bsz, t_len, n_h, _ = q.shape
    n_c = t_len // LA_CHUNK

    def chunks(t):
        return t.reshape(bsz, n_c, LA_CHUNK, n_h, t.shape[-1]).transpose(1, 0, 3, 2, 4)

    qc, kc, vc = chunks(q), chunks(k), chunks(v)
    bc = jnp.cumsum(chunks(log_g), axis=3)
    tri = jnp.tril(jnp.ones((LA_CHUNK, LA_CHUNK), bool))[:, :, None]

    def step(s, inp):
        q_n, k_n, v_n, b_n = inp
        diff = b_n[:, :, :, None, :] - b_n[:, :, None, :, :]
        dec = jnp.where(tri, jnp.exp(jnp.where(tri, diff, 0.0)), 0.0)
        att = jnp.einsum('bhtk,bhsk,bhtsk->bhts', q_n, k_n, dec)
        b_last = b_n[:, :, -1]
        o = jnp.einsum('bhts,bhsv->bhtv', att, v_n) + jnp.einsum('bhtk,bhkv->bhtv', q_n * jnp.exp(b_n), s)
        s = jnp.exp(b_last)[..., None] * s + jnp.einsum(
            'bhsk,bhsv->bhkv', k_n * jnp.exp(b_last[:, :, None, :] - b_n), v_n)
        return s, o

    s_fin, o = lax.scan(step, s0, (qc, kc, vc, bc))
    o = o.transpose(1, 0, 3, 2, 4).reshape(bsz, t_len, n_h, v.shape[-1])
    return o, s_fin


def _bidir_chunk_la(ctx_in, lat_in):
    qc, kc, vc, gc = ctx_in
    ql, kl, vl, gl = lat_in
    bsz, _, n_h, dk = qc.shape
    oc, ol = 0.0, 0.0
    for d in range(2):
        fl = (lambda t: jnp.flip(t, axis=1)) if d == 1 else (lambda t: t)
        s0 = jnp.zeros((bsz, n_h, dk, vc.shape[-1]), jnp.float32)
        o_c, s_ctx = _chunk_la(fl(qc), fl(kc[d]), fl(vc), fl(gc[d]), s0)
        o_l, _ = _chunk_la(fl(ql), fl(kl[d]), fl(vl), fl(gl[d]), s_ctx)
        oc, ol = oc + fl(o_c), ol + fl(o_l)
    return oc, ol


def _hgrn2_stream(q, f, i, lb):
    f32 = jnp.float32
    bsz, t_len, _ = q.shape
    f = f.astype(f32).reshape(bsz, t_len, 2, HG_HEADS * HG_DK)
    log_f = jnp.logaddexp(jnp.log(lb), jnp.log1p(-lb) + jax.nn.log_sigmoid(f))
    k = _heads(-jnp.expm1(log_f), HG_HEADS)
    log_f = _heads(log_f, HG_HEADS)
    return (_heads(q.astype(f32), HG_HEADS), (k[:, :, 0], k[:, :, 1]),
            _heads(i.astype(f32), HG_HEADS), (log_f[:, :, 0], log_f[:, :, 1]))


def _hgrn2_branch(pc, pl, lb, norm_g):
    oc, ol = _bidir_chunk_la(_hgrn2_stream(pc[0], pc[1], pc[2], lb),
                             _hgrn2_stream(pl[0], pl[1], pl[2], lb))
    return (_head_rmsnorm(oc, norm_g) * jax.nn.silu(pc[3].astype(jnp.float32)),
            _head_rmsnorm(ol, norm_g) * jax.nn.silu(pl[3].astype(jnp.float32)))


def _gla_stream(qk, v, gcode, gw, gb):
    f32 = jnp.float32
    bsz, t_len, _ = qk.shape
    q, k = jnp.split(qk.astype(f32), 2, axis=-1)
    q = _heads(q, GLA_HEADS) * GLA_DK ** -0.5
    k = _heads(k, GLA_HEADS)
    code = gcode.astype(f32).reshape(bsz, t_len, 2, GLA_GATE_LORA)
    log_g = jax.nn.log_sigmoid(jnp.einsum('btdr,drc->btdc', code, gw.astype(f32)) + gb.astype(f32)) / GLA_GATE_NORM
    log_g = _heads(log_g, GLA_HEADS)
    return (q, (k, k), _heads(v.astype(f32), GLA_HEADS), (log_g[:, :, 0], log_g[:, :, 1]))


def _gla_branch(pc, pl, gw, gb, norm_g):
    oc, ol = _bidir_chunk_la(_gla_stream(pc[0], pc[1], pc[2], gw, gb),
                             _gla_stream(pl[0], pl[1], pl[2], gw, gb))
    return (_head_rmsnorm(oc, norm_g) * jax.nn.silu(pc[3].astype(jnp.float32)),
            _head_rmsnorm(ol, norm_g) * jax.nn.silu(pl[3].astype(jnp.float32)))


def _sgu(uv, ln_w, ln_b, w_s, b_s):
    bsz, t_len, _ = uv.shape
    u, v = jnp.split(jax.nn.gelu(uv), 2, axis=-1)
    v = _layernorm(v, ln_w, ln_b).reshape(bsz, t_len // SGU_CHUNK, SGU_CHUNK, SGU_GROUPS, SGU_GW)
    s = jnp.einsum('gts,bnsgc->bntgc', w_s, v) + b_s.T[None, None, :, :, None]
    return u * s.reshape(bsz, t_len, BRANCH_W)


def _merge(ys, gate_pre, w_branch_l, w_out_l):
    dt = gate_pre.dtype
    gates = jnp.split(gate_pre, N_BRANCH, axis=-1)
    m = jax.nn.sigmoid(gates[0]) * (ys[0].astype(dt) @ w_branch_l[0])
    for j in range(1, N_BRANCH):
        m = m + jax.nn.sigmoid(gates[j]) * (ys[j].astype(dt) @ w_branch_l[j])
    return m @ w_out_l


def _sqrelu_mlp(h, w1, w2):
    return jnp.square(jax.nn.relu(h @ w1)) @ w2


def setup_inputs(seed: int = 0) -> dict:
    key = jax.random.key(seed)
    ks = iter(jax.random.split(key, 40))

    def nrm(shape, scale):
        return scale * jax.random.normal(next(ks), shape, jnp.float32)

    D, C, L = D_MODEL, BRANCH_W, DEPTH
    return {
        'x': nrm((BATCH, SEQ, D), 1.0),
        'c': nrm((BATCH, D), 1.0),
        'ctx': nrm((BATCH, CTX_LEN, D), 1.0),
        'c_ctx': nrm((D,), 1.0),
        'w_ada': nrm((L, D, 6 * D), 0.5 * D ** -0.5),
        'b_ada': nrm((L, 6 * D), 0.01),
        'g_norm1': 1.0 + nrm((L, D), 0.02),
        'g_norm2': 1.0 + nrm((L, D), 0.02),
        'g_final': 1.0 + nrm((D,), 0.02),
        'w_in': nrm((L, D, N_IN), D ** -0.5),
        'rw_conv': jnp.array([0.25, 0.5, 0.25], jnp.float32)[None, :, None] + nrm((L, 3, 3 * C), 0.05),
        'rw_w0': nrm((L, 2, C), 0.5),
        'rw_w2': nrm((L, 2, RW_DECAY_LORA, C), 0.5 * RW_DECAY_LORA ** -0.5),
        'rw_a0': nrm((L, 2, C), 0.1),
        'rw_a2': nrm((L, 2, RW_A_LORA, C), 0.5 * RW_A_LORA ** -0.5),
        'rw_g2': nrm((L, RW_GATE_LORA, C), RW_GATE_LORA ** -0.5),
        'rw_kk': 0.85 + nrm((L, C), 0.05),
        'rw_ka': 1.0 + nrm((L, C), 0.05),
        'rw_rk': nrm((L, RW_HEADS, RW_HEAD), 0.1),
        'rw_ln_w': 1.0 + nrm((L, C), 0.02),
        'rw_ln_b': nrm((L, C), 0.01),
        'hg_gamma': nrm((L, HG_HEADS * HG_DK), 0.1),
        'hg_norm': 1.0 + nrm((L, C), 0.02),
        'gla_gw': nrm((L, 2, GLA_GATE_LORA, GLA_HEADS * GLA_DK), GLA_GATE_LORA ** -0.5),
        'gla_gb': nrm((L, 2, GLA_HEADS * GLA_DK), 0.1),
        'gla_norm': 1.0 + nrm((L, C), 0.02),
        'sgu_ln_w': 1.0 + nrm((L, C), 0.02),
        'sgu_ln_b': nrm((L, C), 0.01),
        'sgu_w': nrm((L, SGU_GROUPS, SGU_CHUNK, SGU_CHUNK), SGU_CHUNK ** -0.5),
        'sgu_b': 1.0 + nrm((L, SGU_GROUPS, SGU_CHUNK), 0.02),
        'w_branch': nrm((L, N_BRANCH, C, D), C ** -0.5),
        'w_out': nrm((L, D, D), D ** -0.5),
        'w_mlp1': nrm((L, D, MLP_HIDDEN), D ** -0.5),
        'w_mlp2': nrm((L, MLP_HIDDEN, D), MLP_HIDDEN ** -0.5),
    }


def reference(x, c, ctx, c_ctx, w_ada, b_ada, g_norm1, g_norm2, g_final, w_in,
              rw_conv, rw_w0, rw_w2, rw_a0, rw_a2, rw_g2, rw_kk, rw_ka, rw_rk, rw_ln_w, rw_ln_b,
              hg_gamma, hg_norm, gla_gw, gla_gb, gla_norm,
              sgu_ln_w, sgu_ln_b, sgu_w, sgu_b,
              w_branch, w_out, w_mlp1, w_mlp2):
    lbs = jnp.cumsum(jax.nn.softmax(hg_gamma.astype(jnp.float32), axis=0), axis=0)
    lbs = lbs - lbs[0]
    c_act = jax.nn.silu(c)
    cc_act = jax.nn.silu(c_ctx)
    xl, xc = x, ctx
    for l in range(DEPTH):
        need_ctx = l < DEPTH - 1
        mod_l = jnp.split((c_act @ w_ada[l] + b_ada[l])[:, None, :], 6, axis=-1)
        mod_c = jnp.split((cc_act @ w_ada[l] + b_ada[l])[None, None, :], 6, axis=-1)
        hl = _modulate(_rmsnorm(xl, g_norm1[l]), mod_l[0], mod_l[1])
        hc = _modulate(_rmsnorm(xc, g_norm1[l]), mod_c[0], mod_c[1])
        pl = _split_cols(hl @ w_in[l])
        pc = _split_cols(hc @ w_in[l])
        rw_prm = (rw_conv[l], rw_w0[l], rw_w2[l], rw_a0[l], rw_a2[l], rw_g2[l], rw_kk[l], rw_ka[l])
        ya_c, ya_l = _rwkv7_branch(pc[0:4], pl[0:4], rw_prm, rw_rk[l], rw_ln_w[l], rw_ln_b[l])
        yb_c, yb_l = _hgrn2_branch(pc[4:8], pl[4:8], lbs[l], hg_norm[l])
        yc_c, yc_l = _gla_branch(pc[8:12], pl[8:12], gla_gw[l], gla_gb[l], gla_norm[l])
        yd_l = _sgu(pl[12], sgu_ln_w[l], sgu_ln_b[l], sgu_w[l], sgu_b[l])
        xl = xl + mod_l[2] * _merge((ya_l, yb_l, yc_l, yd_l), pl[13], w_branch[l], w_out[l])
        hl2 = _modulate(_rmsnorm(xl, g_norm2[l]), mod_l[3], mod_l[4])
        xl = xl + mod_l[5] * _sqrelu_mlp(hl2, w_mlp1[l], w_mlp2[l])
        if need_ctx:
            yd_c = _sgu(pc[12], sgu_ln_w[l], sgu_ln_b[l], sgu_w[l], sgu_b[l])
            xc = xc + mod_c[2] * _merge((ya_c, yb_c, yc_c, yd_c), pc[13], w_branch[l], w_out[l])
            hc2 = _modulate(_rmsnorm(xc, g_norm2[l]), mod_c[3], mod_c[4])
            xc = xc + mod_c[5] * _sqrelu_mlp(hc2, w_mlp1[l], w_mlp2[l])
    return _rmsnorm(xl, g_final)
```

```python
import functools

import jax
import jax.numpy as jnp
from jax import lax
from jax.experimental import pallas as pl
from jax.experimental.pallas import tpu as pltpu

F32 = jnp.float32
BF16 = jnp.bfloat16

N_BRANCH = 4
RW_HEAD = 64
RW_LN_EPS = 64e-5
HG_DK = 128
GLA_HEADS = 4
GLA_GATE_NORM = 16.0
LA_CHUNK = 16
SGU_CHUNK = 128
SGU_GROUPS = 4
EPS = 1e-6

C_RKV = 0
C_HGF = 1536
C_SGU = 2560
C_HGQ = 3584
C_HGI = 4096
C_HGG = 4608
C_GLQK = 5120
C_GLV = 5632
C_GLG = 6144
C_GATE = 6656
C_SMALL = 14848
N_COLS = 15360

VMEM_LIMIT = 56 * 1024 * 1024


def _cparams(n_axes):
    return pltpu.CompilerParams(dimension_semantics=("arbitrary",) * n_axes,
                                vmem_limit_bytes=VMEM_LIMIT)


def _pick(n_list, *dims):
    for n in n_list:
        if all(d % n == 0 for d in dims):
            return n
    raise ValueError(f"no block size in {n_list} divides {dims}")


def _row_group(i, nctx, bpb):
    return jnp.where(i < nctx, 0, 1 + (i - nctx) // bpb)


def _dot(a, b):
    return jnp.dot(a, b, preferred_element_type=F32)


def _dot_nt(a, b):
    return lax.dot_general(a, b, (((1,), (1,)), ((), ())), preferred_element_type=F32)


def _dot_tn(a, b):
    return lax.dot_general(a, b, (((0,), (0,)), ((), ())), preferred_element_type=F32)


def _split_dot(x, e):
    hi = x.astype(BF16)
    lo = (x - hi.astype(F32)).astype(BF16)
    return _dot(hi, e) + _dot(lo, e)


def _split3_dot(e, x):
    p1 = x.astype(BF16)
    r1 = x - p1.astype(F32)
    p2 = r1.astype(BF16)
    p3 = (r1 - p2.astype(F32)).astype(BF16)
    return _dot(e, p1) + _dot(e, p2) + _dot(e, p3)


def _log_sigmoid(x):
    return jnp.minimum(x, 0.0) - jnp.log(1.0 + jnp.exp(-jnp.abs(x)))


def _sigmoid(x):
    return 1.0 / (1.0 + jnp.exp(-x))


def _ada_kernel(c_ref, w_ref, b_ref, o_ref):
    c = c_ref[...]
    act = c * _sigmoid(c)
    o_ref[0] = _dot(act.astype(BF16), w_ref[0].astype(BF16)) + b_ref[0]


def _ada(c_rows, w_ada, b_ada):
    n_layers, d, n = w_ada.shape
    tn = _pick((1024, 512, 256, 128), n)
    return pl.pallas_call(
        _ada_kernel,
        grid=(n_layers, n // tn),
        in_specs=[pl.BlockSpec((8, d), lambda l, j: (0, 0)),
                  pl.BlockSpec((1, d, tn), lambda l, j: (l, 0, j)),
                  pl.BlockSpec((1, 1, tn), lambda l, j: (l, 0, j))],
        out_specs=pl.BlockSpec((1, 8, tn), lambda l, j: (l, 0, j)),
        out_shape=jax.ShapeDtypeStruct((n_layers, 8, n), F32),
        compiler_params=_cparams(2),
        name="ada_mod",
    )(c_rows, w_ada, b_ada.reshape(n_layers, 1, n))


def _norm_mod(x, g, shift, scale):
    y = x * lax.rsqrt(jnp.mean(x * x, axis=-1, keepdims=True) + EPS) * g
    return y * (1.0 + scale) + shift


def _inproj_kernel(x_ref, mod_ref, g_ref, w_ref, o_ref, h_ref):
    @pl.when(pl.program_id(1) == 0)
    def _():
        h = _norm_mod(x_ref[...], g_ref[...], mod_ref[0, 0:1, :], mod_ref[0, 1:2, :])
        h_ref[...] = h.astype(BF16)

    o_ref[...] = _dot(h_ref[...], w_ref[...])


def _inproj(x_all, mod, g, w, *, tm, nctx, bpb):
    m, d = x_all.shape
    n = w.shape[1]
    tn = _pick((1024, 512), n)
    return pl.pallas_call(
        _inproj_kernel,
        grid=(m // tm, n // tn),
        in_specs=[pl.BlockSpec((tm, d), lambda i, j: (i, 0)),
                  pl.BlockSpec((1, 6, d), lambda i, j: (_row_group(i, nctx, bpb), 0, 0)),
                  pl.BlockSpec((1, d), lambda i, j: (0, 0)),
                  pl.BlockSpec((d, tn), lambda i, j: (0, j))],
        out_specs=pl.BlockSpec((tm, tn), lambda i, j: (i, j)),
        out_shape=jax.ShapeDtypeStruct((m, n), F32),
        scratch_shapes=[pltpu.VMEM((tm, d), BF16)],
        compiler_params=_cparams(2),
        name="inproj",
    )(x_all, mod, g, w)


def _rw_prep_kernel(rkv_ref, prev_ref, next_ref, small_ref, conv_ref, w0_ref, w2_ref, a0_ref,
                    a2_ref, g2_ref, kk_ref, ka_ref, rk_ref, e_ref,
                    r_out, v_out, nkk_out, g_out, bonus_out, w_out, kka_out, km_out,
                    *, tb, nctx, ctx_bps, lat_bps):
    i = pl.program_id(0)
    c = r_out.shape[-1]
    j = jnp.where(i < nctx, i, i - nctx)
    bps = jnp.where(i < nctx, ctx_bps, lat_bps)
    first = lax.rem(j, bps) == 0
    last = lax.rem(j, bps) == bps - 1

    blk = rkv_ref[...]
    rows = lax.broadcasted_iota(jnp.int32, (tb, 1), 0)
    prev_row = jnp.where(first, 0.0, prev_ref[7:8, :])
    next_row = jnp.where(last, 0.0, next_ref[0:1, :])
    xm1 = jnp.where(rows == 0, prev_row, pltpu.roll(blk, 1, 0))
    xp1 = jnp.where(rows == tb - 1, next_row, pltpu.roll(blk, tb - 1, 0))
    conv = conv_ref[0:1, :] * xm1 + conv_ref[1:2, :] * blk + conv_ref[2:3, :] * xp1
    r = conv[:, 0:c]
    k = conv[:, c:2 * c]
    v = conv[:, 2 * c:3 * c]

    small = small_ref[...]
    wl = small[:, 0:128]
    al = small[:, 128:256]
    gl = small[:, 256:384]
    w_pre = w0_ref[...] + _dot(jnp.tanh(wl).astype(BF16), w2_ref[...])
    softplus = jnp.maximum(-w_pre, 0.0) + jnp.log(1.0 + jnp.exp(-jnp.abs(w_pre)))
    decay = jnp.exp(-jnp.exp(-softplus - 0.5))
    a = _sigmoid(a0_ref[...] + _dot(al.astype(BF16), a2_ref[...]))
    g = _dot(_sigmoid(gl).astype(BF16), g2_ref[...])

    e = e_ref[...]
    kkv = k * kk_ref[...]
    kk = kkv * lax.rsqrt(_split_dot(kkv * kkv, e) + 1e-12)
    bonus = _split_dot(r * k * rk_ref[...], e) * v

    r_out[...] = r
    v_out[...] = v
    nkk_out[...] = -kk
    g_out[...] = g
    bonus_out[...] = bonus
    for d in range(2):
        a_d = a[:, d * c:(d + 1) * c]
        w_out[d] = decay[:, d * c:(d + 1) * c]
        kka_out[d] = kk * a_d
        km_out[d] = k * (1.0 + (a_d - 1.0) * ka_ref[...])


def _rw_prep(p_all, prm, *, tb, nctx, ctx_bps, lat_bps):
    m = p_all.shape[0]
    c = prm["kk"].shape[-1]
    nblk = m // tb
    t8 = tb // 8
    full = lambda shape: pl.BlockSpec(shape, lambda i: (0,) * len(shape))
    tok = pl.BlockSpec((tb, c), lambda i: (i, 0))
    tok2 = pl.BlockSpec((2, tb, c), lambda i: (0, i, 0))
    one = jax.ShapeDtypeStruct((m, c), F32)
    two = jax.ShapeDtypeStruct((2, m, c), F32)
    return pl.pallas_call(
        functools.partial(_rw_prep_kernel, tb=tb, nctx=nctx, ctx_bps=ctx_bps, lat_bps=lat_bps),
        grid=(nblk,),
        in_specs=[pl.BlockSpec((tb, 3 * c), lambda i: (i, C_RKV // (3 * c))),
                  pl.BlockSpec((8, 3 * c), lambda i: (jnp.maximum(i * t8 - 1, 0), 0)),
                  pl.BlockSpec((8, 3 * c), lambda i: (jnp.minimum((i + 1) * t8, m // 8 - 1), 0)),
                  pl.BlockSpec((tb, 512), lambda i: (i, C_SMALL // 512)),
                  full((3, 3 * c)), full((1, 2 * c)), full((128, 2 * c)), full((1, 2 * c)),
                  full((128, 2 * c)), full((128, c)), full((1, c)), full((1, c)), full((1, c)),
                  full((c, c))],
        out_specs=[tok, tok, tok, tok, tok, tok2, tok2, tok2],
        out_shape=[one, one, one, one, one, two, two, two],
        compiler_params=_cparams(1),
        name="rwkv_prep",
    )(p_all, p_all, p_all, p_all, prm["conv"], prm["w0"], prm["w2"], prm["a0"], prm["a2"],
      prm["g2"], prm["kk"], prm["ka"], prm["rk"], prm["e"])


def _rw_scan_kernel(r_ref, v_ref, a_ref, w_ref, b_ref, k_ref, y_ref, s_ref, *, tb, nh, hd):
    d = pl.program_id(0)

    @pl.when(pl.program_id(2) == 0)
    def _():
        s_ref[...] = jnp.zeros_like(s_ref)

    eye = (lax.broadcasted_iota(jnp.int32, (hd, hd), 0)
           == lax.broadcasted_iota(jnp.int32, (hd, hd), 1)).astype(F32)

    def step(s, carry):
        t = jnp.where(d == 0, s, tb - 1 - s)
        rr = r_ref[pl.ds(t, 1), :]
        vv = v_ref[pl.ds(t, 1), :]
        aa = a_ref[pl.ds(t, 1), :]
        ww = w_ref[0, pl.ds(t, 1), :]
        bb = b_ref[0, pl.ds(t, 1), :]
        kk = k_ref[0, pl.ds(t, 1), :]
        ys = []
        for h in range(nh):
            sl = slice(h * hd, (h + 1) * hd)
            st = s_ref[h]
            sa = jnp.sum(st * aa[:, sl], axis=-1, keepdims=True)
            vcol = jnp.sum(eye * vv[:, sl], axis=-1, keepdims=True)
            st = st * ww[:, sl] + sa * bb[:, sl] + vcol * kk[:, sl]
            s_ref[h] = st
            ycol = jnp.sum(st * rr[:, sl], axis=-1, keepdims=True)
            ys.append(jnp.sum(eye * ycol, axis=0, keepdims=True))
        y_ref[0, pl.ds(t, 1), :] = jnp.concatenate(ys, axis=-1)
        return carry

    lax.fori_loop(0, tb, step, 0)


def _seq_block(d, b, i, *, n_batch, ctxb, latb):
    is_ctx = i < ctxb
    cs = jnp.where(d == 0, i, ctxb - 1 - i)
    lj = jnp.where(d == 0, i - ctxb, latb - 1 - (i - ctxb))
    return jnp.where(is_ctx, b * ctxb + cs, n_batch * ctxb + b * latb + lj)


def _rw_scan(r, v, nkk, w, kka, km, *, tb, n_batch, ctxb, latb):
    m, c = r.shape
    nh = c // RW_HEAD
    blk = functools.partial(_seq_block, n_batch=n_batch, ctxb=ctxb, latb=latb)
    tok = pl.BlockSpec((tb, c), lambda d, b, i: (blk(d, b, i), 0))
    tokd = pl.BlockSpec((1, tb, c), lambda d, b, i: (d, blk(d, b, i), 0))
    return pl.pallas_call(
        functools.partial(_rw_scan_kernel, tb=tb, nh=nh, hd=RW_HEAD),
        grid=(2, n_batch, ctxb + latb),
        in_specs=[tok, tok, tok, tokd, tokd, tokd],
        out_specs=tokd,
        out_shape=jax.ShapeDtypeStruct((2, m, c), F32),
        scratch_shapes=[pltpu.VMEM((nh, RW_HEAD, RW_HEAD), F32)],
        compiler_params=_cparams(3),
        name="rwkv_scan",
    )(r, v, nkk, w, kka, km)


def _rw_post_kernel(y_ref, g_ref, bonus_ref, lnw_ref, lnb_ref, e_ref, o_ref):
    y = y_ref[0] + y_ref[1]
    e = e_ref[...]
    inv = 1.0 / RW_HEAD
    yc = y - _split_dot(y, e) * inv
    var = _split_dot(yc * yc, e) * inv
    yn = yc * lax.rsqrt(var + RW_LN_EPS)
    o_ref[...] = (yn * lnw_ref[...] + lnb_ref[...] + bonus_ref[...]) * g_ref[...]


def _rw_post(y2, g, bonus, lnw, lnb, e, *, tb):
    _, m, c = y2.shape
    tok = pl.BlockSpec((tb, c), lambda i: (i, 0))
    vec = pl.BlockSpec((1, c), lambda i: (0, 0))
    return pl.pallas_call(
        _rw_post_kernel,
        grid=(m // tb,),
        in_specs=[pl.BlockSpec((2, tb, c), lambda i: (0, i, 0)), tok, tok, vec, vec,
                  pl.BlockSpec((c, c), lambda i: (0, 0))],
        out_specs=tok,
        out_shape=jax.ShapeDtypeStruct((m, c), F32),
        compiler_params=_cparams(1),
        name="rwkv_post",
    )(y2, g, bonus, lnw, lnb, e)


def _la_scan_block(q_ref, k_ref, v_ref, b_ref, o_ref, s_ref, *, tb, nh, dk, dv, reverse, q_scale):
    nch = tb // LA_CHUNK
    rowi = lax.broadcasted_iota(jnp.int32, (LA_CHUNK, 1), 0)

    def chunk(ci, carry):
        cc = (nch - 1 - ci) if reverse else ci
        rows = pl.ds(pl.multiple_of(cc * LA_CHUNK, LA_CHUNK), LA_CHUNK)
        for h in range(nh):
            ks = slice(h * dk, (h + 1) * dk)
            vs = slice(h * dv, (h + 1) * dv)
            q = q_ref[rows, ks] * q_scale
            k = k_ref[rows, ks]
            v = v_ref[rows, vs]
            b = b_ref[rows, ks]
            st = s_ref[h]
            b_last = b[0:1] if reverse else b[LA_CHUNK - 1:LA_CHUNK]
            o = _dot_nt((q * jnp.exp(b)).astype(BF16), st.astype(BF16))
            for s in range(LA_CHUNK):
                valid = (rowi <= s) if reverse else (rowi >= s)
                dm = jnp.where(valid, b - b[s:s + 1], 0.0)
                e = jnp.where(valid, jnp.exp(dm), 0.0)
                col = jnp.sum(q * k[s:s + 1] * e, axis=-1, keepdims=True)
                o = o + col * v[s:s + 1]
            o_ref[rows, vs] = o
            kd = k * jnp.exp(b_last - b)
            s_ref[h] = st * jnp.exp(b_last) + _dot_tn(v.astype(BF16), kd.astype(BF16))
        return carry

    lax.fori_loop(0, nch, chunk, 0)


def _hg_kernel(q_ref, f_ref, v_ref, gamma_ref, tri_ref, o_ref, s_ref, k_scr, b_scr,
               *, layer, tb, nh, dk, dv, reverse):
    @pl.when(pl.program_id(1) == 0)
    def _():
        s_ref[...] = jnp.zeros_like(s_ref)

    gam = gamma_ref[...]
    ex = jnp.exp(gam - jnp.max(gam, axis=0, keepdims=True))
    p = ex / jnp.sum(ex, axis=0, keepdims=True)
    cum = p[0:1]
    for i in range(1, layer + 1):
        cum = cum + p[i:i + 1]
    lb = cum - p[0:1]

    lo = jnp.log(lb)
    hi = jnp.log(1.0 - lb) + _log_sigmoid(f_ref[...])
    mx = jnp.maximum(lo, hi)
    mn = jnp.minimum(lo, hi)
    log_f = mx + jnp.log(1.0 + jnp.exp(mn - mx))
    k_scr[...] = 1.0 - jnp.exp(log_f)
    b_scr[...] = _split3_dot(tri_ref[...], log_f)
    _la_scan_block(q_ref, k_scr, v_ref, b_scr, o_ref, s_ref, tb=tb, nh=nh, dk=dk, dv=dv,
                   reverse=reverse, q_scale=1.0)


def _gla_kernel(qk_ref, v_ref, small_ref, gw_ref, gb_ref, tri_ref, o_ref, s_ref, b_scr,
                *, d, tb, nh, dk, dv, reverse):
    @pl.when(pl.program_id(1) == 0)
    def _():
        s_ref[...] = jnp.zeros_like(s_ref)

    hk = nh * dk
    code = small_ref[:, 384:512]
    pre = _dot(code.astype(BF16), gw_ref[...]) + gb_ref[...]
    log_g = _log_sigmoid(pre[:, d * hk:(d + 1) * hk]) * (1.0 / GLA_GATE_NORM)
    b_scr[...] = _split3_dot(tri_ref[...], log_g)
    _la_scan_block(qk_ref.at[:, 0:hk], qk_ref.at[:, hk:2 * hk], v_ref, b_scr, o_ref, s_ref,
                   tb=tb, nh=nh, dk=dk, dv=dv, reverse=reverse, q_scale=dk ** -0.5)


def _chunk_tri(tb, reverse):
    t = jnp.arange(tb)
    same = (t[:, None] // LA_CHUNK) == (t[None, :] // LA_CHUNK)
    tri = (t[None, :] >= t[:, None]) if reverse else (t[None, :] <= t[:, None])
    return (same & tri).astype(BF16)


def _la_index(d, n_batch, ctxb, latb):
    return lambda b, i: _seq_block(d, b, i, n_batch=n_batch, ctxb=ctxb, latb=latb)


def _hgrn2(p_all, gamma, *, layer, d, tb, n_batch, ctxb, latb):
    m = p_all.shape[0]
    c = gamma.shape[-1]
    nh = c // HG_DK
    blk = _la_index(d, n_batch, ctxb, latb)
    col = lambda off: pl.BlockSpec((tb, c), lambda b, i: (blk(b, i), off // c))
    return pl.pallas_call(
        functools.partial(_hg_kernel, layer=layer, tb=tb, nh=nh, dk=HG_DK, dv=HG_DK, reverse=d == 1),
        grid=(n_batch, ctxb + latb),
        in_specs=[col(C_HGQ), col(C_HGF + d * c), col(C_HGI),
                  pl.BlockSpec(gamma.shape, lambda b, i: (0, 0)),
                  pl.BlockSpec((tb, tb), lambda b, i: (0, 0))],
        out_specs=pl.BlockSpec((tb, c), lambda b, i: (blk(b, i), 0)),
        out_shape=jax.ShapeDtypeStruct((m, c), F32),
        scratch_shapes=[pltpu.VMEM((nh, HG_DK, HG_DK), F32), pltpu.VMEM((tb, c), F32),
                        pltpu.VMEM((tb, c), F32)],
        compiler_params=_cparams(2),
        name=f"hgrn2_scan{d}",
    )(p_all, p_all, p_all, gamma, _chunk_tri(tb, d == 1))


def _gla(p_all, gw, gb, *, d, tb, n_batch, ctxb, latb):
    m = p_all.shape[0]
    hk = gw.shape[-1] // 2
    dk = hk // GLA_HEADS
    c = 2 * hk
    dv = c // GLA_HEADS
    blk = _la_index(d, n_batch, ctxb, latb)
    col = lambda off: pl.BlockSpec((tb, c), lambda b, i: (blk(b, i), off // c))
    return pl.pallas_call(
        functools.partial(_gla_kernel, d=d, tb=tb, nh=GLA_HEADS, dk=dk, dv=dv, reverse=d == 1),
        grid=(n_batch, ctxb + latb),
        in_specs=[col(C_GLQK), col(C_GLV), col(C_SMALL),
                  pl.BlockSpec(gw.shape, lambda b, i: (0, 0)),
                  pl.BlockSpec(gb.shape, lambda b, i: (0, 0)),
                  pl.BlockSpec((tb, tb), lambda b, i: (0, 0))],
        out_specs=pl.BlockSpec((tb, c), lambda b, i: (blk(b, i), 0)),
        out_shape=jax.ShapeDtypeStruct((m, c), F32),
        scratch_shapes=[pltpu.VMEM((GLA_HEADS, dv, dk), F32), pltpu.VMEM((tb, hk), F32)],
        compiler_params=_cparams(2),
        name=f"gla_scan{d}",
    )(p_all, p_all, p_all, gw, gb, _chunk_tri(tb, d == 1))


def _la_post_kernel(of_ref, ob_ref, gate_ref, g_ref, y_ref, *, nh, dv):
    o = of_ref[...] + ob_ref[...]
    gate = gate_ref[...]
    outs = []
    for h in range(nh):
        oh = o[:, h * dv:(h + 1) * dv]
        outs.append(oh * lax.rsqrt(jnp.mean(oh * oh, axis=-1, keepdims=True) + EPS))
    y = jnp.concatenate(outs, axis=-1) * g_ref[...]
    y_ref[...] = y * (gate * _sigmoid(gate))


def _la_post(o_f, o_b, p_all, gate_off, norm_g, *, tb, nh):
    m, c = o_f.shape
    tok = pl.BlockSpec((tb, c), lambda i: (i, 0))
    return pl.pallas_call(
        functools.partial(_la_post_kernel, nh=nh, dv=c // nh),
        grid=(m // tb,),
        in_specs=[tok, tok, pl.BlockSpec((tb, c), lambda i: (i, gate_off // c)),
                  pl.BlockSpec((1, c), lambda i: (0, 0))],
        out_specs=tok,
        out_shape=jax.ShapeDtypeStruct((m, c), F32),
        compiler_params=_cparams(1),
        name="la_post",
    )(o_f, o_b, p_all, norm_g)


def _gelu(x):
    return 0.5 * x * (1.0 + jnp.tanh(0.7978845608028654 * (x + 0.044715 * (x * x * x))))


def _sgu_kernel(u_ref, v_ref, lnw_ref, lnb_ref, ws_ref, bs_ref, o_ref, *, rb):
    u = _gelu(u_ref[...])
    v = _gelu(v_ref[...])
    vc = v - jnp.mean(v, axis=-1, keepdims=True)
    vn = vc * lax.rsqrt(jnp.mean(vc * vc, axis=-1, keepdims=True) + EPS)
    vn = (vn * lnw_ref[...] + lnb_ref[...]).astype(BF16)
    gw = vn.shape[-1] // SGU_GROUPS
    for n in range(rb // SGU_CHUNK):
        rs = slice(n * SGU_CHUNK, (n + 1) * SGU_CHUNK)
        for g in range(SGU_GROUPS):
            cs = slice(g * gw, (g + 1) * gw)
            s = _dot(ws_ref[g], vn[rs, cs]) + bs_ref[g]
            o_ref[rs, cs] = u[rs, cs] * s


def _sgu(p_all, lnw, lnb, ws, bs, *, rb):
    m = p_all.shape[0]
    c = lnw.shape[-1]
    return pl.pallas_call(
        functools.partial(_sgu_kernel, rb=rb),
        grid=(m // rb,),
        in_specs=[pl.BlockSpec((rb, c), lambda i: (i, C_SGU // c)),
                  pl.BlockSpec((rb, c), lambda i: (i, C_SGU // c + 1)),
                  pl.BlockSpec((1, c), lambda i: (0, 0)),
                  pl.BlockSpec((1, c), lambda i: (0, 0)),
                  pl.BlockSpec(ws.shape, lambda i: (0, 0, 0)),
                  pl.BlockSpec(bs.shape, lambda i: (0, 0, 0))],
        out_specs=pl.BlockSpec((rb, c), lambda i: (i, 0)),
        out_shape=jax.ShapeDtypeStruct((m, c), F32),
        compiler_params=_cparams(1),
        name="sgu",
    )(p_all, p_all, lnw, lnb, ws, bs)


def _merge_kernel(ya_ref, yb_ref, yc_ref, yd_ref, g0_ref, g1_ref, g2_ref, g3_ref, w_ref, o_ref):
    ys = (ya_ref, yb_ref, yc_ref, yd_ref)
    gs = (g0_ref, g1_ref, g2_ref, g3_ref)
    acc = None
    for j in range(N_BRANCH):
        t = _sigmoid(gs[j][...]) * _dot(ys[j][...].astype(BF16), w_ref[j])
        acc = t if acc is None else acc + t
    o_ref[...] = acc.astype(BF16)


def _merge(ys, p_all, w_branch, *, tm, row_off):
    m = p_all.shape[0]
    _, c, d = w_branch.shape
    tn = 512
    mo = m - row_off * tm
    ytok = pl.BlockSpec((tm, c), lambda i, j: (i + row_off, 0))
    gate = lambda b: pl.BlockSpec((tm, tn), lambda i, j: (i + row_off, (C_GATE + b * d) // tn + j))
    return pl.pallas_call(
        _merge_kernel,
        grid=(mo // tm, d // tn),
        in_specs=[ytok, ytok, ytok, ytok, gate(0), gate(1), gate(2), gate(3),
                  pl.BlockSpec((N_BRANCH, c, tn), lambda i, j: (0, 0, j))],
        out_specs=pl.BlockSpec((tm, tn), lambda i, j: (i, j)),
        out_shape=jax.ShapeDtypeStruct((mo, d), BF16),
        compiler_params=_cparams(2),
        name="merge",
    )(*ys, p_all, p_all, p_all, p_all, w_branch)


def _outproj_kernel(m_ref, w_ref, x_ref, mod_ref, o_ref):
    o_ref[...] = x_ref[...] + mod_ref[0, 2:3, :] * _dot(m_ref[...], w_ref[...])


def _outproj(mm, w_out, x_all, mod, *, tm, row_off, nctx, bpb):
    mo, d = mm.shape
    tn = _pick((1024, 512), d)
    return pl.pallas_call(
        _outproj_kernel,
        grid=(mo // tm, d // tn),
        in_specs=[pl.BlockSpec((tm, d), lambda i, j: (i, 0)),
                  pl.BlockSpec((d, tn), lambda i, j: (0, j)),
                  pl.BlockSpec((tm, tn), lambda i, j: (i + row_off, j)),
                  pl.BlockSpec((1, 6, tn), lambda i, j: (_row_group(i + row_off, nctx, bpb), 0, j))],
        out_specs=pl.BlockSpec((tm, tn), lambda i, j: (i, j)),
        out_shape=jax.ShapeDtypeStruct((mo, d), F32),
        compiler_params=_cparams(2),
        name="outproj",
    )(mm, w_out, x_all, mod)


def _mlp_kernel(x_ref, mod_ref, g_ref, w1_ref, w2_ref, gf_ref, o_ref, h_ref, acc_ref, *, final_norm):
    j = pl.program_id(1)

    @pl.when(j == 0)
    def _():
        h = _norm_mod(x_ref[...], g_ref[...], mod_ref[0, 3:4, :], mod_ref[0, 4:5, :])
        h_ref[...] = h.astype(BF16)
        acc_ref[...] = jnp.zeros_like(acc_ref)

    a = jnp.maximum(_dot(h_ref[...], w1_ref[...]), 0.0)
    acc_ref[...] += _dot((a * a).astype(BF16), w2_ref[...])

    @pl.when(j == pl.num_programs(1) - 1)
    def _():
        y = x_ref[...] + mod_ref[0, 5:6, :] * acc_ref[...]
        if final_norm:
            y = y * lax.rsqrt(jnp.mean(y * y, axis=-1, keepdims=True) + EPS) * gf_ref[...]
        o_ref[...] = y


def _mlp(x_in, mod, g, w1, w2, g_final, *, tm, row_off, nctx, bpb, final_norm):
    mo, d = x_in.shape
    hid = w1.shape[1]
    th = _pick((512, 256, 128), hid)
    return pl.pallas_call(
        functools.partial(_mlp_kernel, final_norm=final_norm),
        grid=(mo // tm, hid // th),
        in_specs=[pl.BlockSpec((tm, d), lambda i, j: (i, 0)),
                  pl.BlockSpec((1, 6, d), lambda i, j: (_row_group(i + row_off, nctx, bpb), 0, 0)),
                  pl.BlockSpec((1, d), lambda i, j: (0, 0)),
                  pl.BlockSpec((d, th), lambda i, j: (0, j)),
                  pl.BlockSpec((th, d), lambda i, j: (j, 0)),
                  pl.BlockSpec((1, d), lambda i, j: (0, 0))],
        out_specs=pl.BlockSpec((tm, d), lambda i, j: (i, 0)),
        out_shape=jax.ShapeDtypeStruct((mo, d), F32),
        scratch_shapes=[pltpu.VMEM((tm, d), BF16), pltpu.VMEM((tm, d), F32)],
        compiler_params=_cparams(2),
        name="mlp",
    )(x_in, mod, g, w1, w2, g_final)


def _blockdiag2(w):
    _, r, c = w.shape
    z = jnp.zeros((r, c), w.dtype)
    out = jnp.concatenate([jnp.concatenate([w[0], z], axis=1), jnp.concatenate([z, w[1]], axis=1)], axis=0)
    return jnp.pad(out, ((0, 128 - 2 * r), (0, 0))).astype(BF16)


def _permute_w_in(w, d_model):
    c = d_model // N_BRANCH
    o = [0]
    for wd in (3 * c, 64 * 2, 64 * 2, 128, c, 2 * c, c, c, c, c, 32, c, 2 * c, N_BRANCH * d_model):
        o.append(o[-1] + wd)
    seg = lambda k: w[:, o[k]:o[k + 1]]
    pad = jnp.zeros((w.shape[0], 512 - 128 * 3 - 32), w.dtype)
    parts = [seg(0), seg(5), seg(12), seg(4), seg(6), seg(7), seg(8), seg(9), seg(11), seg(13),
             seg(1), seg(2), seg(3), seg(10), pad]
    out = jnp.concatenate(parts, axis=1).astype(BF16)
    assert out.shape[1] == N_COLS
    return out


def kernel(x, c, ctx, c_ctx, w_ada, b_ada, g_norm1, g_norm2, g_final, w_in, rw_conv, rw_w0, rw_w2,
           rw_a0, rw_a2, rw_g2, rw_kk, rw_ka, rw_rk, rw_ln_w, rw_ln_b, hg_gamma, hg_norm, gla_gw,
           gla_gb, gla_norm, sgu_ln_w, sgu_ln_b, sgu_w, sgu_b, w_branch, w_out, w_mlp1, w_mlp2):
    n_batch, seq, d_model = x.shape
    ctx_len = ctx.shape[1]
    depth = w_in.shape[0]
    cw = d_model // N_BRANCH
    assert cw == 512 and d_model == 2048, "column layout constants assume D_MODEL = 2048"
    m_ctx = n_batch * ctx_len

    tm = _pick((1024, 512, 256, 128), m_ctx, seq)
    tb = _pick((256, 128), ctx_len, seq)
    nctx, bpb = m_ctx // tm, seq // tm
    ctxb, latb = ctx_len // tb, seq // tb
    seqs = dict(tb=tb, n_batch=n_batch, ctxb=ctxb, latb=latb)

    x_all = jnp.concatenate([ctx.reshape(m_ctx, d_model), x.reshape(n_batch * seq, d_model)], axis=0)
    c_rows = jnp.concatenate([c_ctx[None, :], c, jnp.zeros((7 - n_batch, d_model), F32)], axis=0)
    mod_all = _ada(c_rows, w_ada, b_ada).reshape(depth, 8, 6, d_model)

    head_ones = (jnp.arange(cw)[:, None] // RW_HEAD == jnp.arange(cw)[None, :] // RW_HEAD).astype(BF16)
    row = lambda a: a.reshape(1, -1)

    for l in range(depth):
        last = l == depth - 1
        mod = mod_all[l]
        p_all = _inproj(x_all, mod, row(g_norm1[l]), _permute_w_in(w_in[l], d_model),
                        tm=tm, nctx=nctx, bpb=bpb)

        prm = dict(conv=rw_conv[l], w0=row(rw_w0[l]), w2=_blockdiag2(rw_w2[l]), a0=row(rw_a0[l]),
                   a2=_blockdiag2(rw_a2[l]), g2=rw_g2[l].astype(BF16), kk=row(rw_kk[l]),
                   ka=row(rw_ka[l]), rk=row(rw_rk[l]), e=head_ones)
        r, v, nkk, g, bonus, w, kka, km = _rw_prep(p_all, prm, tb=tb, nctx=m_ctx // tb,
                                                    ctx_bps=ctxb, lat_bps=latb)
        y2 = _rw_scan(r, v, nkk, w, kka, km, **seqs)
        ya = _rw_post(y2, g, bonus, row(rw_ln_w[l]), row(rw_ln_b[l]), head_ones, tb=tb)

        ob = [_hgrn2(p_all, hg_gamma, layer=l, d=d, **seqs) for d in range(2)]
        yb = _la_post(ob[0], ob[1], p_all, C_HGG, row(hg_norm[l]), tb=tb, nh=cw // HG_DK)

        gw = _blockdiag2(gla_gw[l])
        gb = row(gla_gb[l])
        oc = [_gla(p_all, gw, gb, d=d, **seqs) for d in range(2)]
        yc = _la_post(oc[0], oc[1], p_all, C_GLG, row(gla_norm[l]), tb=tb, nh=GLA_HEADS)

        bs = jnp.broadcast_to(sgu_b[l][:, :, None], sgu_w[l].shape)
        yd = _sgu(p_all, row(sgu_ln_w[l]), row(sgu_ln_b[l]), sgu_w[l].astype(BF16), bs,
                  rb=_pick((512, 256, 128), m_ctx, seq))

        row_off = nctx if last else 0
        mm = _merge((ya, yb, yc, yd), p_all, w_branch[l].astype(BF16), tm=tm // 2, row_off=2 * row_off)
        x_mid = _outproj(mm, w_out[l].astype(BF16), x_all, mod, tm=tm, row_off=row_off, nctx=nctx, bpb=bpb)
        x_all = _mlp(x_mid, mod, row(g_norm2[l]), w_mlp1[l].astype(BF16), w_mlp2[l].astype(BF16),
                     row(g_final), tm=tm // 2, row_off=2 * row_off, nctx=2 * nctx, bpb=2 * bpb,
                     final_norm=last)
    return x_all.reshape(n_batch, seq, d_model)
```

```python
import functools

import jax
import jax.numpy as jnp
from jax import lax
from jax.experimental import pallas as pl
from jax.experimental.pallas import tpu as pltpu

F32 = jnp.float32
BF16 = jnp.bfloat16

N_BRANCH = 4
RW_HEAD = 64
RW_LN_EPS = 64e-5
HG_DK = 128
GLA_HEADS = 4
GLA_GATE_NORM = 16.0
LA_CHUNK = 16
SGU_CHUNK = 128
SGU_GROUPS = 4
EPS = 1e-6

C_RKV = 0
C_HGF = 1536
C_SGU = 2560
C_HGQ = 3584
C_HGI = 4096
C_HGG = 4608
C_GLQK = 5120
C_GLV = 5632
C_GLG = 6144
C_GATE = 6656
C_SMALL = 14848
N_COLS = 15360

VMEM_LIMIT = 56 * 1024 * 1024


def _cparams(n_axes):
    return pltpu.CompilerParams(dimension_semantics=("arbitrary",) * n_axes,
                                vmem_limit_bytes=VMEM_LIMIT)


def _pick(n_list, *dims):
    for n in n_list:
        if all(d % n == 0 for d in dims):
            return n
    raise ValueError(f"no block size in {n_list} divides {dims}")


def _row_group(i, nctx, bpb):
    return jnp.where(i < nctx, 0, 1 + (i - nctx) // bpb)


def _dot(a, b):
    return jnp.dot(a, b, preferred_element_type=F32)


def _dot_nt(a, b):
    return lax.dot_general(a, b, (((1,), (1,)), ((), ())), preferred_element_type=F32)


def _dot_tn(a, b):
    return lax.dot_general(a, b, (((0,), (0,)), ((), ())), preferred_element_type=F32)


def _split_dot(x, e):
    hi = x.astype(BF16)
    lo = (x - hi.astype(F32)).astype(BF16)
    return _dot(hi, e) + _dot(lo, e)


def _split3_dot(e, x):
    p1 = x.astype(BF16)
    r1 = x - p1.astype(F32)
    p2 = r1.astype(BF16)
    p3 = (r1 - p2.astype(F32)).astype(BF16)
    return _dot(e, p1) + _dot(e, p2) + _dot(e, p3)


def _log_sigmoid(x):
    return jnp.minimum(x, 0.0) - jnp.log(1.0 + jnp.exp(-jnp.abs(x)))


def _sigmoid(x):
    return 1.0 / (1.0 + jnp.exp(-x))


def _ada_kernel(c_ref, w_ref, b_ref, o_ref):
    c = c_ref[...]
    act = c * _sigmoid(c)
    o_ref[0] = _dot(act.astype(BF16), w_ref[0].astype(BF16)) + b_ref[0]


def _ada(c_rows, w_ada, b_ada):
    n_layers, d, n = w_ada.shape
    tn = _pick((1024, 512, 256, 128), n)
    return pl.pallas_call(
        _ada_kernel,
        grid=(n_layers, n // tn),
        in_specs=[pl.BlockSpec((8, d), lambda l, j: (0, 0)),
                  pl.BlockSpec((1, d, tn), lambda l, j: (l, 0, j)),
                  pl.BlockSpec((1, 1, tn), lambda l, j: (l, 0, j))],
        out_specs=pl.BlockSpec((1, 8, tn), lambda l, j: (l, 0, j)),
        out_shape=jax.ShapeDtypeStruct((n_layers, 8, n), F32),
        compiler_params=_cparams(2),
        name="ada_mod",
    )(c_rows, w_ada, b_ada.reshape(n_layers, 1, n))


def _norm_mod(x, g, shift, scale):
    y = x * lax.rsqrt(jnp.mean(x * x, axis=-1, keepdims=True) + EPS) * g
    return y * (1.0 + scale) + shift


def _inproj_kernel(x_ref, mod_ref, g_ref, w_ref, o_ref, h_ref):
    @pl.when(pl.program_id(1) == 0)
    def _():
        h = _norm_mod(x_ref[...], g_ref[...], mod_ref[0, 0:1, :], mod_ref[0, 1:2, :])
        h_ref[...] = h.astype(BF16)

    o_ref[...] = _dot(h_ref[...], w_ref[...])


def _inproj(x_all, mod, g, w, *, tm, nctx, bpb):
    m, d = x_all.shape
    n = w.shape[1]
    tn = _pick((1024, 512), n)
    return pl.pallas_call(
        _inproj_kernel,
        grid=(m // tm, n // tn),
        in_specs=[pl.BlockSpec((tm, d), lambda i, j: (i, 0)),
                  pl.BlockSpec((1, 6, d), lambda i, j: (_row_group(i, nctx, bpb), 0, 0)),
                  pl.BlockSpec((1, d), lambda i, j: (0, 0)),
                  pl.BlockSpec((d, tn), lambda i, j: (0, j))],
        out_specs=pl.BlockSpec((tm, tn), lambda i, j: (i, j)),
        out_shape=jax.ShapeDtypeStruct((m, n), F32),
        scratch_shapes=[pltpu.VMEM((tm, d), BF16)],
        compiler_params=_cparams(2),
        name="inproj",
    )(x_all, mod, g, w)


def _rw_prep_kernel(rkv_ref, prev_ref, next_ref, small_ref, conv_ref, w0_ref, w2_ref, a0_ref,
                    a2_ref, g2_ref, kk_ref, ka_ref, rk_ref, e_ref,
                    r_out, v_out, nkk_out, g_out, bonus_out, w_out, kka_out, km_out,
                    *, tb, nctx, ctx_bps, lat_bps):
    i = pl.program_id(0)
    c = r_out.shape[-1]
    j = jnp.where(i < nctx, i, i - nctx)
    bps = jnp.where(i < nctx, ctx_bps, lat_bps)
    first = lax.rem(j, bps) == 0
    last = lax.rem(j, bps) == bps - 1

    blk = rkv_ref[...]
    rows = lax.broadcasted_iota(jnp.int32, (tb, 1), 0)
    prev_row = jnp.where(first, 0.0, prev_ref[7:8, :])
    next_row = jnp.where(last, 0.0, next_ref[0:1, :])
    xm1 = jnp.where(rows == 0, prev_row, pltpu.roll(blk, 1, 0))
    xp1 = jnp.where(rows == tb - 1, next_row, pltpu.roll(blk, tb - 1, 0))
    conv = conv_ref[0:1, :] * xm1 + conv_ref[1:2, :] * blk + conv_ref[2:3, :] * xp1
    r = conv[:, 0:c]
    k = conv[:, c:2 * c]
    v = conv[:, 2 * c:3 * c]

    small = small_ref[...]
    wl = small[:, 0:128]
    al = small[:, 128:256]
    gl = small[:, 256:384]
    w_pre = w0_ref[...] + _dot(jnp.tanh(wl).astype(BF16), w2_ref[...])
    softplus = jnp.maximum(-w_pre, 0.0) + jnp.log(1.0 + jnp.exp(-jnp.abs(w_pre)))
    log_decay = -jnp.exp(-softplus - 0.5)
    a = _sigmoid(a0_ref[...] + _dot(al.astype(BF16), a2_ref[...]))
    g = _dot(_sigmoid(gl).astype(BF16), g2_ref[...])

    e = e_ref[...]
    kkv = k * kk_ref[...]
    kk = kkv * lax.rsqrt(_split_dot(kkv * kkv, e) + 1e-12)
    bonus = _split_dot(r * k * rk_ref[...], e) * v

    r_out[...] = r
    v_out[...] = v
    nkk_out[...] = -kk
    g_out[...] = g
    bonus_out[...] = bonus
    for d in range(2):
        a_d = a[:, d * c:(d + 1) * c]
        w_out[d] = log_decay[:, d * c:(d + 1) * c]
        kka_out[d] = kk * a_d
        km_out[d] = k * (1.0 + (a_d - 1.0) * ka_ref[...])


def _rw_prep(p_all, prm, *, tb, nctx, ctx_bps, lat_bps):
    m = p_all.shape[0]
    c = prm["kk"].shape[-1]
    nblk = m // tb
    t8 = tb // 8
    full = lambda shape: pl.BlockSpec(shape, lambda i: (0,) * len(shape))
    tok = pl.BlockSpec((tb, c), lambda i: (i, 0))
    tok2 = pl.BlockSpec((2, tb, c), lambda i: (0, i, 0))
    one = jax.ShapeDtypeStruct((m, c), F32)
    two = jax.ShapeDtypeStruct((2, m, c), F32)
    return pl.pallas_call(
        functools.partial(_rw_prep_kernel, tb=tb, nctx=nctx, ctx_bps=ctx_bps, lat_bps=lat_bps),
        grid=(nblk,),
        in_specs=[pl.BlockSpec((tb, 3 * c), lambda i: (i, C_RKV // (3 * c))),
                  pl.BlockSpec((8, 3 * c), lambda i: (jnp.maximum(i * t8 - 1, 0), 0)),
                  pl.BlockSpec((8, 3 * c), lambda i: (jnp.minimum((i + 1) * t8, m // 8 - 1), 0)),
                  pl.BlockSpec((tb, 512), lambda i: (i, C_SMALL // 512)),
                  full((3, 3 * c)), full((1, 2 * c)), full((128, 2 * c)), full((1, 2 * c)),
                  full((128, 2 * c)), full((128, c)), full((1, c)), full((1, c)), full((1, c)),
                  full((c, c))],
        out_specs=[tok, tok, tok, tok, tok, tok2, tok2, tok2],
        out_shape=[one, one, one, one, one, two, two, two],
        compiler_params=_cparams(1),
        name="rwkv_prep",
    )(p_all, p_all, p_all, p_all, prm["conv"], prm["w0"], prm["w2"], prm["a0"], prm["a2"],
      prm["g2"], prm["kk"], prm["ka"], prm["rk"], prm["e"])


def _dot_nt3(x, y):
    xh = x.astype(BF16)
    xl = (x - xh.astype(F32)).astype(BF16)
    yh = y.astype(BF16)
    yl = (y - yh.astype(F32)).astype(BF16)
    return _dot_nt(xh, yh) + _dot_nt(xl, yh) + _dot_nt(xh, yl)


def _rw_scan_kernel(r_ref, v_ref, a_ref, lw_ref, b_ref, k_ref, tri_ref, y_ref,
                    s_ref, c_scr, ag_scr, rg_scr, bg_scr, kg_scr,
                    col0_scr, akv0_scr, aar0_scr, col1_scr, akv1_scr, aar1_scr, *, tb, npair, reverse):
    L = LA_CHUNK
    hd = RW_HEAD

    @pl.when(pl.program_id(1) == 0)
    def _():
        s_ref[...] = jnp.zeros_like(s_ref)

    lw = lw_ref[0]
    c = _split3_dot(tri_ref[...], lw)
    c_scr[...] = c
    enc = jnp.exp(-c)
    ag_scr[...] = a_ref[...] * jnp.exp(c - lw)
    rg_scr[...] = r_ref[...] * jnp.exp(c)
    bg_scr[...] = b_ref[0] * enc
    kg_scr[...] = k_ref[0] * enc

    lane = lax.broadcasted_iota(jnp.int32, (1, 2 * hd), 1)
    lo_b = lane < hd
    lo = lo_b.astype(F32)
    hi = 1.0 - lo
    ti = lax.broadcasted_iota(jnp.int32, (2 * L, 4 * L), 0)
    sj = lax.broadcasted_iota(jnp.int32, (2 * L, 4 * L), 1) & (L - 1)
    tt = ti & (L - 1)
    earlier = (sj > tt) if reverse else (sj < tt)
    aa_mask = (earlier | ((ti >= L) & (sj == tt))).astype(F32)
    blk_mask = ((lax.broadcasted_iota(jnp.int32, (2 * hd, 2 * hd), 0) >> 6)
                == (lax.broadcasted_iota(jnp.int32, (2 * hd, 2 * hd), 1) >> 6)).astype(F32)
    zeros2l = jnp.zeros((2 * L, 2 * hd), F32)
    last = 0 if reverse else L - 1
    nch = tb // L

    pairs = range(npair)
    lss = [slice(p * 2 * hd, (p + 1) * 2 * hd) for p in pairs]
    order = range(L - 1, -1, -1) if reverse else range(L)

    def chunk_rows(ci):
        cc = (nch - 1 - ci) if reverse else ci
        return pl.ds(pl.multiple_of(cc * L, L), L)

    def lhs_of(rows):
        return [jnp.concatenate([ag_scr[rows, ls], rg_scr[rows, ls]], axis=0) for ls in lss]

    def prep_aa(rows):
        out = []
        lhs = lhs_of(rows)
        for p in pairs:
            bg = bg_scr[rows, lss[p]]
            kg = kg_scr[rows, lss[p]]
            rhs = jnp.concatenate([bg * lo, bg * hi, kg * lo, kg * hi], axis=0)
            out.append(_dot_nt3(lhs[p], rhs))
        return out

    def prep_akv(rows, aa_raw, slot):
        _, akv_scr, aar_scr = slot
        aa = [x * aa_mask for x in aa_raw]
        for p in pairs:
            v = v_ref[rows, lss[p]]
            vv = jnp.concatenate([zeros2l, v * lo, v * hi], axis=0).astype(BF16)
            akv_scr[p] = _dot(aa[p][0:L].astype(BF16), vv)
            aar_scr[p] = aa[p][L:2 * L]
        return aa

    def prep_cols(aa, slot):
        col_scr = slot[0]
        for p in pairs:
            for s in range(L):
                col_scr[p, s] = jnp.where(lo_b, aa[p][0:L, s:s + 1], aa[p][0:L, L + s:L + s + 1])

    def adv_g(rows):
        lhs = lhs_of(rows)
        return [_dot_nt(lhs[p].astype(BF16), s_ref[p].astype(BF16)) for p in pairs]

    def adv_solve(g, slot):
        col_scr, akv_scr, _ = slot
        u = [g[p][0:L] + akv_scr[p] for p in pairs]
        for s in order:
            for p in pairs:
                u[p] = u[p] + col_scr[p, s] * u[p][s:s + 1]
        return u

    def adv_out(rows, g, u, slot):
        aar_scr = slot[2]
        for p in pairs:
            v = v_ref[rows, lss[p]]
            uv = jnp.concatenate([u[p] * lo, u[p] * hi, v * lo, v * hi], axis=0).astype(BF16)
            y_ref[rows, lss[p]] = g[p][L:2 * L] + _dot(aar_scr[p].astype(BF16), uv)
        for p in pairs:
            ls = lss[p]
            cch = c_scr[rows, ls]
            cl = cch[last:last + 1]
            dec = jnp.exp(cl - cch)
            bk = jnp.concatenate([b_ref[0, rows, ls] * dec, k_ref[0, rows, ls] * dec], axis=0)
            upd = _dot_tn(jnp.concatenate([u[p], v_ref[rows, ls]], axis=0).astype(BF16), bk.astype(BF16))
            s_ref[p] = s_ref[p] * jnp.exp(cl) + upd * blk_mask

    def step(ci_adv, slot_adv, ci_prep, slot_prep):
        rows_a = chunk_rows(ci_adv)
        rows_p = chunk_rows(ci_prep)
        aa_raw = prep_aa(rows_p)
        g = adv_g(rows_a)
        aa = prep_akv(rows_p, aa_raw, slot_prep)
        u = adv_solve(g, slot_adv)
        prep_cols(aa, slot_prep)
        adv_out(rows_a, g, u, slot_adv)

    slot0 = (col0_scr, akv0_scr, aar0_scr)
    slot1 = (col1_scr, akv1_scr, aar1_scr)
    rows0 = chunk_rows(0)
    prep_cols(prep_akv(rows0, prep_aa(rows0), slot0), slot0)

    def two_chunks(j, carry):
        c0 = 2 * j
        step(c0, slot0, c0 + 1, slot1)
        step(c0 + 1, slot1, jnp.minimum(c0 + 2, nch - 1), slot0)
        return carry

    lax.fori_loop(0, nch // 2, two_chunks, 0)


def _seq_block(d, b, i, *, n_batch, ctxb, latb):
    is_ctx = i < ctxb
    cs = jnp.where(d == 0, i, ctxb - 1 - i)
    lj = jnp.where(d == 0, i - ctxb, latb - 1 - (i - ctxb))
    return jnp.where(is_ctx, b * ctxb + cs, n_batch * ctxb + b * latb + lj)


def _rw_scan(r, v, nkk, lw, kka, km, *, d, tb, n_batch, ctxb, latb):
    m, c = r.shape
    npair = c // (2 * RW_HEAD)
    blk = _la_index(d, n_batch, ctxb, latb)
    tok = pl.BlockSpec((tb, c), lambda b, i: (blk(b, i), 0))
    tokd = pl.BlockSpec((1, tb, c), lambda b, i: (d, blk(b, i), 0))
    buf = pltpu.VMEM((tb, c), F32)
    slot = [pltpu.VMEM((npair, LA_CHUNK, LA_CHUNK, 2 * RW_HEAD), F32),
            pltpu.VMEM((npair, LA_CHUNK, 2 * RW_HEAD), F32),
            pltpu.VMEM((npair, LA_CHUNK, 4 * LA_CHUNK), F32)]
    return pl.pallas_call(
        functools.partial(_rw_scan_kernel, tb=tb, npair=npair, reverse=d == 1),
        grid=(n_batch, ctxb + latb),
        in_specs=[tok, tok, tok, tokd, tokd, tokd, pl.BlockSpec((tb, tb), lambda b, i: (0, 0))],
        out_specs=tok,
        out_shape=jax.ShapeDtypeStruct((m, c), F32),
        scratch_shapes=[pltpu.VMEM((npair, 2 * RW_HEAD, 2 * RW_HEAD), F32), buf, buf, buf, buf, buf] + slot + slot,
        compiler_params=_cparams(2),
        name=f"rwkv_scan{d}",
    )(r, v, nkk, lw, kka, km, _chunk_tri(tb, d == 1))


def _rw_post_kernel(yf_ref, yb_ref, g_ref, bonus_ref, lnw_ref, lnb_ref, e_ref, o_ref):
    y = yf_ref[...] + yb_ref[...]
    e = e_ref[...]
    inv = 1.0 / RW_HEAD
    yc = y - _split_dot(y, e) * inv
    var = _split_dot(yc * yc, e) * inv
    yn = yc * lax.rsqrt(var + RW_LN_EPS)
    o_ref[...] = (yn * lnw_ref[...] + lnb_ref[...] + bonus_ref[...]) * g_ref[...]


def _rw_post(y_f, y_b, g, bonus, lnw, lnb, e, *, tb):
    m, c = y_f.shape
    tok = pl.BlockSpec((tb, c), lambda i: (i, 0))
    vec = pl.BlockSpec((1, c), lambda i: (0, 0))
    return pl.pallas_call(
        _rw_post_kernel,
        grid=(m // tb,),
        in_specs=[tok, tok, tok, tok, vec, vec, pl.BlockSpec((c, c), lambda i: (0, 0))],
        out_specs=tok,
        out_shape=jax.ShapeDtypeStruct((m, c), F32),
        compiler_params=_cparams(1),
        name="rwkv_post",
    )(y_f, y_b, g, bonus, lnw, lnb, e)


def _la_scan_block(q_ref, k_ref, v_ref, b_ref, o_ref, s_ref, *, tb, nh, dk, dv, reverse, q_scale):
    nch = tb // LA_CHUNK
    lanes = 128
    pack = lanes // dk
    rowi = lax.broadcasted_iota(jnp.int32, (LA_CHUNK, 1), 0)
    lane = lax.broadcasted_iota(jnp.int32, (1, lanes), 1)
    head_lanes = [(lane // dk == j).astype(F32) for j in range(pack)]
    if pack > 1:
        blk_mask = ((lax.broadcasted_iota(jnp.int32, (pack * dv, lanes), 0) // dv)
                    == (lax.broadcasted_iota(jnp.int32, (pack * dv, lanes), 1) // dk)).astype(F32)

    def chunk(ci, carry):
        cc = (nch - 1 - ci) if reverse else ci
        rows = pl.ds(pl.multiple_of(cc * LA_CHUNK, LA_CHUNK), LA_CHUNK)
        for g in range(nh // pack):
            ks = slice(g * lanes, (g + 1) * lanes)
            q = q_ref[rows, ks] * q_scale
            k = k_ref[rows, ks]
            v = v_ref[rows, g * pack * dv:(g + 1) * pack * dv]
            b = b_ref[rows, ks]
            st = s_ref[g]
            b_last = b[0:1] if reverse else b[LA_CHUNK - 1:LA_CHUNK]
            o_inter = _dot_nt((q * jnp.exp(b)).astype(BF16), st.astype(BF16))
            o = [o_inter[:, j * dv:(j + 1) * dv] for j in range(pack)]
            for s in range(LA_CHUNK):
                valid = (rowi <= s) if reverse else (rowi >= s)
                dm = jnp.where(valid, b - b[s:s + 1], 0.0)
                e = jnp.where(valid, jnp.exp(dm), 0.0)
                term = q * k[s:s + 1] * e
                for j in range(pack):
                    tj = term if pack == 1 else term * head_lanes[j]
                    col = jnp.sum(tj, axis=-1, keepdims=True)
                    o[j] = o[j] + col * v[s:s + 1, j * dv:(j + 1) * dv]
            for j in range(pack):
                o_ref[rows, (g * pack + j) * dv:(g * pack + j + 1) * dv] = o[j]
            kd = k * jnp.exp(b_last - b)
            upd = _dot_tn(v.astype(BF16), kd.astype(BF16))
            if pack > 1:
                upd = upd * blk_mask
            s_ref[g] = st * jnp.exp(b_last) + upd
        return carry

    lax.fori_loop(0, nch, chunk, 0)


def _hg_kernel(q_ref, f_ref, v_ref, gamma_ref, tri_ref, o_ref, s_ref, k_scr, b_scr,
               *, layer, tb, nh, dk, dv, reverse):
    @pl.when(pl.program_id(1) == 0)
    def _():
        s_ref[...] = jnp.zeros_like(s_ref)

    gam = gamma_ref[...]
    ex = jnp.exp(gam - jnp.max(gam, axis=0, keepdims=True))
    p = ex / jnp.sum(ex, axis=0, keepdims=True)
    cum = p[0:1]
    for i in range(1, layer + 1):
        cum = cum + p[i:i + 1]
    lb = cum - p[0:1]

    lo = jnp.log(lb)
    hi = jnp.log(1.0 - lb) + _log_sigmoid(f_ref[...])
    mx = jnp.maximum(lo, hi)
    mn = jnp.minimum(lo, hi)
    log_f = mx + jnp.log(1.0 + jnp.exp(mn - mx))
    k_scr[...] = 1.0 - jnp.exp(log_f)
    b_scr[...] = _split3_dot(tri_ref[...], log_f)
    _la_scan_block(q_ref, k_scr, v_ref, b_scr, o_ref, s_ref, tb=tb, nh=nh, dk=dk, dv=dv,
                   reverse=reverse, q_scale=1.0)


def _gla_kernel(qk_ref, v_ref, small_ref, gw_ref, gb_ref, tri_ref, o_ref, s_ref, b_scr,
                *, d, tb, nh, dk, dv, reverse):
    @pl.when(pl.program_id(1) == 0)
    def _():
        s_ref[...] = jnp.zeros_like(s_ref)

    hk = nh * dk
    code = small_ref[:, 384:512]
    pre = _dot(code.astype(BF16), gw_ref[...]) + gb_ref[...]
    log_g = _log_sigmoid(pre[:, d * hk:(d + 1) * hk]) * (1.0 / GLA_GATE_NORM)
    b_scr[...] = _split3_dot(tri_ref[...], log_g)
    _la_scan_block(qk_ref.at[:, 0:hk], qk_ref.at[:, hk:2 * hk], v_ref, b_scr, o_ref, s_ref,
                   tb=tb, nh=nh, dk=dk, dv=dv, reverse=reverse, q_scale=dk ** -0.5)


def _chunk_tri(tb, reverse):
    t = jnp.arange(tb)
    same = (t[:, None] // LA_CHUNK) == (t[None, :] // LA_CHUNK)
    tri = (t[None, :] >= t[:, None]) if reverse else (t[None, :] <= t[:, None])
    return (same & tri).astype(BF16)


def _la_index(d, n_batch, ctxb, latb):
    return lambda b, i: _seq_block(d, b, i, n_batch=n_batch, ctxb=ctxb, latb=latb)


def _hgrn2(p_all, gamma, *, layer, d, tb, n_batch, ctxb, latb):
    m = p_all.shape[0]
    c = gamma.shape[-1]
    nh = c // HG_DK
    blk = _la_index(d, n_batch, ctxb, latb)
    col = lambda off: pl.BlockSpec((tb, c), lambda b, i: (blk(b, i), off // c))
    return pl.pallas_call(
        functools.partial(_hg_kernel, layer=layer, tb=tb, nh=nh, dk=HG_DK, dv=HG_DK, reverse=d == 1),
        grid=(n_batch, ctxb + latb),
        in_specs=[col(C_HGQ), col(C_HGF + d * c), col(C_HGI),
                  pl.BlockSpec(gamma.shape, lambda b, i: (0, 0)),
                  pl.BlockSpec((tb, tb), lambda b, i: (0, 0))],
        out_specs=pl.BlockSpec((tb, c), lambda b, i: (blk(b, i), 0)),
        out_shape=jax.ShapeDtypeStruct((m, c), F32),
        scratch_shapes=[pltpu.VMEM((nh, HG_DK, HG_DK), F32), pltpu.VMEM((tb, c), F32),
                        pltpu.VMEM((tb, c), F32)],
        compiler_params=_cparams(2),
        name=f"hgrn2_scan{d}",
    )(p_all, p_all, p_all, gamma, _chunk_tri(tb, d == 1))


def _gla(p_all, gw, gb, *, d, tb, n_batch, ctxb, latb):
    m = p_all.shape[0]
    hk = gw.shape[-1] // 2
    dk = hk // GLA_HEADS
    c = 2 * hk
    dv = c // GLA_HEADS
    blk = _la_index(d, n_batch, ctxb, latb)
    col = lambda off: pl.BlockSpec((tb, c), lambda b, i: (blk(b, i), off // c))
    return pl.pallas_call(
        functools.partial(_gla_kernel, d=d, tb=tb, nh=GLA_HEADS, dk=dk, dv=dv, reverse=d == 1),
        grid=(n_batch, ctxb + latb),
        in_specs=[col(C_GLQK), col(C_GLV), col(C_SMALL),
                  pl.BlockSpec(gw.shape, lambda b, i: (0, 0)),
                  pl.BlockSpec(gb.shape, lambda b, i: (0, 0)),
                  pl.BlockSpec((tb, tb), lambda b, i: (0, 0))],
        out_specs=pl.BlockSpec((tb, c), lambda b, i: (blk(b, i), 0)),
        out_shape=jax.ShapeDtypeStruct((m, c), F32),
        scratch_shapes=[pltpu.VMEM((hk // 128, (128 // dk) * dv, 128), F32), pltpu.VMEM((tb, hk), F32)],
        compiler_params=_cparams(2),
        name=f"gla_scan{d}",
    )(p_all, p_all, p_all, gw, gb, _chunk_tri(tb, d == 1))


def _la_post_kernel(of_ref, ob_ref, gate_ref, g_ref, y_ref, *, nh, dv):
    o = of_ref[...] + ob_ref[...]
    gate = gate_ref[...]
    outs = []
    for h in range(nh):
        oh = o[:, h * dv:(h + 1) * dv]
        outs.append(oh * lax.rsqrt(jnp.mean(oh * oh, axis=-1, keepdims=True) + EPS))
    y = jnp.concatenate(outs, axis=-1) * g_ref[...]
    y_ref[...] = y * (gate * _sigmoid(gate))


def _la_post(o_f, o_b, p_all, gate_off, norm_g, *, tb, nh):
    m, c = o_f.shape
    tok = pl.BlockSpec((tb, c), lambda i: (i, 0))
    return pl.pallas_call(
        functools.partial(_la_post_kernel, nh=nh, dv=c // nh),
        grid=(m // tb,),
        in_specs=[tok, tok, pl.BlockSpec((tb, c), lambda i: (i, gate_off // c)),
                  pl.BlockSpec((1, c), lambda i: (0, 0))],
        out_specs=tok,
        out_shape=jax.ShapeDtypeStruct((m, c), F32),
        compiler_params=_cparams(1),
        name="la_post",
    )(o_f, o_b, p_all, norm_g)


def _gelu(x):
    return 0.5 * x * (1.0 + jnp.tanh(0.7978845608028654 * (x + 0.044715 * (x * x * x))))


def _sgu_kernel(u_ref, v_ref, lnw_ref, lnb_ref, ws_ref, bs_ref, o_ref, *, rb):
    u = _gelu(u_ref[...])
    v = _gelu(v_ref[...])
    vc = v - jnp.mean(v, axis=-1, keepdims=True)
    vn = vc * lax.rsqrt(jnp.mean(vc * vc, axis=-1, keepdims=True) + EPS)
    vn = (vn * lnw_ref[...] + lnb_ref[...]).astype(BF16)
    gw = vn.shape[-1] // SGU_GROUPS
    for n in range(rb // SGU_CHUNK):
        rs = slice(n * SGU_CHUNK, (n + 1) * SGU_CHUNK)
        for g in range(SGU_GROUPS):
            cs = slice(g * gw, (g + 1) * gw)
            s = _dot(ws_ref[g], vn[rs, cs]) + bs_ref[g]
            o_ref[rs, cs] = u[rs, cs] * s


def _sgu(p_all, lnw, lnb, ws, bs, *, rb):
    m = p_all.shape[0]
    c = lnw.shape[-1]
    return pl.pallas_call(
        functools.partial(_sgu_kernel, rb=rb),
        grid=(m // rb,),
        in_specs=[pl.BlockSpec((rb, c), lambda i: (i, C_SGU // c)),
                  pl.BlockSpec((rb, c), lambda i: (i, C_SGU // c + 1)),
                  pl.BlockSpec((1, c), lambda i: (0, 0)),
                  pl.BlockSpec((1, c), lambda i: (0, 0)),
                  pl.BlockSpec(ws.shape, lambda i: (0, 0, 0)),
                  pl.BlockSpec(bs.shape, lambda i: (0, 0, 0))],
        out_specs=pl.BlockSpec((rb, c), lambda i: (i, 0)),
        out_shape=jax.ShapeDtypeStruct((m, c), F32),
        compiler_params=_cparams(1),
        name="sgu",
    )(p_all, p_all, lnw, lnb, ws, bs)


def _merge_kernel(ya_ref, yb_ref, yc_ref, yd_ref, g0_ref, g1_ref, g2_ref, g3_ref, w_ref, o_ref):
    ys = (ya_ref, yb_ref, yc_ref, yd_ref)
    gs = (g0_ref, g1_ref, g2_ref, g3_ref)
    acc = None
    for j in range(N_BRANCH):
        t = _sigmoid(gs[j][...]) * _dot(ys[j][...].astype(BF16), w_ref[j])
        acc = t if acc is None else acc + t
    o_ref[...] = acc.astype(BF16)


def _merge(ys, p_all, w_branch, *, tm, row_off):
    m = p_all.shape[0]
    _, c, d = w_branch.shape
    tn = 512
    mo = m - row_off * tm
    ytok = pl.BlockSpec((tm, c), lambda i, j: (i + row_off, 0))
    gate = lambda b: pl.BlockSpec((tm, tn), lambda i, j: (i + row_off, (C_GATE + b * d) // tn + j))
    return pl.pallas_call(
        _merge_kernel,
        grid=(mo // tm, d // tn),
        in_specs=[ytok, ytok, ytok, ytok, gate(0), gate(1), gate(2), gate(3),
                  pl.BlockSpec((N_BRANCH, c, tn), lambda i, j: (0, 0, j))],
        out_specs=pl.BlockSpec((tm, tn), lambda i, j: (i, j)),
        out_shape=jax.ShapeDtypeStruct((mo, d), BF16),
        compiler_params=_cparams(2),
        name="merge",
    )(*ys, p_all, p_all, p_all, p_all, w_branch)


def _outproj_kernel(m_ref, w_ref, x_ref, mod_ref, o_ref):
    o_ref[...] = x_ref[...] + mod_ref[0, 2:3, :] * _dot(m_ref[...], w_ref[...])


def _outproj(mm, w_out, x_all, mod, *, tm, row_off, nctx, bpb):
    mo, d = mm.shape
    tn = _pick((1024, 512), d)
    return pl.pallas_call(
        _outproj_kernel,
        grid=(mo // tm, d // tn),
        in_specs=[pl.BlockSpec((tm, d), lambda i, j: (i, 0)),
                  pl.BlockSpec((d, tn), lambda i, j: (0, j)),
                  pl.BlockSpec((tm, tn), lambda i, j: (i + row_off, j)),
                  pl.BlockSpec((1, 6, tn), lambda i, j: (_row_group(i + row_off, nctx, bpb), 0, j))],
        out_specs=pl.BlockSpec((tm, tn), lambda i, j: (i, j)),
        out_shape=jax.ShapeDtypeStruct((mo, d), F32),
        compiler_params=_cparams(2),
        name="outproj",
    )(mm, w_out, x_all, mod)


def _mlp_kernel(x_ref, mod_ref, g_ref, w1_ref, w2_ref, gf_ref, o_ref, h_ref, acc_ref, *, final_norm):
    j = pl.program_id(1)

    @pl.when(j == 0)
    def _():
        h = _norm_mod(x_ref[...], g_ref[...], mod_ref[0, 3:4, :], mod_ref[0, 4:5, :])
        h_ref[...] = h.astype(BF16)
        acc_ref[...] = jnp.zeros_like(acc_ref)

    a = jnp.maximum(_dot(h_ref[...], w1_ref[...]), 0.0)
    acc_ref[...] += _dot((a * a).astype(BF16), w2_ref[...])

    @pl.when(j == pl.num_programs(1) - 1)
    def _():
        y = x_ref[...] + mod_ref[0, 5:6, :] * acc_ref[...]
        if final_norm:
            y = y * lax.rsqrt(jnp.mean(y * y, axis=-1, keepdims=True) + EPS) * gf_ref[...]
        o_ref[...] = y


def _mlp(x_in, mod, g, w1, w2, g_final, *, tm, row_off, nctx, bpb, final_norm):
    mo, d = x_in.shape
    hid = w1.shape[1]
    th = _pick((512, 256, 128), hid)
    return pl.pallas_call(
        functools.partial(_mlp_kernel, final_norm=final_norm),
        grid=(mo // tm, hid // th),
        in_specs=[pl.BlockSpec((tm, d), lambda i, j: (i, 0)),
                  pl.BlockSpec((1, 6, d), lambda i, j: (_row_group(i + row_off, nctx, bpb), 0, 0)),
                  pl.BlockSpec((1, d), lambda i, j: (0, 0)),
                  pl.BlockSpec((d, th), lambda i, j: (0, j)),
                  pl.BlockSpec((th, d), lambda i, j: (j, 0)),
                  pl.BlockSpec((1, d), lambda i, j: (0, 0))],
        out_specs=pl.BlockSpec((tm, d), lambda i, j: (i, 0)),
        out_shape=jax.ShapeDtypeStruct((mo, d), F32),
        scratch_shapes=[pltpu.VMEM((tm, d), BF16), pltpu.VMEM((tm, d), F32)],
        compiler_params=_cparams(2),
        name="mlp",
    )(x_in, mod, g, w1, w2, g_final)


def _blockdiag2(w):
    _, r, c = w.shape
    z = jnp.zeros((r, c), w.dtype)
    out = jnp.concatenate([jnp.concatenate([w[0], z], axis=1), jnp.concatenate([z, w[1]], axis=1)], axis=0)
    return jnp.pad(out, ((0, 128 - 2 * r), (0, 0))).astype(BF16)


def _permute_w_in(w, d_model):
    c = d_model // N_BRANCH
    o = [0]
    for wd in (3 * c, 64 * 2, 64 * 2, 128, c, 2 * c, c, c, c, c, 32, c, 2 * c, N_BRANCH * d_model):
        o.append(o[-1] + wd)
    seg = lambda k: w[:, o[k]:o[k + 1]]
    pad = jnp.zeros((w.shape[0], 512 - 128 * 3 - 32), w.dtype)
    parts = [seg(0), seg(5), seg(12), seg(4), seg(6), seg(7), seg(8), seg(9), seg(11), seg(13),
             seg(1), seg(2), seg(3), seg(10), pad]
    out = jnp.concatenate(parts, axis=1).astype(BF16)
    assert out.shape[1] == N_COLS
    return out


def kernel(x, c, ctx, c_ctx, w_ada, b_ada, g_norm1, g_norm2, g_final, w_in, rw_conv, rw_w0, rw_w2,
           rw_a0, rw_a2, rw_g2, rw_kk, rw_ka, rw_rk, rw_ln_w, rw_ln_b, hg_gamma, hg_norm, gla_gw,
           gla_gb, gla_norm, sgu_ln_w, sgu_ln_b, sgu_w, sgu_b, w_branch, w_out, w_mlp1, w_mlp2):
    n_batch, seq, d_model = x.shape
    ctx_len = ctx.shape[1]
    depth = w_in.shape[0]
    cw = d_model // N_BRANCH
    assert cw == 512 and d_model == 2048, "column layout constants assume D_MODEL = 2048"
    m_ctx = n_batch * ctx_len

    tm = _pick((1024, 512, 256, 128), m_ctx, seq)
    tb = _pick((256, 128), ctx_len, seq)
    nctx, bpb = m_ctx // tm, seq // tm
    ctxb, latb = ctx_len // tb, seq // tb
    seqs = dict(tb=tb, n_batch=n_batch, ctxb=ctxb, latb=latb)

    x_all = jnp.concatenate([ctx.reshape(m_ctx, d_model), x.reshape(n_batch * seq, d_model)], axis=0)
    c_rows = jnp.concatenate([c_ctx[None, :], c, jnp.zeros((7 - n_batch, d_model), F32)], axis=0)
    mod_all = _ada(c_rows, w_ada, b_ada).reshape(depth, 8, 6, d_model)

    head_ones = (jnp.arange(cw)[:, None] // RW_HEAD == jnp.arange(cw)[None, :] // RW_HEAD).astype(BF16)
    row = lambda a: a.reshape(1, -1)

    for l in range(depth):
        last = l == depth - 1
        mod = mod_all[l]
        p_all = _inproj(x_all, mod, row(g_norm1[l]), _permute_w_in(w_in[l], d_model),
                        tm=tm, nctx=nctx, bpb=bpb)

        prm = dict(conv=rw_conv[l], w0=row(rw_w0[l]), w2=_blockdiag2(rw_w2[l]), a0=row(rw_a0[l]),
                   a2=_blockdiag2(rw_a2[l]), g2=rw_g2[l].astype(BF16), kk=row(rw_kk[l]),
                   ka=row(rw_ka[l]), rk=row(rw_rk[l]), e=head_ones)
        r, v, nkk, g, bonus, w, kka, km = _rw_prep(p_all, prm, tb=tb, nctx=m_ctx // tb,
                                                    ctx_bps=ctxb, lat_bps=latb)
        ys = [_rw_scan(r, v, nkk, w, kka, km, d=d, **seqs) for d in range(2)]
        ya = _rw_post(ys[0], ys[1], g, bonus, row(rw_ln_w[l]), row(rw_ln_b[l]), head_ones, tb=tb)

        ob = [_hgrn2(p_all, hg_gamma, layer=l, d=d, **seqs) for d in range(2)]
        yb = _la_post(ob[0], ob[1], p_all, C_HGG, row(hg_norm[l]), tb=tb, nh=cw // HG_DK)

        gw = _blockdiag2(gla_gw[l])
        gb = row(gla_gb[l])
        oc = [_gla(p_all, gw, gb, d=d, **seqs) for d in range(2)]
        yc = _la_post(oc[0], oc[1], p_all, C_GLG, row(gla_norm[l]), tb=tb, nh=GLA_HEADS)

        bs = jnp.broadcast_to(sgu_b[l][:, :, None], sgu_w[l].shape)
        yd = _sgu(p_all, row(sgu_ln_w[l]), row(sgu_ln_b[l]), sgu_w[l].astype(BF16), bs,
                  rb=_pick((512, 256, 128), m_ctx, seq))

        row_off = nctx if last else 0
        mm = _merge((ya, yb, yc, yd), p_all, w_branch[l].astype(BF16), tm=tm // 2, row_off=2 * row_off)
        x_mid = _outproj(mm, w_out[l].astype(BF16), x_all, mod, tm=tm, row_off=row_off, nctx=nctx, bpb=bpb)
        x_all = _mlp(x_mid, mod, row(g_norm2[l]), w_mlp1[l].astype(BF16), w_mlp2[l].astype(BF16),
                     row(g_final), tm=tm // 2, row_off=2 * row_off, nctx=2 * nctx, bpb=2 * bpb,
                     final_norm=last)
    return x_all.reshape(n_batch, seq, d_model)
```

```python
import functools

import jax
import jax.numpy as jnp
from jax import lax
from jax.experimental import pallas as pl
from jax.experimental.pallas import tpu as pltpu

F32 = jnp.float32
BF16 = jnp.bfloat16

N_BRANCH = 4
RW_HEAD = 64
RW_LN_EPS = 64e-5
HG_DK = 128
GLA_HEADS = 4
GLA_GATE_NORM = 16.0
LA_CHUNK = 16
SGU_CHUNK = 128
SGU_GROUPS = 4
EPS = 1e-6

C_RKV = 0
C_HGF = 1536
C_SGU = 2560
C_HGQ = 3584
C_HGI = 4096
C_HGG = 4608
C_GLQK = 5120
C_GLV = 5632
C_GLG = 6144
C_GATE = 6656
C_SMALL = 14848
N_COLS = 15360

VMEM_LIMIT = 56 * 1024 * 1024


def _cparams(n_axes):
    return pltpu.CompilerParams(dimension_semantics=("arbitrary",) * n_axes,
                                vmem_limit_bytes=VMEM_LIMIT)


def _pick(n_list, *dims):
    for n in n_list:
        if all(d % n == 0 for d in dims):
            return n
    raise ValueError(f"no block size in {n_list} divides {dims}")


def _row_group(i, nctx, bpb):
    return jnp.where(i < nctx, 0, 1 + (i - nctx) // bpb)


def _dot(a, b):
    return jnp.dot(a, b, preferred_element_type=F32)


def _dot_nt(a, b):
    return lax.dot_general(a, b, (((1,), (1,)), ((), ())), preferred_element_type=F32)


def _dot_tn(a, b):
    return lax.dot_general(a, b, (((0,), (0,)), ((), ())), preferred_element_type=F32)


def _split_dot(x, e):
    hi = x.astype(BF16)
    lo = (x - hi.astype(F32)).astype(BF16)
    return _dot(hi, e) + _dot(lo, e)


def _split3_dot(e, x):
    p1 = x.astype(BF16)
    r1 = x - p1.astype(F32)
    p2 = r1.astype(BF16)
    p3 = (r1 - p2.astype(F32)).astype(BF16)
    return _dot(e, p1) + _dot(e, p2) + _dot(e, p3)


def _log_sigmoid(x):
    return jnp.minimum(x, 0.0) - jnp.log(1.0 + jnp.exp(-jnp.abs(x)))


def _sigmoid(x):
    return 1.0 / (1.0 + jnp.exp(-x))


def _ada_kernel(c_ref, w_ref, b_ref, o_ref):
    c = c_ref[...]
    act = c * _sigmoid(c)
    o_ref[0] = _dot(act.astype(BF16), w_ref[0].astype(BF16)) + b_ref[0]


def _ada(c_rows, w_ada, b_ada):
    n_layers, d, n = w_ada.shape
    tn = _pick((1024, 512, 256, 128), n)
    return pl.pallas_call(
        _ada_kernel,
        grid=(n_layers, n // tn),
        in_specs=[pl.BlockSpec((8, d), lambda l, j: (0, 0)),
                  pl.BlockSpec((1, d, tn), lambda l, j: (l, 0, j)),
                  pl.BlockSpec((1, 1, tn), lambda l, j: (l, 0, j))],
        out_specs=pl.BlockSpec((1, 8, tn), lambda l, j: (l, 0, j)),
        out_shape=jax.ShapeDtypeStruct((n_layers, 8, n), F32),
        compiler_params=_cparams(2),
        name="ada_mod",
    )(c_rows, w_ada, b_ada.reshape(n_layers, 1, n))


def _norm_mod(x, g, shift, scale):
    y = x * lax.rsqrt(jnp.mean(x * x, axis=-1, keepdims=True) + EPS) * g
    return y * (1.0 + scale) + shift


def _inproj_kernel(x_ref, mod_ref, g_ref, w_ref, o_ref, h_ref):
    @pl.when(pl.program_id(1) == 0)
    def _():
        h = _norm_mod(x_ref[...], g_ref[...], mod_ref[0, 0:1, :], mod_ref[0, 1:2, :])
        h_ref[...] = h.astype(BF16)

    o_ref[...] = _dot(h_ref[...], w_ref[...])


def _inproj(x_all, mod, g, w, *, tm, nctx, bpb):
    m, d = x_all.shape
    n = w.shape[1]
    tn = _pick((1024, 512), n)
    return pl.pallas_call(
        _inproj_kernel,
        grid=(m // tm, n // tn),
        in_specs=[pl.BlockSpec((tm, d), lambda i, j: (i, 0)),
                  pl.BlockSpec((1, 6, d), lambda i, j: (_row_group(i, nctx, bpb), 0, 0)),
                  pl.BlockSpec((1, d), lambda i, j: (0, 0)),
                  pl.BlockSpec((d, tn), lambda i, j: (0, j))],
        out_specs=pl.BlockSpec((tm, tn), lambda i, j: (i, j)),
        out_shape=jax.ShapeDtypeStruct((m, n), F32),
        scratch_shapes=[pltpu.VMEM((tm, d), BF16)],
        compiler_params=_cparams(2),
        name="inproj",
    )(x_all, mod, g, w)


def _rw_prep_kernel(rkv_ref, prev_ref, next_ref, small_ref, conv_ref, w0_ref, w2_ref, a0_ref,
                    a2_ref, g2_ref, kk_ref, ka_ref, rk_ref, e_ref,
                    r_out, v_out, nkk_out, g_out, bonus_out, w_out, kka_out, km_out,
                    *, tb, nctx, ctx_bps, lat_bps):
    i = pl.program_id(0)
    c = r_out.shape[-1]
    j = jnp.where(i < nctx, i, i - nctx)
    bps = jnp.where(i < nctx, ctx_bps, lat_bps)
    first = lax.rem(j, bps) == 0
    last = lax.rem(j, bps) == bps - 1

    blk = rkv_ref[...]
    rows = lax.broadcasted_iota(jnp.int32, (tb, 1), 0)
    prev_row = jnp.where(first, 0.0, prev_ref[7:8, :])
    next_row = jnp.where(last, 0.0, next_ref[0:1, :])
    xm1 = jnp.where(rows == 0, prev_row, pltpu.roll(blk, 1, 0))
    xp1 = jnp.where(rows == tb - 1, next_row, pltpu.roll(blk, tb - 1, 0))
    conv = conv_ref[0:1, :] * xm1 + conv_ref[1:2, :] * blk + conv_ref[2:3, :] * xp1
    r = conv[:, 0:c]
    k = conv[:, c:2 * c]
    v = conv[:, 2 * c:3 * c]

    small = small_ref[...]
    wl = small[:, 0:128]
    al = small[:, 128:256]
    gl = small[:, 256:384]
    w_pre = w0_ref[...] + _dot(jnp.tanh(wl).astype(BF16), w2_ref[...])
    softplus = jnp.maximum(-w_pre, 0.0) + jnp.log(1.0 + jnp.exp(-jnp.abs(w_pre)))
    log_decay = -jnp.exp(-softplus - 0.5)
    a = _sigmoid(a0_ref[...] + _dot(al.astype(BF16), a2_ref[...]))
    g = _dot(_sigmoid(gl).astype(BF16), g2_ref[...])

    e = e_ref[...]
    kkv = k * kk_ref[...]
    kk = kkv * lax.rsqrt(_split_dot(kkv * kkv, e) + 1e-12)
    bonus = _split_dot(r * k * rk_ref[...], e) * v

    r_out[...] = r
    v_out[...] = v
    nkk_out[...] = -kk
    g_out[...] = g
    bonus_out[...] = bonus
    for d in range(2):
        a_d = a[:, d * c:(d + 1) * c]
        w_out[d] = log_decay[:, d * c:(d + 1) * c]
        kka_out[d] = kk * a_d
        km_out[d] = k * (1.0 + (a_d - 1.0) * ka_ref[...])


def _rw_prep(p_all, prm, *, tb, nctx, ctx_bps, lat_bps):
    m = p_all.shape[0]
    c = prm["kk"].shape[-1]
    nblk = m // tb
    t8 = tb // 8
    full = lambda shape: pl.BlockSpec(shape, lambda i: (0,) * len(shape))
    tok = pl.BlockSpec((tb, c), lambda i: (i, 0))
    tok2 = pl.BlockSpec((2, tb, c), lambda i: (0, i, 0))
    one = jax.ShapeDtypeStruct((m, c), F32)
    two = jax.ShapeDtypeStruct((2, m, c), F32)
    return pl.pallas_call(
        functools.partial(_rw_prep_kernel, tb=tb, nctx=nctx, ctx_bps=ctx_bps, lat_bps=lat_bps),
        grid=(nblk,),
        in_specs=[pl.BlockSpec((tb, 3 * c), lambda i: (i, C_RKV // (3 * c))),
                  pl.BlockSpec((8, 3 * c), lambda i: (jnp.maximum(i * t8 - 1, 0), 0)),
                  pl.BlockSpec((8, 3 * c), lambda i: (jnp.minimum((i + 1) * t8, m // 8 - 1), 0)),
                  pl.BlockSpec((tb, 512), lambda i: (i, C_SMALL // 512)),
                  full((3, 3 * c)), full((1, 2 * c)), full((128, 2 * c)), full((1, 2 * c)),
                  full((128, 2 * c)), full((128, c)), full((1, c)), full((1, c)), full((1, c)),
                  full((c, c))],
        out_specs=[tok, tok, tok, tok, tok, tok2, tok2, tok2],
        out_shape=[one, one, one, one, one, two, two, two],
        compiler_params=_cparams(1),
        name="rwkv_prep",
    )(p_all, p_all, p_all, p_all, prm["conv"], prm["w0"], prm["w2"], prm["a0"], prm["a2"],
      prm["g2"], prm["kk"], prm["ka"], prm["rk"], prm["e"])


class _RwUnit:
    def __init__(self, refs, scratch, *, tb, npair, reverse):
        (self.r_ref, self.v_ref, self.a_ref, self.lw_ref, self.b_ref, self.k_ref, self.tri_ref,
         self.y_ref) = refs
        (self.s_ref, self.c_scr, self.ag_scr, self.rg_scr, self.bg_scr, self.kg_scr) = scratch[:6]
        self.slots = (scratch[6:9], scratch[9:12])
        self.reverse = reverse
        self.nch = tb // LA_CHUNK
        self.pairs = range(npair)
        L, hd = LA_CHUNK, RW_HEAD
        self.lss = [slice(p * 2 * hd, (p + 1) * 2 * hd) for p in self.pairs]
        lane = lax.broadcasted_iota(jnp.int32, (1, 2 * hd), 1)
        self.lo = (lane < hd).astype(F32)
        self.hi = 1.0 - self.lo
        ti = lax.broadcasted_iota(jnp.int32, (2 * L, 2 * hd), 0)
        sl = lax.broadcasted_iota(jnp.int32, (2 * L, 2 * hd), 1)
        sj = sl & (L - 1)
        tt = ti & (L - 1)
        earlier = (sj > tt) if reverse else (sj < tt)
        self.aa_mask = ((earlier | ((ti >= L) & (sj == tt))) & (sl < 4 * L)).astype(F32)
        self.col_idx0 = jnp.where(lax.broadcasted_iota(jnp.int32, (L, 2 * hd), 1) < hd, 0, L)
        self.blk_mask = ((lax.broadcasted_iota(jnp.int32, (2 * hd, 2 * hd), 0) >> 6)
                         == (lax.broadcasted_iota(jnp.int32, (2 * hd, 2 * hd), 1) >> 6)).astype(F32)
        self.zeros2l = jnp.zeros((2 * L, 2 * hd), F32)
        self.zeros4l = jnp.zeros((4 * L, 2 * hd), F32)
        self.last = 0 if reverse else L - 1
        self.order = range(L - 1, -1, -1) if reverse else range(L)

    def reset(self):
        self.s_ref[...] = jnp.zeros_like(self.s_ref)

    def prologue(self):
        lw = self.lw_ref[0]
        c = _split3_dot(self.tri_ref[...], lw)
        self.c_scr[...] = c
        enc = jnp.exp(-c)
        self.ag_scr[...] = self.a_ref[...] * jnp.exp(c - lw)
        self.rg_scr[...] = self.r_ref[...] * jnp.exp(c)
        self.bg_scr[...] = self.b_ref[0] * enc
        self.kg_scr[...] = self.k_ref[0] * enc

    def rows(self, ci):
        L = LA_CHUNK
        cc = (self.nch - 1 - ci) if self.reverse else ci
        return pl.ds(pl.multiple_of(cc * L, L), L)

    def halves(self, x):
        return [x * self.lo, x * self.hi]

    def lhs_of(self, rows):
        return [jnp.concatenate([self.ag_scr[rows, ls], self.rg_scr[rows, ls]], axis=0).astype(BF16)
                for ls in self.lss]

    def prep_aa(self, rows):
        lhs = self.lhs_of(rows)
        out = []
        for p in self.pairs:
            ls = self.lss[p]
            rhs = jnp.concatenate(self.halves(self.bg_scr[rows, ls]) + self.halves(self.kg_scr[rows, ls])
                                  + [self.zeros4l], axis=0)
            out.append(_dot_nt(lhs[p], rhs.astype(BF16)))
        return out

    def prep_akv(self, rows, aa_raw, slot):
        _, akv_scr, aar_scr = slot
        L = LA_CHUNK
        aa = [x * self.aa_mask for x in aa_raw]
        for p in self.pairs:
            v = self.v_ref[rows, self.lss[p]]
            vv = jnp.concatenate([self.zeros2l] + self.halves(v) + [self.zeros4l], axis=0).astype(BF16)
            akv_scr[p] = _dot(aa[p][0:L].astype(BF16), vv)
            aar_scr[p] = aa[p][L:2 * L]
        return aa

    def prep_cols(self, aa, slot):
        L = LA_CHUNK
        for p in self.pairs:
            for s in range(L):
                slot[0][p, s] = jnp.take_along_axis(aa[p][0:L], self.col_idx0 + s, axis=1)

    def adv_g(self, rows):
        lhs = self.lhs_of(rows)
        return [_dot_nt(lhs[p], self.s_ref[p].astype(BF16)) for p in self.pairs]

    def adv_solve(self, g, slot):
        L = LA_CHUNK
        col_scr, akv_scr, _ = slot
        u = [g[p][0:L] + akv_scr[p] for p in self.pairs]
        for s in self.order:
            for p in self.pairs:
                u[p] = u[p] + col_scr[p, s] * u[p][s:s + 1]
        return u

    def adv_out(self, rows, g, u, slot):
        L = LA_CHUNK
        for p in self.pairs:
            v = self.v_ref[rows, self.lss[p]]
            uv = jnp.concatenate(self.halves(u[p]) + self.halves(v) + [self.zeros4l], axis=0).astype(BF16)
            self.y_ref[rows, self.lss[p]] = g[p][L:2 * L] + _dot(slot[2][p].astype(BF16), uv)
        for p in self.pairs:
            ls = self.lss[p]
            cch = self.c_scr[rows, ls]
            cl = cch[self.last:self.last + 1]
            dec = jnp.exp(cl - cch)
            bk = jnp.concatenate([self.b_ref[0, rows, ls] * dec, self.k_ref[0, rows, ls] * dec], axis=0)
            upd = _dot_tn(jnp.concatenate([u[p], self.v_ref[rows, ls]], axis=0).astype(BF16),
                          bk.astype(BF16))
            self.s_ref[p] = self.s_ref[p] * jnp.exp(cl) + upd * self.blk_mask


N_RW_REFS = 8
N_RW_SCRATCH = 12


def _rw_scan_kernel(*refs, tb, npair):
    n_in = N_RW_REFS - 1
    units = []
    for d in range(2):
        ins = refs[d * n_in:(d + 1) * n_in]
        out = refs[2 * n_in + d]
        scr = refs[2 * n_in + 2 + d * N_RW_SCRATCH:2 * n_in + 2 + (d + 1) * N_RW_SCRATCH]
        units.append(_RwUnit(tuple(ins) + (out,), scr, tb=tb, npair=npair, reverse=d == 1))
    nch = tb // LA_CHUNK

    @pl.when(pl.program_id(1) == 0)
    def _():
        for un in units:
            un.reset()

    for un in units:
        un.prologue()

    def step(ci_adv, sa, ci_prep, sp):
        rows_a = [un.rows(ci_adv) for un in units]
        rows_p = [un.rows(ci_prep) for un in units]
        aa_raw = [un.prep_aa(rp) for un, rp in zip(units, rows_p)]
        g = [un.adv_g(ra) for un, ra in zip(units, rows_a)]
        aa = [un.prep_akv(rp, x, un.slots[sp]) for un, rp, x in zip(units, rows_p, aa_raw)]
        u = [un.adv_solve(x, un.slots[sa]) for un, x in zip(units, g)]
        for un, x in zip(units, aa):
            un.prep_cols(x, un.slots[sp])
        for un, ra, x, y in zip(units, rows_a, g, u):
            un.adv_out(ra, x, y, un.slots[sa])

    for un in units:
        r0 = un.rows(0)
        un.prep_cols(un.prep_akv(r0, un.prep_aa(r0), un.slots[0]), un.slots[0])

    def two_chunks(j, carry):
        c0 = 2 * j
        step(c0, 0, c0 + 1, 1)
        step(c0 + 1, 1, jnp.minimum(c0 + 2, nch - 1), 0)
        return carry

    lax.fori_loop(0, nch // 2, two_chunks, 0)


def _seq_block(d, b, i, *, n_batch, ctxb, latb):
    is_ctx = i < ctxb
    cs = jnp.where(d == 0, i, ctxb - 1 - i)
    lj = jnp.where(d == 0, i - ctxb, latb - 1 - (i - ctxb))
    return jnp.where(is_ctx, b * ctxb + cs, n_batch * ctxb + b * latb + lj)


def _rw_scan(r, v, nkk, lw, kka, km, *, tb, n_batch, ctxb, latb):
    m, c = r.shape
    npair = c // (2 * RW_HEAD)
    lanes = 2 * RW_HEAD
    in_specs, args = [], []
    for d in range(2):
        blk = _la_index(d, n_batch, ctxb, latb)
        tok = pl.BlockSpec((tb, c), lambda b, i, blk=blk: (blk(b, i), 0))
        tokd = pl.BlockSpec((1, tb, c), lambda b, i, blk=blk, d=d: (d, blk(b, i), 0))
        in_specs += [tok, tok, tok, tokd, tokd, tokd, pl.BlockSpec((tb, tb), lambda b, i: (0, 0))]
        args += [r, v, nkk, lw, kka, km, _chunk_tri(tb, d == 1)]
    out_specs = [pl.BlockSpec((tb, c), lambda b, i, blk=_la_index(d, n_batch, ctxb, latb): (blk(b, i), 0))
                 for d in range(2)]
    buf = pltpu.VMEM((tb, c), F32)
    unit_scratch = [pltpu.VMEM((npair, lanes, lanes), F32), buf, buf, buf, buf, buf] + 2 * [
        pltpu.VMEM((npair, LA_CHUNK, LA_CHUNK, lanes), F32),
        pltpu.VMEM((npair, LA_CHUNK, lanes), F32),
        pltpu.VMEM((npair, LA_CHUNK, lanes), F32)]
    assert len(unit_scratch) == N_RW_SCRATCH
    return pl.pallas_call(
        functools.partial(_rw_scan_kernel, tb=tb, npair=npair),
        grid=(n_batch, ctxb + latb),
        in_specs=in_specs,
        out_specs=out_specs,
        out_shape=[jax.ShapeDtypeStruct((m, c), F32)] * 2,
        scratch_shapes=unit_scratch + unit_scratch,
        compiler_params=_cparams(2),
        name="rwkv_scan",
    )(*args)


def _rw_post_kernel(yf_ref, yb_ref, g_ref, bonus_ref, lnw_ref, lnb_ref, e_ref, o_ref):
    y = yf_ref[...] + yb_ref[...]
    e = e_ref[...]
    inv = 1.0 / RW_HEAD
    yc = y - _split_dot(y, e) * inv
    var = _split_dot(yc * yc, e) * inv
    yn = yc * lax.rsqrt(var + RW_LN_EPS)
    o_ref[...] = (yn * lnw_ref[...] + lnb_ref[...] + bonus_ref[...]) * g_ref[...]


def _rw_post(y_f, y_b, g, bonus, lnw, lnb, e, *, tb):
    m, c = y_f.shape
    tok = pl.BlockSpec((tb, c), lambda i: (i, 0))
    vec = pl.BlockSpec((1, c), lambda i: (0, 0))
    return pl.pallas_call(
        _rw_post_kernel,
        grid=(m // tb,),
        in_specs=[tok, tok, tok, tok, vec, vec, pl.BlockSpec((c, c), lambda i: (0, 0))],
        out_specs=tok,
        out_shape=jax.ShapeDtypeStruct((m, c), F32),
        compiler_params=_cparams(1),
        name="rwkv_post",
    )(y_f, y_b, g, bonus, lnw, lnb, e)


class _LaUnit:
    def __init__(self, q_ref, k_ref, v_ref, b_ref, o_ref, s_ref, *, tb, nh, dk, dv, reverse, q_scale):
        self.q_ref, self.k_ref, self.v_ref, self.b_ref, self.o_ref, self.s_ref = (
            q_ref, k_ref, v_ref, b_ref, o_ref, s_ref)
        self.nch = tb // LA_CHUNK
        self.dk, self.dv, self.reverse, self.q_scale = dk, dv, reverse, q_scale
        self.lanes = 128
        self.pack = self.lanes // dk
        self.groups = range(nh // self.pack)
        self.rowi = lax.broadcasted_iota(jnp.int32, (LA_CHUNK, 1), 0)
        lane = lax.broadcasted_iota(jnp.int32, (1, self.lanes), 1)
        self.head_lanes = [(lane // dk == j).astype(F32) for j in range(self.pack)]
        if self.pack > 1:
            rows = lax.broadcasted_iota(jnp.int32, (self.pack * dv, self.lanes), 0)
            cols = lax.broadcasted_iota(jnp.int32, (self.pack * dv, self.lanes), 1)
            self.blk_mask = ((rows // dv) == (cols // dk)).astype(F32)

    def reset(self):
        self.s_ref[...] = jnp.zeros_like(self.s_ref)

    def load(self, ci):
        cc = (self.nch - 1 - ci) if self.reverse else ci
        rows = pl.ds(pl.multiple_of(cc * LA_CHUNK, LA_CHUNK), LA_CHUNK)
        pack, dv, lanes = self.pack, self.dv, self.lanes
        out = []
        for g in self.groups:
            ks = slice(g * lanes, (g + 1) * lanes)
            q = self.q_ref[rows, ks] * self.q_scale
            k = self.k_ref[rows, ks]
            v = self.v_ref[rows, g * pack * dv:(g + 1) * pack * dv]
            b = self.b_ref[rows, ks]
            o_inter = _dot_nt((q * jnp.exp(b)).astype(BF16), self.s_ref[g].astype(BF16))
            out.append((rows, q, k, v, b, [o_inter[:, j * dv:(j + 1) * dv] for j in range(pack)]))
        return out

    def intra(self, ops):
        pack, dv = self.pack, self.dv
        for g in self.groups:
            rows, q, k, v, b, o = ops[g]
            for s in range(LA_CHUNK):
                valid = (self.rowi <= s) if self.reverse else (self.rowi >= s)
                term = (q * k[s:s + 1]) * jnp.exp(b - b[s:s + 1])
                for j in range(pack):
                    tj = term if pack == 1 else term * self.head_lanes[j]
                    col = jnp.where(valid, jnp.sum(tj, axis=-1, keepdims=True), 0.0)
                    o[j] = o[j] + col * v[s:s + 1, j * dv:(j + 1) * dv]
            for j in range(pack):
                self.o_ref[rows, (g * pack + j) * dv:(g * pack + j + 1) * dv] = o[j]

    def update(self, ops):
        for g in self.groups:
            _, _, k, v, b, _ = ops[g]
            b_last = b[0:1] if self.reverse else b[LA_CHUNK - 1:LA_CHUNK]
            upd = _dot_tn(v.astype(BF16), (k * jnp.exp(b_last - b)).astype(BF16))
            if self.pack > 1:
                upd = upd * self.blk_mask
            self.s_ref[g] = self.s_ref[g] * jnp.exp(b_last) + upd


def _la_run(units, nch):
    def chunk(ci, carry):
        ops = [un.load(ci) for un in units]
        for un, x in zip(units, ops):
            un.update(x)
        for un, x in zip(units, ops):
            un.intra(x)
        return carry

    lax.fori_loop(0, nch, chunk, 0)


def _hg_kernel(qf_ref, ff_ref, vf_ref, trif_ref, qb_ref, fb_ref, vb_ref, trib_ref, gamma_ref,
               of_ref, ob_ref, sf_ref, kf_scr, bf_scr, sb_ref, kb_scr, bb_scr, *, layer, tb, nh, dk, dv):
    gam = gamma_ref[...]
    ex = jnp.exp(gam - jnp.max(gam, axis=0, keepdims=True))
    p = ex / jnp.sum(ex, axis=0, keepdims=True)
    cum = p[0:1]
    for i in range(1, layer + 1):
        cum = cum + p[i:i + 1]
    lb = cum - p[0:1]
    lo = jnp.log(lb)
    l1 = jnp.log(1.0 - lb)

    units = []
    for d, (q_ref, f_ref, v_ref, tri_ref, o_ref, s_ref, k_scr, b_scr) in enumerate((
            (qf_ref, ff_ref, vf_ref, trif_ref, of_ref, sf_ref, kf_scr, bf_scr),
            (qb_ref, fb_ref, vb_ref, trib_ref, ob_ref, sb_ref, kb_scr, bb_scr))):
        hi = l1 + _log_sigmoid(f_ref[...])
        mx = jnp.maximum(lo, hi)
        mn = jnp.minimum(lo, hi)
        log_f = mx + jnp.log(1.0 + jnp.exp(mn - mx))
        k_scr[...] = 1.0 - jnp.exp(log_f)
        b_scr[...] = _split3_dot(tri_ref[...], log_f)
        units.append(_LaUnit(q_ref, k_scr, v_ref, b_scr, o_ref, s_ref, tb=tb, nh=nh, dk=dk, dv=dv,
                             reverse=d == 1, q_scale=1.0))

    @pl.when(pl.program_id(1) == 0)
    def _():
        for un in units:
            un.reset()

    _la_run(units, tb // LA_CHUNK)


def _gla_kernel(qkf_ref, vf_ref, smallf_ref, trif_ref, qkb_ref, vb_ref, smallb_ref, trib_ref,
                gw_ref, gb_ref, of_ref, ob_ref, sf_ref, bf_scr, sb_ref, bb_scr, *, tb, nh, dk, dv):
    hk = nh * dk
    units = []
    for d, (qk_ref, v_ref, small_ref, tri_ref, o_ref, s_ref, b_scr) in enumerate((
            (qkf_ref, vf_ref, smallf_ref, trif_ref, of_ref, sf_ref, bf_scr),
            (qkb_ref, vb_ref, smallb_ref, trib_ref, ob_ref, sb_ref, bb_scr))):
        code = small_ref[:, 384:512]
        pre = _dot(code.astype(BF16), gw_ref[:, d * hk:(d + 1) * hk]) + gb_ref[:, d * hk:(d + 1) * hk]
        log_g = _log_sigmoid(pre) * (1.0 / GLA_GATE_NORM)
        b_scr[...] = _split3_dot(tri_ref[...], log_g)
        units.append(_LaUnit(qk_ref.at[:, 0:hk], qk_ref.at[:, hk:2 * hk], v_ref, b_scr, o_ref, s_ref,
                             tb=tb, nh=nh, dk=dk, dv=dv, reverse=d == 1, q_scale=dk ** -0.5))

    @pl.when(pl.program_id(1) == 0)
    def _():
        for un in units:
            un.reset()

    _la_run(units, tb // LA_CHUNK)


def _chunk_tri(tb, reverse):
    t = jnp.arange(tb)
    same = (t[:, None] // LA_CHUNK) == (t[None, :] // LA_CHUNK)
    tri = (t[None, :] >= t[:, None]) if reverse else (t[None, :] <= t[:, None])
    return (same & tri).astype(BF16)


def _la_index(d, n_batch, ctxb, latb):
    return lambda b, i: _seq_block(d, b, i, n_batch=n_batch, ctxb=ctxb, latb=latb)


def _hgrn2(p_all, gamma, *, layer, tb, n_batch, ctxb, latb):
    m = p_all.shape[0]
    c = gamma.shape[-1]
    nh = c // HG_DK
    in_specs, args, out_specs = [], [], []
    for d in range(2):
        blk = _la_index(d, n_batch, ctxb, latb)
        col = lambda off, blk=blk: pl.BlockSpec((tb, c), lambda b, i: (blk(b, i), off // c))
        in_specs += [col(C_HGQ), col(C_HGF + d * c), col(C_HGI), pl.BlockSpec((tb, tb), lambda b, i: (0, 0))]
        args += [p_all, p_all, p_all, _chunk_tri(tb, d == 1)]
        out_specs.append(col(0))
    unit_scratch = [pltpu.VMEM((nh, HG_DK, HG_DK), F32), pltpu.VMEM((tb, c), F32), pltpu.VMEM((tb, c), F32)]
    return pl.pallas_call(
        functools.partial(_hg_kernel, layer=layer, tb=tb, nh=nh, dk=HG_DK, dv=HG_DK),
        grid=(n_batch, ctxb + latb),
        in_specs=in_specs + [pl.BlockSpec(gamma.shape, lambda b, i: (0, 0))],
        out_specs=out_specs,
        out_shape=[jax.ShapeDtypeStruct((m, c), F32)] * 2,
        scratch_shapes=unit_scratch + unit_scratch,
        compiler_params=_cparams(2),
        name="hgrn2_scan",
    )(*args, gamma)


def _gla(p_all, gw, gb, *, tb, n_batch, ctxb, latb):
    m = p_all.shape[0]
    hk = gw.shape[-1] // 2
    dk = hk // GLA_HEADS
    c = 2 * hk
    dv = c // GLA_HEADS
    in_specs, args, out_specs = [], [], []
    for d in range(2):
        blk = _la_index(d, n_batch, ctxb, latb)
        col = lambda off, blk=blk: pl.BlockSpec((tb, c), lambda b, i: (blk(b, i), off // c))
        in_specs += [col(C_GLQK), col(C_GLV), col(C_SMALL), pl.BlockSpec((tb, tb), lambda b, i: (0, 0))]
        args += [p_all, p_all, p_all, _chunk_tri(tb, d == 1)]
        out_specs.append(col(0))
    unit_scratch = [pltpu.VMEM((hk // 128, (128 // dk) * dv, 128), F32), pltpu.VMEM((tb, hk), F32)]
    return pl.pallas_call(
        functools.partial(_gla_kernel, tb=tb, nh=GLA_HEADS, dk=dk, dv=dv),
        grid=(n_batch, ctxb + latb),
        in_specs=in_specs + [pl.BlockSpec(gw.shape, lambda b, i: (0, 0)),
                             pl.BlockSpec(gb.shape, lambda b, i: (0, 0))],
        out_specs=out_specs,
        out_shape=[jax.ShapeDtypeStruct((m, c), F32)] * 2,
        scratch_shapes=unit_scratch + unit_scratch,
        compiler_params=_cparams(2),
        name="gla_scan",
    )(*args, gw, gb)


def _la_post_kernel(of_ref, ob_ref, gate_ref, g_ref, y_ref, *, nh, dv):
    o = of_ref[...] + ob_ref[...]
    gate = gate_ref[...]
    outs = []
    for h in range(nh):
        oh = o[:, h * dv:(h + 1) * dv]
        outs.append(oh * lax.rsqrt(jnp.mean(oh * oh, axis=-1, keepdims=True) + EPS))
    y = jnp.concatenate(outs, axis=-1) * g_ref[...]
    y_ref[...] = y * (gate * _sigmoid(gate))


def _la_post(o_f, o_b, p_all, gate_off, norm_g, *, tb, nh):
    m, c = o_f.shape
    tok = pl.BlockSpec((tb, c), lambda i: (i, 0))
    return pl.pallas_call(
        functools.partial(_la_post_kernel, nh=nh, dv=c // nh),
        grid=(m // tb,),
        in_specs=[tok, tok, pl.BlockSpec((tb, c), lambda i: (i, gate_off // c)),
                  pl.BlockSpec((1, c), lambda i: (0, 0))],
        out_specs=tok,
        out_shape=jax.ShapeDtypeStruct((m, c), F32),
        compiler_params=_cparams(1),
        name="la_post",
    )(o_f, o_b, p_all, norm_g)


def _gelu(x):
    return 0.5 * x * (1.0 + jnp.tanh(0.7978845608028654 * (x + 0.044715 * (x * x * x))))


def _sgu_kernel(u_ref, v_ref, lnw_ref, lnb_ref, ws_ref, bs_ref, o_ref, *, rb):
    u = _gelu(u_ref[...])
    v = _gelu(v_ref[...])
    vc = v - jnp.mean(v, axis=-1, keepdims=True)
    vn = vc * lax.rsqrt(jnp.mean(vc * vc, axis=-1, keepdims=True) + EPS)
    vn = (vn * lnw_ref[...] + lnb_ref[...]).astype(BF16)
    gw = vn.shape[-1] // SGU_GROUPS
    for n in range(rb // SGU_CHUNK):
        rs = slice(n * SGU_CHUNK, (n + 1) * SGU_CHUNK)
        for g in range(SGU_GROUPS):
            cs = slice(g * gw, (g + 1) * gw)
            s = _dot(ws_ref[g], vn[rs, cs]) + bs_ref[g]
            o_ref[rs, cs] = u[rs, cs] * s


def _sgu(p_all, lnw, lnb, ws, bs, *, rb):
    m = p_all.shape[0]
    c = lnw.shape[-1]
    return pl.pallas_call(
        functools.partial(_sgu_kernel, rb=rb),
        grid=(m // rb,),
        in_specs=[pl.BlockSpec((rb, c), lambda i: (i, C_SGU // c)),
                  pl.BlockSpec((rb, c), lambda i: (i, C_SGU // c + 1)),
                  pl.BlockSpec((1, c), lambda i: (0, 0)),
                  pl.BlockSpec((1, c), lambda i: (0, 0)),
                  pl.BlockSpec(ws.shape, lambda i: (0, 0, 0)),
                  pl.BlockSpec(bs.shape, lambda i: (0, 0, 0))],
        out_specs=pl.BlockSpec((rb, c), lambda i: (i, 0)),
        out_shape=jax.ShapeDtypeStruct((m, c), F32),
        compiler_params=_cparams(1),
        name="sgu",
    )(p_all, p_all, lnw, lnb, ws, bs)


def _merge_kernel(ya_ref, yb_ref, yc_ref, yd_ref, g0_ref, g1_ref, g2_ref, g3_ref, w_ref, o_ref):
    ys = (ya_ref, yb_ref, yc_ref, yd_ref)
    gs = (g0_ref, g1_ref, g2_ref, g3_ref)
    acc = None
    for j in range(N_BRANCH):
        t = _sigmoid(gs[j][...]) * _dot(ys[j][...].astype(BF16), w_ref[j])
        acc = t if acc is None else acc + t
    o_ref[...] = acc.astype(BF16)


def _merge(ys, p_all, w_branch, *, tm, row_off):
    m = p_all.shape[0]
    _, c, d = w_branch.shape
    tn = 512
    mo = m - row_off * tm
    ytok = pl.BlockSpec((tm, c), lambda i, j: (i + row_off, 0))
    gate = lambda b: pl.BlockSpec((tm, tn), lambda i, j: (i + row_off, (C_GATE + b * d) // tn + j))
    return pl.pallas_call(
        _merge_kernel,
        grid=(mo // tm, d // tn),
        in_specs=[ytok, ytok, ytok, ytok, gate(0), gate(1), gate(2), gate(3),
                  pl.BlockSpec((N_BRANCH, c, tn), lambda i, j: (0, 0, j))],
        out_specs=pl.BlockSpec((tm, tn), lambda i, j: (i, j)),
        out_shape=jax.ShapeDtypeStruct((mo, d), BF16),
        compiler_params=_cparams(2),
        name="merge",
    )(*ys, p_all, p_all, p_all, p_all, w_branch)


def _outproj_kernel(m_ref, w_ref, x_ref, mod_ref, o_ref):
    o_ref[...] = x_ref[...] + mod_ref[0, 2:3, :] * _dot(m_ref[...], w_ref[...])


def _outproj(mm, w_out, x_all, mod, *, tm, row_off, nctx, bpb):
    mo, d = mm.shape
    tn = _pick((1024, 512), d)
    return pl.pallas_call(
        _outproj_kernel,
        grid=(mo // tm, d // tn),
        in_specs=[pl.BlockSpec((tm, d), lambda i, j: (i, 0)),
                  pl.BlockSpec((d, tn), lambda i, j: (0, j)),
                  pl.BlockSpec((tm, tn), lambda i, j: (i + row_off, j)),
                  pl.BlockSpec((1, 6, tn), lambda i, j: (_row_group(i + row_off, nctx, bpb), 0, j))],
        out_specs=pl.BlockSpec((tm, tn), lambda i, j: (i, j)),
        out_shape=jax.ShapeDtypeStruct((mo, d), F32),
        compiler_params=_cparams(2),
        name="outproj",
    )(mm, w_out, x_all, mod)


def _mlp_kernel(x_ref, mod_ref, g_ref, w1_ref, w2_ref, gf_ref, o_ref, h_ref, acc_ref, *, final_norm):
    j = pl.program_id(1)

    @pl.when(j == 0)
    def _():
        h = _norm_mod(x_ref[...], g_ref[...], mod_ref[0, 3:4, :], mod_ref[0, 4:5, :])
        h_ref[...] = h.astype(BF16)
        acc_ref[...] = jnp.zeros_like(acc_ref)

    a = jnp.maximum(_dot(h_ref[...], w1_ref[...]), 0.0)
    acc_ref[...] += _dot((a * a).astype(BF16), w2_ref[...])

    @pl.when(j == pl.num_programs(1) - 1)
    def _():
        y = x_ref[...] + mod_ref[0, 5:6, :] * acc_ref[...]
        if final_norm:
            y = y * lax.rsqrt(jnp.mean(y * y, axis=-1, keepdims=True) + EPS) * gf_ref[...]
        o_ref[...] = y


def _mlp(x_in, mod, g, w1, w2, g_final, *, tm, row_off, nctx, bpb, final_norm):
    mo, d = x_in.shape
    hid = w1.shape[1]
    th = _pick((512, 256, 128), hid)
    return pl.pallas_call(
        functools.partial(_mlp_kernel, final_norm=final_norm),
        grid=(mo // tm, hid // th),
        in_specs=[pl.BlockSpec((tm, d), lambda i, j: (i, 0)),
                  pl.BlockSpec((1, 6, d), lambda i, j: (_row_group(i + row_off, nctx, bpb), 0, 0)),
                  pl.BlockSpec((1, d), lambda i, j: (0, 0)),
                  pl.BlockSpec((d, th), lambda i, j: (0, j)),
                  pl.BlockSpec((th, d), lambda i, j: (j, 0)),
                  pl.BlockSpec((1, d), lambda i, j: (0, 0))],
        out_specs=pl.BlockSpec((tm, d), lambda i, j: (i, 0)),
        out_shape=jax.ShapeDtypeStruct((mo, d), F32),
        scratch_shapes=[pltpu.VMEM((tm, d), BF16), pltpu.VMEM((tm, d), F32)],
        compiler_params=_cparams(2),
        name="mlp",
    )(x_in, mod, g, w1, w2, g_final)


def _blockdiag2(w):
    _, r, c = w.shape
    z = jnp.zeros((r, c), w.dtype)
    out = jnp.concatenate([jnp.concatenate([w[0], z], axis=1), jnp.concatenate([z, w[1]], axis=1)], axis=0)
    return jnp.pad(out, ((0, 128 - 2 * r), (0, 0))).astype(BF16)


def _permute_w_in(w, d_model):
    c = d_model // N_BRANCH
    o = [0]
    for wd in (3 * c, 64 * 2, 64 * 2, 128, c, 2 * c, c, c, c, c, 32, c, 2 * c, N_BRANCH * d_model):
        o.append(o[-1] + wd)
    seg = lambda k: w[:, o[k]:o[k + 1]]
    pad = jnp.zeros((w.shape[0], 512 - 128 * 3 - 32), w.dtype)
    parts = [seg(0), seg(5), seg(12), seg(4), seg(6), seg(7), seg(8), seg(9), seg(11), seg(13),
             seg(1), seg(2), seg(3), seg(10), pad]
    out = jnp.concatenate(parts, axis=1).astype(BF16)
    assert out.shape[1] == N_COLS
    return out


def kernel(x, c, ctx, c_ctx, w_ada, b_ada, g_norm1, g_norm2, g_final, w_in, rw_conv, rw_w0, rw_w2,
           rw_a0, rw_a2, rw_g2, rw_kk, rw_ka, rw_rk, rw_ln_w, rw_ln_b, hg_gamma, hg_norm, gla_gw,
           gla_gb, gla_norm, sgu_ln_w, sgu_ln_b, sgu_w, sgu_b, w_branch, w_out, w_mlp1, w_mlp2):
    n_batch, seq, d_model = x.shape
    ctx_len = ctx.shape[1]
    depth = w_in.shape[0]
    cw = d_model // N_BRANCH
    assert cw == 512 and d_model == 2048, "column layout constants assume D_MODEL = 2048"
    m_ctx = n_batch * ctx_len

    tm = _pick((1024, 512, 256, 128), m_ctx, seq)
    tb = _pick((256, 128), ctx_len, seq)
    nctx, bpb = m_ctx // tm, seq // tm
    ctxb, latb = ctx_len // tb, seq // tb
    seqs = dict(tb=tb, n_batch=n_batch, ctxb=ctxb, latb=latb)

    x_all = jnp.concatenate([ctx.reshape(m_ctx, d_model), x.reshape(n_batch * seq, d_model)], axis=0)
    c_rows = jnp.concatenate([c_ctx[None, :], c, jnp.zeros((7 - n_batch, d_model), F32)], axis=0)
    mod_all = _ada(c_rows, w_ada, b_ada).reshape(depth, 8, 6, d_model)

    head_ones = (jnp.arange(cw)[:, None] // RW_HEAD == jnp.arange(cw)[None, :] // RW_HEAD).astype(BF16)
    row = lambda a: a.reshape(1, -1)

    for l in range(depth):
        last = l == depth - 1
        mod = mod_all[l]
        p_all = _inproj(x_all, mod, row(g_norm1[l]), _permute_w_in(w_in[l], d_model),
                        tm=tm, nctx=nctx, bpb=bpb)

        prm = dict(conv=rw_conv[l], w0=row(rw_w0[l]), w2=_blockdiag2(rw_w2[l]), a0=row(rw_a0[l]),
                   a2=_blockdiag2(rw_a2[l]), g2=rw_g2[l].astype(BF16), kk=row(rw_kk[l]),
                   ka=row(rw_ka[l]), rk=row(rw_rk[l]), e=head_ones)
        r, v, nkk, g, bonus, w, kka, km = _rw_prep(p_all, prm, tb=tb, nctx=m_ctx // tb,
                                                    ctx_bps=ctxb, lat_bps=latb)
        ys = _rw_scan(r, v, nkk, w, kka, km, **seqs)
        ya = _rw_post(ys[0], ys[1], g, bonus, row(rw_ln_w[l]), row(rw_ln_b[l]), head_ones, tb=tb)

        ob = _hgrn2(p_all, hg_gamma, layer=l, **seqs)
        yb = _la_post(ob[0], ob[1], p_all, C_HGG, row(hg_norm[l]), tb=tb, nh=cw // HG_DK)

        gw = _blockdiag2(gla_gw[l])
        gb = row(gla_gb[l])
        oc = _gla(p_all, gw, gb, **seqs)
        yc = _la_post(oc[0], oc[1], p_all, C_GLG, row(gla_norm[l]), tb=tb, nh=GLA_HEADS)

        bs = jnp.broadcast_to(sgu_b[l][:, :, None], sgu_w[l].shape)
        yd = _sgu(p_all, row(sgu_ln_w[l]), row(sgu_ln_b[l]), sgu_w[l].astype(BF16), bs,
                  rb=_pick((512, 256, 128), m_ctx, seq))

        row_off = nctx if last else 0
        mm = _merge((ya, yb, yc, yd), p_all, w_branch[l].astype(BF16), tm=tm // 2, row_off=2 * row_off)
        x_mid = _outproj(mm, w_out[l].astype(BF16), x_all, mod, tm=tm, row_off=row_off, nctx=nctx, bpb=bpb)
        x_all = _mlp(x_mid, mod, row(g_norm2[l]), w_mlp1[l].astype(BF16), w_mlp2[l].astype(BF16),
                     row(g_final), tm=tm // 2, row_off=2 * row_off, nctx=2 * nctx, bpb=2 * bpb,
                     final_norm=last)
    return x_all.reshape(n_batch, seq, d_model)
```

```python
import functools

import jax
import jax.numpy as jnp
from jax import lax
from jax.experimental import pallas as pl
from jax.experimental.pallas import tpu as pltpu

F32 = jnp.float32
BF16 = jnp.bfloat16

N_BRANCH = 4
RW_HEAD = 64
RW_LN_EPS = 64e-5
HG_DK = 128
GLA_HEADS = 4
GLA_GATE_NORM = 16.0
LA_CHUNK = 16
SGU_CHUNK = 128
SGU_GROUPS = 4
EPS = 1e-6

C_RKV = 0
C_HGF = 1536
C_SGU = 2560
C_HGQ = 3584
C_HGI = 4096
C_HGG = 4608
C_GLQK = 5120
C_GLV = 5632
C_GLG = 6144
C_SMALL = 6656
N_MIX = 7168
N_GATE = 8192

VMEM_LIMIT = 56 * 1024 * 1024


def _cparams(n_axes):
    return pltpu.CompilerParams(dimension_semantics=("arbitrary",) * n_axes,
                                vmem_limit_bytes=VMEM_LIMIT)


def _pick(n_list, *dims):
    for n in n_list:
        if all(d % n == 0 for d in dims):
            return n
    raise ValueError(f"no block size in {n_list} divides {dims}")


def _row_group(i, nctx, bpb):
    return jnp.where(i < nctx, 0, 1 + (i - nctx) // bpb)


def _dot(a, b):
    return jnp.dot(a, b, preferred_element_type=F32)


def _dot_nt(a, b):
    return lax.dot_general(a, b, (((1,), (1,)), ((), ())), preferred_element_type=F32)


def _dot_tn(a, b):
    return lax.dot_general(a, b, (((0,), (0,)), ((), ())), preferred_element_type=F32)


def _split_dot(x, e):
    hi = x.astype(BF16)
    lo = (x - hi.astype(F32)).astype(BF16)
    return _dot(hi, e) + _dot(lo, e)


def _split3_dot(e, x):
    p1 = x.astype(BF16)
    r1 = x - p1.astype(F32)
    p2 = r1.astype(BF16)
    p3 = (r1 - p2.astype(F32)).astype(BF16)
    return _dot(e, p1) + _dot(e, p2) + _dot(e, p3)


def _log_sigmoid(x):
    return jnp.minimum(x, 0.0) - jnp.log(1.0 + jnp.exp(-jnp.abs(x)))


def _sigmoid(x):
    return 0.5 * jnp.tanh(0.5 * x) + 0.5


def _ada_kernel(c_ref, w_ref, b_ref, o_ref):
    c = c_ref[...]
    act = c * _sigmoid(c)
    o_ref[0] = _dot(act.astype(BF16), w_ref[0].astype(BF16)) + b_ref[0]


def _ada(c_rows, w_ada, b_ada):
    n_layers, d, n = w_ada.shape
    tn = _pick((1024, 512, 256, 128), n)
    return pl.pallas_call(
        _ada_kernel,
        grid=(n_layers, n // tn),
        in_specs=[pl.BlockSpec((8, d), lambda l, j: (0, 0)),
                  pl.BlockSpec((1, d, tn), lambda l, j: (l, 0, j)),
                  pl.BlockSpec((1, 1, tn), lambda l, j: (l, 0, j))],
        out_specs=pl.BlockSpec((1, 8, tn), lambda l, j: (l, 0, j)),
        out_shape=jax.ShapeDtypeStruct((n_layers, 8, n), F32),
        compiler_params=_cparams(2),
        name="ada_mod",
    )(c_rows, w_ada, b_ada.reshape(n_layers, 1, n))


def _norm_mod(x, g, shift, scale):
    y = x * lax.rsqrt(jnp.mean(x * x, axis=-1, keepdims=True) + EPS) * g
    return y * (1.0 + scale) + shift


def _two_part_specs(block, nctx, lat_off, col):
    one = pl.Buffered(1)
    return [pl.BlockSpec(block, lambda i, j: (jnp.minimum(i, nctx - 1), col(j)), pipeline_mode=one),
            pl.BlockSpec(block, lambda i, j: (lat_off + jnp.maximum(i - nctx, 0), col(j)), pipeline_mode=one)]


def _inproj_kernel(xc_ref, xl_ref, mod_ref, g_ref, w_ref, og_ref, om_ref, h_ref, *, nctx, n_gate_tiles):
    i, j = pl.program_id(0), pl.program_id(1)

    @pl.when(j == 0)
    def _():
        x = jnp.where(i < nctx, xc_ref[...], xl_ref[...])
        h = _norm_mod(x, g_ref[...], mod_ref[0, 0:1, :], mod_ref[0, 1:2, :])
        h_ref[...] = h.astype(BF16)

    acc = _dot(h_ref[...], w_ref[...])

    @pl.when(j < n_gate_tiles)
    def _():
        og_ref[...] = acc.astype(BF16)

    @pl.when(j >= n_gate_tiles)
    def _():
        om_ref[...] = acc


def _inproj(x_parts, mod, g, w, *, tm, nctx, bpb):
    xc, xl, lat_off = x_parts
    d = xc.shape[1]
    m = (nctx + (xl.shape[0] // tm - lat_off)) * tm
    tn = _pick((1024, 512), N_GATE, N_MIX)
    ng = N_GATE // tn
    return pl.pallas_call(
        functools.partial(_inproj_kernel, nctx=nctx, n_gate_tiles=ng),
        grid=(m // tm, (N_GATE + N_MIX) // tn),
        in_specs=_two_part_specs((tm, d), nctx, lat_off, lambda j: 0) + [
            pl.BlockSpec((1, 6, d), lambda i, j: (_row_group(i, nctx, bpb), 0, 0)),
            pl.BlockSpec((1, d), lambda i, j: (0, 0)),
            pl.BlockSpec((d, tn), lambda i, j: (0, j))],
        out_specs=[pl.BlockSpec((tm, tn), lambda i, j: (i, jnp.minimum(j, ng - 1))),
                   pl.BlockSpec((tm, tn), lambda i, j: (i, jnp.maximum(j - ng, 0)))],
        out_shape=[jax.ShapeDtypeStruct((m, N_GATE), BF16), jax.ShapeDtypeStruct((m, N_MIX), F32)],
        scratch_shapes=[pltpu.VMEM((tm, d), BF16)],
        compiler_params=_cparams(2),
        name="inproj",
    )(xc, xl, mod, g, w)


def _rw_prep_kernel(rkv_ref, prev_ref, next_ref, small_ref, conv_ref, w0_ref, w2_ref, a0_ref,
                    a2_ref, g2_ref, kk_ref, ka_ref, rk_ref, e_ref,
                    r_out, v_out, nkk_out, g_out, bonus_out, w_out, kka_out, km_out,
                    *, tb, nctx, ctx_bps, lat_bps):
    i = pl.program_id(0)
    c = r_out.shape[-1]
    j = jnp.where(i < nctx, i, i - nctx)
    bps = jnp.where(i < nctx, ctx_bps, lat_bps)
    first = lax.rem(j, bps) == 0
    last = lax.rem(j, bps) == bps - 1

    blk = rkv_ref[...]
    rows = lax.broadcasted_iota(jnp.int32, (tb, 1), 0)
    prev_row = jnp.where(first, 0.0, prev_ref[7:8, :])
    next_row = jnp.where(last, 0.0, next_ref[0:1, :])
    xm1 = jnp.where(rows == 0, prev_row, pltpu.roll(blk, 1, 0))
    xp1 = jnp.where(rows == tb - 1, next_row, pltpu.roll(blk, tb - 1, 0))
    conv = conv_ref[0:1, :] * xm1 + conv_ref[1:2, :] * blk + conv_ref[2:3, :] * xp1
    r = conv[:, 0:c]
    k = conv[:, c:2 * c]
    v = conv[:, 2 * c:3 * c]

    small = small_ref[...]
    wl = small[:, 0:128]
    al = small[:, 128:256]
    gl = small[:, 256:384]
    w_pre = w0_ref[...] + _dot(jnp.tanh(wl).astype(BF16), w2_ref[...])
    softplus = jnp.maximum(-w_pre, 0.0) + jnp.log(1.0 + jnp.exp(-jnp.abs(w_pre)))
    log_decay = -jnp.exp(-softplus - 0.5)
    a = _sigmoid(a0_ref[...] + _dot(al.astype(BF16), a2_ref[...]))
    g = _dot(_sigmoid(gl).astype(BF16), g2_ref[...])

    e = e_ref[...]
    kkv = k * kk_ref[...]
    kk = kkv * lax.rsqrt(_split_dot(kkv * kkv, e) + 1e-12)
    bonus = _split_dot(r * k * rk_ref[...], e) * v

    r_out[...] = r
    v_out[...] = v
    nkk_out[...] = -kk
    g_out[...] = g
    bonus_out[...] = bonus
    for d in range(2):
        a_d = a[:, d * c:(d + 1) * c]
        w_out[d] = log_decay[:, d * c:(d + 1) * c]
        kka_out[d] = kk * a_d
        km_out[d] = k * (1.0 + (a_d - 1.0) * ka_ref[...])


def _rw_prep(p_all, prm, *, tb, nctx, ctx_bps, lat_bps):
    m = p_all.shape[0]
    c = prm["kk"].shape[-1]
    nblk = m // tb
    t8 = tb // 8
    full = lambda shape: pl.BlockSpec(shape, lambda i: (0,) * len(shape))
    tok = pl.BlockSpec((tb, c), lambda i: (i, 0))
    tok2 = pl.BlockSpec((2, tb, c), lambda i: (0, i, 0))
    one = jax.ShapeDtypeStruct((m, c), F32)
    two = jax.ShapeDtypeStruct((2, m, c), F32)
    return pl.pallas_call(
        functools.partial(_rw_prep_kernel, tb=tb, nctx=nctx, ctx_bps=ctx_bps, lat_bps=lat_bps),
        grid=(nblk,),
        in_specs=[pl.BlockSpec((tb, 3 * c), lambda i: (i, C_RKV // (3 * c))),
                  pl.BlockSpec((8, 3 * c), lambda i: (jnp.maximum(i * t8 - 1, 0), 0)),
                  pl.BlockSpec((8, 3 * c), lambda i: (jnp.minimum((i + 1) * t8, m // 8 - 1), 0)),
                  pl.BlockSpec((tb, 512), lambda i: (i, C_SMALL // 512)),
                  full((3, 3 * c)), full((1, 2 * c)), full((128, 2 * c)), full((1, 2 * c)),
                  full((128, 2 * c)), full((128, c)), full((1, c)), full((1, c)), full((1, c)),
                  full((c, c))],
        out_specs=[tok, tok, tok, tok, tok, tok2, tok2, tok2],
        out_shape=[one, one, one, one, one, two, two, two],
        compiler_params=_cparams(1),
        name="rwkv_prep",
    )(p_all, p_all, p_all, p_all, prm["conv"], prm["w0"], prm["w2"], prm["a0"], prm["a2"],
      prm["g2"], prm["kk"], prm["ka"], prm["rk"], prm["e"])


class _RwUnit:
    def __init__(self, refs, scratch, *, tb, npair, reverse):
        (self.r_ref, self.v_ref, self.a_ref, self.lw_ref, self.b_ref, self.k_ref, self.tri_ref,
         self.y_ref) = refs
        (self.s_ref, self.c_scr, self.ag_scr, self.rg_scr, self.bg_scr, self.kg_scr) = scratch[:6]
        self.slots = (scratch[6:9], scratch[9:12])
        self.reverse = reverse
        self.nch = tb // LA_CHUNK
        self.pairs = range(npair)
        L, hd = LA_CHUNK, RW_HEAD
        self.lss = [slice(p * 2 * hd, (p + 1) * 2 * hd) for p in self.pairs]
        lane = lax.broadcasted_iota(jnp.int32, (1, 2 * hd), 1)
        self.lo = (lane < hd).astype(F32)
        self.hi = 1.0 - self.lo
        ti = lax.broadcasted_iota(jnp.int32, (2 * L, 2 * hd), 0)
        sl = lax.broadcasted_iota(jnp.int32, (2 * L, 2 * hd), 1)
        sj = sl & (L - 1)
        tt = ti & (L - 1)
        earlier = (sj > tt) if reverse else (sj < tt)
        self.aa_mask = ((earlier | ((ti >= L) & (sj == tt))) & (sl < 4 * L)).astype(F32)
        self.col_idx0 = jnp.where(lax.broadcasted_iota(jnp.int32, (L, 2 * hd), 1) < hd, 0, L)
        self.blk_mask = ((lax.broadcasted_iota(jnp.int32, (2 * hd, 2 * hd), 0) >> 6)
                         == (lax.broadcasted_iota(jnp.int32, (2 * hd, 2 * hd), 1) >> 6)).astype(F32)
        self.zeros2l = jnp.zeros((2 * L, 2 * hd), F32)
        self.zeros4l = jnp.zeros((4 * L, 2 * hd), F32)
        self.last = 0 if reverse else L - 1
        self.order = range(L - 1, -1, -1) if reverse else range(L)

    def reset(self):
        self.s_ref[...] = jnp.zeros_like(self.s_ref)

    def prologue(self):
        lw = self.lw_ref[0]
        c = _split3_dot(self.tri_ref[...], lw)
        self.c_scr[...] = c
        enc = jnp.exp(-c)
        self.ag_scr[...] = self.a_ref[...] * jnp.exp(c - lw)
        self.rg_scr[...] = self.r_ref[...] * jnp.exp(c)
        self.bg_scr[...] = self.b_ref[0] * enc
        self.kg_scr[...] = self.k_ref[0] * enc

    def rows(self, ci):
        L = LA_CHUNK
        cc = (self.nch - 1 - ci) if self.reverse else ci
        return pl.ds(pl.multiple_of(cc * L, L), L)

    def halves(self, x):
        return [x * self.lo, x * self.hi]

    def lhs_of(self, rows):
        return [jnp.concatenate([self.ag_scr[rows, ls], self.rg_scr[rows, ls]], axis=0).astype(BF16)
                for ls in self.lss]

    def prep_aa(self, rows):
        lhs = self.lhs_of(rows)
        out = []
        for p in self.pairs:
            ls = self.lss[p]
            rhs = jnp.concatenate(self.halves(self.bg_scr[rows, ls]) + self.halves(self.kg_scr[rows, ls])
                                  + [self.zeros4l], axis=0)
            out.append(_dot_nt(lhs[p], rhs.astype(BF16)))
        return out

    def prep_akv(self, rows, aa_raw, slot):
        _, akv_scr, aar_scr = slot
        L = LA_CHUNK
        aa = [x * self.aa_mask for x in aa_raw]
        for p in self.pairs:
            v = self.v_ref[rows, self.lss[p]]
            vv = jnp.concatenate([self.zeros2l] + self.halves(v) + [self.zeros4l], axis=0).astype(BF16)
            akv_scr[p] = _dot(aa[p][0:L].astype(BF16), vv)
            aar_scr[p] = aa[p][L:2 * L]
        return aa

    def prep_cols(self, aa, slot):
        L = LA_CHUNK
        for p in self.pairs:
            for s in range(L):
                slot[0][p, s] = jnp.take_along_axis(aa[p][0:L], self.col_idx0 + s, axis=1)

    def adv_g(self, rows):
        lhs = self.lhs_of(rows)
        return [_dot_nt(lhs[p], self.s_ref[p].astype(BF16)) for p in self.pairs]

    def adv_solve(self, g, slot):
        L = LA_CHUNK
        col_scr, akv_scr, _ = slot
        u = [g[p][0:L] + akv_scr[p] for p in self.pairs]
        for s in self.order:
            for p in self.pairs:
                u[p] = u[p] + col_scr[p, s] * u[p][s:s + 1]
        return u

    def adv_out(self, rows, g, u, slot):
        L = LA_CHUNK
        for p in self.pairs:
            v = self.v_ref[rows, self.lss[p]]
            uv = jnp.concatenate(self.halves(u[p]) + self.halves(v) + [self.zeros4l], axis=0).astype(BF16)
            self.y_ref[rows, self.lss[p]] = g[p][L:2 * L] + _dot(slot[2][p].astype(BF16), uv)
        for p in self.pairs:
            ls = self.lss[p]
            cch = self.c_scr[rows, ls]
            cl = cch[self.last:self.last + 1]
            dec = jnp.exp(cl - cch)
            bk = jnp.concatenate([self.b_ref[0, rows, ls] * dec, self.k_ref[0, rows, ls] * dec], axis=0)
            upd = _dot_tn(jnp.concatenate([u[p], self.v_ref[rows, ls]], axis=0).astype(BF16),
                          bk.astype(BF16))
            self.s_ref[p] = self.s_ref[p] * jnp.exp(cl) + upd * self.blk_mask


N_RW_REFS = 8
N_RW_SCRATCH = 12


def _rw_scan_kernel(*refs, tb, npair):
    n_in = N_RW_REFS - 1
    units = []
    for d in range(2):
        ins = refs[d * n_in:(d + 1) * n_in]
        out = refs[2 * n_in + d]
        scr = refs[2 * n_in + 2 + d * N_RW_SCRATCH:2 * n_in + 2 + (d + 1) * N_RW_SCRATCH]
        units.append(_RwUnit(tuple(ins) + (out,), scr, tb=tb, npair=npair, reverse=d == 1))
    nch = tb // LA_CHUNK

    @pl.when(pl.program_id(1) == 0)
    def _():
        for un in units:
            un.reset()

    for un in units:
        un.prologue()

    def step(ci_adv, sa, ci_prep, sp):
        rows_a = [un.rows(ci_adv) for un in units]
        rows_p = [un.rows(ci_prep) for un in units]
        aa_raw = [un.prep_aa(rp) for un, rp in zip(units, rows_p)]
        g = [un.adv_g(ra) for un, ra in zip(units, rows_a)]
        aa = [un.prep_akv(rp, x, un.slots[sp]) for un, rp, x in zip(units, rows_p, aa_raw)]
        u = [un.adv_solve(x, un.slots[sa]) for un, x in zip(units, g)]
        for un, x in zip(units, aa):
            un.prep_cols(x, un.slots[sp])
        for un, ra, x, y in zip(units, rows_a, g, u):
            un.adv_out(ra, x, y, un.slots[sa])

    for un in units:
        r0 = un.rows(0)
        un.prep_cols(un.prep_akv(r0, un.prep_aa(r0), un.slots[0]), un.slots[0])

    def two_chunks(j, carry):
        c0 = 2 * j
        step(c0, 0, c0 + 1, 1)
        step(c0 + 1, 1, jnp.minimum(c0 + 2, nch - 1), 0)
        return carry

    lax.fori_loop(0, nch // 2, two_chunks, 0)


def _seq_block(d, b, i, *, n_batch, ctxb, latb):
    is_ctx = i < ctxb
    cs = jnp.where(d == 0, i, ctxb - 1 - i)
    lj = jnp.where(d == 0, i - ctxb, latb - 1 - (i - ctxb))
    return jnp.where(is_ctx, b * ctxb + cs, n_batch * ctxb + b * latb + lj)


def _rw_scan(r, v, nkk, lw, kka, km, *, tb, n_batch, ctxb, latb):
    m, c = r.shape
    npair = c // (2 * RW_HEAD)
    lanes = 2 * RW_HEAD
    in_specs, args = [], []
    for d in range(2):
        blk = _la_index(d, n_batch, ctxb, latb)
        tok = pl.BlockSpec((tb, c), lambda b, i, blk=blk: (blk(b, i), 0))
        tokd = pl.BlockSpec((1, tb, c), lambda b, i, blk=blk, d=d: (d, blk(b, i), 0))
        in_specs += [tok, tok, tok, tokd, tokd, tokd, pl.BlockSpec((tb, tb), lambda b, i: (0, 0))]
        args += [r, v, nkk, lw, kka, km, _chunk_tri(tb, d == 1)]
    out_specs = [pl.BlockSpec((tb, c), lambda b, i, blk=_la_index(d, n_batch, ctxb, latb): (blk(b, i), 0))
                 for d in range(2)]
    buf = pltpu.VMEM((tb, c), F32)
    unit_scratch = [pltpu.VMEM((npair, lanes, lanes), F32), buf, buf, buf, buf, buf] + 2 * [
        pltpu.VMEM((npair, LA_CHUNK, LA_CHUNK, lanes), F32),
        pltpu.VMEM((npair, LA_CHUNK, lanes), F32),
        pltpu.VMEM((npair, LA_CHUNK, lanes), F32)]
    assert len(unit_scratch) == N_RW_SCRATCH
    return pl.pallas_call(
        functools.partial(_rw_scan_kernel, tb=tb, npair=npair),
        grid=(n_batch, ctxb + latb),
        in_specs=in_specs,
        out_specs=out_specs,
        out_shape=[jax.ShapeDtypeStruct((m, c), F32)] * 2,
        scratch_shapes=unit_scratch + unit_scratch,
        compiler_params=_cparams(2),
        name="rwkv_scan",
    )(*args)


def _rw_post(y, g, bonus, lnw, lnb, e):
    inv = 1.0 / RW_HEAD
    yc = y - _split_dot(y, e) * inv
    var = _split_dot(yc * yc, e) * inv
    yn = yc * lax.rsqrt(var + RW_LN_EPS)
    return (yn * lnw + lnb + bonus) * g


class _LaUnit:
    def __init__(self, q_ref, k_ref, v_ref, b_ref, o_ref, s_ref, *, tb, nh, dk, dv, reverse, q_scale):
        self.q_ref, self.k_ref, self.v_ref, self.b_ref, self.o_ref, self.s_ref = (
            q_ref, k_ref, v_ref, b_ref, o_ref, s_ref)
        self.nch = tb // LA_CHUNK
        self.dk, self.dv, self.reverse, self.q_scale = dk, dv, reverse, q_scale
        self.lanes = 128
        self.pack = self.lanes // dk
        self.groups = range(nh // self.pack)
        self.rowi = lax.broadcasted_iota(jnp.int32, (LA_CHUNK, 1), 0)
        lane = lax.broadcasted_iota(jnp.int32, (1, self.lanes), 1)
        self.head_lanes = [(lane // dk == j).astype(F32) for j in range(self.pack)]
        if self.pack > 1:
            rows = lax.broadcasted_iota(jnp.int32, (self.pack * dv, self.lanes), 0)
            cols = lax.broadcasted_iota(jnp.int32, (self.pack * dv, self.lanes), 1)
            self.blk_mask = ((rows // dv) == (cols // dk)).astype(F32)

    def reset(self):
        self.s_ref[...] = jnp.zeros_like(self.s_ref)

    def load(self, ci):
        cc = (self.nch - 1 - ci) if self.reverse else ci
        rows = pl.ds(pl.multiple_of(cc * LA_CHUNK, LA_CHUNK), LA_CHUNK)
        pack, dv, lanes = self.pack, self.dv, self.lanes
        out = []
        for g in self.groups:
            ks = slice(g * lanes, (g + 1) * lanes)
            q = self.q_ref[rows, ks] * self.q_scale
            k = self.k_ref[rows, ks]
            v = self.v_ref[rows, g * pack * dv:(g + 1) * pack * dv]
            b = self.b_ref[rows, ks]
            o_inter = _dot_nt((q * jnp.exp(b)).astype(BF16), self.s_ref[g].astype(BF16))
            out.append((rows, q, k, v, b, [o_inter[:, j * dv:(j + 1) * dv] for j in range(pack)]))
        return out

    def intra(self, ops):
        pack, dv = self.pack, self.dv
        for g in self.groups:
            rows, q, k, v, b, o = ops[g]
            for s in range(LA_CHUNK):
                valid = (self.rowi <= s) if self.reverse else (self.rowi >= s)
                term = (q * k[s:s + 1]) * jnp.exp(b - b[s:s + 1])
                for j in range(pack):
                    tj = term if pack == 1 else term * self.head_lanes[j]
                    col = jnp.where(valid, jnp.sum(tj, axis=-1, keepdims=True), 0.0)
                    o[j] = o[j] + col * v[s:s + 1, j * dv:(j + 1) * dv]
            for j in range(pack):
                self.o_ref[rows, (g * pack + j) * dv:(g * pack + j + 1) * dv] = o[j]

    def update(self, ops):
        for g in self.groups:
            _, _, k, v, b, _ = ops[g]
            b_last = b[0:1] if self.reverse else b[LA_CHUNK - 1:LA_CHUNK]
            upd = _dot_tn(v.astype(BF16), (k * jnp.exp(b_last - b)).astype(BF16))
            if self.pack > 1:
                upd = upd * self.blk_mask
            self.s_ref[g] = self.s_ref[g] * jnp.exp(b_last) + upd


def _la_run(units, nch):
    def chunk(ci, carry):
        ops = [un.load(ci) for un in units]
        for un, x in zip(units, ops):
            un.update(x)
        for un, x in zip(units, ops):
            un.intra(x)
        return carry

    lax.fori_loop(0, nch, chunk, 0)


def _hg_kernel(qf_ref, ff_ref, vf_ref, trif_ref, qb_ref, fb_ref, vb_ref, trib_ref, gamma_ref,
               of_ref, ob_ref, sf_ref, kf_scr, bf_scr, sb_ref, kb_scr, bb_scr, *, layer, tb, nh, dk, dv):
    gam = gamma_ref[...]
    ex = jnp.exp(gam - jnp.max(gam, axis=0, keepdims=True))
    p = ex / jnp.sum(ex, axis=0, keepdims=True)
    cum = p[0:1]
    for i in range(1, layer + 1):
        cum = cum + p[i:i + 1]
    lb = cum - p[0:1]
    lo = jnp.log(lb)
    l1 = jnp.log(1.0 - lb)

    units = []
    for d, (q_ref, f_ref, v_ref, tri_ref, o_ref, s_ref, k_scr, b_scr) in enumerate((
            (qf_ref, ff_ref, vf_ref, trif_ref, of_ref, sf_ref, kf_scr, bf_scr),
            (qb_ref, fb_ref, vb_ref, trib_ref, ob_ref, sb_ref, kb_scr, bb_scr))):
        hi = l1 + _log_sigmoid(f_ref[...])
        mx = jnp.maximum(lo, hi)
        mn = jnp.minimum(lo, hi)
        log_f = mx + jnp.log(1.0 + jnp.exp(mn - mx))
        k_scr[...] = 1.0 - jnp.exp(log_f)
        b_scr[...] = _split3_dot(tri_ref[...], log_f)
        units.append(_LaUnit(q_ref, k_scr, v_ref, b_scr, o_ref, s_ref, tb=tb, nh=nh, dk=dk, dv=dv,
                             reverse=d == 1, q_scale=1.0))

    @pl.when(pl.program_id(1) == 0)
    def _():
        for un in units:
            un.reset()

    _la_run(units, tb // LA_CHUNK)


def _gla_kernel(qkf_ref, vf_ref, smallf_ref, trif_ref, qkb_ref, vb_ref, smallb_ref, trib_ref,
                gw_ref, gb_ref, of_ref, ob_ref, sf_ref, bf_scr, sb_ref, bb_scr, *, tb, nh, dk, dv):
    hk = nh * dk
    units = []
    for d, (qk_ref, v_ref, small_ref, tri_ref, o_ref, s_ref, b_scr) in enumerate((
            (qkf_ref, vf_ref, smallf_ref, trif_ref, of_ref, sf_ref, bf_scr),
            (qkb_ref, vb_ref, smallb_ref, trib_ref, ob_ref, sb_ref, bb_scr))):
        code = small_ref[:, 384:512]
        pre = _dot(code.astype(BF16), gw_ref[:, d * hk:(d + 1) * hk]) + gb_ref[:, d * hk:(d + 1) * hk]
        log_g = _log_sigmoid(pre) * (1.0 / GLA_GATE_NORM)
        b_scr[...] = _split3_dot(tri_ref[...], log_g)
        units.append(_LaUnit(qk_ref.at[:, 0:hk], qk_ref.at[:, hk:2 * hk], v_ref, b_scr, o_ref, s_ref,
                             tb=tb, nh=nh, dk=dk, dv=dv, reverse=d == 1, q_scale=dk ** -0.5))

    @pl.when(pl.program_id(1) == 0)
    def _():
        for un in units:
            un.reset()

    _la_run(units, tb // LA_CHUNK)


def _chunk_tri(tb, reverse):
    t = jnp.arange(tb)
    same = (t[:, None] // LA_CHUNK) == (t[None, :] // LA_CHUNK)
    tri = (t[None, :] >= t[:, None]) if reverse else (t[None, :] <= t[:, None])
    return (same & tri).astype(BF16)


def _la_index(d, n_batch, ctxb, latb):
    return lambda b, i: _seq_block(d, b, i, n_batch=n_batch, ctxb=ctxb, latb=latb)


def _hgrn2(p_all, gamma, *, layer, tb, n_batch, ctxb, latb):
    m = p_all.shape[0]
    c = gamma.shape[-1]
    nh = c // HG_DK
    in_specs, args, out_specs = [], [], []
    for d in range(2):
        blk = _la_index(d, n_batch, ctxb, latb)
        col = lambda off, blk=blk: pl.BlockSpec((tb, c), lambda b, i: (blk(b, i), off // c))
        in_specs += [col(C_HGQ), col(C_HGF + d * c), col(C_HGI), pl.BlockSpec((tb, tb), lambda b, i: (0, 0))]
        args += [p_all, p_all, p_all, _chunk_tri(tb, d == 1)]
        out_specs.append(col(0))
    unit_scratch = [pltpu.VMEM((nh, HG_DK, HG_DK), F32), pltpu.VMEM((tb, c), F32), pltpu.VMEM((tb, c), F32)]
    return pl.pallas_call(
        functools.partial(_hg_kernel, layer=layer, tb=tb, nh=nh, dk=HG_DK, dv=HG_DK),
        grid=(n_batch, ctxb + latb),
        in_specs=in_specs + [pl.BlockSpec(gamma.shape, lambda b, i: (0, 0))],
        out_specs=out_specs,
        out_shape=[jax.ShapeDtypeStruct((m, c), F32)] * 2,
        scratch_shapes=unit_scratch + unit_scratch,
        compiler_params=_cparams(2),
        name="hgrn2_scan",
    )(*args, gamma)


def _gla(p_all, gw, gb, *, tb, n_batch, ctxb, latb):
    m = p_all.shape[0]
    hk = gw.shape[-1] // 2
    dk = hk // GLA_HEADS
    c = 2 * hk
    dv = c // GLA_HEADS
    in_specs, args, out_specs = [], [], []
    for d in range(2):
        blk = _la_index(d, n_batch, ctxb, latb)
        col = lambda off, blk=blk: pl.BlockSpec((tb, c), lambda b, i: (blk(b, i), off // c))
        in_specs += [col(C_GLQK), col(C_GLV), col(C_SMALL), pl.BlockSpec((tb, tb), lambda b, i: (0, 0))]
        args += [p_all, p_all, p_all, _chunk_tri(tb, d == 1)]
        out_specs.append(col(0))
    unit_scratch = [pltpu.VMEM((hk // 128, (128 // dk) * dv, 128), F32), pltpu.VMEM((tb, hk), F32)]
    return pl.pallas_call(
        functools.partial(_gla_kernel, tb=tb, nh=GLA_HEADS, dk=dk, dv=dv),
        grid=(n_batch, ctxb + latb),
        in_specs=in_specs + [pl.BlockSpec(gw.shape, lambda b, i: (0, 0)),
                             pl.BlockSpec(gb.shape, lambda b, i: (0, 0))],
        out_specs=out_specs,
        out_shape=[jax.ShapeDtypeStruct((m, c), F32)] * 2,
        scratch_shapes=unit_scratch + unit_scratch,
        compiler_params=_cparams(2),
        name="gla_scan",
    )(*args, gw, gb)


def _la_post(o, gate, norm_g, nh):
    dv = o.shape[-1] // nh
    outs = []
    for h in range(nh):
        oh = o[:, h * dv:(h + 1) * dv]
        outs.append(oh * lax.rsqrt(jnp.mean(oh * oh, axis=-1, keepdims=True) + EPS))
    return jnp.concatenate(outs, axis=-1) * norm_g * (gate * _sigmoid(gate))


def _gelu(x):
    return 0.5 * x * (1.0 + jnp.tanh(0.7978845608028654 * (x + 0.044715 * (x * x * x))))


def _sgu_kernel(u_ref, v_ref, lnw_ref, lnb_ref, ws_ref, bs_ref, o_ref, *, rb):
    u = _gelu(u_ref[...])
    v = _gelu(v_ref[...])
    vc = v - jnp.mean(v, axis=-1, keepdims=True)
    vn = vc * lax.rsqrt(jnp.mean(vc * vc, axis=-1, keepdims=True) + EPS)
    vn = (vn * lnw_ref[...] + lnb_ref[...]).astype(BF16)
    gw = vn.shape[-1] // SGU_GROUPS
    for n in range(rb // SGU_CHUNK):
        rs = slice(n * SGU_CHUNK, (n + 1) * SGU_CHUNK)
        for g in range(SGU_GROUPS):
            cs = slice(g * gw, (g + 1) * gw)
            s = _dot(ws_ref[g], vn[rs, cs]) + bs_ref[g]
            o_ref[rs, cs] = u[rs, cs] * s


def _sgu(p_all, lnw, lnb, ws, bs, *, rb):
    m = p_all.shape[0]
    c = lnw.shape[-1]
    return pl.pallas_call(
        functools.partial(_sgu_kernel, rb=rb),
        grid=(m // rb,),
        in_specs=[pl.BlockSpec((rb, c), lambda i: (i, C_SGU // c)),
                  pl.BlockSpec((rb, c), lambda i: (i, C_SGU // c + 1)),
                  pl.BlockSpec((1, c), lambda i: (0, 0)),
                  pl.BlockSpec((1, c), lambda i: (0, 0)),
                  pl.BlockSpec(ws.shape, lambda i: (0, 0, 0)),
                  pl.BlockSpec(bs.shape, lambda i: (0, 0, 0))],
        out_specs=pl.BlockSpec((rb, c), lambda i: (i, 0)),
        out_shape=jax.ShapeDtypeStruct((m, c), F32),
        compiler_params=_cparams(1),
        name="sgu",
    )(p_all, p_all, lnw, lnb, ws, bs)


def _merge_kernel(raf_ref, rab_ref, rg_ref, rbonus_ref, hf_ref, hb_ref, hgate_ref, gf_ref, gb_ref,
                  ggate_ref, yd_ref, lnw_ref, lnb_ref, e_ref, hnorm_ref, gnorm_ref,
                  g0_ref, g1_ref, g2_ref, g3_ref, w_ref, o_ref, y_scr, *, hg_heads):
    @pl.when(pl.program_id(1) == 0)
    def _():
        y_scr[0] = _rw_post(raf_ref[...] + rab_ref[...], rg_ref[...], rbonus_ref[...], lnw_ref[...],
                            lnb_ref[...], e_ref[...]).astype(BF16)
        y_scr[1] = _la_post(hf_ref[...] + hb_ref[...], hgate_ref[...], hnorm_ref[...], hg_heads).astype(BF16)
        y_scr[2] = _la_post(gf_ref[...] + gb_ref[...], ggate_ref[...], gnorm_ref[...], GLA_HEADS).astype(BF16)
        y_scr[3] = yd_ref[...].astype(BF16)

    gs = (g0_ref, g1_ref, g2_ref, g3_ref)
    acc = None
    for j in range(N_BRANCH):
        t = _sigmoid(gs[j][...].astype(F32)) * _dot(y_scr[j], w_ref[j])
        acc = t if acc is None else acc + t
    o_ref[...] = acc.astype(BF16)


def _merge(rw, hg, gla, yd, p_gate, p_mix, vecs, e, w_branch, *, tm, row_off):
    m = p_mix.shape[0]
    _, c, d = w_branch.shape
    tn = 512
    mo = m - row_off * tm
    ytok = pl.BlockSpec((tm, c), lambda i, j: (i + row_off, 0))
    mix = lambda off: pl.BlockSpec((tm, c), lambda i, j: (i + row_off, off // c))
    gate = lambda b: pl.BlockSpec((tm, tn), lambda i, j: (i + row_off, b * d // tn + j))
    vec = pl.BlockSpec((1, c), lambda i, j: (0, 0))
    return pl.pallas_call(
        functools.partial(_merge_kernel, hg_heads=c // HG_DK),
        grid=(mo // tm, d // tn),
        in_specs=[ytok, ytok, ytok, ytok, ytok, ytok, mix(C_HGG), ytok, ytok, mix(C_GLG), ytok,
                  vec, vec, pl.BlockSpec((c, c), lambda i, j: (0, 0)), vec, vec,
                  gate(0), gate(1), gate(2), gate(3),
                  pl.BlockSpec((N_BRANCH, c, tn), lambda i, j: (0, 0, j))],
        out_specs=pl.BlockSpec((tm, tn), lambda i, j: (i, j)),
        out_shape=jax.ShapeDtypeStruct((mo, d), BF16),
        scratch_shapes=[pltpu.VMEM((N_BRANCH, tm, c), BF16)],
        compiler_params=_cparams(2),
        name="merge",
    )(*rw, hg[0], hg[1], p_mix, gla[0], gla[1], p_mix, yd, *vecs[:2], e, *vecs[2:],
      p_gate, p_gate, p_gate, p_gate, w_branch)


def _outproj_kernel(m_ref, w_ref, xc_ref, xl_ref, mod_ref, o_ref, *, nctx, row_off):
    x = jnp.where(pl.program_id(0) + row_off < nctx, xc_ref[...], xl_ref[...])
    o_ref[...] = x + mod_ref[0, 2:3, :] * _dot(m_ref[...], w_ref[...])


def _outproj(mm, w_out, x_parts, mod, *, tm, row_off, nctx, bpb):
    xc, xl, lat_off = x_parts
    mo, d = mm.shape
    tn = _pick((1024, 512), d)
    xspecs = [pl.BlockSpec((tm, tn), lambda i, j: (jnp.minimum(i + row_off, nctx - 1), j)),
              pl.BlockSpec((tm, tn), lambda i, j: (lat_off + jnp.maximum(i + row_off - nctx, 0), j))]
    return pl.pallas_call(
        functools.partial(_outproj_kernel, nctx=nctx, row_off=row_off),
        grid=(mo // tm, d // tn),
        in_specs=[pl.BlockSpec((tm, d), lambda i, j: (i, 0)),
                  pl.BlockSpec((d, tn), lambda i, j: (0, j))] + xspecs + [
                  pl.BlockSpec((1, 6, tn), lambda i, j: (_row_group(i + row_off, nctx, bpb), 0, j))],
        out_specs=pl.BlockSpec((tm, tn), lambda i, j: (i, j)),
        out_shape=jax.ShapeDtypeStruct((mo, d), F32),
        compiler_params=_cparams(2),
        name="outproj",
    )(mm, w_out, xc, xl, mod)


def _mlp_kernel(x_ref, mod_ref, g_ref, w1_ref, w2_ref, gf_ref, o_ref, h_ref, acc_ref, *, final_norm):
    j = pl.program_id(1)

    @pl.when(j == 0)
    def _():
        h = _norm_mod(x_ref[...], g_ref[...], mod_ref[0, 3:4, :], mod_ref[0, 4:5, :])
        h_ref[...] = h.astype(BF16)
        acc_ref[...] = jnp.zeros_like(acc_ref)

    a = jnp.maximum(_dot(h_ref[...], w1_ref[...]), 0.0)
    acc_ref[...] += _dot((a * a).astype(BF16), w2_ref[...])

    @pl.when(j == pl.num_programs(1) - 1)
    def _():
        y = x_ref[...] + mod_ref[0, 5:6, :] * acc_ref[...]
        if final_norm:
            y = y * lax.rsqrt(jnp.mean(y * y, axis=-1, keepdims=True) + EPS) * gf_ref[...]
        o_ref[...] = y


def _mlp(x_in, mod, g, w1, w2, g_final, *, tm, row_off, nctx, bpb, final_norm):
    mo, d = x_in.shape
    hid = w1.shape[1]
    th = _pick((512, 256, 128), hid)
    return pl.pallas_call(
        functools.partial(_mlp_kernel, final_norm=final_norm),
        grid=(mo // tm, hid // th),
        in_specs=[pl.BlockSpec((tm, d), lambda i, j: (i, 0)),
                  pl.BlockSpec((1, 6, d), lambda i, j: (_row_group(i + row_off, nctx, bpb), 0, 0)),
                  pl.BlockSpec((1, d), lambda i, j: (0, 0)),
                  pl.BlockSpec((d, th), lambda i, j: (0, j)),
                  pl.BlockSpec((th, d), lambda i, j: (j, 0)),
                  pl.BlockSpec((1, d), lambda i, j: (0, 0))],
        out_specs=pl.BlockSpec((tm, d), lambda i, j: (i, 0)),
        out_shape=jax.ShapeDtypeStruct((mo, d), F32),
        scratch_shapes=[pltpu.VMEM((tm, d), BF16), pltpu.VMEM((tm, d), F32)],
        compiler_params=_cparams(2),
        name="mlp",
    )(x_in, mod, g, w1, w2, g_final)


def _blockdiag2(w):
    _, r, c = w.shape
    z = jnp.zeros((r, c), w.dtype)
    out = jnp.concatenate([jnp.concatenate([w[0], z], axis=1), jnp.concatenate([z, w[1]], axis=1)], axis=0)
    return jnp.pad(out, ((0, 128 - 2 * r), (0, 0))).astype(BF16)


def _permute_w_in(w, d_model):
    c = d_model // N_BRANCH
    o = [0]
    for wd in (3 * c, 64 * 2, 64 * 2, 128, c, 2 * c, c, c, c, c, 32, c, 2 * c, N_BRANCH * d_model):
        o.append(o[-1] + wd)
    seg = lambda k: w[:, o[k]:o[k + 1]]
    pad = jnp.zeros((w.shape[0], 512 - 128 * 3 - 32), w.dtype)
    parts = [seg(13), seg(0), seg(5), seg(12), seg(4), seg(6), seg(7), seg(8), seg(9), seg(11),
             seg(1), seg(2), seg(3), seg(10), pad]
    out = jnp.concatenate(parts, axis=1).astype(BF16)
    assert out.shape[1] == N_GATE + N_MIX
    return out


def kernel(x, c, ctx, c_ctx, w_ada, b_ada, g_norm1, g_norm2, g_final, w_in, rw_conv, rw_w0, rw_w2,
           rw_a0, rw_a2, rw_g2, rw_kk, rw_ka, rw_rk, rw_ln_w, rw_ln_b, hg_gamma, hg_norm, gla_gw,
           gla_gb, gla_norm, sgu_ln_w, sgu_ln_b, sgu_w, sgu_b, w_branch, w_out, w_mlp1, w_mlp2):
    n_batch, seq, d_model = x.shape
    ctx_len = ctx.shape[1]
    depth = w_in.shape[0]
    cw = d_model // N_BRANCH
    assert cw == 512 and d_model == 2048, "column layout constants assume D_MODEL = 2048"
    m_ctx = n_batch * ctx_len

    tm = _pick((1024, 512, 256, 128), m_ctx, seq)
    tb = _pick((256, 128), ctx_len, seq)
    nctx, bpb = m_ctx // tm, seq // tm
    ctxb, latb = ctx_len // tb, seq // tb
    seqs = dict(tb=tb, n_batch=n_batch, ctxb=ctxb, latb=latb)

    x_parts = (ctx.reshape(m_ctx, d_model), x.reshape(n_batch * seq, d_model), 0)
    c_rows = jnp.concatenate([c_ctx[None, :], c, jnp.zeros((7 - n_batch, d_model), F32)], axis=0)
    mod_all = _ada(c_rows, w_ada, b_ada).reshape(depth, 8, 6, d_model)

    head_ones = (jnp.arange(cw)[:, None] // RW_HEAD == jnp.arange(cw)[None, :] // RW_HEAD).astype(BF16)
    row = lambda a: a.reshape(1, -1)

    for l in range(depth):
        last = l == depth - 1
        mod = mod_all[l]
        p_gate, p_all = _inproj(x_parts, mod, row(g_norm1[l]), _permute_w_in(w_in[l], d_model),
                                tm=tm, nctx=nctx, bpb=bpb)

        prm = dict(conv=rw_conv[l], w0=row(rw_w0[l]), w2=_blockdiag2(rw_w2[l]), a0=row(rw_a0[l]),
                   a2=_blockdiag2(rw_a2[l]), g2=rw_g2[l].astype(BF16), kk=row(rw_kk[l]),
                   ka=row(rw_ka[l]), rk=row(rw_rk[l]), e=head_ones)
        r, v, nkk, g, bonus, w, kka, km = _rw_prep(p_all, prm, tb=tb, nctx=m_ctx // tb,
                                                    ctx_bps=ctxb, lat_bps=latb)
        ys = _rw_scan(r, v, nkk, w, kka, km, **seqs)

        ob = _hgrn2(p_all, hg_gamma, layer=l, **seqs)

        gw = _blockdiag2(gla_gw[l])
        gb = row(gla_gb[l])
        oc = _gla(p_all, gw, gb, **seqs)

        bs = jnp.broadcast_to(sgu_b[l][:, :, None], sgu_w[l].shape)
        yd = _sgu(p_all, row(sgu_ln_w[l]), row(sgu_ln_b[l]), sgu_w[l].astype(BF16), bs,
                  rb=_pick((512, 256, 128), m_ctx, seq))

        row_off = nctx if last else 0
        vecs = (row(rw_ln_w[l]), row(rw_ln_b[l]), row(hg_norm[l]), row(gla_norm[l]))
        mm = _merge((ys[0], ys[1], g, bonus), ob, oc, yd, p_gate, p_all, vecs, head_ones,
                    w_branch[l].astype(BF16), tm=tm // 2, row_off=2 * row_off)
        x_mid = _outproj(mm, w_out[l].astype(BF16), x_parts, mod, tm=tm, row_off=row_off, nctx=nctx, bpb=bpb)
        x_all = _mlp(x_mid, mod, row(g_norm2[l]), w_mlp1[l].astype(BF16), w_mlp2[l].astype(BF16),
                     row(g_final), tm=tm // 2, row_off=2 * row_off, nctx=2 * nctx, bpb=2 * bpb,
                     final_norm=last)
        x_parts = (x_all, x_all, nctx)
    return x_all.reshape(n_batch, seq, d_model)
```

```python
import functools

import jax
import jax.numpy as jnp
from jax import lax
from jax.experimental import pallas as pl
from jax.experimental.pallas import tpu as pltpu

F32 = jnp.float32
BF16 = jnp.bfloat16

N_BRANCH = 4
RW_HEAD = 64
RW_LN_EPS = 64e-5
HG_DK = 128
GLA_HEADS = 4
GLA_GATE_NORM = 16.0
LA_CHUNK = 16
SGU_CHUNK = 128
SGU_GROUPS = 4
EPS = 1e-6

C_RKV = 0
C_HGF = 1536
C_SGU = 2560
C_HGQ = 3584
C_HGI = 4096
C_HGG = 4608
C_GLQK = 5120
C_GLV = 5632
C_GLG = 6144
C_SMALL = 6656
N_MIX = 7168
N_GATE = 8192

VMEM_LIMIT = 58 * 1024 * 1024


def _cparams(n_axes):
    return pltpu.CompilerParams(dimension_semantics=("arbitrary",) * n_axes,
                                vmem_limit_bytes=VMEM_LIMIT)


def _pick(n_list, *dims):
    for n in n_list:
        if all(d % n == 0 for d in dims):
            return n
    raise ValueError(f"no block size in {n_list} divides {dims}")


def _row_group(i, nctx, bpb):
    return jnp.where(i < nctx, 0, 1 + (i - nctx) // bpb)


def _dot(a, b):
    return jnp.dot(a, b, preferred_element_type=F32)


def _dot_nt(a, b):
    return lax.dot_general(a, b, (((1,), (1,)), ((), ())), preferred_element_type=F32)


def _dot_tn(a, b):
    return lax.dot_general(a, b, (((0,), (0,)), ((), ())), preferred_element_type=F32)


def _split_dot(x, e):
    hi = x.astype(BF16)
    lo = (x - hi.astype(F32)).astype(BF16)
    return _dot(hi, e) + _dot(lo, e)


def _split3_dot(e, x):
    p1 = x.astype(BF16)
    r1 = x - p1.astype(F32)
    p2 = r1.astype(BF16)
    p3 = (r1 - p2.astype(F32)).astype(BF16)
    return _dot(e, p1) + _dot(e, p2) + _dot(e, p3)


def _log_sigmoid(x):
    return jnp.minimum(x, 0.0) - jnp.log(1.0 + jnp.exp(-jnp.abs(x)))


def _sigmoid(x):
    return 0.5 * jnp.tanh(0.5 * x) + 0.5


def _ada_kernel(c_ref, w_ref, b_ref, o_ref):
    c = c_ref[...]
    act = c * _sigmoid(c)
    o_ref[0] = _dot(act.astype(BF16), w_ref[0].astype(BF16)) + b_ref[0]


def _ada(c_rows, w_ada, b_ada):
    n_layers, d, n = w_ada.shape
    tn = _pick((1024, 512, 256, 128), n)
    return pl.pallas_call(
        _ada_kernel,
        grid=(n_layers, n // tn),
        in_specs=[pl.BlockSpec((8, d), lambda l, j: (0, 0)),
                  pl.BlockSpec((1, d, tn), lambda l, j: (l, 0, j)),
                  pl.BlockSpec((1, 1, tn), lambda l, j: (l, 0, j))],
        out_specs=pl.BlockSpec((1, 8, tn), lambda l, j: (l, 0, j)),
        out_shape=jax.ShapeDtypeStruct((n_layers, 8, n), F32),
        compiler_params=_cparams(2),
        name="ada_mod",
    )(c_rows, w_ada, b_ada.reshape(n_layers, 1, n))


def _norm_mod(x, g, shift, scale):
    y = x * lax.rsqrt(jnp.mean(x * x, axis=-1, keepdims=True) + EPS) * g
    return y * (1.0 + scale) + shift


def _two_part_specs(block, nctx, lat_off, col):
    return [pl.BlockSpec(block, lambda i, j: (jnp.minimum(i, nctx - 1), col(j)), pipeline_mode=pl.Buffered(1)),
            pl.BlockSpec(block, lambda i, j: (lat_off + jnp.maximum(i - nctx, 0), col(j)))]


def _inproj_kernel(xc_ref, xl_ref, mod_ref, g_ref, w_ref, og_ref, om_ref, h_ref, *, nctx, n_gate_tiles):
    i, j = pl.program_id(0), pl.program_id(1)

    @pl.when(j == 0)
    def _():
        x = jnp.where(i < nctx, xc_ref[...], xl_ref[...])
        h = _norm_mod(x, g_ref[...], mod_ref[0, 0:1, :], mod_ref[0, 1:2, :])
        h_ref[...] = h.astype(BF16)

    @pl.when(j < n_gate_tiles)
    def _():
        og_ref[...] = _dot(h_ref[...], w_ref[...]).astype(BF16)

    @pl.when(j >= n_gate_tiles)
    def _():
        om_ref[...] = _dot(h_ref[...], w_ref[...])


def _inproj(x_parts, mod, g, w, *, tm, nctx, bpb):
    xc, xl, lat_off = x_parts
    d = xc.shape[1]
    m = (nctx + (xl.shape[0] // tm - lat_off)) * tm
    tn = _pick((1024, 512), N_GATE, N_MIX)
    ng = N_GATE // tn
    return pl.pallas_call(
        functools.partial(_inproj_kernel, nctx=nctx, n_gate_tiles=ng),
        grid=(m // tm, (N_GATE + N_MIX) // tn),
        in_specs=_two_part_specs((tm, d), nctx, lat_off, lambda j: 0) + [
            pl.BlockSpec((1, 6, d), lambda i, j: (_row_group(i, nctx, bpb), 0, 0)),
            pl.BlockSpec((1, d), lambda i, j: (0, 0)),
            pl.BlockSpec((d, tn), lambda i, j: (0, j))],
        out_specs=[pl.BlockSpec((tm, tn), lambda i, j: (i, jnp.minimum(j, ng - 1))),
                   pl.BlockSpec((tm, tn), lambda i, j: (i, jnp.maximum(j - ng, 0)))],
        out_shape=[jax.ShapeDtypeStruct((m, N_GATE), BF16), jax.ShapeDtypeStruct((m, N_MIX), F32)],
        scratch_shapes=[pltpu.VMEM((tm, d), BF16)],
        compiler_params=_cparams(2),
        name="inproj",
    )(xc, xl, mod, g, w)


def _rw_prep_kernel(rkv_ref, prev_ref, next_ref, small_ref, conv_ref, w0_ref, w2_ref, a0_ref,
                    a2_ref, g2_ref, kk_ref, ka_ref, rk_ref, e_ref,
                    r_out, v_out, nkk_out, g_out, bonus_out, w_out, kka_out, km_out,
                    *, tb, nctx, ctx_bps, lat_bps):
    i = pl.program_id(0)
    c = r_out.shape[-1]
    j = jnp.where(i < nctx, i, i - nctx)
    bps = jnp.where(i < nctx, ctx_bps, lat_bps)
    first = lax.rem(j, bps) == 0
    last = lax.rem(j, bps) == bps - 1

    blk = rkv_ref[...]
    rows = lax.broadcasted_iota(jnp.int32, (tb, 1), 0)
    prev_row = jnp.where(first, 0.0, prev_ref[7:8, :])
    next_row = jnp.where(last, 0.0, next_ref[0:1, :])
    xm1 = jnp.where(rows == 0, prev_row, pltpu.roll(blk, 1, 0))
    xp1 = jnp.where(rows == tb - 1, next_row, pltpu.roll(blk, tb - 1, 0))
    conv = conv_ref[0:1, :] * xm1 + conv_ref[1:2, :] * blk + conv_ref[2:3, :] * xp1
    r = conv[:, 0:c]
    k = conv[:, c:2 * c]
    v = conv[:, 2 * c:3 * c]

    small = small_ref[...]
    wl = small[:, 0:128]
    al = small[:, 128:256]
    gl = small[:, 256:384]
    w_pre = w0_ref[...] + _dot(jnp.tanh(wl).astype(BF16), w2_ref[...])
    softplus = jnp.maximum(-w_pre, 0.0) + jnp.log(1.0 + jnp.exp(-jnp.abs(w_pre)))
    log_decay = -jnp.exp(-softplus - 0.5)
    a = _sigmoid(a0_ref[...] + _dot(al.astype(BF16), a2_ref[...]))
    g = _dot(_sigmoid(gl).astype(BF16), g2_ref[...])

    e = e_ref[...]
    kkv = k * kk_ref[...]
    kk = kkv * lax.rsqrt(_split_dot(kkv * kkv, e) + 1e-12)
    bonus = _split_dot(r * k * rk_ref[...], e) * v

    r_out[...] = r
    v_out[...] = v
    nkk_out[...] = -kk
    g_out[...] = g
    bonus_out[...] = bonus
    for d in range(2):
        a_d = a[:, d * c:(d + 1) * c]
        w_out[d] = log_decay[:, d * c:(d + 1) * c]
        kka_out[d] = kk * a_d
        km_out[d] = k * (1.0 + (a_d - 1.0) * ka_ref[...])


def _rw_prep(p_all, prm, *, tb, nctx, ctx_bps, lat_bps):
    m = p_all.shape[0]
    c = prm["kk"].shape[-1]
    nblk = m // tb
    t8 = tb // 8
    full = lambda shape: pl.BlockSpec(shape, lambda i: (0,) * len(shape))
    tok = pl.BlockSpec((tb, c), lambda i: (i, 0))
    tok2 = pl.BlockSpec((2, tb, c), lambda i: (0, i, 0))
    one = jax.ShapeDtypeStruct((m, c), F32)
    two = jax.ShapeDtypeStruct((2, m, c), F32)
    return pl.pallas_call(
        functools.partial(_rw_prep_kernel, tb=tb, nctx=nctx, ctx_bps=ctx_bps, lat_bps=lat_bps),
        grid=(nblk,),
        in_specs=[pl.BlockSpec((tb, 3 * c), lambda i: (i, C_RKV // (3 * c))),
                  pl.BlockSpec((8, 3 * c), lambda i: (jnp.maximum(i * t8 - 1, 0), 0)),
                  pl.BlockSpec((8, 3 * c), lambda i: (jnp.minimum((i + 1) * t8, m // 8 - 1), 0)),
                  pl.BlockSpec((tb, 512), lambda i: (i, C_SMALL // 512)),
                  full((3, 3 * c)), full((1, 2 * c)), full((128, 2 * c)), full((1, 2 * c)),
                  full((128, 2 * c)), full((128, c)), full((1, c)), full((1, c)), full((1, c)),
                  full((c, c))],
        out_specs=[tok, tok, tok, tok, tok, tok2, tok2, tok2],
        out_shape=[one, one, one, one, one, two, two, two],
        compiler_params=_cparams(1),
        name="rwkv_prep",
    )(p_all, p_all, p_all, p_all, prm["conv"], prm["w0"], prm["w2"], prm["a0"], prm["a2"],
      prm["g2"], prm["kk"], prm["ka"], prm["rk"], prm["e"])


class _RwUnit:
    def __init__(self, refs, scratch, *, tb, npair, reverse):
        (self.r_ref, self.v_ref, self.a_ref, self.lw_ref, self.b_ref, self.k_ref, self.tri_ref,
         self.y_ref) = refs
        (self.s_ref, self.c_scr, self.ag_scr, self.rg_scr, self.bg_scr, self.kg_scr) = scratch[:6]
        self.slots = (scratch[6:9], scratch[9:12])
        self.reverse = reverse
        self.nch = tb // LA_CHUNK
        self.pairs = range(npair)
        L, hd = LA_CHUNK, RW_HEAD
        self.lss = [slice(p * 2 * hd, (p + 1) * 2 * hd) for p in self.pairs]
        lane = lax.broadcasted_iota(jnp.int32, (1, 2 * hd), 1)
        self.lo = (lane < hd).astype(F32)
        self.hi = 1.0 - self.lo
        ti = lax.broadcasted_iota(jnp.int32, (2 * L, 2 * hd), 0)
        sl = lax.broadcasted_iota(jnp.int32, (2 * L, 2 * hd), 1)
        sj = sl & (L - 1)
        tt = ti & (L - 1)
        earlier = (sj > tt) if reverse else (sj < tt)
        self.aa_mask = ((earlier | ((ti >= L) & (sj == tt))) & (sl < 4 * L)).astype(F32)
        self.col_idx0 = jnp.where(lax.broadcasted_iota(jnp.int32, (L, 2 * hd), 1) < hd, 0, L)
        self.blk_mask = ((lax.broadcasted_iota(jnp.int32, (2 * hd, 2 * hd), 0) >> 6)
                         == (lax.broadcasted_iota(jnp.int32, (2 * hd, 2 * hd), 1) >> 6)).astype(F32)
        self.zeros2l = jnp.zeros((2 * L, 2 * hd), F32)
        self.zeros4l = jnp.zeros((4 * L, 2 * hd), F32)
        self.last = 0 if reverse else L - 1
        self.order = range(L - 1, -1, -1) if reverse else range(L)

    def reset(self):
        self.s_ref[...] = jnp.zeros_like(self.s_ref)

    def prologue(self):
        lw = self.lw_ref[0]
        c = _split3_dot(self.tri_ref[...], lw)
        self.c_scr[...] = c
        enc = jnp.exp(-c)
        self.ag_scr[...] = self.a_ref[...] * jnp.exp(c - lw)
        self.rg_scr[...] = self.r_ref[...] * jnp.exp(c)
        self.bg_scr[...] = self.b_ref[0] * enc
        self.kg_scr[...] = self.k_ref[0] * enc

    def rows(self, ci):
        L = LA_CHUNK
        cc = (self.nch - 1 - ci) if self.reverse else ci
        return pl.ds(pl.multiple_of(cc * L, L), L)

    def halves(self, x):
        return [x * self.lo, x * self.hi]

    def lhs_of(self, rows):
        return [jnp.concatenate([self.ag_scr[rows, ls], self.rg_scr[rows, ls]], axis=0).astype(BF16)
                for ls in self.lss]

    def prep_aa(self, rows):
        lhs = self.lhs_of(rows)
        out = []
        for p in self.pairs:
            ls = self.lss[p]
            rhs = jnp.concatenate(self.halves(self.bg_scr[rows, ls]) + self.halves(self.kg_scr[rows, ls])
                                  + [self.zeros4l], axis=0)
            out.append(_dot_nt(lhs[p], rhs.astype(BF16)))
        return out

    def prep_akv(self, rows, aa_raw, slot):
        _, akv_scr, aar_scr = slot
        L = LA_CHUNK
        aa = [x * self.aa_mask for x in aa_raw]
        for p in self.pairs:
            v = self.v_ref[rows, self.lss[p]]
            vv = jnp.concatenate([self.zeros2l] + self.halves(v) + [self.zeros4l], axis=0).astype(BF16)
            akv_scr[p] = _dot(aa[p][0:L].astype(BF16), vv)
            aar_scr[p] = aa[p][L:2 * L]
        return aa

    def prep_cols(self, aa, slot):
        L = LA_CHUNK
        for p in self.pairs:
            for s in range(L):
                slot[0][p, s] = jnp.take_along_axis(aa[p][0:L], self.col_idx0 + s, axis=1)

    def adv_g(self, rows):
        lhs = self.lhs_of(rows)
        return [_dot_nt(lhs[p], self.s_ref[p].astype(BF16)) for p in self.pairs]

    def adv_solve(self, g, slot):
        L = LA_CHUNK
        col_scr, akv_scr, _ = slot
        u = [g[p][0:L] + akv_scr[p] for p in self.pairs]
        for s in self.order:
            for p in self.pairs:
                u[p] = u[p] + col_scr[p, s] * u[p][s:s + 1]
        return u

    def adv_out(self, rows, g, u, slot):
        L = LA_CHUNK
        for p in self.pairs:
            v = self.v_ref[rows, self.lss[p]]
            uv = jnp.concatenate(self.halves(u[p]) + self.halves(v) + [self.zeros4l], axis=0).astype(BF16)
            self.y_ref[rows, self.lss[p]] = g[p][L:2 * L] + _dot(slot[2][p].astype(BF16), uv)
        for p in self.pairs:
            ls = self.lss[p]
            cch = self.c_scr[rows, ls]
            cl = cch[self.last:self.last + 1]
            dec = jnp.exp(cl - cch)
            bk = jnp.concatenate([self.b_ref[0, rows, ls] * dec, self.k_ref[0, rows, ls] * dec], axis=0)
            upd = _dot_tn(jnp.concatenate([u[p], self.v_ref[rows, ls]], axis=0).astype(BF16),
                          bk.astype(BF16))
            self.s_ref[p] = self.s_ref[p] * jnp.exp(cl) + upd * self.blk_mask


N_RW_REFS = 8
N_RW_SCRATCH = 12


def _rw_scan_kernel(*refs, tb, npair):
    n_in = N_RW_REFS - 1
    units = []
    for d in range(2):
        ins = refs[d * n_in:(d + 1) * n_in]
        out = refs[2 * n_in + d]
        scr = refs[2 * n_in + 2 + d * N_RW_SCRATCH:2 * n_in + 2 + (d + 1) * N_RW_SCRATCH]
        units.append(_RwUnit(tuple(ins) + (out,), scr, tb=tb, npair=npair, reverse=d == 1))
    nch = tb // LA_CHUNK

    @pl.when(pl.program_id(1) == 0)
    def _():
        for un in units:
            un.reset()

    for un in units:
        un.prologue()

    def step(ci_adv, sa, ci_prep, sp):
        rows_a = [un.rows(ci_adv) for un in units]
        rows_p = [un.rows(ci_prep) for un in units]
        aa_raw = [un.prep_aa(rp) for un, rp in zip(units, rows_p)]
        g = [un.adv_g(ra) for un, ra in zip(units, rows_a)]
        aa = [un.prep_akv(rp, x, un.slots[sp]) for un, rp, x in zip(units, rows_p, aa_raw)]
        u = [un.adv_solve(x, un.slots[sa]) for un, x in zip(units, g)]
        for un, x in zip(units, aa):
            un.prep_cols(x, un.slots[sp])
        for un, ra, x, y in zip(units, rows_a, g, u):
            un.adv_out(ra, x, y, un.slots[sa])

    for un in units:
        r0 = un.rows(0)
        un.prep_cols(un.prep_akv(r0, un.prep_aa(r0), un.slots[0]), un.slots[0])

    def two_chunks(j, carry):
        c0 = 2 * j
        step(c0, 0, c0 + 1, 1)
        step(c0 + 1, 1, jnp.minimum(c0 + 2, nch - 1), 0)
        return carry

    lax.fori_loop(0, nch // 2, two_chunks, 0)


def _seq_block(d, b, i, *, n_batch, ctxb, latb):
    is_ctx = i < ctxb
    cs = jnp.where(d == 0, i, ctxb - 1 - i)
    lj = jnp.where(d == 0, i - ctxb, latb - 1 - (i - ctxb))
    return jnp.where(is_ctx, b * ctxb + cs, n_batch * ctxb + b * latb + lj)


def _rw_scan(r, v, nkk, lw, kka, km, *, tb, n_batch, ctxb, latb):
    m, c = r.shape
    npair = c // (2 * RW_HEAD)
    lanes = 2 * RW_HEAD
    in_specs, args = [], []
    for d in range(2):
        blk = _la_index(d, n_batch, ctxb, latb)
        tok = pl.BlockSpec((tb, c), lambda b, i, blk=blk: (blk(b, i), 0))
        tokd = pl.BlockSpec((1, tb, c), lambda b, i, blk=blk, d=d: (d, blk(b, i), 0))
        in_specs += [tok, tok, tok, tokd, tokd, tokd, pl.BlockSpec((tb, tb), lambda b, i: (0, 0))]
        args += [r, v, nkk, lw, kka, km, _chunk_tri(tb, d == 1)]
    out_specs = [pl.BlockSpec((tb, c), lambda b, i, blk=_la_index(d, n_batch, ctxb, latb): (blk(b, i), 0))
                 for d in range(2)]
    buf = pltpu.VMEM((tb, c), F32)
    unit_scratch = [pltpu.VMEM((npair, lanes, lanes), F32), buf, buf, buf, buf, buf] + 2 * [
        pltpu.VMEM((npair, LA_CHUNK, LA_CHUNK, lanes), F32),
        pltpu.VMEM((npair, LA_CHUNK, lanes), F32),
        pltpu.VMEM((npair, LA_CHUNK, lanes), F32)]
    assert len(unit_scratch) == N_RW_SCRATCH
    return pl.pallas_call(
        functools.partial(_rw_scan_kernel, tb=tb, npair=npair),
        grid=(n_batch, ctxb + latb),
        in_specs=in_specs,
        out_specs=out_specs,
        out_shape=[jax.ShapeDtypeStruct((m, c), F32)] * 2,
        scratch_shapes=unit_scratch + unit_scratch,
        compiler_params=_cparams(2),
        name="rwkv_scan",
    )(*args)


def _rw_post(y, g, bonus, lnw, lnb, e):
    inv = 1.0 / RW_HEAD
    yc = y - _split_dot(y, e) * inv
    var = _split_dot(yc * yc, e) * inv
    yn = yc * lax.rsqrt(var + RW_LN_EPS)
    return (yn * lnw + lnb + bonus) * g


class _LaUnit:
    def __init__(self, q_ref, k_ref, v_ref, b_ref, o_ref, s_ref, *, tb, nh, dk, dv, reverse, q_scale):
        self.q_ref, self.k_ref, self.v_ref, self.b_ref, self.o_ref, self.s_ref = (
            q_ref, k_ref, v_ref, b_ref, o_ref, s_ref)
        self.nch = tb // LA_CHUNK
        self.dk, self.dv, self.reverse, self.q_scale = dk, dv, reverse, q_scale
        self.lanes = 128
        self.pack = self.lanes // dk
        self.groups = range(nh // self.pack)
        self.rowi = lax.broadcasted_iota(jnp.int32, (LA_CHUNK, 1), 0)
        lane = lax.broadcasted_iota(jnp.int32, (1, self.lanes), 1)
        self.head_lanes = [(lane // dk == j).astype(F32) for j in range(self.pack)]
        if self.pack > 1:
            rows = lax.broadcasted_iota(jnp.int32, (self.pack * dv, self.lanes), 0)
            cols = lax.broadcasted_iota(jnp.int32, (self.pack * dv, self.lanes), 1)
            self.blk_mask = ((rows // dv) == (cols // dk)).astype(F32)

    def reset(self):
        self.s_ref[...] = jnp.zeros_like(self.s_ref)

    def load(self, ci):
        cc = (self.nch - 1 - ci) if self.reverse else ci
        rows = pl.ds(pl.multiple_of(cc * LA_CHUNK, LA_CHUNK), LA_CHUNK)
        pack, dv, lanes = self.pack, self.dv, self.lanes
        out = []
        for g in self.groups:
            ks = slice(g * lanes, (g + 1) * lanes)
            q = self.q_ref[rows, ks] * self.q_scale
            k = self.k_ref[rows, ks]
            v = self.v_ref[rows, g * pack * dv:(g + 1) * pack * dv]
            b = self.b_ref[rows, ks]
            o_inter = _dot_nt((q * jnp.exp(b)).astype(BF16), self.s_ref[g].astype(BF16))
            out.append((rows, q, k, v, b, [o_inter[:, j * dv:(j + 1) * dv] for j in range(pack)]))
        return out

    def intra(self, ops):
        pack, dv = self.pack, self.dv
        for g in self.groups:
            rows, q, k, v, b, o = ops[g]
            for s in range(LA_CHUNK):
                valid = (self.rowi <= s) if self.reverse else (self.rowi >= s)
                term = (q * k[s:s + 1]) * jnp.exp(b - b[s:s + 1])
                for j in range(pack):
                    tj = term if pack == 1 else term * self.head_lanes[j]
                    col = jnp.where(valid, jnp.sum(tj, axis=-1, keepdims=True), 0.0)
                    o[j] = o[j] + col * v[s:s + 1, j * dv:(j + 1) * dv]
            for j in range(pack):
                self.o_ref[rows, (g * pack + j) * dv:(g * pack + j + 1) * dv] = o[j]

    def update(self, ops):
        for g in self.groups:
            _, _, k, v, b, _ = ops[g]
            b_last = b[0:1] if self.reverse else b[LA_CHUNK - 1:LA_CHUNK]
            upd = _dot_tn(v.astype(BF16), (k * jnp.exp(b_last - b)).astype(BF16))
            if self.pack > 1:
                upd = upd * self.blk_mask
            self.s_ref[g] = self.s_ref[g] * jnp.exp(b_last) + upd


def _la_run(units, nch):
    def chunk(ci, carry):
        ops = [un.load(ci) for un in units]
        for un, x in zip(units, ops):
            un.update(x)
        for un, x in zip(units, ops):
            un.intra(x)
        return carry

    lax.fori_loop(0, nch, chunk, 0)


def _hg_kernel(qf_ref, ff_ref, vf_ref, trif_ref, qb_ref, fb_ref, vb_ref, trib_ref, gamma_ref,
               of_ref, ob_ref, sf_ref, kf_scr, bf_scr, sb_ref, kb_scr, bb_scr, *, layer, tb, nh, dk, dv):
    gam = gamma_ref[...]
    ex = jnp.exp(gam - jnp.max(gam, axis=0, keepdims=True))
    p = ex / jnp.sum(ex, axis=0, keepdims=True)
    cum = p[0:1]
    for i in range(1, layer + 1):
        cum = cum + p[i:i + 1]
    lb = cum - p[0:1]
    lo = jnp.log(lb)
    l1 = jnp.log(1.0 - lb)

    units = []
    for d, (q_ref, f_ref, v_ref, tri_ref, o_ref, s_ref, k_scr, b_scr) in enumerate((
            (qf_ref, ff_ref, vf_ref, trif_ref, of_ref, sf_ref, kf_scr, bf_scr),
            (qb_ref, fb_ref, vb_ref, trib_ref, ob_ref, sb_ref, kb_scr, bb_scr))):
        hi = l1 + _log_sigmoid(f_ref[...])
        mx = jnp.maximum(lo, hi)
        mn = jnp.minimum(lo, hi)
        log_f = mx + jnp.log(1.0 + jnp.exp(mn - mx))
        k_scr[...] = 1.0 - jnp.exp(log_f)
        b_scr[...] = _split3_dot(tri_ref[...], log_f)
        units.append(_LaUnit(q_ref, k_scr, v_ref, b_scr, o_ref, s_ref, tb=tb, nh=nh, dk=dk, dv=dv,
                             reverse=d == 1, q_scale=1.0))

    @pl.when(pl.program_id(1) == 0)
    def _():
        for un in units:
            un.reset()

    _la_run(units, tb // LA_CHUNK)


def _gla_kernel(qkf_ref, vf_ref, smallf_ref, trif_ref, qkb_ref, vb_ref, smallb_ref, trib_ref,
                gw_ref, gb_ref, of_ref, ob_ref, sf_ref, bf_scr, sb_ref, bb_scr, *, tb, nh, dk, dv):
    hk = nh * dk
    units = []
    for d, (qk_ref, v_ref, small_ref, tri_ref, o_ref, s_ref, b_scr) in enumerate((
            (qkf_ref, vf_ref, smallf_ref, trif_ref, of_ref, sf_ref, bf_scr),
            (qkb_ref, vb_ref, smallb_ref, trib_ref, ob_ref, sb_ref, bb_scr))):
        code = small_ref[:, 384:512]
        pre = _dot(code.astype(BF16), gw_ref[:, d * hk:(d + 1) * hk]) + gb_ref[:, d * hk:(d + 1) * hk]
        log_g = _log_sigmoid(pre) * (1.0 / GLA_GATE_NORM)
        b_scr[...] = _split3_dot(tri_ref[...], log_g)
        units.append(_LaUnit(qk_ref.at[:, 0:hk], qk_ref.at[:, hk:2 * hk], v_ref, b_scr, o_ref, s_ref,
                             tb=tb, nh=nh, dk=dk, dv=dv, reverse=d == 1, q_scale=dk ** -0.5))

    @pl.when(pl.program_id(1) == 0)
    def _():
        for un in units:
            un.reset()

    _la_run(units, tb // LA_CHUNK)


def _chunk_tri(tb, reverse):
    t = jnp.arange(tb)
    same = (t[:, None] // LA_CHUNK) == (t[None, :] // LA_CHUNK)
    tri = (t[None, :] >= t[:, None]) if reverse else (t[None, :] <= t[:, None])
    return (same & tri).astype(BF16)


def _la_index(d, n_batch, ctxb, latb):
    return lambda b, i: _seq_block(d, b, i, n_batch=n_batch, ctxb=ctxb, latb=latb)


def _hgrn2(p_all, gamma, *, layer, tb, n_batch, ctxb, latb):
    m = p_all.shape[0]
    c = gamma.shape[-1]
    nh = c // HG_DK
    in_specs, args, out_specs = [], [], []
    for d in range(2):
        blk = _la_index(d, n_batch, ctxb, latb)
        col = lambda off, blk=blk: pl.BlockSpec((tb, c), lambda b, i: (blk(b, i), off // c))
        in_specs += [col(C_HGQ), col(C_HGF + d * c), col(C_HGI), pl.BlockSpec((tb, tb), lambda b, i: (0, 0))]
        args += [p_all, p_all, p_all, _chunk_tri(tb, d == 1)]
        out_specs.append(col(0))
    unit_scratch = [pltpu.VMEM((nh, HG_DK, HG_DK), F32), pltpu.VMEM((tb, c), F32), pltpu.VMEM((tb, c), F32)]
    return pl.pallas_call(
        functools.partial(_hg_kernel, layer=layer, tb=tb, nh=nh, dk=HG_DK, dv=HG_DK),
        grid=(n_batch, ctxb + latb),
        in_specs=in_specs + [pl.BlockSpec(gamma.shape, lambda b, i: (0, 0))],
        out_specs=out_specs,
        out_shape=[jax.ShapeDtypeStruct((m, c), F32)] * 2,
        scratch_shapes=unit_scratch + unit_scratch,
        compiler_params=_cparams(2),
        name="hgrn2_scan",
    )(*args, gamma)


def _gla(p_all, gw, gb, *, tb, n_batch, ctxb, latb):
    m = p_all.shape[0]
    hk = gw.shape[-1] // 2
    dk = hk // GLA_HEADS
    c = 2 * hk
    dv = c // GLA_HEADS
    in_specs, args, out_specs = [], [], []
    for d in range(2):
        blk = _la_index(d, n_batch, ctxb, latb)
        col = lambda off, blk=blk: pl.BlockSpec((tb, c), lambda b, i: (blk(b, i), off // c))
        in_specs += [col(C_GLQK), col(C_GLV), col(C_SMALL), pl.BlockSpec((tb, tb), lambda b, i: (0, 0))]
        args += [p_all, p_all, p_all, _chunk_tri(tb, d == 1)]
        out_specs.append(col(0))
    unit_scratch = [pltpu.VMEM((hk // 128, (128 // dk) * dv, 128), F32), pltpu.VMEM((tb, hk), F32)]
    return pl.pallas_call(
        functools.partial(_gla_kernel, tb=tb, nh=GLA_HEADS, dk=dk, dv=dv),
        grid=(n_batch, ctxb + latb),
        in_specs=in_specs + [pl.BlockSpec(gw.shape, lambda b, i: (0, 0)),
                             pl.BlockSpec(gb.shape, lambda b, i: (0, 0))],
        out_specs=out_specs,
        out_shape=[jax.ShapeDtypeStruct((m, c), F32)] * 2,
        scratch_shapes=unit_scratch + unit_scratch,
        compiler_params=_cparams(2),
        name="gla_scan",
    )(*args, gw, gb)


def _la_post(o, gate, norm_g, nh):
    dv = o.shape[-1] // nh
    outs = []
    for h in range(nh):
        oh = o[:, h * dv:(h + 1) * dv]
        outs.append(oh * lax.rsqrt(jnp.mean(oh * oh, axis=-1, keepdims=True) + EPS))
    return jnp.concatenate(outs, axis=-1) * norm_g * (gate * _sigmoid(gate))


def _gelu(x):
    return 0.5 * x * (1.0 + jnp.tanh(0.7978845608028654 * (x + 0.044715 * (x * x * x))))


def _sgu_kernel(u_ref, v_ref, lnw_ref, lnb_ref, ws_ref, bs_ref, o_ref, *, rb):
    u = _gelu(u_ref[...])
    v = _gelu(v_ref[...])
    vc = v - jnp.mean(v, axis=-1, keepdims=True)
    vn = vc * lax.rsqrt(jnp.mean(vc * vc, axis=-1, keepdims=True) + EPS)
    vn = (vn * lnw_ref[...] + lnb_ref[...]).astype(BF16)
    gw = vn.shape[-1] // SGU_GROUPS
    for n in range(rb // SGU_CHUNK):
        rs = slice(n * SGU_CHUNK, (n + 1) * SGU_CHUNK)
        for g in range(SGU_GROUPS):
            cs = slice(g * gw, (g + 1) * gw)
            s = _dot(ws_ref[g], vn[rs, cs]) + bs_ref[g]
            o_ref[rs, cs] = u[rs, cs] * s


def _sgu(p_all, lnw, lnb, ws, bs, *, rb):
    m = p_all.shape[0]
    c = lnw.shape[-1]
    return pl.pallas_call(
        functools.partial(_sgu_kernel, rb=rb),
        grid=(m // rb,),
        in_specs=[pl.BlockSpec((rb, c), lambda i: (i, C_SGU // c)),
                  pl.BlockSpec((rb, c), lambda i: (i, C_SGU // c + 1)),
                  pl.BlockSpec((1, c), lambda i: (0, 0)),
                  pl.BlockSpec((1, c), lambda i: (0, 0)),
                  pl.BlockSpec(ws.shape, lambda i: (0, 0, 0)),
                  pl.BlockSpec(bs.shape, lambda i: (0, 0, 0))],
        out_specs=pl.BlockSpec((rb, c), lambda i: (i, 0)),
        out_shape=jax.ShapeDtypeStruct((m, c), F32),
        compiler_params=_cparams(1),
        name="sgu",
    )(p_all, p_all, lnw, lnb, ws, bs)


def _merge_kernel(raf_ref, rab_ref, rg_ref, rbonus_ref, hf_ref, hb_ref, hgate_ref, gf_ref, gb_ref,
                  ggate_ref, yd_ref, lnw_ref, lnb_ref, e_ref, hnorm_ref, gnorm_ref,
                  g0_ref, g1_ref, g2_ref, g3_ref, w_ref, o_ref, y_scr, *, hg_heads):
    @pl.when(pl.program_id(1) == 0)
    def _():
        y_scr[0] = _rw_post(raf_ref[...] + rab_ref[...], rg_ref[...], rbonus_ref[...], lnw_ref[...],
                            lnb_ref[...], e_ref[...]).astype(BF16)
        y_scr[1] = _la_post(hf_ref[...] + hb_ref[...], hgate_ref[...], hnorm_ref[...], hg_heads).astype(BF16)
        y_scr[2] = _la_post(gf_ref[...] + gb_ref[...], ggate_ref[...], gnorm_ref[...], GLA_HEADS).astype(BF16)
        y_scr[3] = yd_ref[...].astype(BF16)

    gs = (g0_ref, g1_ref, g2_ref, g3_ref)
    acc = None
    for j in range(N_BRANCH):
        t = _sigmoid(gs[j][...].astype(F32)) * _dot(y_scr[j], w_ref[j])
        acc = t if acc is None else acc + t
    o_ref[...] = acc.astype(BF16)


def _merge(rw, hg, gla, yd, p_gate, p_mix, vecs, e, w_branch, *, tm, row_off):
    m = p_mix.shape[0]
    _, c, d = w_branch.shape
    tn = 512
    mo = m - row_off * tm
    ytok = pl.BlockSpec((tm, c), lambda i, j: (i + row_off, 0))
    mix = lambda off: pl.BlockSpec((tm, c), lambda i, j: (i + row_off, off // c))
    gate = lambda b: pl.BlockSpec((tm, tn), lambda i, j: (i + row_off, b * d // tn + j))
    vec = pl.BlockSpec((1, c), lambda i, j: (0, 0))
    return pl.pallas_call(
        functools.partial(_merge_kernel, hg_heads=c // HG_DK),
        grid=(mo // tm, d // tn),
        in_specs=[ytok, ytok, ytok, ytok, ytok, ytok, mix(C_HGG), ytok, ytok, mix(C_GLG), ytok,
                  vec, vec, pl.BlockSpec((c, c), lambda i, j: (0, 0)), vec, vec,
                  gate(0), gate(1), gate(2), gate(3),
                  pl.BlockSpec((N_BRANCH, c, tn), lambda i, j: (0, 0, j))],
        out_specs=pl.BlockSpec((tm, tn), lambda i, j: (i, j)),
        out_shape=jax.ShapeDtypeStruct((mo, d), BF16),
        scratch_shapes=[pltpu.VMEM((N_BRANCH, tm, c), BF16)],
        compiler_params=_cparams(2),
        name="merge",
    )(*rw, hg[0], hg[1], p_mix, gla[0], gla[1], p_mix, yd, *vecs[:2], e, *vecs[2:],
      p_gate, p_gate, p_gate, p_gate, w_branch)


def _outproj_kernel(m_ref, w_ref, xc_ref, xl_ref, mod_ref, o_ref, *, nctx, row_off):
    x = jnp.where(pl.program_id(0) + row_off < nctx, xc_ref[...], xl_ref[...])
    o_ref[...] = x + mod_ref[0, 2:3, :] * _dot(m_ref[...], w_ref[...])


def _outproj(mm, w_out, x_parts, mod, *, tm, row_off, nctx, bpb):
    xc, xl, lat_off = x_parts
    mo, d = mm.shape
    tn = _pick((1024, 512), d)
    xspecs = [pl.BlockSpec((tm, tn), lambda i, j: (jnp.minimum(i + row_off, nctx - 1),
                                                   jnp.where(i + row_off < nctx, j, 0))),
              pl.BlockSpec((tm, tn), lambda i, j: (lat_off + jnp.maximum(i + row_off - nctx, 0),
                                                   jnp.where(i + row_off < nctx, 0, j)))]
    return pl.pallas_call(
        functools.partial(_outproj_kernel, nctx=nctx, row_off=row_off),
        grid=(mo // tm, d // tn),
        in_specs=[pl.BlockSpec((tm, d), lambda i, j: (i, 0)),
                  pl.BlockSpec((d, tn), lambda i, j: (0, j))] + xspecs + [
                  pl.BlockSpec((1, 6, tn), lambda i, j: (_row_group(i + row_off, nctx, bpb), 0, j))],
        out_specs=pl.BlockSpec((tm, tn), lambda i, j: (i, j)),
        out_shape=jax.ShapeDtypeStruct((mo, d), F32),
        compiler_params=_cparams(2),
        name="outproj",
    )(mm, w_out, xc, xl, mod)


def _mlp_kernel(x_ref, mod_ref, g_ref, w1_ref, w2_ref, gf_ref, o_ref, h_ref, acc_ref, *, final_norm):
    j = pl.program_id(1)

    @pl.when(j == 0)
    def _():
        h = _norm_mod(x_ref[...], g_ref[...], mod_ref[0, 3:4, :], mod_ref[0, 4:5, :])
        h_ref[...] = h.astype(BF16)
        acc_ref[...] = jnp.zeros_like(acc_ref)

    a = jnp.maximum(_dot(h_ref[...], w1_ref[...]), 0.0)
    acc_ref[...] += _dot((a * a).astype(BF16), w2_ref[...])

    @pl.when(j == pl.num_programs(1) - 1)
    def _():
        y = x_ref[...] + mod_ref[0, 5:6, :] * acc_ref[...]
        if final_norm:
            y = y * lax.rsqrt(jnp.mean(y * y, axis=-1, keepdims=True) + EPS) * gf_ref[...]
        o_ref[...] = y


def _mlp(x_in, mod, g, w1, w2, g_final, *, tm, row_off, nctx, bpb, final_norm):
    mo, d = x_in.shape
    hid = w1.shape[1]
    th = _pick((512, 256, 128), hid)
    return pl.pallas_call(
        functools.partial(_mlp_kernel, final_norm=final_norm),
        grid=(mo // tm, hid // th),
        in_specs=[pl.BlockSpec((tm, d), lambda i, j: (i, 0)),
                  pl.BlockSpec((1, 6, d), lambda i, j: (_row_group(i + row_off, nctx, bpb), 0, 0)),
                  pl.BlockSpec((1, d), lambda i, j: (0, 0)),
                  pl.BlockSpec((d, th), lambda i, j: (0, j)),
                  pl.BlockSpec((th, d), lambda i, j: (j, 0)),
                  pl.BlockSpec((1, d), lambda i, j: (0, 0))],
        out_specs=pl.BlockSpec((tm, d), lambda i, j: (i, 0)),
        out_shape=jax.ShapeDtypeStruct((mo, d), F32),
        scratch_shapes=[pltpu.VMEM((tm, d), BF16), pltpu.VMEM((tm, d), F32)],
        compiler_params=_cparams(2),
        name="mlp",
    )(x_in, mod, g, w1, w2, g_final)


def _blockdiag2(w):
    _, r, c = w.shape
    z = jnp.zeros((r, c), w.dtype)
    out = jnp.concatenate([jnp.concatenate([w[0], z], axis=1), jnp.concatenate([z, w[1]], axis=1)], axis=0)
    return jnp.pad(out, ((0, 128 - 2 * r), (0, 0))).astype(BF16)


def _permute_w_in(w, d_model):
    c = d_model // N_BRANCH
    o = [0]
    for wd in (3 * c, 64 * 2, 64 * 2, 128, c, 2 * c, c, c, c, c, 32, c, 2 * c, N_BRANCH * d_model):
        o.append(o[-1] + wd)
    seg = lambda k: w[:, o[k]:o[k + 1]]
    pad = jnp.zeros((w.shape[0], 512 - 128 * 3 - 32), w.dtype)
    parts = [seg(13), seg(0), seg(5), seg(12), seg(4), seg(6), seg(7), seg(8), seg(9), seg(11),
             seg(1), seg(2), seg(3), seg(10), pad]
    out = jnp.concatenate(parts, axis=1).astype(BF16)
    assert out.shape[1] == N_GATE + N_MIX
    return out


def kernel(x, c, ctx, c_ctx, w_ada, b_ada, g_norm1, g_norm2, g_final, w_in, rw_conv, rw_w0, rw_w2,
           rw_a0, rw_a2, rw_g2, rw_kk, rw_ka, rw_rk, rw_ln_w, rw_ln_b, hg_gamma, hg_norm, gla_gw,
           gla_gb, gla_norm, sgu_ln_w, sgu_ln_b, sgu_w, sgu_b, w_branch, w_out, w_mlp1, w_mlp2):
    n_batch, seq, d_model = x.shape
    ctx_len = ctx.shape[1]
    depth = w_in.shape[0]
    cw = d_model // N_BRANCH
    assert cw == 512 and d_model == 2048, "column layout constants assume D_MODEL = 2048"
    m_ctx = n_batch * ctx_len

    tm = _pick((1024, 512, 256, 128), m_ctx, seq)
    tb = _pick((256, 128), ctx_len, seq)
    nctx, bpb = m_ctx // tm, seq // tm
    ctxb, latb = ctx_len // tb, seq // tb
    seqs = dict(tb=tb, n_batch=n_batch, ctxb=ctxb, latb=latb)

    x_parts = (ctx.reshape(m_ctx, d_model), x.reshape(n_batch * seq, d_model), 0)
    c_rows = jnp.concatenate([c_ctx[None, :], c, jnp.zeros((7 - n_batch, d_model), F32)], axis=0)
    mod_all = _ada(c_rows, w_ada, b_ada).reshape(depth, 8, 6, d_model)

    head_ones = (jnp.arange(cw)[:, None] // RW_HEAD == jnp.arange(cw)[None, :] // RW_HEAD).astype(BF16)
    row = lambda a: a.reshape(1, -1)

    for l in range(depth):
        last = l == depth - 1
        mod = mod_all[l]
        p_gate, p_all = _inproj(x_parts, mod, row(g_norm1[l]), _permute_w_in(w_in[l], d_model),
                                tm=tm, nctx=nctx, bpb=bpb)

        prm = dict(conv=rw_conv[l], w0=row(rw_w0[l]), w2=_blockdiag2(rw_w2[l]), a0=row(rw_a0[l]),
                   a2=_blockdiag2(rw_a2[l]), g2=rw_g2[l].astype(BF16), kk=row(rw_kk[l]),
                   ka=row(rw_ka[l]), rk=row(rw_rk[l]), e=head_ones)
        r, v, nkk, g, bonus, w, kka, km = _rw_prep(p_all, prm, tb=tb, nctx=m_ctx // tb,
                                                    ctx_bps=ctxb, lat_bps=latb)
        ys = _rw_scan(r, v, nkk, w, kka, km, **seqs)

        ob = _hgrn2(p_all, hg_gamma, layer=l, **seqs)

        gw = _blockdiag2(gla_gw[l])
        gb = row(gla_gb[l])
        oc = _gla(p_all, gw, gb, **seqs)

        bs = jnp.broadcast_to(sgu_b[l][:, :, None], sgu_w[l].shape)
        yd = _sgu(p_all, row(sgu_ln_w[l]), row(sgu_ln_b[l]), sgu_w[l].astype(BF16), bs,
                  rb=_pick((512, 256, 128), m_ctx, seq))

        row_off = nctx if last else 0
        vecs = (row(rw_ln_w[l]), row(rw_ln_b[l]), row(hg_norm[l]), row(gla_norm[l]))
        mm = _merge((ys[0], ys[1], g, bonus), ob, oc, yd, p_gate, p_all, vecs, head_ones,
                    w_branch[l].astype(BF16), tm=tm // 2, row_off=2 * row_off)
        x_mid = _outproj(mm, w_out[l].astype(BF16), x_parts, mod, tm=tm, row_off=row_off, nctx=nctx, bpb=bpb)
        x_all = _mlp(x_mid, mod, row(g_norm2[l]), w_mlp1[l].astype(BF16), w_mlp2[l].astype(BF16),
                     row(g_final), tm=tm // 2, row_off=2 * row_off, nctx=2 * nctx, bpb=2 * bpb,
                     final_norm=last)
        x_parts = (x_all, x_all, nctx)
    return x_all.reshape(n_batch, seq, d_model)
```

```python
import functools

import jax
import jax.numpy as jnp
from jax import lax
from jax.experimental import pallas as pl
from jax.experimental.pallas import tpu as pltpu

F32 = jnp.float32
BF16 = jnp.bfloat16

N_BRANCH = 4
RW_HEAD = 64
RW_LN_EPS = 64e-5
HG_DK = 128
GLA_HEADS = 4
GLA_GATE_NORM = 16.0
LA_CHUNK = 16
RW_CHUNK = 32
RW_SUB = 16
SGU_CHUNK = 128
SGU_GROUPS = 4
EPS = 1e-6

C_RKV = 0
C_HGF = 1536
C_SGU = 2560
C_HGQ = 3584
C_HGI = 4096
C_HGG = 4608
C_GLQK = 5120
C_GLV = 5632
C_GLG = 6144
C_SMALL = 6656
N_MIX = 7168
N_GATE = 8192

VMEM_LIMIT = 58 * 1024 * 1024


def _cparams(n_axes):
    return pltpu.CompilerParams(dimension_semantics=("arbitrary",) * n_axes,
                                vmem_limit_bytes=VMEM_LIMIT)


def _pick(n_list, *dims):
    for n in n_list:
        if all(d % n == 0 for d in dims):
            return n
    raise ValueError(f"no block size in {n_list} divides {dims}")


def _row_group(i, nctx, bpb):
    return jnp.where(i < nctx, 0, 1 + (i - nctx) // bpb)


def _dot(a, b):
    return jnp.dot(a, b, preferred_element_type=F32)


def _dot_nt(a, b):
    return lax.dot_general(a, b, (((1,), (1,)), ((), ())), preferred_element_type=F32)


def _dot_tn(a, b):
    return lax.dot_general(a, b, (((0,), (0,)), ((), ())), preferred_element_type=F32)


def _split_dot(x, e):
    hi = x.astype(BF16)
    lo = (x - hi.astype(F32)).astype(BF16)
    return _dot(hi, e) + _dot(lo, e)


def _split3_dot(e, x):
    p1 = x.astype(BF16)
    r1 = x - p1.astype(F32)
    p2 = r1.astype(BF16)
    p3 = (r1 - p2.astype(F32)).astype(BF16)
    return _dot(e, p1) + _dot(e, p2) + _dot(e, p3)


def _log_sigmoid(x):
    return jnp.minimum(x, 0.0) - jnp.log(1.0 + jnp.exp(-jnp.abs(x)))


def _sigmoid(x):
    return 0.5 * jnp.tanh(0.5 * x) + 0.5


def _ada_kernel(c_ref, w_ref, b_ref, o_ref):
    c = c_ref[...]
    act = c * _sigmoid(c)
    o_ref[0] = _dot(act.astype(BF16), w_ref[0].astype(BF16)) + b_ref[0]


def _ada(c_rows, w_ada, b_ada):
    n_layers, d, n = w_ada.shape
    tn = _pick((1024, 512, 256, 128), n)
    return pl.pallas_call(
        _ada_kernel,
        grid=(n_layers, n // tn),
        in_specs=[pl.BlockSpec((8, d), lambda l, j: (0, 0)),
                  pl.BlockSpec((1, d, tn), lambda l, j: (l, 0, j)),
                  pl.BlockSpec((1, 1, tn), lambda l, j: (l, 0, j))],
        out_specs=pl.BlockSpec((1, 8, tn), lambda l, j: (l, 0, j)),
        out_shape=jax.ShapeDtypeStruct((n_layers, 8, n), F32),
        compiler_params=_cparams(2),
        name="ada_mod",
    )(c_rows, w_ada, b_ada.reshape(n_layers, 1, n))


def _norm_mod(x, g, shift, scale):
    y = x * lax.rsqrt(jnp.mean(x * x, axis=-1, keepdims=True) + EPS) * g
    return y * (1.0 + scale) + shift


def _two_part_specs(block, nctx, lat_off, col):
    return [pl.BlockSpec(block, lambda i, j: (jnp.minimum(i, nctx - 1), col(j)), pipeline_mode=pl.Buffered(1)),
            pl.BlockSpec(block, lambda i, j: (lat_off + jnp.maximum(i - nctx, 0), col(j)))]


def _inproj_kernel(xc_ref, xl_ref, mod_ref, g_ref, w_ref, og_ref, om_ref, h_ref, *, nctx, n_gate_tiles):
    i, j = pl.program_id(0), pl.program_id(1)

    @pl.when(j == 0)
    def _():
        x = jnp.where(i < nctx, xc_ref[...], xl_ref[...])
        h = _norm_mod(x, g_ref[...], mod_ref[0, 0:1, :], mod_ref[0, 1:2, :])
        h_ref[...] = h.astype(BF16)

    @pl.when(j < n_gate_tiles)
    def _():
        og_ref[...] = _dot(h_ref[...], w_ref[...]).astype(BF16)

    @pl.when(j >= n_gate_tiles)
    def _():
        om_ref[...] = _dot(h_ref[...], w_ref[...])


def _inproj(x_parts, mod, g, w, *, tm, nctx, bpb):
    xc, xl, lat_off = x_parts
    d = xc.shape[1]
    m = (nctx + (xl.shape[0] // tm - lat_off)) * tm
    tn = _pick((1024, 512), N_GATE, N_MIX)
    ng = N_GATE // tn
    return pl.pallas_call(
        functools.partial(_inproj_kernel, nctx=nctx, n_gate_tiles=ng),
        grid=(m // tm, (N_GATE + N_MIX) // tn),
        in_specs=_two_part_specs((tm, d), nctx, lat_off, lambda j: 0) + [
            pl.BlockSpec((1, 6, d), lambda i, j: (_row_group(i, nctx, bpb), 0, 0)),
            pl.BlockSpec((1, d), lambda i, j: (0, 0)),
            pl.BlockSpec((d, tn), lambda i, j: (0, j))],
        out_specs=[pl.BlockSpec((tm, tn), lambda i, j: (i, jnp.minimum(j, ng - 1))),
                   pl.BlockSpec((tm, tn), lambda i, j: (i, jnp.maximum(j - ng, 0)))],
        out_shape=[jax.ShapeDtypeStruct((m, N_GATE), BF16), jax.ShapeDtypeStruct((m, N_MIX), F32)],
        scratch_shapes=[pltpu.VMEM((tm, d), BF16)],
        compiler_params=_cparams(2),
        name="inproj",
    )(xc, xl, mod, g, w)


def _rw_prep_kernel(rkv_ref, prev_ref, next_ref, small_ref, conv_ref, w0_ref, w2_ref, a0_ref,
                    a2_ref, g2_ref, kk_ref, ka_ref, rk_ref, e_ref,
                    r_out, v_out, nkk_out, g_out, bonus_out, w_out, kka_out, km_out,
                    *, tb, nctx, ctx_bps, lat_bps):
    i = pl.program_id(0)
    c = r_out.shape[-1]
    j = jnp.where(i < nctx, i, i - nctx)
    bps = jnp.where(i < nctx, ctx_bps, lat_bps)
    first = lax.rem(j, bps) == 0
    last = lax.rem(j, bps) == bps - 1

    blk = rkv_ref[...]
    rows = lax.broadcasted_iota(jnp.int32, (tb, 1), 0)
    prev_row = jnp.where(first, 0.0, prev_ref[7:8, :])
    next_row = jnp.where(last, 0.0, next_ref[0:1, :])
    xm1 = jnp.where(rows == 0, prev_row, pltpu.roll(blk, 1, 0))
    xp1 = jnp.where(rows == tb - 1, next_row, pltpu.roll(blk, tb - 1, 0))
    conv = conv_ref[0:1, :] * xm1 + conv_ref[1:2, :] * blk + conv_ref[2:3, :] * xp1
    r = conv[:, 0:c]
    k = conv[:, c:2 * c]
    v = conv[:, 2 * c:3 * c]

    small = small_ref[...]
    wl = small[:, 0:128]
    al = small[:, 128:256]
    gl = small[:, 256:384]
    w_pre = w0_ref[...] + _dot(jnp.tanh(wl).astype(BF16), w2_ref[...])
    softplus = jnp.maximum(-w_pre, 0.0) + jnp.log(1.0 + jnp.exp(-jnp.abs(w_pre)))
    log_decay = -jnp.exp(-softplus - 0.5)
    a = _sigmoid(a0_ref[...] + _dot(al.astype(BF16), a2_ref[...]))
    g = _dot(_sigmoid(gl).astype(BF16), g2_ref[...])

    e = e_ref[...]
    kkv = k * kk_ref[...]
    kk = kkv * lax.rsqrt(_split_dot(kkv * kkv, e) + 1e-12)
    bonus = _split_dot(r * k * rk_ref[...], e) * v

    r_out[...] = r
    v_out[...] = v
    nkk_out[...] = -kk
    g_out[...] = g
    bonus_out[...] = bonus
    for d in range(2):
        a_d = a[:, d * c:(d + 1) * c]
        w_out[d] = log_decay[:, d * c:(d + 1) * c]
        kka_out[d] = kk * a_d
        km_out[d] = k * (1.0 + (a_d - 1.0) * ka_ref[...])


def _rw_prep(p_all, prm, *, tb, nctx, ctx_bps, lat_bps):
    m = p_all.shape[0]
    c = prm["kk"].shape[-1]
    nblk = m // tb
    t8 = tb // 8
    full = lambda shape: pl.BlockSpec(shape, lambda i: (0,) * len(shape))
    tok = pl.BlockSpec((tb, c), lambda i: (i, 0))
    tok2 = pl.BlockSpec((2, tb, c), lambda i: (0, i, 0))
    one = jax.ShapeDtypeStruct((m, c), F32)
    two = jax.ShapeDtypeStruct((2, m, c), F32)
    return pl.pallas_call(
        functools.partial(_rw_prep_kernel, tb=tb, nctx=nctx, ctx_bps=ctx_bps, lat_bps=lat_bps),
        grid=(nblk,),
        in_specs=[pl.BlockSpec((tb, 3 * c), lambda i: (i, C_RKV // (3 * c))),
                  pl.BlockSpec((8, 3 * c), lambda i: (jnp.maximum(i * t8 - 1, 0), 0)),
                  pl.BlockSpec((8, 3 * c), lambda i: (jnp.minimum((i + 1) * t8, m // 8 - 1), 0)),
                  pl.BlockSpec((tb, 512), lambda i: (i, C_SMALL // 512)),
                  full((3, 3 * c)), full((1, 2 * c)), full((128, 2 * c)), full((1, 2 * c)),
                  full((128, 2 * c)), full((128, c)), full((1, c)), full((1, c)), full((1, c)),
                  full((c, c))],
        out_specs=[tok, tok, tok, tok, tok, tok2, tok2, tok2],
        out_shape=[one, one, one, one, one, two, two, two],
        compiler_params=_cparams(1),
        name="rwkv_prep",
    )(p_all, p_all, p_all, p_all, prm["conv"], prm["w0"], prm["w2"], prm["a0"], prm["a2"],
      prm["g2"], prm["kk"], prm["ka"], prm["rk"], prm["e"])


class _RwUnit:
    def __init__(self, refs, scratch, *, tb, npair, reverse):
        (self.r_ref, self.v_ref, self.a_ref, self.lw_ref, self.b_ref, self.k_ref, self.tri_ref,
         self.y_ref) = refs
        (self.s_ref, self.c_scr, self.ag_scr, self.rg_scr, self.bg_scr, self.kg_scr) = scratch[:6]
        self.slots = (scratch[6:10], scratch[10:14])
        self.reverse = reverse
        self.nch = tb // RW_CHUNK
        self.pairs = range(npair)
        L, hd = RW_CHUNK, RW_HEAD
        self.lss = [slice(p * 2 * hd, (p + 1) * 2 * hd) for p in self.pairs]
        lane = lax.broadcasted_iota(jnp.int32, (1, 2 * hd), 1)
        self.lo = (lane < hd).astype(F32)
        self.hi = 1.0 - self.lo
        ti = lax.broadcasted_iota(jnp.int32, (2 * L, 2 * hd), 0)
        sl = lax.broadcasted_iota(jnp.int32, (2 * L, 2 * hd), 1)
        sj = sl & (L - 1)
        tt = ti & (L - 1)
        earlier = (sj > tt) if reverse else (sj < tt)
        assert 4 * L == 2 * hd and L == 2 * RW_SUB
        self.aa_mask = (earlier | ((ti >= L) & (sj == tt))).astype(F32)
        self.col_idx0 = jnp.where(lax.broadcasted_iota(jnp.int32, (RW_SUB, 2 * hd), 1) < hd, 0, L)
        self.blk_mask = ((lax.broadcasted_iota(jnp.int32, (2 * hd, 2 * hd), 0) >> 6)
                         == (lax.broadcasted_iota(jnp.int32, (2 * hd, 2 * hd), 1) >> 6)).astype(F32)
        self.zeros2l = jnp.zeros((2 * L, 2 * hd), F32)
        self.zeros_sub = jnp.zeros((RW_SUB, 2 * hd), F32)
        self.blocks = (1, 0) if reverse else (0, 1)
        self.last = 0 if reverse else L - 1
        self.order = range(L - 1, -1, -1) if reverse else range(L)

    def reset(self):
        self.s_ref[...] = jnp.zeros_like(self.s_ref)

    def prologue(self):
        lw = self.lw_ref[0]
        c = _split3_dot(self.tri_ref[...], lw)
        self.c_scr[...] = c
        enc = jnp.exp(-c)
        self.ag_scr[...] = self.a_ref[...] * jnp.exp(c - lw)
        self.rg_scr[...] = self.r_ref[...] * jnp.exp(c)
        self.bg_scr[...] = self.b_ref[0] * enc
        self.kg_scr[...] = self.k_ref[0] * enc

    def rows(self, ci):
        L = RW_CHUNK
        cc = (self.nch - 1 - ci) if self.reverse else ci
        return pl.ds(pl.multiple_of(cc * L, L), L)

    def halves(self, x):
        return [x * self.lo, x * self.hi]

    def lhs_of(self, rows):
        return [jnp.concatenate([self.ag_scr[rows, ls], self.rg_scr[rows, ls]], axis=0).astype(BF16)
                for ls in self.lss]

    def prep_aa(self, rows):
        lhs = self.lhs_of(rows)
        out = []
        for p in self.pairs:
            ls = self.lss[p]
            rhs = jnp.concatenate(self.halves(self.bg_scr[rows, ls]) + self.halves(self.kg_scr[rows, ls]),
                                  axis=0)
            out.append(_dot_nt(lhs[p], rhs.astype(BF16)))
        return out

    def prep_akv(self, rows, aa_raw, slot):
        _, akv_scr, aar_scr, na_scr = slot
        L = RW_CHUNK
        aa = [x * self.aa_mask for x in aa_raw]
        for p in self.pairs:
            v = self.v_ref[rows, self.lss[p]]
            vv = jnp.concatenate([self.zeros2l] + self.halves(v), axis=0).astype(BF16)
            akv_scr[p] = _dot(aa[p][0:L].astype(BF16), vv)
            aar_scr[p] = aa[p][L:2 * L]
            na_scr[p] = aa[p][0:L]
        return aa

    def prep_cols(self, aa, slot):
        for p in self.pairs:
            for s in range(RW_CHUNK):
                r0 = (s // RW_SUB) * RW_SUB
                slot[0][p, s] = jnp.take_along_axis(aa[p][r0:r0 + RW_SUB], self.col_idx0 + s, axis=1)

    def adv_g(self, rows):
        lhs = self.lhs_of(rows)
        return [_dot_nt(lhs[p], self.s_ref[p].astype(BF16)) for p in self.pairs]

    def adv_solve(self, g, slot):
        L, sub = RW_CHUNK, RW_SUB
        col_scr, akv_scr, _, na_scr = slot
        first, second = self.blocks
        ub = {}
        for blk in self.blocks:
            r0 = blk * sub
            ub[blk] = [g[p][r0:r0 + sub] + akv_scr[p, r0:r0 + sub] for p in self.pairs]
            if blk == second:
                for p in self.pairs:
                    done = ub[first][p]
                    z = self.zeros_sub
                    parts = [z, z, z, z]
                    parts[first], parts[2 + first] = done * self.lo, done * self.hi
                    stack = jnp.concatenate(parts + [self.zeros2l], axis=0).astype(BF16)
                    ub[blk][p] = ub[blk][p] + _dot(na_scr[p, r0:r0 + sub].astype(BF16), stack)
            steps = [s for s in self.order if s // sub == blk]
            for s in steps:
                for p in self.pairs:
                    ub[blk][p] = ub[blk][p] + col_scr[p, s] * ub[blk][p][s - r0:s - r0 + 1]
        return [jnp.concatenate([ub[0][p], ub[1][p]], axis=0) for p in self.pairs]

    def adv_out(self, rows, g, u, slot):
        L = RW_CHUNK
        for p in self.pairs:
            v = self.v_ref[rows, self.lss[p]]
            uv = jnp.concatenate(self.halves(u[p]) + self.halves(v), axis=0).astype(BF16)
            self.y_ref[rows, self.lss[p]] = g[p][L:2 * L] + _dot(slot[2][p].astype(BF16), uv)
        for p in self.pairs:
            ls = self.lss[p]
            cch = self.c_scr[rows, ls]
            cl = cch[self.last:self.last + 1]
            dec = jnp.exp(cl - cch)
            bk = jnp.concatenate([self.b_ref[0, rows, ls] * dec, self.k_ref[0, rows, ls] * dec], axis=0)
            upd = _dot_tn(jnp.concatenate([u[p], self.v_ref[rows, ls]], axis=0).astype(BF16),
                          bk.astype(BF16))
            self.s_ref[p] = self.s_ref[p] * jnp.exp(cl) + upd * self.blk_mask


N_RW_REFS = 8
N_RW_SCRATCH = 14


def _rw_scan_kernel(*refs, tb, npair):
    n_in = N_RW_REFS - 1
    units = []
    for d in range(2):
        ins = refs[d * n_in:(d + 1) * n_in]
        out = refs[2 * n_in + d]
        scr = refs[2 * n_in + 2 + d * N_RW_SCRATCH:2 * n_in + 2 + (d + 1) * N_RW_SCRATCH]
        units.append(_RwUnit(tuple(ins) + (out,), scr, tb=tb, npair=npair, reverse=d == 1))
    nch = tb // RW_CHUNK

    @pl.when(pl.program_id(1) == 0)
    def _():
        for un in units:
            un.reset()

    for un in units:
        un.prologue()

    def step(ci_adv, sa, ci_prep, sp):
        rows_a = [un.rows(ci_adv) for un in units]
        rows_p = [un.rows(ci_prep) for un in units]
        aa_raw = [un.prep_aa(rp) for un, rp in zip(units, rows_p)]
        g = [un.adv_g(ra) for un, ra in zip(units, rows_a)]
        aa = [un.prep_akv(rp, x, un.slots[sp]) for un, rp, x in zip(units, rows_p, aa_raw)]
        u = [un.adv_solve(x, un.slots[sa]) for un, x in zip(units, g)]
        for un, x in zip(units, aa):
            un.prep_cols(x, un.slots[sp])
        for un, ra, x, y in zip(units, rows_a, g, u):
            un.adv_out(ra, x, y, un.slots[sa])

    for un in units:
        r0 = un.rows(0)
        un.prep_cols(un.prep_akv(r0, un.prep_aa(r0), un.slots[0]), un.slots[0])

    def two_chunks(j, carry):
        c0 = 2 * j
        step(c0, 0, c0 + 1, 1)
        step(c0 + 1, 1, jnp.minimum(c0 + 2, nch - 1), 0)
        return carry

    lax.fori_loop(0, nch // 2, two_chunks, 0)


def _seq_block(d, b, i, *, n_batch, ctxb, latb):
    is_ctx = i < ctxb
    cs = jnp.where(d == 0, i, ctxb - 1 - i)
    lj = jnp.where(d == 0, i - ctxb, latb - 1 - (i - ctxb))
    return jnp.where(is_ctx, b * ctxb + cs, n_batch * ctxb + b * latb + lj)


def _rw_scan(r, v, nkk, lw, kka, km, *, tb, n_batch, ctxb, latb):
    m, c = r.shape
    npair = c // (2 * RW_HEAD)
    lanes = 2 * RW_HEAD
    in_specs, args = [], []
    for d in range(2):
        blk = _la_index(d, n_batch, ctxb, latb)
        tok = pl.BlockSpec((tb, c), lambda b, i, blk=blk: (blk(b, i), 0))
        tokd = pl.BlockSpec((1, tb, c), lambda b, i, blk=blk, d=d: (d, blk(b, i), 0))
        in_specs += [tok, tok, tok, tokd, tokd, tokd, pl.BlockSpec((tb, tb), lambda b, i: (0, 0))]
        args += [r, v, nkk, lw, kka, km, _chunk_tri(tb, d == 1, RW_CHUNK)]
    out_specs = [pl.BlockSpec((tb, c), lambda b, i, blk=_la_index(d, n_batch, ctxb, latb): (blk(b, i), 0))
                 for d in range(2)]
    buf = pltpu.VMEM((tb, c), F32)
    unit_scratch = [pltpu.VMEM((npair, lanes, lanes), F32), buf, buf, buf, buf, buf] + 2 * [
        pltpu.VMEM((npair, RW_CHUNK, RW_SUB, lanes), F32),
        pltpu.VMEM((npair, RW_CHUNK, lanes), F32),
        pltpu.VMEM((npair, RW_CHUNK, lanes), F32),
        pltpu.VMEM((npair, RW_CHUNK, lanes), F32)]
    assert len(unit_scratch) == N_RW_SCRATCH
    return pl.pallas_call(
        functools.partial(_rw_scan_kernel, tb=tb, npair=npair),
        grid=(n_batch, ctxb + latb),
        in_specs=in_specs,
        out_specs=out_specs,
        out_shape=[jax.ShapeDtypeStruct((m, c), F32)] * 2,
        scratch_shapes=unit_scratch + unit_scratch,
        compiler_params=_cparams(2),
        name="rwkv_scan",
    )(*args)


def _rw_post(y, g, bonus, lnw, lnb, e):
    inv = 1.0 / RW_HEAD
    yc = y - _split_dot(y, e) * inv
    var = _split_dot(yc * yc, e) * inv
    yn = yc * lax.rsqrt(var + RW_LN_EPS)
    return (yn * lnw + lnb + bonus) * g


class _LaUnit:
    def __init__(self, q_ref, k_ref, v_ref, b_ref, o_ref, s_ref, *, tb, nh, dk, dv, reverse, q_scale):
        self.q_ref, self.k_ref, self.v_ref, self.b_ref, self.o_ref, self.s_ref = (
            q_ref, k_ref, v_ref, b_ref, o_ref, s_ref)
        self.nch = tb // LA_CHUNK
        self.dk, self.dv, self.reverse, self.q_scale = dk, dv, reverse, q_scale
        self.lanes = 128
        self.pack = self.lanes // dk
        self.groups = range(nh // self.pack)
        self.rowi = lax.broadcasted_iota(jnp.int32, (LA_CHUNK, 1), 0)
        lane = lax.broadcasted_iota(jnp.int32, (1, self.lanes), 1)
        self.head_lanes = [(lane // dk == j).astype(F32) for j in range(self.pack)]
        if self.pack > 1:
            rows = lax.broadcasted_iota(jnp.int32, (self.pack * dv, self.lanes), 0)
            cols = lax.broadcasted_iota(jnp.int32, (self.pack * dv, self.lanes), 1)
            self.blk_mask = ((rows // dv) == (cols // dk)).astype(F32)

    def reset(self):
        self.s_ref[...] = jnp.zeros_like(self.s_ref)

    def load(self, ci):
        cc = (self.nch - 1 - ci) if self.reverse else ci
        rows = pl.ds(pl.multiple_of(cc * LA_CHUNK, LA_CHUNK), LA_CHUNK)
        pack, dv, lanes = self.pack, self.dv, self.lanes
        out = []
        for g in self.groups:
            ks = slice(g * lanes, (g + 1) * lanes)
            q = self.q_ref[rows, ks] * self.q_scale
            k = self.k_ref[rows, ks]
            v = self.v_ref[rows, g * pack * dv:(g + 1) * pack * dv]
            b = self.b_ref[rows, ks]
            o_inter = _dot_nt((q * jnp.exp(b)).astype(BF16), self.s_ref[g].astype(BF16))
            out.append((rows, q, k, v, b, [o_inter[:, j * dv:(j + 1) * dv] for j in range(pack)]))
        return out

    def intra(self, ops):
        pack, dv = self.pack, self.dv
        for g in self.groups:
            rows, q, k, v, b, o = ops[g]
            for s in range(LA_CHUNK):
                valid = (self.rowi <= s) if self.reverse else (self.rowi >= s)
                term = (q * k[s:s + 1]) * jnp.exp(b - b[s:s + 1])
                for j in range(pack):
                    tj = term if pack == 1 else term * self.head_lanes[j]
                    col = jnp.where(valid, jnp.sum(tj, axis=-1, keepdims=True), 0.0)
                    o[j] = o[j] + col * v[s:s + 1, j * dv:(j + 1) * dv]
            for j in range(pack):
                self.o_ref[rows, (g * pack + j) * dv:(g * pack + j + 1) * dv] = o[j]

    def update(self, ops):
        for g in self.groups:
            _, _, k, v, b, _ = ops[g]
            b_last = b[0:1] if self.reverse else b[LA_CHUNK - 1:LA_CHUNK]
            upd = _dot_tn(v.astype(BF16), (k * jnp.exp(b_last - b)).astype(BF16))
            if self.pack > 1:
                upd = upd * self.blk_mask
            self.s_ref[g] = self.s_ref[g] * jnp.exp(b_last) + upd


def _la_run(units, nch):
    def chunk(ci, carry):
        ops = [un.load(ci) for un in units]
        for un, x in zip(units, ops):
            un.update(x)
        for un, x in zip(units, ops):
            un.intra(x)
        return carry

    lax.fori_loop(0, nch, chunk, 0)


def _hg_kernel(qf_ref, ff_ref, vf_ref, trif_ref, qb_ref, fb_ref, vb_ref, trib_ref, gamma_ref,
               of_ref, ob_ref, sf_ref, kf_scr, bf_scr, sb_ref, kb_scr, bb_scr, *, layer, tb, nh, dk, dv):
    gam = gamma_ref[...]
    ex = jnp.exp(gam - jnp.max(gam, axis=0, keepdims=True))
    p = ex / jnp.sum(ex, axis=0, keepdims=True)
    cum = p[0:1]
    for i in range(1, layer + 1):
        cum = cum + p[i:i + 1]
    lb = cum - p[0:1]
    lo = jnp.log(lb)
    l1 = jnp.log(1.0 - lb)

    units = []
    for d, (q_ref, f_ref, v_ref, tri_ref, o_ref, s_ref, k_scr, b_scr) in enumerate((
            (qf_ref, ff_ref, vf_ref, trif_ref, of_ref, sf_ref, kf_scr, bf_scr),
            (qb_ref, fb_ref, vb_ref, trib_ref, ob_ref, sb_ref, kb_scr, bb_scr))):
        hi = l1 + _log_sigmoid(f_ref[...])
        mx = jnp.maximum(lo, hi)
        mn = jnp.minimum(lo, hi)
        log_f = mx + jnp.log(1.0 + jnp.exp(mn - mx))
        k_scr[...] = 1.0 - jnp.exp(log_f)
        b_scr[...] = _split3_dot(tri_ref[...], log_f)
        units.append(_LaUnit(q_ref, k_scr, v_ref, b_scr, o_ref, s_ref, tb=tb, nh=nh, dk=dk, dv=dv,
                             reverse=d == 1, q_scale=1.0))

    @pl.when(pl.program_id(1) == 0)
    def _():
        for un in units:
            un.reset()

    _la_run(units, tb // LA_CHUNK)


def _gla_kernel(qkf_ref, vf_ref, smallf_ref, trif_ref, qkb_ref, vb_ref, smallb_ref, trib_ref,
                gw_ref, gb_ref, of_ref, ob_ref, sf_ref, bf_scr, sb_ref, bb_scr, *, tb, nh, dk, dv):
    hk = nh * dk
    units = []
    for d, (qk_ref, v_ref, small_ref, tri_ref, o_ref, s_ref, b_scr) in enumerate((
            (qkf_ref, vf_ref, smallf_ref, trif_ref, of_ref, sf_ref, bf_scr),
            (qkb_ref, vb_ref, smallb_ref, trib_ref, ob_ref, sb_ref, bb_scr))):
        code = small_ref[:, 384:512]
        pre = _dot(code.astype(BF16), gw_ref[:, d * hk:(d + 1) * hk]) + gb_ref[:, d * hk:(d + 1) * hk]
        log_g = _log_sigmoid(pre) * (1.0 / GLA_GATE_NORM)
        b_scr[...] = _split3_dot(tri_ref[...], log_g)
        units.append(_LaUnit(qk_ref.at[:, 0:hk], qk_ref.at[:, hk:2 * hk], v_ref, b_scr, o_ref, s_ref,
                             tb=tb, nh=nh, dk=dk, dv=dv, reverse=d == 1, q_scale=dk ** -0.5))

    @pl.when(pl.program_id(1) == 0)
    def _():
        for un in units:
            un.reset()

    _la_run(units, tb // LA_CHUNK)


def _chunk_tri(tb, reverse, chunk=LA_CHUNK):
    t = jnp.arange(tb)
    same = (t[:, None] // chunk) == (t[None, :] // chunk)
    tri = (t[None, :] >= t[:, None]) if reverse else (t[None, :] <= t[:, None])
    return (same & tri).astype(BF16)


def _la_index(d, n_batch, ctxb, latb):
    return lambda b, i: _seq_block(d, b, i, n_batch=n_batch, ctxb=ctxb, latb=latb)


def _hgrn2(p_all, gamma, *, layer, tb, n_batch, ctxb, latb):
    m = p_all.shape[0]
    c = gamma.shape[-1]
    nh = c // HG_DK
    in_specs, args, out_specs = [], [], []
    for d in range(2):
        blk = _la_index(d, n_batch, ctxb, latb)
        col = lambda off, blk=blk: pl.BlockSpec((tb, c), lambda b, i: (blk(b, i), off // c))
        in_specs += [col(C_HGQ), col(C_HGF + d * c), col(C_HGI), pl.BlockSpec((tb, tb), lambda b, i: (0, 0))]
        args += [p_all, p_all, p_all, _chunk_tri(tb, d == 1)]
        out_specs.append(col(0))
    unit_scratch = [pltpu.VMEM((nh, HG_DK, HG_DK), F32), pltpu.VMEM((tb, c), F32), pltpu.VMEM((tb, c), F32)]
    return pl.pallas_call(
        functools.partial(_hg_kernel, layer=layer, tb=tb, nh=nh, dk=HG_DK, dv=HG_DK),
        grid=(n_batch, ctxb + latb),
        in_specs=in_specs + [pl.BlockSpec(gamma.shape, lambda b, i: (0, 0))],
        out_specs=out_specs,
        out_shape=[jax.ShapeDtypeStruct((m, c), F32)] * 2,
        scratch_shapes=unit_scratch + unit_scratch,
        compiler_params=_cparams(2),
        name="hgrn2_scan",
    )(*args, gamma)


def _gla(p_all, gw, gb, *, tb, n_batch, ctxb, latb):
    m = p_all.shape[0]
    hk = gw.shape[-1] // 2
    dk = hk // GLA_HEADS
    c = 2 * hk
    dv = c // GLA_HEADS
    in_specs, args, out_specs = [], [], []
    for d in range(2):
        blk = _la_index(d, n_batch, ctxb, latb)
        col = lambda off, blk=blk: pl.BlockSpec((tb, c), lambda b, i: (blk(b, i), off // c))
        in_specs += [col(C_GLQK), col(C_GLV), col(C_SMALL), pl.BlockSpec((tb, tb), lambda b, i: (0, 0))]
        args += [p_all, p_all, p_all, _chunk_tri(tb, d == 1)]
        out_specs.append(col(0))
    unit_scratch = [pltpu.VMEM((hk // 128, (128 // dk) * dv, 128), F32), pltpu.VMEM((tb, hk), F32)]
    return pl.pallas_call(
        functools.partial(_gla_kernel, tb=tb, nh=GLA_HEADS, dk=dk, dv=dv),
        grid=(n_batch, ctxb + latb),
        in_specs=in_specs + [pl.BlockSpec(gw.shape, lambda b, i: (0, 0)),
                             pl.BlockSpec(gb.shape, lambda b, i: (0, 0))],
        out_specs=out_specs,
        out_shape=[jax.ShapeDtypeStruct((m, c), F32)] * 2,
        scratch_shapes=unit_scratch + unit_scratch,
        compiler_params=_cparams(2),
        name="gla_scan",
    )(*args, gw, gb)


def _la_post(o, gate, norm_g, nh):
    dv = o.shape[-1] // nh
    outs = []
    for h in range(nh):
        oh = o[:, h * dv:(h + 1) * dv]
        outs.append(oh * lax.rsqrt(jnp.mean(oh * oh, axis=-1, keepdims=True) + EPS))
    return jnp.concatenate(outs, axis=-1) * norm_g * (gate * _sigmoid(gate))


def _gelu(x):
    return 0.5 * x * (1.0 + jnp.tanh(0.7978845608028654 * (x + 0.044715 * (x * x * x))))


def _sgu_kernel(u_ref, v_ref, lnw_ref, lnb_ref, ws_ref, bs_ref, o_ref, *, rb):
    u = _gelu(u_ref[...])
    v = _gelu(v_ref[...])
    vc = v - jnp.mean(v, axis=-1, keepdims=True)
    vn = vc * lax.rsqrt(jnp.mean(vc * vc, axis=-1, keepdims=True) + EPS)
    vn = (vn * lnw_ref[...] + lnb_ref[...]).astype(BF16)
    gw = vn.shape[-1] // SGU_GROUPS
    for n in range(rb // SGU_CHUNK):
        rs = slice(n * SGU_CHUNK, (n + 1) * SGU_CHUNK)
        for g in range(SGU_GROUPS):
            cs = slice(g * gw, (g + 1) * gw)
            s = _dot(ws_ref[g], vn[rs, cs]) + bs_ref[g]
            o_ref[rs, cs] = u[rs, cs] * s


def _sgu(p_all, lnw, lnb, ws, bs, *, rb):
    m = p_all.shape[0]
    c = lnw.shape[-1]
    return pl.pallas_call(
        functools.partial(_sgu_kernel, rb=rb),
        grid=(m // rb,),
        in_specs=[pl.BlockSpec((rb, c), lambda i: (i, C_SGU // c)),
                  pl.BlockSpec((rb, c), lambda i: (i, C_SGU // c + 1)),
                  pl.BlockSpec((1, c), lambda i: (0, 0)),
                  pl.BlockSpec((1, c), lambda i: (0, 0)),
                  pl.BlockSpec(ws.shape, lambda i: (0, 0, 0)),
                  pl.BlockSpec(bs.shape, lambda i: (0, 0, 0))],
        out_specs=pl.BlockSpec((rb, c), lambda i: (i, 0)),
        out_shape=jax.ShapeDtypeStruct((m, c), F32),
        compiler_params=_cparams(1),
        name="sgu",
    )(p_all, p_all, lnw, lnb, ws, bs)


def _merge_kernel(raf_ref, rab_ref, rg_ref, rbonus_ref, hf_ref, hb_ref, hgate_ref, gf_ref, gb_ref,
                  ggate_ref, yd_ref, lnw_ref, lnb_ref, e_ref, hnorm_ref, gnorm_ref,
                  g0_ref, g1_ref, g2_ref, g3_ref, w_ref, o_ref, y_scr, *, hg_heads):
    @pl.when(pl.program_id(1) == 0)
    def _():
        y_scr[0] = _rw_post(raf_ref[...] + rab_ref[...], rg_ref[...], rbonus_ref[...], lnw_ref[...],
                            lnb_ref[...], e_ref[...]).astype(BF16)
        y_scr[1] = _la_post(hf_ref[...] + hb_ref[...], hgate_ref[...], hnorm_ref[...], hg_heads).astype(BF16)
        y_scr[2] = _la_post(gf_ref[...] + gb_ref[...], ggate_ref[...], gnorm_ref[...], GLA_HEADS).astype(BF16)
        y_scr[3] = yd_ref[...].astype(BF16)

    gs = (g0_ref, g1_ref, g2_ref, g3_ref)
    acc = None
    for j in range(N_BRANCH):
        t = _sigmoid(gs[j][...].astype(F32)) * _dot(y_scr[j], w_ref[j])
        acc = t if acc is None else acc + t
    o_ref[...] = acc.astype(BF16)


def _merge(rw, hg, gla, yd, p_gate, p_mix, vecs, e, w_branch, *, tm, row_off):
    m = p_mix.shape[0]
    _, c, d = w_branch.shape
    tn = _pick((1024, 512), d)
    mo = m - row_off * tm
    ytok = pl.BlockSpec((tm, c), lambda i, j: (i + row_off, 0))
    mix = lambda off: pl.BlockSpec((tm, c), lambda i, j: (i + row_off, off // c))
    gate = lambda b: pl.BlockSpec((tm, tn), lambda i, j: (i + row_off, b * d // tn + j))
    vec = pl.BlockSpec((1, c), lambda i, j: (0, 0))
    return pl.pallas_call(
        functools.partial(_merge_kernel, hg_heads=c // HG_DK),
        grid=(mo // tm, d // tn),
        in_specs=[ytok, ytok, ytok, ytok, ytok, ytok, mix(C_HGG), ytok, ytok, mix(C_GLG), ytok,
                  vec, vec, pl.BlockSpec((c, c), lambda i, j: (0, 0)), vec, vec,
                  gate(0), gate(1), gate(2), gate(3),
                  pl.BlockSpec((N_BRANCH, c, tn), lambda i, j: (0, 0, j))],
        out_specs=pl.BlockSpec((tm, tn), lambda i, j: (i, j)),
        out_shape=jax.ShapeDtypeStruct((mo, d), BF16),
        scratch_shapes=[pltpu.VMEM((N_BRANCH, tm, c), BF16)],
        compiler_params=_cparams(2),
        name="merge",
    )(*rw, hg[0], hg[1], p_mix, gla[0], gla[1], p_mix, yd, *vecs[:2], e, *vecs[2:],
      p_gate, p_gate, p_gate, p_gate, w_branch)


def _outproj_kernel(m_ref, w_ref, xc_ref, xl_ref, mod_ref, o_ref, *, nctx, row_off):
    x = jnp.where(pl.program_id(0) + row_off < nctx, xc_ref[...], xl_ref[...])
    o_ref[...] = x + mod_ref[0, 2:3, :] * _dot(m_ref[...], w_ref[...])


def _outproj(mm, w_out, x_parts, mod, *, tm, row_off, nctx, bpb):
    xc, xl, lat_off = x_parts
    mo, d = mm.shape
    tn = _pick((1024, 512), d)
    xspecs = [pl.BlockSpec((tm, tn), lambda i, j: (jnp.minimum(i + row_off, nctx - 1),
                                                   jnp.where(i + row_off < nctx, j, 0))),
              pl.BlockSpec((tm, tn), lambda i, j: (lat_off + jnp.maximum(i + row_off - nctx, 0),
                                                   jnp.where(i + row_off < nctx, 0, j)))]
    return pl.pallas_call(
        functools.partial(_outproj_kernel, nctx=nctx, row_off=row_off),
        grid=(mo // tm, d // tn),
        in_specs=[pl.BlockSpec((tm, d), lambda i, j: (i, 0)),
                  pl.BlockSpec((d, tn), lambda i, j: (0, j))] + xspecs + [
                  pl.BlockSpec((1, 6, tn), lambda i, j: (_row_group(i + row_off, nctx, bpb), 0, j))],
        out_specs=pl.BlockSpec((tm, tn), lambda i, j: (i, j)),
        out_shape=jax.ShapeDtypeStruct((mo, d), F32),
        compiler_params=_cparams(2),
        name="outproj",
    )(mm, w_out, xc, xl, mod)


def _mlp_kernel(x_ref, mod_ref, g_ref, w1_ref, w2_ref, gf_ref, o_ref, h_ref, acc_ref, *, final_norm):
    j = pl.program_id(1)

    @pl.when(j == 0)
    def _():
        h = _norm_mod(x_ref[...], g_ref[...], mod_ref[0, 3:4, :], mod_ref[0, 4:5, :])
        h_ref[...] = h.astype(BF16)
        acc_ref[...] = jnp.zeros_like(acc_ref)

    a = jnp.maximum(_dot(h_ref[...], w1_ref[...]), 0.0)
    acc_ref[...] += _dot((a * a).astype(BF16), w2_ref[...])

    @pl.when(j == pl.num_programs(1) - 1)
    def _():
        y = x_ref[...] + mod_ref[0, 5:6, :] * acc_ref[...]
        if final_norm:
            y = y * lax.rsqrt(jnp.mean(y * y, axis=-1, keepdims=True) + EPS) * gf_ref[...]
        o_ref[...] = y


def _mlp(x_in, mod, g, w1, w2, g_final, *, tm, row_off, nctx, bpb, final_norm):
    mo, d = x_in.shape
    hid = w1.shape[1]
    th = _pick((1024, 512, 256, 128), hid)
    return pl.pallas_call(
        functools.partial(_mlp_kernel, final_norm=final_norm),
        grid=(mo // tm, hid // th),
        in_specs=[pl.BlockSpec((tm, d), lambda i, j: (i, 0)),
                  pl.BlockSpec((1, 6, d), lambda i, j: (_row_group(i + row_off, nctx, bpb), 0, 0)),
                  pl.BlockSpec((1, d), lambda i, j: (0, 0)),
                  pl.BlockSpec((d, th), lambda i, j: (0, j)),
                  pl.BlockSpec((th, d), lambda i, j: (j, 0)),
                  pl.BlockSpec((1, d), lambda i, j: (0, 0))],
        out_specs=pl.BlockSpec((tm, d), lambda i, j: (i, 0)),
        out_shape=jax.ShapeDtypeStruct((mo, d), F32),
        scratch_shapes=[pltpu.VMEM((tm, d), BF16), pltpu.VMEM((tm, d), F32)],
        compiler_params=_cparams(2),
        name="mlp",
    )(x_in, mod, g, w1, w2, g_final)


def _blockdiag2(w):
    _, r, c = w.shape
    z = jnp.zeros((r, c), w.dtype)
    out = jnp.concatenate([jnp.concatenate([w[0], z], axis=1), jnp.concatenate([z, w[1]], axis=1)], axis=0)
    return jnp.pad(out, ((0, 128 - 2 * r), (0, 0))).astype(BF16)


def _permute_w_in(w, d_model):
    c = d_model // N_BRANCH
    o = [0]
    for wd in (3 * c, 64 * 2, 64 * 2, 128, c, 2 * c, c, c, c, c, 32, c, 2 * c, N_BRANCH * d_model):
        o.append(o[-1] + wd)
    seg = lambda k: w[:, o[k]:o[k + 1]]
    pad = jnp.zeros((w.shape[0], 512 - 128 * 3 - 32), w.dtype)
    parts = [seg(13), seg(0), seg(5), seg(12), seg(4), seg(6), seg(7), seg(8), seg(9), seg(11),
             seg(1), seg(2), seg(3), seg(10), pad]
    out = jnp.concatenate(parts, axis=1).astype(BF16)
    assert out.shape[1] == N_GATE + N_MIX
    return out


def kernel(x, c, ctx, c_ctx, w_ada, b_ada, g_norm1, g_norm2, g_final, w_in, rw_conv, rw_w0, rw_w2,
           rw_a0, rw_a2, rw_g2, rw_kk, rw_ka, rw_rk, rw_ln_w, rw_ln_b, hg_gamma, hg_norm, gla_gw,
           gla_gb, gla_norm, sgu_ln_w, sgu_ln_b, sgu_w, sgu_b, w_branch, w_out, w_mlp1, w_mlp2):
    n_batch, seq, d_model = x.shape
    ctx_len = ctx.shape[1]
    depth = w_in.shape[0]
    cw = d_model // N_BRANCH
    assert cw == 512 and d_model == 2048, "column layout constants assume D_MODEL = 2048"
    m_ctx = n_batch * ctx_len

    tm = _pick((1024, 512, 256, 128), m_ctx, seq)
    tb = _pick((256, 128), ctx_len, seq)
    nctx, bpb = m_ctx // tm, seq // tm
    ctxb, latb = ctx_len // tb, seq // tb
    seqs = dict(tb=tb, n_batch=n_batch, ctxb=ctxb, latb=latb)

    x_parts = (ctx.reshape(m_ctx, d_model), x.reshape(n_batch * seq, d_model), 0)
    c_rows = jnp.concatenate([c_ctx[None, :], c, jnp.zeros((7 - n_batch, d_model), F32)], axis=0)
    mod_all = _ada(c_rows, w_ada, b_ada).reshape(depth, 8, 6, d_model)

    head_ones = (jnp.arange(cw)[:, None] // RW_HEAD == jnp.arange(cw)[None, :] // RW_HEAD).astype(BF16)
    row = lambda a: a.reshape(1, -1)

    for l in range(depth):
        last = l == depth - 1
        mod = mod_all[l]
        p_gate, p_all = _inproj(x_parts, mod, row(g_norm1[l]), _permute_w_in(w_in[l], d_model),
                                tm=tm, nctx=nctx, bpb=bpb)

        prm = dict(conv=rw_conv[l], w0=row(rw_w0[l]), w2=_blockdiag2(rw_w2[l]), a0=row(rw_a0[l]),
                   a2=_blockdiag2(rw_a2[l]), g2=rw_g2[l].astype(BF16), kk=row(rw_kk[l]),
                   ka=row(rw_ka[l]), rk=row(rw_rk[l]), e=head_ones)
        r, v, nkk, g, bonus, w, kka, km = _rw_prep(p_all, prm, tb=tb, nctx=m_ctx // tb,
                                                    ctx_bps=ctxb, lat_bps=latb)
        ys = _rw_scan(r, v, nkk, w, kka, km, **seqs)

        ob = _hgrn2(p_all, hg_gamma, layer=l, **seqs)

        gw = _blockdiag2(gla_gw[l])
        gb = row(gla_gb[l])
        oc = _gla(p_all, gw, gb, **seqs)

        bs = jnp.broadcast_to(sgu_b[l][:, :, None], sgu_w[l].shape)
        yd = _sgu(p_all, row(sgu_ln_w[l]), row(sgu_ln_b[l]), sgu_w[l].astype(BF16), bs,
                  rb=_pick((512, 256, 128), m_ctx, seq))

        row_off = nctx if last else 0
        vecs = (row(rw_ln_w[l]), row(rw_ln_b[l]), row(hg_norm[l]), row(gla_norm[l]))
        mm = _merge((ys[0], ys[1], g, bonus), ob, oc, yd, p_gate, p_all, vecs, head_ones,
                    w_branch[l].astype(BF16), tm=tm // 2, row_off=2 * row_off)
        x_mid = _outproj(mm, w_out[l].astype(BF16), x_parts, mod, tm=tm, row_off=row_off, nctx=nctx, bpb=bpb)
        x_all = _mlp(x_mid, mod, row(g_norm2[l]), w_mlp1[l].astype(BF16), w_mlp2[l].astype(BF16),
                     row(g_final), tm=tm // 2, row_off=2 * row_off, nctx=2 * nctx, bpb=2 * bpb,
                     final_norm=last)
        x_parts = (x_all, x_all, nctx)
    return x_all.reshape(n_batch, seq, d_model)
```

```python
import functools

import jax
import jax.numpy as jnp
from jax import lax
from jax.experimental import pallas as pl
from jax.experimental.pallas import tpu as pltpu

F32 = jnp.float32
BF16 = jnp.bfloat16

N_BRANCH = 4
RW_HEAD = 64
RW_LN_EPS = 64e-5
HG_DK = 128
GLA_HEADS = 4
GLA_GATE_NORM = 16.0
LA_CHUNK = 16
RW_CHUNK = 32
RW_SUB = 16
SGU_CHUNK = 128
SGU_GROUPS = 4
EPS = 1e-6

C_RKV = 0
C_HGF = 1536
C_SGU = 2560
C_HGQ = 3584
C_HGI = 4096
C_HGG = 4608
C_GLQK = 5120
C_GLV = 5632
C_GLG = 6144
C_SMALL = 6656
N_MIX = 7168
N_GATE = 8192

VMEM_LIMIT = 58 * 1024 * 1024


def _cparams(n_axes):
    return pltpu.CompilerParams(dimension_semantics=("arbitrary",) * n_axes,
                                vmem_limit_bytes=VMEM_LIMIT)


def _pick(n_list, *dims):
    for n in n_list:
        if all(d % n == 0 for d in dims):
            return n
    raise ValueError(f"no block size in {n_list} divides {dims}")


def _row_group(i, nctx, bpb):
    return jnp.where(i < nctx, 0, 1 + (i - nctx) // bpb)


def _dot(a, b):
    return jnp.dot(a, b, preferred_element_type=F32)


def _dot_nt(a, b):
    return lax.dot_general(a, b, (((1,), (1,)), ((), ())), preferred_element_type=F32)


def _dot_tn(a, b):
    return lax.dot_general(a, b, (((0,), (0,)), ((), ())), preferred_element_type=F32)


def _split_dot(x, e):
    hi = x.astype(BF16)
    lo = (x - hi.astype(F32)).astype(BF16)
    return _dot(hi, e) + _dot(lo, e)


def _split3_dot(e, x):
    p1 = x.astype(BF16)
    r1 = x - p1.astype(F32)
    p2 = r1.astype(BF16)
    p3 = (r1 - p2.astype(F32)).astype(BF16)
    return _dot(e, p1) + _dot(e, p2) + _dot(e, p3)


def _log_sigmoid(x):
    return jnp.minimum(x, 0.0) - jnp.log(1.0 + jnp.exp(-jnp.abs(x)))


def _sigmoid(x):
    return 0.5 * jnp.tanh(0.5 * x) + 0.5


def _ada_kernel(c_ref, w_ref, b_ref, o_ref):
    c = c_ref[...]
    act = c * _sigmoid(c)
    o_ref[0] = _dot(act.astype(BF16), w_ref[0].astype(BF16)) + b_ref[0]


def _ada(c_rows, w_ada, b_ada):
    n_layers, d, n = w_ada.shape
    tn = _pick((1024, 512, 256, 128), n)
    return pl.pallas_call(
        _ada_kernel,
        grid=(n_layers, n // tn),
        in_specs=[pl.BlockSpec((8, d), lambda l, j: (0, 0)),
                  pl.BlockSpec((1, d, tn), lambda l, j: (l, 0, j)),
                  pl.BlockSpec((1, 1, tn), lambda l, j: (l, 0, j))],
        out_specs=pl.BlockSpec((1, 8, tn), lambda l, j: (l, 0, j)),
        out_shape=jax.ShapeDtypeStruct((n_layers, 8, n), F32),
        compiler_params=_cparams(2),
        name="ada_mod",
    )(c_rows, w_ada, b_ada.reshape(n_layers, 1, n))


def _norm_mod(x, g, shift, scale):
    y = x * lax.rsqrt(jnp.mean(x * x, axis=-1, keepdims=True) + EPS) * g
    return y * (1.0 + scale) + shift


def _two_part_specs(block, nctx, lat_off, col):
    return [pl.BlockSpec(block, lambda i, j: (jnp.minimum(i, nctx - 1), col(j)), pipeline_mode=pl.Buffered(1)),
            pl.BlockSpec(block, lambda i, j: (lat_off + jnp.maximum(i - nctx, 0), col(j)))]


def _inproj_kernel(xc_ref, xl_ref, mod_ref, g_ref, w_ref, og_ref, om_ref, h_ref, *, nctx, n_gate_tiles):
    i, j = pl.program_id(0), pl.program_id(1)

    @pl.when(j == 0)
    def _():
        x = jnp.where(i < nctx, xc_ref[...], xl_ref[...])
        h = _norm_mod(x, g_ref[...], mod_ref[0, 0:1, :], mod_ref[0, 1:2, :])
        h_ref[...] = h.astype(BF16)

    @pl.when(j < n_gate_tiles)
    def _():
        og_ref[...] = _dot(h_ref[...], w_ref[...]).astype(BF16)

    @pl.when(j >= n_gate_tiles)
    def _():
        om_ref[...] = _dot(h_ref[...], w_ref[...])


def _inproj(x_parts, mod, g, w, *, tm, nctx, bpb):
    xc, xl, lat_off = x_parts
    d = xc.shape[1]
    m = (nctx + (xl.shape[0] // tm - lat_off)) * tm
    tn = _pick((1024, 512), N_GATE, N_MIX)
    ng = N_GATE // tn
    return pl.pallas_call(
        functools.partial(_inproj_kernel, nctx=nctx, n_gate_tiles=ng),
        grid=(m // tm, (N_GATE + N_MIX) // tn),
        in_specs=_two_part_specs((tm, d), nctx, lat_off, lambda j: 0) + [
            pl.BlockSpec((1, 6, d), lambda i, j: (_row_group(i, nctx, bpb), 0, 0)),
            pl.BlockSpec((1, d), lambda i, j: (0, 0)),
            pl.BlockSpec((d, tn), lambda i, j: (0, j))],
        out_specs=[pl.BlockSpec((tm, tn), lambda i, j: (i, jnp.minimum(j, ng - 1))),
                   pl.BlockSpec((tm, tn), lambda i, j: (i, jnp.maximum(j - ng, 0)))],
        out_shape=[jax.ShapeDtypeStruct((m, N_GATE), BF16), jax.ShapeDtypeStruct((m, N_MIX), F32)],
        scratch_shapes=[pltpu.VMEM((tm, d), BF16)],
        compiler_params=_cparams(2),
        name="inproj",
    )(xc, xl, mod, g, w)


def _rw_prep_kernel(rkv_ref, prev_ref, next_ref, small_ref, conv_ref, w0_ref, w2_ref, a0_ref,
                    a2_ref, g2_ref, kk_ref, ka_ref, rk_ref, e_ref,
                    r_out, v_out, nkk_out, g_out, bonus_out, w_out, kka_out, km_out,
                    *, tb, nctx, ctx_bps, lat_bps):
    i = pl.program_id(0)
    c = r_out.shape[-1]
    j = jnp.where(i < nctx, i, i - nctx)
    bps = jnp.where(i < nctx, ctx_bps, lat_bps)
    first = lax.rem(j, bps) == 0
    last = lax.rem(j, bps) == bps - 1

    blk = rkv_ref[...]
    rows = lax.broadcasted_iota(jnp.int32, (tb, 1), 0)
    prev_row = jnp.where(first, 0.0, prev_ref[7:8, :])
    next_row = jnp.where(last, 0.0, next_ref[0:1, :])
    xm1 = jnp.where(rows == 0, prev_row, pltpu.roll(blk, 1, 0))
    xp1 = jnp.where(rows == tb - 1, next_row, pltpu.roll(blk, tb - 1, 0))
    conv = conv_ref[0:1, :] * xm1 + conv_ref[1:2, :] * blk + conv_ref[2:3, :] * xp1
    r = conv[:, 0:c]
    k = conv[:, c:2 * c]
    v = conv[:, 2 * c:3 * c]

    small = small_ref[...]
    wl = small[:, 0:128]
    al = small[:, 128:256]
    gl = small[:, 256:384]
    w_pre = w0_ref[...] + _dot(jnp.tanh(wl).astype(BF16), w2_ref[...])
    softplus = jnp.maximum(-w_pre, 0.0) + jnp.log(1.0 + jnp.exp(-jnp.abs(w_pre)))
    log_decay = -jnp.exp(-softplus - 0.5)
    a = _sigmoid(a0_ref[...] + _dot(al.astype(BF16), a2_ref[...]))
    g = _dot(_sigmoid(gl).astype(BF16), g2_ref[...])

    e = e_ref[...]
    kkv = k * kk_ref[...]
    kk = kkv * lax.rsqrt(_split_dot(kkv * kkv, e) + 1e-12)
    bonus = _split_dot(r * k * rk_ref[...], e) * v

    r_out[...] = r.astype(BF16)
    v_out[...] = v.astype(BF16)
    nkk_out[...] = (-kk).astype(BF16)
    g_out[...] = g
    bonus_out[...] = bonus
    for d in range(2):
        a_d = a[:, d * c:(d + 1) * c]
        w_out[d] = log_decay[:, d * c:(d + 1) * c]
        kka_out[d] = (kk * a_d).astype(BF16)
        km_out[d] = (k * (1.0 + (a_d - 1.0) * ka_ref[...])).astype(BF16)


def _rw_prep(p_all, prm, *, tb, nctx, ctx_bps, lat_bps):
    m = p_all.shape[0]
    c = prm["kk"].shape[-1]
    nblk = m // tb
    t8 = tb // 8
    full = lambda shape: pl.BlockSpec(shape, lambda i: (0,) * len(shape))
    tok = pl.BlockSpec((tb, c), lambda i: (i, 0))
    tok2 = pl.BlockSpec((2, tb, c), lambda i: (0, i, 0))
    one, one_b = jax.ShapeDtypeStruct((m, c), F32), jax.ShapeDtypeStruct((m, c), BF16)
    two, two_b = jax.ShapeDtypeStruct((2, m, c), F32), jax.ShapeDtypeStruct((2, m, c), BF16)
    return pl.pallas_call(
        functools.partial(_rw_prep_kernel, tb=tb, nctx=nctx, ctx_bps=ctx_bps, lat_bps=lat_bps),
        grid=(nblk,),
        in_specs=[pl.BlockSpec((tb, 3 * c), lambda i: (i, C_RKV // (3 * c))),
                  pl.BlockSpec((8, 3 * c), lambda i: (jnp.maximum(i * t8 - 1, 0), 0)),
                  pl.BlockSpec((8, 3 * c), lambda i: (jnp.minimum((i + 1) * t8, m // 8 - 1), 0)),
                  pl.BlockSpec((tb, 512), lambda i: (i, C_SMALL // 512)),
                  full((3, 3 * c)), full((1, 2 * c)), full((128, 2 * c)), full((1, 2 * c)),
                  full((128, 2 * c)), full((128, c)), full((1, c)), full((1, c)), full((1, c)),
                  full((c, c))],
        out_specs=[tok, tok, tok, tok, tok, tok2, tok2, tok2],
        out_shape=[one_b, one_b, one_b, one, one, two, two_b, two_b],
        compiler_params=_cparams(1),
        name="rwkv_prep",
    )(p_all, p_all, p_all, p_all, prm["conv"], prm["w0"], prm["w2"], prm["a0"], prm["a2"],
      prm["g2"], prm["kk"], prm["ka"], prm["rk"], prm["e"])


class _RwUnit:
    def __init__(self, refs, scratch, *, tb, npair, reverse):
        (self.r_ref, self.v_ref, self.a_ref, self.lw_ref, self.b_ref, self.k_ref, self.tri_ref,
         self.y_ref) = refs
        (self.s_ref, self.c_scr, self.ag_scr, self.rg_scr, self.bg_scr, self.kg_scr) = scratch[:6]
        self.slots = (scratch[6:10], scratch[10:14])
        self.reverse = reverse
        self.nch = tb // RW_CHUNK
        self.pairs = range(npair)
        L, hd = RW_CHUNK, RW_HEAD
        self.lss = [slice(p * 2 * hd, (p + 1) * 2 * hd) for p in self.pairs]
        lane = lax.broadcasted_iota(jnp.int32, (1, 2 * hd), 1)
        self.lo = (lane < hd).astype(F32)
        self.hi = 1.0 - self.lo
        ti = lax.broadcasted_iota(jnp.int32, (2 * L, 2 * hd), 0)
        sl = lax.broadcasted_iota(jnp.int32, (2 * L, 2 * hd), 1)
        sj = sl & (L - 1)
        tt = ti & (L - 1)
        earlier = (sj > tt) if reverse else (sj < tt)
        assert 4 * L == 2 * hd and L == 2 * RW_SUB
        self.aa_mask = (earlier | ((ti >= L) & (sj == tt))).astype(F32)
        self.col_idx0 = jnp.where(lax.broadcasted_iota(jnp.int32, (RW_SUB, 2 * hd), 1) < hd, 0, L)
        self.blk_mask = ((lax.broadcasted_iota(jnp.int32, (2 * hd, 2 * hd), 0) >> 6)
                         == (lax.broadcasted_iota(jnp.int32, (2 * hd, 2 * hd), 1) >> 6)).astype(F32)
        self.zeros2l = jnp.zeros((2 * L, 2 * hd), F32)
        self.zeros_sub = jnp.zeros((RW_SUB, 2 * hd), F32)
        self.blocks = (1, 0) if reverse else (0, 1)
        self.last = 0 if reverse else L - 1
        self.order = range(L - 1, -1, -1) if reverse else range(L)

    def reset(self):
        self.s_ref[...] = jnp.zeros_like(self.s_ref)

    def prologue(self):
        lw = self.lw_ref[0]
        c = _split3_dot(self.tri_ref[...], lw)
        self.c_scr[...] = c
        enc = jnp.exp(-c)
        self.ag_scr[...] = self.a_ref[...] * jnp.exp(c - lw)
        self.rg_scr[...] = self.r_ref[...] * jnp.exp(c)
        self.bg_scr[...] = self.b_ref[0] * enc
        self.kg_scr[...] = self.k_ref[0] * enc

    def rows(self, ci):
        L = RW_CHUNK
        cc = (self.nch - 1 - ci) if self.reverse else ci
        return pl.ds(pl.multiple_of(cc * L, L), L)

    def halves(self, x):
        return [x * self.lo, x * self.hi]

    def lhs_of(self, rows):
        return [jnp.concatenate([self.ag_scr[rows, ls], self.rg_scr[rows, ls]], axis=0).astype(BF16)
                for ls in self.lss]

    def prep_aa(self, rows):
        lhs = self.lhs_of(rows)
        out = []
        for p in self.pairs:
            ls = self.lss[p]
            rhs = jnp.concatenate(self.halves(self.bg_scr[rows, ls]) + self.halves(self.kg_scr[rows, ls]),
                                  axis=0)
            out.append(_dot_nt(lhs[p], rhs.astype(BF16)))
        return out

    def prep_akv(self, rows, aa_raw, slot):
        _, akv_scr, aar_scr, na_scr = slot
        L = RW_CHUNK
        aa = [x * self.aa_mask for x in aa_raw]
        for p in self.pairs:
            v = self.v_ref[rows, self.lss[p]]
            vv = jnp.concatenate([self.zeros2l] + self.halves(v), axis=0).astype(BF16)
            akv_scr[p] = _dot(aa[p][0:L].astype(BF16), vv)
            aar_scr[p] = aa[p][L:2 * L]
            na_scr[p] = aa[p][0:L]
        return aa

    def prep_cols(self, aa, slot):
        for p in self.pairs:
            for s in range(RW_CHUNK):
                r0 = (s // RW_SUB) * RW_SUB
                slot[0][p, s] = jnp.take_along_axis(aa[p][r0:r0 + RW_SUB], self.col_idx0 + s, axis=1)

    def adv_g(self, rows):
        lhs = self.lhs_of(rows)
        return [_dot_nt(lhs[p], self.s_ref[p].astype(BF16)) for p in self.pairs]

    def adv_solve(self, g, slot):
        L, sub = RW_CHUNK, RW_SUB
        col_scr, akv_scr, _, na_scr = slot
        first, second = self.blocks
        ub = {}
        for blk in self.blocks:
            r0 = blk * sub
            ub[blk] = [g[p][r0:r0 + sub] + akv_scr[p, r0:r0 + sub] for p in self.pairs]
            if blk == second:
                for p in self.pairs:
                    done = ub[first][p]
                    z = self.zeros_sub
                    parts = [z, z, z, z]
                    parts[first], parts[2 + first] = done * self.lo, done * self.hi
                    stack = jnp.concatenate(parts + [self.zeros2l], axis=0).astype(BF16)
                    ub[blk][p] = ub[blk][p] + _dot(na_scr[p, r0:r0 + sub].astype(BF16), stack)
            steps = [s for s in self.order if s // sub == blk]
            for s in steps:
                for p in self.pairs:
                    ub[blk][p] = ub[blk][p] + col_scr[p, s] * ub[blk][p][s - r0:s - r0 + 1]
        return [jnp.concatenate([ub[0][p], ub[1][p]], axis=0) for p in self.pairs]

    def adv_out(self, rows, g, u, slot):
        L = RW_CHUNK
        for p in self.pairs:
            v = self.v_ref[rows, self.lss[p]]
            uv = jnp.concatenate(self.halves(u[p]) + self.halves(v), axis=0).astype(BF16)
            self.y_ref[rows, self.lss[p]] = g[p][L:2 * L] + _dot(slot[2][p].astype(BF16), uv)
        for p in self.pairs:
            ls = self.lss[p]
            cch = self.c_scr[rows, ls]
            cl = cch[self.last:self.last + 1]
            dec = jnp.exp(cl - cch)
            bk = jnp.concatenate([self.b_ref[0, rows, ls] * dec, self.k_ref[0, rows, ls] * dec], axis=0)
            upd = _dot_tn(jnp.concatenate([u[p], self.v_ref[rows, ls]], axis=0).astype(BF16),
                          bk.astype(BF16))
            self.s_ref[p] = self.s_ref[p] * jnp.exp(cl) + upd * self.blk_mask


N_RW_REFS = 8
N_RW_SCRATCH = 14


def _rw_scan_kernel(*refs, tb, npair):
    n_in = N_RW_REFS - 1
    units = []
    for d in range(2):
        ins = refs[d * n_in:(d + 1) * n_in]
        out = refs[2 * n_in + d]
        scr = refs[2 * n_in + 2 + d * N_RW_SCRATCH:2 * n_in + 2 + (d + 1) * N_RW_SCRATCH]
        units.append(_RwUnit(tuple(ins) + (out,), scr, tb=tb, npair=npair, reverse=d == 1))
    nch = tb // RW_CHUNK

    @pl.when(pl.program_id(1) == 0)
    def _():
        for un in units:
            un.reset()

    for un in units:
        un.prologue()

    def step(ci_adv, sa, ci_prep, sp):
        rows_a = [un.rows(ci_adv) for un in units]
        rows_p = [un.rows(ci_prep) for un in units]
        aa_raw = [un.prep_aa(rp) for un, rp in zip(units, rows_p)]
        g = [un.adv_g(ra) for un, ra in zip(units, rows_a)]
        aa = [un.prep_akv(rp, x, un.slots[sp]) for un, rp, x in zip(units, rows_p, aa_raw)]
        u = [un.adv_solve(x, un.slots[sa]) for un, x in zip(units, g)]
        for un, x in zip(units, aa):
            un.prep_cols(x, un.slots[sp])
        for un, ra, x, y in zip(units, rows_a, g, u):
            un.adv_out(ra, x, y, un.slots[sa])

    for un in units:
        r0 = un.rows(0)
        un.prep_cols(un.prep_akv(r0, un.prep_aa(r0), un.slots[0]), un.slots[0])

    def two_chunks(j, carry):
        c0 = 2 * j
        step(c0, 0, c0 + 1, 1)
        step(c0 + 1, 1, jnp.minimum(c0 + 2, nch - 1), 0)
        return carry

    lax.fori_loop(0, nch // 2, two_chunks, 0)


def _seq_block(d, b, i, *, n_batch, ctxb, latb):
    is_ctx = i < ctxb
    cs = jnp.where(d == 0, i, ctxb - 1 - i)
    lj = jnp.where(d == 0, i - ctxb, latb - 1 - (i - ctxb))
    return jnp.where(is_ctx, b * ctxb + cs, n_batch * ctxb + b * latb + lj)


def _rw_scan(r, v, nkk, lw, kka, km, *, tb, n_batch, ctxb, latb):
    m, c = r.shape
    npair = c // (2 * RW_HEAD)
    lanes = 2 * RW_HEAD
    in_specs, args = [], []
    for d in range(2):
        blk = _la_index(d, n_batch, ctxb, latb)
        tok = pl.BlockSpec((tb, c), lambda b, i, blk=blk: (blk(b, i), 0))
        tokd = pl.BlockSpec((1, tb, c), lambda b, i, blk=blk, d=d: (d, blk(b, i), 0))
        in_specs += [tok, tok, tok, tokd, tokd, tokd, pl.BlockSpec((tb, tb), lambda b, i: (0, 0))]
        args += [r, v, nkk, lw, kka, km, _chunk_tri(tb, d == 1, RW_CHUNK)]
    out_specs = [pl.BlockSpec((tb, c), lambda b, i, blk=_la_index(d, n_batch, ctxb, latb): (blk(b, i), 0))
                 for d in range(2)]
    buf = pltpu.VMEM((tb, c), F32)
    unit_scratch = [pltpu.VMEM((npair, lanes, lanes), F32), buf, buf, buf, buf, buf] + 2 * [
        pltpu.VMEM((npair, RW_CHUNK, RW_SUB, lanes), F32),
        pltpu.VMEM((npair, RW_CHUNK, lanes), F32),
        pltpu.VMEM((npair, RW_CHUNK, lanes), F32),
        pltpu.VMEM((npair, RW_CHUNK, lanes), F32)]
    assert len(unit_scratch) == N_RW_SCRATCH
    return pl.pallas_call(
        functools.partial(_rw_scan_kernel, tb=tb, npair=npair),
        grid=(n_batch, ctxb + latb),
        in_specs=in_specs,
        out_specs=out_specs,
        out_shape=[jax.ShapeDtypeStruct((m, c), F32)] * 2,
        scratch_shapes=unit_scratch + unit_scratch,
        compiler_params=_cparams(2),
        name="rwkv_scan",
    )(*args)


def _rw_post(y, g, bonus, lnw, lnb, e):
    inv = 1.0 / RW_HEAD
    yc = y - _split_dot(y, e) * inv
    var = _split_dot(yc * yc, e) * inv
    yn = yc * lax.rsqrt(var + RW_LN_EPS)
    return (yn * lnw + lnb + bonus) * g


class _LaUnit:
    def __init__(self, q_ref, k_ref, v_ref, b_ref, o_ref, s_ref, *, tb, nh, dk, dv, reverse, q_scale):
        self.q_ref, self.k_ref, self.v_ref, self.b_ref, self.o_ref, self.s_ref = (
            q_ref, k_ref, v_ref, b_ref, o_ref, s_ref)
        self.nch = tb // LA_CHUNK
        self.dk, self.dv, self.reverse, self.q_scale = dk, dv, reverse, q_scale
        self.lanes = 128
        self.pack = self.lanes // dk
        self.groups = range(nh // self.pack)
        self.rowi = lax.broadcasted_iota(jnp.int32, (LA_CHUNK, 1), 0)
        lane = lax.broadcasted_iota(jnp.int32, (1, self.lanes), 1)
        self.head_lanes = [(lane // dk == j).astype(F32) for j in range(self.pack)]
        if self.pack > 1:
            rows = lax.broadcasted_iota(jnp.int32, (self.pack * dv, self.lanes), 0)
            cols = lax.broadcasted_iota(jnp.int32, (self.pack * dv, self.lanes), 1)
            self.blk_mask = ((rows // dv) == (cols // dk)).astype(F32)

    def reset(self):
        self.s_ref[...] = jnp.zeros_like(self.s_ref)

    def load(self, ci):
        cc = (self.nch - 1 - ci) if self.reverse else ci
        rows = pl.ds(pl.multiple_of(cc * LA_CHUNK, LA_CHUNK), LA_CHUNK)
        pack, dv, lanes = self.pack, self.dv, self.lanes
        out = []
        for g in self.groups:
            ks = slice(g * lanes, (g + 1) * lanes)
            q = self.q_ref[rows, ks] * self.q_scale
            k = self.k_ref[rows, ks]
            v = self.v_ref[rows, g * pack * dv:(g + 1) * pack * dv]
            b = self.b_ref[rows, ks]
            o_inter = _dot_nt((q * jnp.exp(b)).astype(BF16), self.s_ref[g].astype(BF16))
            out.append((rows, q, k, v, b, [o_inter[:, j * dv:(j + 1) * dv] for j in range(pack)]))
        return out

    def intra(self, ops):
        pack, dv = self.pack, self.dv
        for g in self.groups:
            rows, q, k, v, b, o = ops[g]
            for s in range(LA_CHUNK):
                valid = (self.rowi <= s) if self.reverse else (self.rowi >= s)
                term = (q * k[s:s + 1]) * jnp.exp(b - b[s:s + 1])
                for j in range(pack):
                    tj = term if pack == 1 else term * self.head_lanes[j]
                    col = jnp.where(valid, jnp.sum(tj, axis=-1, keepdims=True), 0.0)
                    o[j] = o[j] + col * v[s:s + 1, j * dv:(j + 1) * dv]
            for j in range(pack):
                self.o_ref[rows, (g * pack + j) * dv:(g * pack + j + 1) * dv] = o[j]

    def update(self, ops):
        for g in self.groups:
            _, _, k, v, b, _ = ops[g]
            b_last = b[0:1] if self.reverse else b[LA_CHUNK - 1:LA_CHUNK]
            upd = _dot_tn(v.astype(BF16), (k * jnp.exp(b_last - b)).astype(BF16))
            if self.pack > 1:
                upd = upd * self.blk_mask
            self.s_ref[g] = self.s_ref[g] * jnp.exp(b_last) + upd


def _la_run(units, nch):
    def chunk(ci, carry):
        ops = [un.load(ci) for un in units]
        for un, x in zip(units, ops):
            un.update(x)
        for un, x in zip(units, ops):
            un.intra(x)
        return carry

    lax.fori_loop(0, nch, chunk, 0)


def _hg_kernel(qf_ref, ff_ref, vf_ref, trif_ref, qb_ref, fb_ref, vb_ref, trib_ref, gamma_ref,
               of_ref, ob_ref, sf_ref, kf_scr, bf_scr, sb_ref, kb_scr, bb_scr, *, layer, tb, nh, dk, dv):
    gam = gamma_ref[...]
    ex = jnp.exp(gam - jnp.max(gam, axis=0, keepdims=True))
    p = ex / jnp.sum(ex, axis=0, keepdims=True)
    cum = p[0:1]
    for i in range(1, layer + 1):
        cum = cum + p[i:i + 1]
    lb = cum - p[0:1]
    lo = jnp.log(lb)
    l1 = jnp.log(1.0 - lb)

    units = []
    for d, (q_ref, f_ref, v_ref, tri_ref, o_ref, s_ref, k_scr, b_scr) in enumerate((
            (qf_ref, ff_ref, vf_ref, trif_ref, of_ref, sf_ref, kf_scr, bf_scr),
            (qb_ref, fb_ref, vb_ref, trib_ref, ob_ref, sb_ref, kb_scr, bb_scr))):
        hi = l1 + _log_sigmoid(f_ref[...])
        mx = jnp.maximum(lo, hi)
        mn = jnp.minimum(lo, hi)
        log_f = mx + jnp.log(1.0 + jnp.exp(mn - mx))
        k_scr[...] = 1.0 - jnp.exp(log_f)
        b_scr[...] = _split3_dot(tri_ref[...], log_f)
        units.append(_LaUnit(q_ref, k_scr, v_ref, b_scr, o_ref, s_ref, tb=tb, nh=nh, dk=dk, dv=dv,
                             reverse=d == 1, q_scale=1.0))

    @pl.when(pl.program_id(1) == 0)
    def _():
        for un in units:
            un.reset()

    _la_run(units, tb // LA_CHUNK)


def _gla_kernel(qkf_ref, vf_ref, smallf_ref, trif_ref, qkb_ref, vb_ref, smallb_ref, trib_ref,
                gw_ref, gb_ref, of_ref, ob_ref, sf_ref, bf_scr, sb_ref, bb_scr, *, tb, nh, dk, dv):
    hk = nh * dk
    units = []
    for d, (qk_ref, v_ref, small_ref, tri_ref, o_ref, s_ref, b_scr) in enumerate((
            (qkf_ref, vf_ref, smallf_ref, trif_ref, of_ref, sf_ref, bf_scr),
            (qkb_ref, vb_ref, smallb_ref, trib_ref, ob_ref, sb_ref, bb_scr))):
        code = small_ref[:, 384:512]
        pre = _dot(code.astype(BF16), gw_ref[:, d * hk:(d + 1) * hk]) + gb_ref[:, d * hk:(d + 1) * hk]
        log_g = _log_sigmoid(pre) * (1.0 / GLA_GATE_NORM)
        b_scr[...] = _split3_dot(tri_ref[...], log_g)
        units.append(_LaUnit(qk_ref.at[:, 0:hk], qk_ref.at[:, hk:2 * hk], v_ref, b_scr, o_ref, s_ref,
                             tb=tb, nh=nh, dk=dk, dv=dv, reverse=d == 1, q_scale=dk ** -0.5))

    @pl.when(pl.program_id(1) == 0)
    def _():
        for un in units:
            un.reset()

    _la_run(units, tb // LA_CHUNK)


def _chunk_tri(tb, reverse, chunk=LA_CHUNK):
    t = jnp.arange(tb)
    same = (t[:, None] // chunk) == (t[None, :] // chunk)
    tri = (t[None, :] >= t[:, None]) if reverse else (t[None, :] <= t[:, None])
    return (same & tri).astype(BF16)


def _la_index(d, n_batch, ctxb, latb):
    return lambda b, i: _seq_block(d, b, i, n_batch=n_batch, ctxb=ctxb, latb=latb)


def _hgrn2(p_all, gamma, *, layer, tb, n_batch, ctxb, latb):
    m = p_all.shape[0]
    c = gamma.shape[-1]
    nh = c // HG_DK
    in_specs, args, out_specs = [], [], []
    for d in range(2):
        blk = _la_index(d, n_batch, ctxb, latb)
        col = lambda off, blk=blk: pl.BlockSpec((tb, c), lambda b, i: (blk(b, i), off // c))
        in_specs += [col(C_HGQ), col(C_HGF + d * c), col(C_HGI), pl.BlockSpec((tb, tb), lambda b, i: (0, 0))]
        args += [p_all, p_all, p_all, _chunk_tri(tb, d == 1)]
        out_specs.append(col(0))
    unit_scratch = [pltpu.VMEM((nh, HG_DK, HG_DK), F32), pltpu.VMEM((tb, c), F32), pltpu.VMEM((tb, c), F32)]
    return pl.pallas_call(
        functools.partial(_hg_kernel, layer=layer, tb=tb, nh=nh, dk=HG_DK, dv=HG_DK),
        grid=(n_batch, ctxb + latb),
        in_specs=in_specs + [pl.BlockSpec(gamma.shape, lambda b, i: (0, 0))],
        out_specs=out_specs,
        out_shape=[jax.ShapeDtypeStruct((m, c), F32)] * 2,
        scratch_shapes=unit_scratch + unit_scratch,
        compiler_params=_cparams(2),
        name="hgrn2_scan",
    )(*args, gamma)


def _gla(p_all, gw, gb, *, tb, n_batch, ctxb, latb):
    m = p_all.shape[0]
    hk = gw.shape[-1] // 2
    dk = hk // GLA_HEADS
    c = 2 * hk
    dv = c // GLA_HEADS
    in_specs, args, out_specs = [], [], []
    for d in range(2):
        blk = _la_index(d, n_batch, ctxb, latb)
        col = lambda off, blk=blk: pl.BlockSpec((tb, c), lambda b, i: (blk(b, i), off // c))
        in_specs += [col(C_GLQK), col(C_GLV), col(C_SMALL), pl.BlockSpec((tb, tb), lambda b, i: (0, 0))]
        args += [p_all, p_all, p_all, _chunk_tri(tb, d == 1)]
        out_specs.append(col(0))
    unit_scratch = [pltpu.VMEM((hk // 128, (128 // dk) * dv, 128), F32), pltpu.VMEM((tb, hk), F32)]
    return pl.pallas_call(
        functools.partial(_gla_kernel, tb=tb, nh=GLA_HEADS, dk=dk, dv=dv),
        grid=(n_batch, ctxb + latb),
        in_specs=in_specs + [pl.BlockSpec(gw.shape, lambda b, i: (0, 0)),
                             pl.BlockSpec(gb.shape, lambda b, i: (0, 0))],
        out_specs=out_specs,
        out_shape=[jax.ShapeDtypeStruct((m, c), F32)] * 2,
        scratch_shapes=unit_scratch + unit_scratch,
        compiler_params=_cparams(2),
        name="gla_scan",
    )(*args, gw, gb)


def _la_post(o, gate, norm_g, nh):
    dv = o.shape[-1] // nh
    outs = []
    for h in range(nh):
        oh = o[:, h * dv:(h + 1) * dv]
        outs.append(oh * lax.rsqrt(jnp.mean(oh * oh, axis=-1, keepdims=True) + EPS))
    return jnp.concatenate(outs, axis=-1) * norm_g * (gate * _sigmoid(gate))


def _gelu(x):
    return 0.5 * x * (1.0 + jnp.tanh(0.7978845608028654 * (x + 0.044715 * (x * x * x))))


def _sgu_kernel(u_ref, v_ref, lnw_ref, lnb_ref, ws_ref, bs_ref, o_ref, *, rb):
    u = _gelu(u_ref[...])
    v = _gelu(v_ref[...])
    vc = v - jnp.mean(v, axis=-1, keepdims=True)
    vn = vc * lax.rsqrt(jnp.mean(vc * vc, axis=-1, keepdims=True) + EPS)
    vn = (vn * lnw_ref[...] + lnb_ref[...]).astype(BF16)
    gw = vn.shape[-1] // SGU_GROUPS
    for n in range(rb // SGU_CHUNK):
        rs = slice(n * SGU_CHUNK, (n + 1) * SGU_CHUNK)
        for g in range(SGU_GROUPS):
            cs = slice(g * gw, (g + 1) * gw)
            s = _dot(ws_ref[g], vn[rs, cs]) + bs_ref[g]
            o_ref[rs, cs] = u[rs, cs] * s


def _sgu(p_all, lnw, lnb, ws, bs, *, rb):
    m = p_all.shape[0]
    c = lnw.shape[-1]
    return pl.pallas_call(
        functools.partial(_sgu_kernel, rb=rb),
        grid=(m // rb,),
        in_specs=[pl.BlockSpec((rb, c), lambda i: (i, C_SGU // c)),
                  pl.BlockSpec((rb, c), lambda i: (i, C_SGU // c + 1)),
                  pl.BlockSpec((1, c), lambda i: (0, 0)),
                  pl.BlockSpec((1, c), lambda i: (0, 0)),
                  pl.BlockSpec(ws.shape, lambda i: (0, 0, 0)),
                  pl.BlockSpec(bs.shape, lambda i: (0, 0, 0))],
        out_specs=pl.BlockSpec((rb, c), lambda i: (i, 0)),
        out_shape=jax.ShapeDtypeStruct((m, c), F32),
        compiler_params=_cparams(1),
        name="sgu",
    )(p_all, p_all, lnw, lnb, ws, bs)


def _merge_kernel(raf_ref, rab_ref, rg_ref, rbonus_ref, hf_ref, hb_ref, hgate_ref, gf_ref, gb_ref,
                  ggate_ref, yd_ref, lnw_ref, lnb_ref, e_ref, hnorm_ref, gnorm_ref,
                  g0_ref, g1_ref, g2_ref, g3_ref, w_ref, o_ref, y_scr, *, hg_heads):
    @pl.when(pl.program_id(1) == 0)
    def _():
        y_scr[0] = _rw_post(raf_ref[...] + rab_ref[...], rg_ref[...], rbonus_ref[...], lnw_ref[...],
                            lnb_ref[...], e_ref[...]).astype(BF16)
        y_scr[1] = _la_post(hf_ref[...] + hb_ref[...], hgate_ref[...], hnorm_ref[...], hg_heads).astype(BF16)
        y_scr[2] = _la_post(gf_ref[...] + gb_ref[...], ggate_ref[...], gnorm_ref[...], GLA_HEADS).astype(BF16)
        y_scr[3] = yd_ref[...].astype(BF16)

    gs = (g0_ref, g1_ref, g2_ref, g3_ref)
    acc = None
    for j in range(N_BRANCH):
        t = _sigmoid(gs[j][...].astype(F32)) * _dot(y_scr[j], w_ref[j])
        acc = t if acc is None else acc + t
    o_ref[...] = acc.astype(BF16)


def _merge(rw, hg, gla, yd, p_gate, p_mix, vecs, e, w_branch, *, layer, tm, row_off):
    m = p_mix.shape[0]
    _, _, c, d = w_branch.shape
    tn = _pick((1024, 512), d)
    mo = m - row_off * tm
    ytok = pl.BlockSpec((tm, c), lambda i, j: (i + row_off, 0))
    mix = lambda off: pl.BlockSpec((tm, c), lambda i, j: (i + row_off, off // c))
    gate = lambda b: pl.BlockSpec((tm, tn), lambda i, j: (i + row_off, b * d // tn + j))
    vec = pl.BlockSpec((1, c), lambda i, j: (0, 0))
    return pl.pallas_call(
        functools.partial(_merge_kernel, hg_heads=c // HG_DK),
        grid=(mo // tm, d // tn),
        in_specs=[ytok, ytok, ytok, ytok, ytok, ytok, mix(C_HGG), ytok, ytok, mix(C_GLG), ytok,
                  vec, vec, pl.BlockSpec((c, c), lambda i, j: (0, 0)), vec, vec,
                  gate(0), gate(1), gate(2), gate(3),
                  pl.BlockSpec((None, N_BRANCH, c, tn), lambda i, j: (layer, 0, 0, j))],
        out_specs=pl.BlockSpec((tm, tn), lambda i, j: (i, j)),
        out_shape=jax.ShapeDtypeStruct((mo, d), BF16),
        scratch_shapes=[pltpu.VMEM((N_BRANCH, tm, c), BF16)],
        compiler_params=_cparams(2),
        name="merge",
    )(*rw, hg[0], hg[1], p_mix, gla[0], gla[1], p_mix, yd, *vecs[:2], e, *vecs[2:],
      p_gate, p_gate, p_gate, p_gate, w_branch)


def _outproj_kernel(m_ref, w_ref, xc_ref, xl_ref, mod_ref, o_ref, *, nctx, row_off):
    x = jnp.where(pl.program_id(0) + row_off < nctx, xc_ref[...], xl_ref[...])
    o_ref[...] = x + mod_ref[0, 2:3, :] * _dot(m_ref[...], w_ref[...])


def _outproj(mm, w_out, x_parts, mod, *, layer, tm, row_off, nctx, bpb):
    xc, xl, lat_off = x_parts
    mo, d = mm.shape
    tn = _pick((1024, 512), d)
    xspecs = [pl.BlockSpec((tm, tn), lambda i, j: (jnp.minimum(i + row_off, nctx - 1),
                                                   jnp.where(i + row_off < nctx, j, 0))),
              pl.BlockSpec((tm, tn), lambda i, j: (lat_off + jnp.maximum(i + row_off - nctx, 0),
                                                   jnp.where(i + row_off < nctx, 0, j)))]
    return pl.pallas_call(
        functools.partial(_outproj_kernel, nctx=nctx, row_off=row_off),
        grid=(mo // tm, d // tn),
        in_specs=[pl.BlockSpec((tm, d), lambda i, j: (i, 0)),
                  pl.BlockSpec((None, d, tn), lambda i, j: (layer, 0, j))] + xspecs + [
                  pl.BlockSpec((1, 6, tn), lambda i, j: (_row_group(i + row_off, nctx, bpb), 0, j))],
        out_specs=pl.BlockSpec((tm, tn), lambda i, j: (i, j)),
        out_shape=jax.ShapeDtypeStruct((mo, d), F32),
        compiler_params=_cparams(2),
        name="outproj",
    )(mm, w_out, xc, xl, mod)


def _mlp_kernel(x_ref, mod_ref, g_ref, w1_ref, w2_ref, gf_ref, o_ref, h_ref, acc_ref, *, final_norm):
    j = pl.program_id(1)

    @pl.when(j == 0)
    def _():
        h = _norm_mod(x_ref[...], g_ref[...], mod_ref[0, 3:4, :], mod_ref[0, 4:5, :])
        h_ref[...] = h.astype(BF16)
        acc_ref[...] = jnp.zeros_like(acc_ref)

    a = jnp.maximum(_dot(h_ref[...], w1_ref[...]), 0.0)
    acc_ref[...] += _dot((a * a).astype(BF16), w2_ref[...])

    @pl.when(j == pl.num_programs(1) - 1)
    def _():
        y = x_ref[...] + mod_ref[0, 5:6, :] * acc_ref[...]
        if final_norm:
            y = y * lax.rsqrt(jnp.mean(y * y, axis=-1, keepdims=True) + EPS) * gf_ref[...]
        o_ref[...] = y


def _mlp(x_in, mod, g, w1, w2, g_final, *, layer, tm, row_off, nctx, bpb, final_norm):
    mo, d = x_in.shape
    hid = w1.shape[-1]
    th = _pick((1024, 512, 256, 128), hid)
    return pl.pallas_call(
        functools.partial(_mlp_kernel, final_norm=final_norm),
        grid=(mo // tm, hid // th),
        in_specs=[pl.BlockSpec((tm, d), lambda i, j: (i, 0)),
                  pl.BlockSpec((1, 6, d), lambda i, j: (_row_group(i + row_off, nctx, bpb), 0, 0)),
                  pl.BlockSpec((1, d), lambda i, j: (0, 0)),
                  pl.BlockSpec((None, d, th), lambda i, j: (layer, 0, j)),
                  pl.BlockSpec((None, th, d), lambda i, j: (layer, j, 0)),
                  pl.BlockSpec((1, d), lambda i, j: (0, 0))],
        out_specs=pl.BlockSpec((tm, d), lambda i, j: (i, 0)),
        out_shape=jax.ShapeDtypeStruct((mo, d), F32),
        scratch_shapes=[pltpu.VMEM((tm, d), BF16), pltpu.VMEM((tm, d), F32)],
        compiler_params=_cparams(2),
        name="mlp",
    )(x_in, mod, g, w1, w2, g_final)


def _blockdiag2(w):
    _, r, c = w.shape
    z = jnp.zeros((r, c), w.dtype)
    out = jnp.concatenate([jnp.concatenate([w[0], z], axis=1), jnp.concatenate([z, w[1]], axis=1)], axis=0)
    return jnp.pad(out, ((0, 128 - 2 * r), (0, 0))).astype(BF16)


def _permute_w_in(w, d_model):
    c = d_model // N_BRANCH
    o = [0]
    for wd in (3 * c, 64 * 2, 64 * 2, 128, c, 2 * c, c, c, c, c, 32, c, 2 * c, N_BRANCH * d_model):
        o.append(o[-1] + wd)
    seg = lambda k: w[:, o[k]:o[k + 1]]
    pad = jnp.zeros((w.shape[0], 512 - 128 * 3 - 32), w.dtype)
    parts = [seg(13), seg(0), seg(5), seg(12), seg(4), seg(6), seg(7), seg(8), seg(9), seg(11),
             seg(1), seg(2), seg(3), seg(10), pad]
    out = jnp.concatenate(parts, axis=1).astype(BF16)
    assert out.shape[1] == N_GATE + N_MIX
    return out


def kernel(x, c, ctx, c_ctx, w_ada, b_ada, g_norm1, g_norm2, g_final, w_in, rw_conv, rw_w0, rw_w2,
           rw_a0, rw_a2, rw_g2, rw_kk, rw_ka, rw_rk, rw_ln_w, rw_ln_b, hg_gamma, hg_norm, gla_gw,
           gla_gb, gla_norm, sgu_ln_w, sgu_ln_b, sgu_w, sgu_b, w_branch, w_out, w_mlp1, w_mlp2):
    n_batch, seq, d_model = x.shape
    ctx_len = ctx.shape[1]
    depth = w_in.shape[0]
    cw = d_model // N_BRANCH
    assert cw == 512 and d_model == 2048, "column layout constants assume D_MODEL = 2048"
    m_ctx = n_batch * ctx_len

    tm = _pick((1024, 512, 256, 128), m_ctx, seq)
    tb = _pick((256, 128), ctx_len, seq)
    nctx, bpb = m_ctx // tm, seq // tm
    ctxb, latb = ctx_len // tb, seq // tb
    seqs = dict(tb=tb, n_batch=n_batch, ctxb=ctxb, latb=latb)

    x_parts = (ctx.reshape(m_ctx, d_model), x.reshape(n_batch * seq, d_model), 0)
    c_rows = jnp.concatenate([c_ctx[None, :], c, jnp.zeros((7 - n_batch, d_model), F32)], axis=0)
    mod_all = _ada(c_rows, w_ada, b_ada).reshape(depth, 8, 6, d_model)

    head_ones = (jnp.arange(cw)[:, None] // RW_HEAD == jnp.arange(cw)[None, :] // RW_HEAD).astype(BF16)
    w_branch_b, w_out_b = w_branch.astype(BF16), w_out.astype(BF16)
    w_mlp1_b, w_mlp2_b = w_mlp1.astype(BF16), w_mlp2.astype(BF16)
    row = lambda a: a.reshape(1, -1)

    for l in range(depth):
        last = l == depth - 1
        mod = mod_all[l]
        p_gate, p_all = _inproj(x_parts, mod, row(g_norm1[l]), _permute_w_in(w_in[l], d_model),
                                tm=tm, nctx=nctx, bpb=bpb)

        prm = dict(conv=rw_conv[l], w0=row(rw_w0[l]), w2=_blockdiag2(rw_w2[l]), a0=row(rw_a0[l]),
                   a2=_blockdiag2(rw_a2[l]), g2=rw_g2[l].astype(BF16), kk=row(rw_kk[l]),
                   ka=row(rw_ka[l]), rk=row(rw_rk[l]), e=head_ones)
        r, v, nkk, g, bonus, w, kka, km = _rw_prep(p_all, prm, tb=tb, nctx=m_ctx // tb,
                                                    ctx_bps=ctxb, lat_bps=latb)
        ys = _rw_scan(r, v, nkk, w, kka, km, **seqs)

        ob = _hgrn2(p_all, hg_gamma, layer=l, **seqs)

        gw = _blockdiag2(gla_gw[l])
        gb = row(gla_gb[l])
        oc = _gla(p_all, gw, gb, **seqs)

        bs = jnp.broadcast_to(sgu_b[l][:, :, None], sgu_w[l].shape)
        yd = _sgu(p_all, row(sgu_ln_w[l]), row(sgu_ln_b[l]), sgu_w[l].astype(BF16), bs,
                  rb=_pick((512, 256, 128), m_ctx, seq))

        row_off = nctx if last else 0
        vecs = (row(rw_ln_w[l]), row(rw_ln_b[l]), row(hg_norm[l]), row(gla_norm[l]))
        mm = _merge((ys[0], ys[1], g, bonus), ob, oc, yd, p_gate, p_all, vecs, head_ones,
                    w_branch_b, layer=l, tm=tm // 2, row_off=2 * row_off)
        x_mid = _outproj(mm, w_out_b, x_parts, mod, layer=l, tm=tm, row_off=row_off, nctx=nctx, bpb=bpb)
        x_all = _mlp(x_mid, mod, row(g_norm2[l]), w_mlp1_b, w_mlp2_b, row(g_final), layer=l,
                     tm=tm // 2, row_off=2 * row_off, nctx=2 * nctx, bpb=2 * bpb, final_norm=last)
        x_parts = (x_all, x_all, nctx)
    return x_all.reshape(n_batch, seq, d_model)
```

```python
import functools

import jax
import jax.numpy as jnp
from jax import lax
from jax.experimental import pallas as pl
from jax.experimental.pallas import tpu as pltpu

F32 = jnp.float32
BF16 = jnp.bfloat16

N_BRANCH = 4
RW_HEAD = 64
RW_LN_EPS = 64e-5
HG_DK = 128
GLA_HEADS = 4
GLA_GATE_NORM = 16.0
LA_CHUNK = 16
RW_CHUNK = 32
RW_SUB = 16
SGU_CHUNK = 128
SGU_GROUPS = 4
EPS = 1e-6

C_RKV = 0
C_HGF = 1536
C_SGU = 2560
C_HGQ = 3584
C_HGI = 4096
C_HGG = 4608
C_GLQK = 5120
C_GLV = 5632
C_GLG = 6144
C_SMALL = 6656
N_MIX = 7168
N_GATE = 8192

VMEM_LIMIT = 58 * 1024 * 1024


def _cparams(n_axes):
    return pltpu.CompilerParams(dimension_semantics=("arbitrary",) * n_axes,
                                vmem_limit_bytes=VMEM_LIMIT)


def _pick(n_list, *dims):
    for n in n_list:
        if all(d % n == 0 for d in dims):
            return n
    raise ValueError(f"no block size in {n_list} divides {dims}")


def _row_group(i, nctx, bpb):
    return jnp.where(i < nctx, 0, 1 + (i - nctx) // bpb)


def _dot(a, b):
    return jnp.dot(a, b, preferred_element_type=F32)


def _dot_nt(a, b):
    return lax.dot_general(a, b, (((1,), (1,)), ((), ())), preferred_element_type=F32)


def _dot_tn(a, b):
    return lax.dot_general(a, b, (((0,), (0,)), ((), ())), preferred_element_type=F32)


def _split_dot(x, e):
    hi = x.astype(BF16)
    lo = (x - hi.astype(F32)).astype(BF16)
    return _dot(hi, e) + _dot(lo, e)


def _split3_dot(e, x):
    p1 = x.astype(BF16)
    r1 = x - p1.astype(F32)
    p2 = r1.astype(BF16)
    p3 = (r1 - p2.astype(F32)).astype(BF16)
    return _dot(e, p1) + _dot(e, p2) + _dot(e, p3)


def _log_sigmoid(x):
    return jnp.minimum(x, 0.0) - jnp.log(1.0 + jnp.exp(-jnp.abs(x)))


def _sigmoid(x):
    return 0.5 * jnp.tanh(0.5 * x) + 0.5


def _ada_kernel(c_ref, w_ref, b_ref, o_ref):
    c = c_ref[...]
    act = c * _sigmoid(c)
    o_ref[0] = _dot(act.astype(BF16), w_ref[0].astype(BF16)) + b_ref[0]


def _ada(c_rows, w_ada, b_ada):
    n_layers, d, n = w_ada.shape
    tn = _pick((1024, 512, 256, 128), n)
    return pl.pallas_call(
        _ada_kernel,
        grid=(n_layers, n // tn),
        in_specs=[pl.BlockSpec((8, d), lambda l, j: (0, 0)),
                  pl.BlockSpec((1, d, tn), lambda l, j: (l, 0, j)),
                  pl.BlockSpec((1, 1, tn), lambda l, j: (l, 0, j))],
        out_specs=pl.BlockSpec((1, 8, tn), lambda l, j: (l, 0, j)),
        out_shape=jax.ShapeDtypeStruct((n_layers, 8, n), F32),
        compiler_params=_cparams(2),
        name="ada_mod",
    )(c_rows, w_ada, b_ada.reshape(n_layers, 1, n))


def _norm_mod(x, g, shift, scale):
    y = x * lax.rsqrt(jnp.mean(x * x, axis=-1, keepdims=True) + EPS) * g
    return y * (1.0 + scale) + shift


def _two_part_specs(block, nctx, lat_off, col):
    return [pl.BlockSpec(block, lambda i, j: (jnp.minimum(i, nctx - 1), col(j)), pipeline_mode=pl.Buffered(1)),
            pl.BlockSpec(block, lambda i, j: (lat_off + jnp.maximum(i - nctx, 0), col(j)))]


def _inproj_kernel(xc_ref, xl_ref, mod_ref, g_ref, w_ref, og_ref, om_ref, h_ref, *, nctx, n_gate_tiles):
    i, j = pl.program_id(0), pl.program_id(1)

    @pl.when(j == 0)
    def _():
        x = jnp.where(i < nctx, xc_ref[...], xl_ref[...])
        h = _norm_mod(x, g_ref[...], mod_ref[0, 0:1, :], mod_ref[0, 1:2, :])
        h_ref[...] = h.astype(BF16)

    @pl.when(j < n_gate_tiles)
    def _():
        og_ref[...] = _dot(h_ref[...], w_ref[...]).astype(BF16)

    @pl.when(j >= n_gate_tiles)
    def _():
        om_ref[...] = _dot(h_ref[...], w_ref[...])


def _inproj(x_parts, mod, g, w, *, tm, nctx, bpb):
    xc, xl, lat_off = x_parts
    d = xc.shape[1]
    m = (nctx + (xl.shape[0] // tm - lat_off)) * tm
    tn = _pick((1024, 512), N_GATE, N_MIX)
    ng = N_GATE // tn
    return pl.pallas_call(
        functools.partial(_inproj_kernel, nctx=nctx, n_gate_tiles=ng),
        grid=(m // tm, (N_GATE + N_MIX) // tn),
        in_specs=_two_part_specs((tm, d), nctx, lat_off, lambda j: 0) + [
            pl.BlockSpec((1, 6, d), lambda i, j: (_row_group(i, nctx, bpb), 0, 0)),
            pl.BlockSpec((1, d), lambda i, j: (0, 0)),
            pl.BlockSpec((d, tn), lambda i, j: (0, j))],
        out_specs=[pl.BlockSpec((tm, tn), lambda i, j: (i, jnp.minimum(j, ng - 1))),
                   pl.BlockSpec((tm, tn), lambda i, j: (i, jnp.maximum(j - ng, 0)))],
        out_shape=[jax.ShapeDtypeStruct((m, N_GATE), BF16), jax.ShapeDtypeStruct((m, N_MIX), F32)],
        scratch_shapes=[pltpu.VMEM((tm, d), BF16)],
        compiler_params=_cparams(2),
        name="inproj",
    )(xc, xl, mod, g, w)


def _rw_prep_kernel(rkv_ref, prev_ref, next_ref, small_ref, conv_ref, w0_ref, w2_ref, a0_ref,
                    a2_ref, g2_ref, kk_ref, ka_ref, rk_ref, e_ref,
                    r_out, v_out, nkk_out, g_out, bonus_out, w_out, kka_out, km_out,
                    *, tb, nctx, ctx_bps, lat_bps):
    i = pl.program_id(0)
    c = r_out.shape[-1]
    j = jnp.where(i < nctx, i, i - nctx)
    bps = jnp.where(i < nctx, ctx_bps, lat_bps)
    first = lax.rem(j, bps) == 0
    last = lax.rem(j, bps) == bps - 1

    blk = rkv_ref[...]
    rows = lax.broadcasted_iota(jnp.int32, (tb, 1), 0)
    prev_row = jnp.where(first, 0.0, prev_ref[7:8, :])
    next_row = jnp.where(last, 0.0, next_ref[0:1, :])
    xm1 = jnp.where(rows == 0, prev_row, pltpu.roll(blk, 1, 0))
    xp1 = jnp.where(rows == tb - 1, next_row, pltpu.roll(blk, tb - 1, 0))
    conv = conv_ref[0:1, :] * xm1 + conv_ref[1:2, :] * blk + conv_ref[2:3, :] * xp1
    r = conv[:, 0:c]
    k = conv[:, c:2 * c]
    v = conv[:, 2 * c:3 * c]

    small = small_ref[...]
    wl = small[:, 0:128]
    al = small[:, 128:256]
    gl = small[:, 256:384]
    w_pre = w0_ref[...] + _dot(jnp.tanh(wl).astype(BF16), w2_ref[...])
    softplus = jnp.maximum(-w_pre, 0.0) + jnp.log(1.0 + jnp.exp(-jnp.abs(w_pre)))
    log_decay = -jnp.exp(-softplus - 0.5)
    a = _sigmoid(a0_ref[...] + _dot(al.astype(BF16), a2_ref[...]))
    g = _dot(_sigmoid(gl).astype(BF16), g2_ref[...])

    e = e_ref[...]
    kkv = k * kk_ref[...]
    kk = kkv * lax.rsqrt(_split_dot(kkv * kkv, e) + 1e-12)
    bonus = _split_dot(r * k * rk_ref[...], e) * v

    r_out[...] = r.astype(BF16)
    v_out[...] = v.astype(BF16)
    nkk_out[...] = (-kk).astype(BF16)
    g_out[...] = g
    bonus_out[...] = bonus
    for d in range(2):
        a_d = a[:, d * c:(d + 1) * c]
        w_out[d] = log_decay[:, d * c:(d + 1) * c]
        kka_out[d] = (kk * a_d).astype(BF16)
        km_out[d] = (k * (1.0 + (a_d - 1.0) * ka_ref[...])).astype(BF16)


def _rw_prep(p_all, prm, *, tb, nctx, ctx_bps, lat_bps):
    m = p_all.shape[0]
    c = prm["kk"].shape[-1]
    nblk = m // tb
    t8 = tb // 8
    full = lambda shape: pl.BlockSpec(shape, lambda i: (0,) * len(shape))
    tok = pl.BlockSpec((tb, c), lambda i: (i, 0))
    tok2 = pl.BlockSpec((2, tb, c), lambda i: (0, i, 0))
    one, one_b = jax.ShapeDtypeStruct((m, c), F32), jax.ShapeDtypeStruct((m, c), BF16)
    two, two_b = jax.ShapeDtypeStruct((2, m, c), F32), jax.ShapeDtypeStruct((2, m, c), BF16)
    return pl.pallas_call(
        functools.partial(_rw_prep_kernel, tb=tb, nctx=nctx, ctx_bps=ctx_bps, lat_bps=lat_bps),
        grid=(nblk,),
        in_specs=[pl.BlockSpec((tb, 3 * c), lambda i: (i, C_RKV // (3 * c))),
                  pl.BlockSpec((8, 3 * c), lambda i: (jnp.maximum(i * t8 - 1, 0), 0)),
                  pl.BlockSpec((8, 3 * c), lambda i: (jnp.minimum((i + 1) * t8, m // 8 - 1), 0)),
                  pl.BlockSpec((tb, 512), lambda i: (i, C_SMALL // 512)),
                  full((3, 3 * c)), full((1, 2 * c)), full((128, 2 * c)), full((1, 2 * c)),
                  full((128, 2 * c)), full((128, c)), full((1, c)), full((1, c)), full((1, c)),
                  full((c, c))],
        out_specs=[tok, tok, tok, tok, tok, tok2, tok2, tok2],
        out_shape=[one_b, one_b, one_b, one, one, two, two_b, two_b],
        compiler_params=_cparams(1),
        name="rwkv_prep",
    )(p_all, p_all, p_all, p_all, prm["conv"], prm["w0"], prm["w2"], prm["a0"], prm["a2"],
      prm["g2"], prm["kk"], prm["ka"], prm["rk"], prm["e"])


class _RwUnit:
    def __init__(self, refs, scratch, *, tb, npair, reverse):
        (self.r_ref, self.v_ref, self.a_ref, self.lw_ref, self.b_ref, self.k_ref, self.tri_ref,
         self.y_ref) = refs
        (self.s_ref, self.c_scr, self.ag_scr, self.rg_scr, self.bg_scr, self.kg_scr) = scratch[:6]
        self.slots = (scratch[6:10], scratch[10:14])
        self.reverse = reverse
        self.nch = tb // RW_CHUNK
        self.pairs = range(npair)
        L, hd = RW_CHUNK, RW_HEAD
        self.lss = [slice(p * 2 * hd, (p + 1) * 2 * hd) for p in self.pairs]
        lane = lax.broadcasted_iota(jnp.int32, (1, 2 * hd), 1)
        self.lo = (lane < hd).astype(F32)
        self.hi = 1.0 - self.lo
        ti = lax.broadcasted_iota(jnp.int32, (2 * L, 2 * hd), 0)
        sl = lax.broadcasted_iota(jnp.int32, (2 * L, 2 * hd), 1)
        sj = sl & (L - 1)
        tt = ti & (L - 1)
        earlier = (sj > tt) if reverse else (sj < tt)
        assert 4 * L == 2 * hd and L == 2 * RW_SUB
        self.aa_mask = (earlier | ((ti >= L) & (sj == tt))).astype(F32)
        self.col_idx0 = jnp.where(lax.broadcasted_iota(jnp.int32, (RW_SUB, 2 * hd), 1) < hd, 0, L)
        self.blk_mask = ((lax.broadcasted_iota(jnp.int32, (2 * hd, 2 * hd), 0) >> 6)
                         == (lax.broadcasted_iota(jnp.int32, (2 * hd, 2 * hd), 1) >> 6)).astype(F32)
        self.zeros2l = jnp.zeros((2 * L, 2 * hd), F32)
        self.zeros_sub = jnp.zeros((RW_SUB, 2 * hd), F32)
        self.blocks = (1, 0) if reverse else (0, 1)
        self.last = 0 if reverse else L - 1
        self.order = range(L - 1, -1, -1) if reverse else range(L)

    def reset(self):
        self.s_ref[...] = jnp.zeros_like(self.s_ref)

    def prologue(self):
        lw = self.lw_ref[0]
        c = _split3_dot(self.tri_ref[...], lw)
        self.c_scr[...] = c
        enc = jnp.exp(-c)
        self.ag_scr[...] = self.a_ref[...] * jnp.exp(c - lw)
        self.rg_scr[...] = self.r_ref[...] * jnp.exp(c)
        self.bg_scr[...] = self.b_ref[0] * enc
        self.kg_scr[...] = self.k_ref[0] * enc

    def rows(self, ci):
        L = RW_CHUNK
        cc = (self.nch - 1 - ci) if self.reverse else ci
        return pl.ds(pl.multiple_of(cc * L, L), L)

    def halves(self, x):
        return [x * self.lo, x * self.hi]

    def lhs_of(self, rows):
        return [jnp.concatenate([self.ag_scr[rows, ls], self.rg_scr[rows, ls]], axis=0).astype(BF16)
                for ls in self.lss]

    def prep_aa(self, rows):
        lhs = self.lhs_of(rows)
        out = []
        for p in self.pairs:
            ls = self.lss[p]
            rhs = jnp.concatenate(self.halves(self.bg_scr[rows, ls]) + self.halves(self.kg_scr[rows, ls]),
                                  axis=0)
            out.append(_dot_nt(lhs[p], rhs.astype(BF16)))
        return out

    def prep_akv(self, rows, aa_raw, slot):
        _, akv_scr, aar_scr, na_scr = slot
        L = RW_CHUNK
        aa = [x * self.aa_mask for x in aa_raw]
        for p in self.pairs:
            v = self.v_ref[rows, self.lss[p]]
            vv = jnp.concatenate([self.zeros2l] + self.halves(v), axis=0).astype(BF16)
            akv_scr[p] = _dot(aa[p][0:L].astype(BF16), vv)
            aar_scr[p] = aa[p][L:2 * L]
            na_scr[p] = aa[p][0:L]
        return aa

    def prep_cols(self, aa, slot):
        for p in self.pairs:
            for s in range(RW_CHUNK):
                r0 = (s // RW_SUB) * RW_SUB
                slot[0][p, s] = jnp.take_along_axis(aa[p][r0:r0 + RW_SUB], self.col_idx0 + s, axis=1)

    def adv_g(self, rows):
        lhs = self.lhs_of(rows)
        return [_dot_nt(lhs[p], self.s_ref[p].astype(BF16)) for p in self.pairs]

    def adv_solve(self, g, slot):
        L, sub = RW_CHUNK, RW_SUB
        col_scr, akv_scr, _, na_scr = slot
        first, second = self.blocks
        ub = {}
        for blk in self.blocks:
            r0 = blk * sub
            ub[blk] = [g[p][r0:r0 + sub] + akv_scr[p, r0:r0 + sub] for p in self.pairs]
            if blk == second:
                for p in self.pairs:
                    done = ub[first][p]
                    z = self.zeros_sub
                    parts = [z, z, z, z]
                    parts[first], parts[2 + first] = done * self.lo, done * self.hi
                    stack = jnp.concatenate(parts + [self.zeros2l], axis=0).astype(BF16)
                    ub[blk][p] = ub[blk][p] + _dot(na_scr[p, r0:r0 + sub].astype(BF16), stack)
            steps = [s for s in self.order if s // sub == blk]
            for s in steps:
                for p in self.pairs:
                    ub[blk][p] = ub[blk][p] + col_scr[p, s] * ub[blk][p][s - r0:s - r0 + 1]
        return [jnp.concatenate([ub[0][p], ub[1][p]], axis=0) for p in self.pairs]

    def adv_out(self, rows, g, u, slot):
        L = RW_CHUNK
        for p in self.pairs:
            v = self.v_ref[rows, self.lss[p]]
            uv = jnp.concatenate(self.halves(u[p]) + self.halves(v), axis=0).astype(BF16)
            self.y_ref[rows, self.lss[p]] = g[p][L:2 * L] + _dot(slot[2][p].astype(BF16), uv)
        for p in self.pairs:
            ls = self.lss[p]
            cch = self.c_scr[rows, ls]
            cl = cch[self.last:self.last + 1]
            dec = jnp.exp(cl - cch)
            bk = jnp.concatenate([self.b_ref[0, rows, ls] * dec, self.k_ref[0, rows, ls] * dec], axis=0)
            upd = _dot_tn(jnp.concatenate([u[p], self.v_ref[rows, ls]], axis=0).astype(BF16),
                          bk.astype(BF16))
            self.s_ref[p] = self.s_ref[p] * jnp.exp(cl) + upd * self.blk_mask


N_RW_REFS = 8
N_RW_SCRATCH = 14


def _rw_scan_kernel(*refs, tb, npair):
    n_in = N_RW_REFS - 1
    units = []
    for d in range(2):
        ins = refs[d * n_in:(d + 1) * n_in]
        out = refs[2 * n_in + d]
        scr = refs[2 * n_in + 2 + d * N_RW_SCRATCH:2 * n_in + 2 + (d + 1) * N_RW_SCRATCH]
        units.append(_RwUnit(tuple(ins) + (out,), scr, tb=tb, npair=npair, reverse=d == 1))
    nch = tb // RW_CHUNK

    @pl.when(pl.program_id(1) == 0)
    def _():
        for un in units:
            un.reset()

    for un in units:
        un.prologue()

    def step(ci_adv, sa, ci_prep, sp):
        rows_a = [un.rows(ci_adv) for un in units]
        rows_p = [un.rows(ci_prep) for un in units]
        aa_raw = [un.prep_aa(rp) for un, rp in zip(units, rows_p)]
        g = [un.adv_g(ra) for un, ra in zip(units, rows_a)]
        aa = [un.prep_akv(rp, x, un.slots[sp]) for un, rp, x in zip(units, rows_p, aa_raw)]
        u = [un.adv_solve(x, un.slots[sa]) for un, x in zip(units, g)]
        for un, x in zip(units, aa):
            un.prep_cols(x, un.slots[sp])
        for un, ra, x, y in zip(units, rows_a, g, u):
            un.adv_out(ra, x, y, un.slots[sa])

    for un in units:
        r0 = un.rows(0)
        un.prep_cols(un.prep_akv(r0, un.prep_aa(r0), un.slots[0]), un.slots[0])

    def two_chunks(j, carry):
        c0 = 2 * j
        step(c0, 0, c0 + 1, 1)
        step(c0 + 1, 1, jnp.minimum(c0 + 2, nch - 1), 0)
        return carry

    lax.fori_loop(0, nch // 2, two_chunks, 0)


def _seq_block(d, b, i, *, n_batch, ctxb, latb):
    is_ctx = i < ctxb
    cs = jnp.where(d == 0, i, ctxb - 1 - i)
    lj = jnp.where(d == 0, i - ctxb, latb - 1 - (i - ctxb))
    return jnp.where(is_ctx, b * ctxb + cs, n_batch * ctxb + b * latb + lj)


def _rw_scan(r, v, nkk, lw, kka, km, *, tb, n_batch, ctxb, latb):
    m, c = r.shape
    npair = c // (2 * RW_HEAD)
    lanes = 2 * RW_HEAD
    in_specs, args = [], []
    for d in range(2):
        blk = _la_index(d, n_batch, ctxb, latb)
        tok = pl.BlockSpec((tb, c), lambda b, i, blk=blk: (blk(b, i), 0))
        tokd = pl.BlockSpec((1, tb, c), lambda b, i, blk=blk, d=d: (d, blk(b, i), 0))
        in_specs += [tok, tok, tok, tokd, tokd, tokd, pl.BlockSpec((tb, tb), lambda b, i: (0, 0))]
        args += [r, v, nkk, lw, kka, km, _chunk_tri(tb, d == 1, RW_CHUNK)]
    out_specs = [pl.BlockSpec((tb, c), lambda b, i, blk=_la_index(d, n_batch, ctxb, latb): (blk(b, i), 0))
                 for d in range(2)]
    buf = pltpu.VMEM((tb, c), F32)
    unit_scratch = [pltpu.VMEM((npair, lanes, lanes), F32), buf, buf, buf, buf, buf] + 2 * [
        pltpu.VMEM((npair, RW_CHUNK, RW_SUB, lanes), F32),
        pltpu.VMEM((npair, RW_CHUNK, lanes), F32),
        pltpu.VMEM((npair, RW_CHUNK, lanes), F32),
        pltpu.VMEM((npair, RW_CHUNK, lanes), F32)]
    assert len(unit_scratch) == N_RW_SCRATCH
    return pl.pallas_call(
        functools.partial(_rw_scan_kernel, tb=tb, npair=npair),
        grid=(n_batch, ctxb + latb),
        in_specs=in_specs,
        out_specs=out_specs,
        out_shape=[jax.ShapeDtypeStruct((m, c), F32)] * 2,
        scratch_shapes=unit_scratch + unit_scratch,
        compiler_params=_cparams(2),
        name="rwkv_scan",
    )(*args)


def _rw_post(y, g, bonus, lnw, lnb, e):
    inv = 1.0 / RW_HEAD
    yc = y - _split_dot(y, e) * inv
    var = _split_dot(yc * yc, e) * inv
    yn = yc * lax.rsqrt(var + RW_LN_EPS)
    return (yn * lnw + lnb + bonus) * g


class _LaUnit:
    def __init__(self, q_ref, k_ref, v_ref, b_ref, o_ref, s_ref, *, tb, nh, dk, dv, reverse, q_scale):
        self.q_ref, self.k_ref, self.v_ref, self.b_ref, self.o_ref, self.s_ref = (
            q_ref, k_ref, v_ref, b_ref, o_ref, s_ref)
        self.nch = tb // LA_CHUNK
        self.dk, self.dv, self.reverse, self.q_scale = dk, dv, reverse, q_scale
        self.lanes = 128
        self.pack = self.lanes // dk
        self.groups = range(nh // self.pack)
        self.rowi = lax.broadcasted_iota(jnp.int32, (LA_CHUNK, 1), 0)
        lane = lax.broadcasted_iota(jnp.int32, (1, self.lanes), 1)
        self.head_lanes = [(lane // dk == j).astype(F32) for j in range(self.pack)]
        if self.pack > 1:
            rows = lax.broadcasted_iota(jnp.int32, (self.pack * dv, self.lanes), 0)
            cols = lax.broadcasted_iota(jnp.int32, (self.pack * dv, self.lanes), 1)
            self.blk_mask = ((rows // dv) == (cols // dk)).astype(F32)

    def reset(self):
        self.s_ref[...] = jnp.zeros_like(self.s_ref)

    def load(self, ci):
        cc = (self.nch - 1 - ci) if self.reverse else ci
        rows = pl.ds(pl.multiple_of(cc * LA_CHUNK, LA_CHUNK), LA_CHUNK)
        pack, dv, lanes = self.pack, self.dv, self.lanes
        out = []
        for g in self.groups:
            ks = slice(g * lanes, (g + 1) * lanes)
            q = self.q_ref[rows, ks] * self.q_scale
            k = self.k_ref[rows, ks]
            v = self.v_ref[rows, g * pack * dv:(g + 1) * pack * dv]
            b = self.b_ref[rows, ks]
            o_inter = _dot_nt((q * jnp.exp(b)).astype(BF16), self.s_ref[g].astype(BF16))
            out.append((rows, q, k, v, b, [o_inter[:, j * dv:(j + 1) * dv] for j in range(pack)]))
        return out

    def intra(self, ops):
        pack, dv, half = self.pack, self.dv, LA_CHUNK // 2
        rowi = self.rowi[0:half]
        for g in self.groups:
            rows, q, k, v, b, o = ops[g]
            qh = [q[0:half], q[half:]]
            bh = [b[0:half], b[half:]]
            oh = [[oj[0:half], oj[half:]] for oj in o]
            for s in range(LA_CHUNK):
                for h in range(2):
                    lo_row, hi_row = h * half, (h + 1) * half - 1
                    if (hi_row > s) if self.reverse else (lo_row < s):
                        if (lo_row > s) if self.reverse else (hi_row < s):
                            continue
                        valid = (rowi + lo_row <= s) if self.reverse else (rowi + lo_row >= s)
                    else:
                        valid = None
                    term = (qh[h] * k[s:s + 1]) * jnp.exp(bh[h] - b[s:s + 1])
                    for j in range(pack):
                        tj = term if pack == 1 else term * self.head_lanes[j]
                        col = jnp.sum(tj, axis=-1, keepdims=True)
                        if valid is not None:
                            col = jnp.where(valid, col, 0.0)
                        oh[j][h] = oh[j][h] + col * v[s:s + 1, j * dv:(j + 1) * dv]
            for j in range(pack):
                self.o_ref[rows, (g * pack + j) * dv:(g * pack + j + 1) * dv] = jnp.concatenate(oh[j], axis=0)

    def update(self, ops):
        for g in self.groups:
            _, _, k, v, b, _ = ops[g]
            b_last = b[0:1] if self.reverse else b[LA_CHUNK - 1:LA_CHUNK]
            upd = _dot_tn(v.astype(BF16), (k * jnp.exp(b_last - b)).astype(BF16))
            if self.pack > 1:
                upd = upd * self.blk_mask
            self.s_ref[g] = self.s_ref[g] * jnp.exp(b_last) + upd


def _la_run(units, nch):
    def chunk(ci, carry):
        ops = [un.load(ci) for un in units]
        for un, x in zip(units, ops):
            un.update(x)
        for un, x in zip(units, ops):
            un.intra(x)
        return carry

    lax.fori_loop(0, nch, chunk, 0)


def _hg_kernel(qf_ref, ff_ref, vf_ref, trif_ref, qb_ref, fb_ref, vb_ref, trib_ref, gamma_ref,
               of_ref, ob_ref, sf_ref, kf_scr, bf_scr, sb_ref, kb_scr, bb_scr, *, layer, tb, nh, dk, dv):
    gam = gamma_ref[...]
    ex = jnp.exp(gam - jnp.max(gam, axis=0, keepdims=True))
    p = ex / jnp.sum(ex, axis=0, keepdims=True)
    cum = p[0:1]
    for i in range(1, layer + 1):
        cum = cum + p[i:i + 1]
    lb = cum - p[0:1]
    lo = jnp.log(lb)
    l1 = jnp.log(1.0 - lb)

    units = []
    for d, (q_ref, f_ref, v_ref, tri_ref, o_ref, s_ref, k_scr, b_scr) in enumerate((
            (qf_ref, ff_ref, vf_ref, trif_ref, of_ref, sf_ref, kf_scr, bf_scr),
            (qb_ref, fb_ref, vb_ref, trib_ref, ob_ref, sb_ref, kb_scr, bb_scr))):
        hi = l1 + _log_sigmoid(f_ref[...])
        mx = jnp.maximum(lo, hi)
        mn = jnp.minimum(lo, hi)
        log_f = mx + jnp.log(1.0 + jnp.exp(mn - mx))
        k_scr[...] = 1.0 - jnp.exp(log_f)
        b_scr[...] = _split3_dot(tri_ref[...], log_f)
        units.append(_LaUnit(q_ref, k_scr, v_ref, b_scr, o_ref, s_ref, tb=tb, nh=nh, dk=dk, dv=dv,
                             reverse=d == 1, q_scale=1.0))

    @pl.when(pl.program_id(1) == 0)
    def _():
        for un in units:
            un.reset()

    _la_run(units, tb // LA_CHUNK)


def _gla_kernel(qkf_ref, vf_ref, smallf_ref, trif_ref, qkb_ref, vb_ref, smallb_ref, trib_ref,
                gw_ref, gb_ref, of_ref, ob_ref, sf_ref, bf_scr, sb_ref, bb_scr, *, tb, nh, dk, dv):
    hk = nh * dk
    units = []
    for d, (qk_ref, v_ref, small_ref, tri_ref, o_ref, s_ref, b_scr) in enumerate((
            (qkf_ref, vf_ref, smallf_ref, trif_ref, of_ref, sf_ref, bf_scr),
            (qkb_ref, vb_ref, smallb_ref, trib_ref, ob_ref, sb_ref, bb_scr))):
        code = small_ref[:, 384:512]
        pre = _dot(code.astype(BF16), gw_ref[:, d * hk:(d + 1) * hk]) + gb_ref[:, d * hk:(d + 1) * hk]
        log_g = _log_sigmoid(pre) * (1.0 / GLA_GATE_NORM)
        b_scr[...] = _split3_dot(tri_ref[...], log_g)
        units.append(_LaUnit(qk_ref.at[:, 0:hk], qk_ref.at[:, hk:2 * hk], v_ref, b_scr, o_ref, s_ref,
                             tb=tb, nh=nh, dk=dk, dv=dv, reverse=d == 1, q_scale=dk ** -0.5))

    @pl.when(pl.program_id(1) == 0)
    def _():
        for un in units:
            un.reset()

    _la_run(units, tb // LA_CHUNK)


def _chunk_tri(tb, reverse, chunk=LA_CHUNK):
    t = jnp.arange(tb)
    same = (t[:, None] // chunk) == (t[None, :] // chunk)
    tri = (t[None, :] >= t[:, None]) if reverse else (t[None, :] <= t[:, None])
    return (same & tri).astype(BF16)


def _la_index(d, n_batch, ctxb, latb):
    return lambda b, i: _seq_block(d, b, i, n_batch=n_batch, ctxb=ctxb, latb=latb)


def _hgrn2(p_all, gamma, *, layer, tb, n_batch, ctxb, latb):
    m = p_all.shape[0]
    c = gamma.shape[-1]
    nh = c // HG_DK
    in_specs, args, out_specs = [], [], []
    for d in range(2):
        blk = _la_index(d, n_batch, ctxb, latb)
        col = lambda off, blk=blk: pl.BlockSpec((tb, c), lambda b, i: (blk(b, i), off // c))
        in_specs += [col(C_HGQ), col(C_HGF + d * c), col(C_HGI), pl.BlockSpec((tb, tb), lambda b, i: (0, 0))]
        args += [p_all, p_all, p_all, _chunk_tri(tb, d == 1)]
        out_specs.append(col(0))
    unit_scratch = [pltpu.VMEM((nh, HG_DK, HG_DK), F32), pltpu.VMEM((tb, c), F32), pltpu.VMEM((tb, c), F32)]
    return pl.pallas_call(
        functools.partial(_hg_kernel, layer=layer, tb=tb, nh=nh, dk=HG_DK, dv=HG_DK),
        grid=(n_batch, ctxb + latb),
        in_specs=in_specs + [pl.BlockSpec(gamma.shape, lambda b, i: (0, 0))],
        out_specs=out_specs,
        out_shape=[jax.ShapeDtypeStruct((m, c), F32)] * 2,
        scratch_shapes=unit_scratch + unit_scratch,
        compiler_params=_cparams(2),
        name="hgrn2_scan",
    )(*args, gamma)


def _gla(p_all, gw, gb, *, tb, n_batch, ctxb, latb):
    m = p_all.shape[0]
    hk = gw.shape[-1] // 2
    dk = hk // GLA_HEADS
    c = 2 * hk
    dv = c // GLA_HEADS
    in_specs, args, out_specs = [], [], []
    for d in range(2):
        blk = _la_index(d, n_batch, ctxb, latb)
        col = lambda off, blk=blk: pl.BlockSpec((tb, c), lambda b, i: (blk(b, i), off // c))
        in_specs += [col(C_GLQK), col(C_GLV), col(C_SMALL), pl.BlockSpec((tb, tb), lambda b, i: (0, 0))]
        args += [p_all, p_all, p_all, _chunk_tri(tb, d == 1)]
        out_specs.append(col(0))
    unit_scratch = [pltpu.VMEM((hk // 128, (128 // dk) * dv, 128), F32), pltpu.VMEM((tb, hk), F32)]
    return pl.pallas_call(
        functools.partial(_gla_kernel, tb=tb, nh=GLA_HEADS, dk=dk, dv=dv),
        grid=(n_batch, ctxb + latb),
        in_specs=in_specs + [pl.BlockSpec(gw.shape, lambda b, i: (0, 0)),
                             pl.BlockSpec(gb.shape, lambda b, i: (0, 0))],
        out_specs=out_specs,
        out_shape=[jax.ShapeDtypeStruct((m, c), F32)] * 2,
        scratch_shapes=unit_scratch + unit_scratch,
        compiler_params=_cparams(2),
        name="gla_scan",
    )(*args, gw, gb)


def _la_post(o, gate, norm_g, nh):
    dv = o.shape[-1] // nh
    outs = []
    for h in range(nh):
        oh = o[:, h * dv:(h + 1) * dv]
        outs.append(oh * lax.rsqrt(jnp.mean(oh * oh, axis=-1, keepdims=True) + EPS))
    return jnp.concatenate(outs, axis=-1) * norm_g * (gate * _sigmoid(gate))


def _gelu(x):
    return 0.5 * x * (1.0 + jnp.tanh(0.7978845608028654 * (x + 0.044715 * (x * x * x))))


def _sgu_kernel(u_ref, v_ref, lnw_ref, lnb_ref, ws_ref, bs_ref, o_ref, *, rb):
    u = _gelu(u_ref[...])
    v = _gelu(v_ref[...])
    vc = v - jnp.mean(v, axis=-1, keepdims=True)
    vn = vc * lax.rsqrt(jnp.mean(vc * vc, axis=-1, keepdims=True) + EPS)
    vn = (vn * lnw_ref[...] + lnb_ref[...]).astype(BF16)
    gw = vn.shape[-1] // SGU_GROUPS
    for n in range(rb // SGU_CHUNK):
        rs = slice(n * SGU_CHUNK, (n + 1) * SGU_CHUNK)
        for g in range(SGU_GROUPS):
            cs = slice(g * gw, (g + 1) * gw)
            s = _dot(ws_ref[g], vn[rs, cs]) + bs_ref[g]
            o_ref[rs, cs] = u[rs, cs] * s


def _sgu(p_all, lnw, lnb, ws, bs, *, rb):
    m = p_all.shape[0]
    c = lnw.shape[-1]
    return pl.pallas_call(
        functools.partial(_sgu_kernel, rb=rb),
        grid=(m // rb,),
        in_specs=[pl.BlockSpec((rb, c), lambda i: (i, C_SGU // c)),
                  pl.BlockSpec((rb, c), lambda i: (i, C_SGU // c + 1)),
                  pl.BlockSpec((1, c), lambda i: (0, 0)),
                  pl.BlockSpec((1, c), lambda i: (0, 0)),
                  pl.BlockSpec(ws.shape, lambda i: (0, 0, 0)),
                  pl.BlockSpec(bs.shape, lambda i: (0, 0, 0))],
        out_specs=pl.BlockSpec((rb, c), lambda i: (i, 0)),
        out_shape=jax.ShapeDtypeStruct((m, c), F32),
        compiler_params=_cparams(1),
        name="sgu",
    )(p_all, p_all, lnw, lnb, ws, bs)


def _merge_kernel(raf_ref, rab_ref, rg_ref, rbonus_ref, hf_ref, hb_ref, hgate_ref, gf_ref, gb_ref,
                  ggate_ref, yd_ref, lnw_ref, lnb_ref, e_ref, hnorm_ref, gnorm_ref,
                  g0_ref, g1_ref, g2_ref, g3_ref, w_ref, o_ref, y_scr, *, hg_heads):
    @pl.when(pl.program_id(1) == 0)
    def _():
        y_scr[0] = _rw_post(raf_ref[...] + rab_ref[...], rg_ref[...], rbonus_ref[...], lnw_ref[...],
                            lnb_ref[...], e_ref[...]).astype(BF16)
        y_scr[1] = _la_post(hf_ref[...] + hb_ref[...], hgate_ref[...], hnorm_ref[...], hg_heads).astype(BF16)
        y_scr[2] = _la_post(gf_ref[...] + gb_ref[...], ggate_ref[...], gnorm_ref[...], GLA_HEADS).astype(BF16)
        y_scr[3] = yd_ref[...].astype(BF16)

    gs = (g0_ref, g1_ref, g2_ref, g3_ref)
    acc = None
    for j in range(N_BRANCH):
        t = _sigmoid(gs[j][...].astype(F32)) * _dot(y_scr[j], w_ref[j])
        acc = t if acc is None else acc + t
    o_ref[...] = acc.astype(BF16)


def _merge(rw, hg, gla, yd, p_gate, p_mix, vecs, e, w_branch, *, layer, tm, row_off):
    m = p_mix.shape[0]
    _, _, c, d = w_branch.shape
    tn = _pick((1024, 512), d)
    mo = m - row_off * tm
    ytok = pl.BlockSpec((tm, c), lambda i, j: (i + row_off, 0))
    mix = lambda off: pl.BlockSpec((tm, c), lambda i, j: (i + row_off, off // c))
    gate = lambda b: pl.BlockSpec((tm, tn), lambda i, j: (i + row_off, b * d // tn + j))
    vec = pl.BlockSpec((1, c), lambda i, j: (0, 0))
    return pl.pallas_call(
        functools.partial(_merge_kernel, hg_heads=c // HG_DK),
        grid=(mo // tm, d // tn),
        in_specs=[ytok, ytok, ytok, ytok, ytok, ytok, mix(C_HGG), ytok, ytok, mix(C_GLG), ytok,
                  vec, vec, pl.BlockSpec((c, c), lambda i, j: (0, 0)), vec, vec,
                  gate(0), gate(1), gate(2), gate(3),
                  pl.BlockSpec((None, N_BRANCH, c, tn), lambda i, j: (layer, 0, 0, j))],
        out_specs=pl.BlockSpec((tm, tn), lambda i, j: (i, j)),
        out_shape=jax.ShapeDtypeStruct((mo, d), BF16),
        scratch_shapes=[pltpu.VMEM((N_BRANCH, tm, c), BF16)],
        compiler_params=_cparams(2),
        name="merge",
    )(*rw, hg[0], hg[1], p_mix, gla[0], gla[1], p_mix, yd, *vecs[:2], e, *vecs[2:],
      p_gate, p_gate, p_gate, p_gate, w_branch)


def _outproj_kernel(m_ref, w_ref, xc_ref, xl_ref, mod_ref, o_ref, *, nctx, row_off):
    x = jnp.where(pl.program_id(0) + row_off < nctx, xc_ref[...], xl_ref[...])
    o_ref[...] = x + mod_ref[0, 2:3, :] * _dot(m_ref[...], w_ref[...])


def _outproj(mm, w_out, x_parts, mod, *, layer, tm, row_off, nctx, bpb):
    xc, xl, lat_off = x_parts
    mo, d = mm.shape
    tn = _pick((1024, 512), d)
    xspecs = [pl.BlockSpec((tm, tn), lambda i, j: (jnp.minimum(i + row_off, nctx - 1),
                                                   jnp.where(i + row_off < nctx, j, 0))),
              pl.BlockSpec((tm, tn), lambda i, j: (lat_off + jnp.maximum(i + row_off - nctx, 0),
                                                   jnp.where(i + row_off < nctx, 0, j)))]
    return pl.pallas_call(
        functools.partial(_outproj_kernel, nctx=nctx, row_off=row_off),
        grid=(mo // tm, d // tn),
        in_specs=[pl.BlockSpec((tm, d), lambda i, j: (i, 0)),
                  pl.BlockSpec((None, d, tn), lambda i, j: (layer, 0, j))] + xspecs + [
                  pl.BlockSpec((1, 6, tn), lambda i, j: (_row_group(i + row_off, nctx, bpb), 0, j))],
        out_specs=pl.BlockSpec((tm, tn), lambda i, j: (i, j)),
        out_shape=jax.ShapeDtypeStruct((mo, d), F32),
        compiler_params=_cparams(2),
        name="outproj",
    )(mm, w_out, xc, xl, mod)


def _mlp_kernel(x_ref, mod_ref, g_ref, w1_ref, w2_ref, gf_ref, o_ref, h_ref, acc_ref, *, final_norm):
    j = pl.program_id(1)

    @pl.when(j == 0)
    def _():
        h = _norm_mod(x_ref[...], g_ref[...], mod_ref[0, 3:4, :], mod_ref[0, 4:5, :])
        h_ref[...] = h.astype(BF16)
        acc_ref[...] = jnp.zeros_like(acc_ref)

    a = jnp.maximum(_dot(h_ref[...], w1_ref[...]), 0.0)
    acc_ref[...] += _dot((a * a).astype(BF16), w2_ref[...])

    @pl.when(j == pl.num_programs(1) - 1)
    def _():
        y = x_ref[...] + mod_ref[0, 5:6, :] * acc_ref[...]
        if final_norm:
            y = y * lax.rsqrt(jnp.mean(y * y, axis=-1, keepdims=True) + EPS) * gf_ref[...]
        o_ref[...] = y


def _mlp(x_in, mod, g, w1, w2, g_final, *, layer, tm, row_off, nctx, bpb, final_norm):
    mo, d = x_in.shape
    hid = w1.shape[-1]
    th = _pick((1024, 512, 256, 128), hid)
    return pl.pallas_call(
        functools.partial(_mlp_kernel, final_norm=final_norm),
        grid=(mo // tm, hid // th),
        in_specs=[pl.BlockSpec((tm, d), lambda i, j: (i, 0)),
                  pl.BlockSpec((1, 6, d), lambda i, j: (_row_group(i + row_off, nctx, bpb), 0, 0)),
                  pl.BlockSpec((1, d), lambda i, j: (0, 0)),
                  pl.BlockSpec((None, d, th), lambda i, j: (layer, 0, j)),
                  pl.BlockSpec((None, th, d), lambda i, j: (layer, j, 0)),
                  pl.BlockSpec((1, d), lambda i, j: (0, 0))],
        out_specs=pl.BlockSpec((tm, d), lambda i, j: (i, 0)),
        out_shape=jax.ShapeDtypeStruct((mo, d), F32),
        scratch_shapes=[pltpu.VMEM((tm, d), BF16), pltpu.VMEM((tm, d), F32)],
        compiler_params=_cparams(2),
        name="mlp",
    )(x_in, mod, g, w1, w2, g_final)


def _blockdiag2(w):
    _, r, c = w.shape
    z = jnp.zeros((r, c), w.dtype)
    out = jnp.concatenate([jnp.concatenate([w[0], z], axis=1), jnp.concatenate([z, w[1]], axis=1)], axis=0)
    return jnp.pad(out, ((0, 128 - 2 * r), (0, 0))).astype(BF16)


def _permute_w_in(w, d_model):
    c = d_model // N_BRANCH
    o = [0]
    for wd in (3 * c, 64 * 2, 64 * 2, 128, c, 2 * c, c, c, c, c, 32, c, 2 * c, N_BRANCH * d_model):
        o.append(o[-1] + wd)
    seg = lambda k: w[:, o[k]:o[k + 1]]
    pad = jnp.zeros((w.shape[0], 512 - 128 * 3 - 32), w.dtype)
    parts = [seg(13), seg(0), seg(5), seg(12), seg(4), seg(6), seg(7), seg(8), seg(9), seg(11),
             seg(1), seg(2), seg(3), seg(10), pad]
    out = jnp.concatenate(parts, axis=1).astype(BF16)
    assert out.shape[1] == N_GATE + N_MIX
    return out


def kernel(x, c, ctx, c_ctx, w_ada, b_ada, g_norm1, g_norm2, g_final, w_in, rw_conv, rw_w0, rw_w2,
           rw_a0, rw_a2, rw_g2, rw_kk, rw_ka, rw_rk, rw_ln_w, rw_ln_b, hg_gamma, hg_norm, gla_gw,
           gla_gb, gla_norm, sgu_ln_w, sgu_ln_b, sgu_w, sgu_b, w_branch, w_out, w_mlp1, w_mlp2):
    n_batch, seq, d_model = x.shape
    ctx_len = ctx.shape[1]
    depth = w_in.shape[0]
    cw = d_model // N_BRANCH
    assert cw == 512 and d_model == 2048, "column layout constants assume D_MODEL = 2048"
    m_ctx = n_batch * ctx_len

    tm = _pick((1024, 512, 256, 128), m_ctx, seq)
    tb = _pick((256, 128), ctx_len, seq)
    nctx, bpb = m_ctx // tm, seq // tm
    ctxb, latb = ctx_len // tb, seq // tb
    seqs = dict(tb=tb, n_batch=n_batch, ctxb=ctxb, latb=latb)

    x_parts = (ctx.reshape(m_ctx, d_model), x.reshape(n_batch * seq, d_model), 0)
    c_rows = jnp.concatenate([c_ctx[None, :], c, jnp.zeros((7 - n_batch, d_model), F32)], axis=0)
    mod_all = _ada(c_rows, w_ada, b_ada).reshape(depth, 8, 6, d_model)

    head_ones = (jnp.arange(cw)[:, None] // RW_HEAD == jnp.arange(cw)[None, :] // RW_HEAD).astype(BF16)
    w_branch_b, w_out_b = w_branch.astype(BF16), w_out.astype(BF16)
    w_mlp1_b, w_mlp2_b = w_mlp1.astype(BF16), w_mlp2.astype(BF16)
    row = lambda a: a.reshape(1, -1)

    for l in range(depth):
        last = l == depth - 1
        mod = mod_all[l]
        p_gate, p_all = _inproj(x_parts, mod, row(g_norm1[l]), _permute_w_in(w_in[l], d_model),
                                tm=tm, nctx=nctx, bpb=bpb)

        prm = dict(conv=rw_conv[l], w0=row(rw_w0[l]), w2=_blockdiag2(rw_w2[l]), a0=row(rw_a0[l]),
                   a2=_blockdiag2(rw_a2[l]), g2=rw_g2[l].astype(BF16), kk=row(rw_kk[l]),
                   ka=row(rw_ka[l]), rk=row(rw_rk[l]), e=head_ones)
        r, v, nkk, g, bonus, w, kka, km = _rw_prep(p_all, prm, tb=tb, nctx=m_ctx // tb,
                                                    ctx_bps=ctxb, lat_bps=latb)
        ys = _rw_scan(r, v, nkk, w, kka, km, **seqs)

        ob = _hgrn2(p_all, hg_gamma, layer=l, **seqs)

        gw = _blockdiag2(gla_gw[l])
        gb = row(gla_gb[l])
        oc = _gla(p_all, gw, gb, **seqs)

        bs = jnp.broadcast_to(sgu_b[l][:, :, None], sgu_w[l].shape)
        yd = _sgu(p_all, row(sgu_ln_w[l]), row(sgu_ln_b[l]), sgu_w[l].astype(BF16), bs,
                  rb=_pick((512, 256, 128), m_ctx, seq))

        row_off = nctx if last else 0
        vecs = (row(rw_ln_w[l]), row(rw_ln_b[l]), row(hg_norm[l]), row(gla_norm[l]))
        mm = _merge((ys[0], ys[1], g, bonus), ob, oc, yd, p_gate, p_all, vecs, head_ones,
                    w_branch_b, layer=l, tm=tm // 2, row_off=2 * row_off)
        x_mid = _outproj(mm, w_out_b, x_parts, mod, layer=l, tm=tm, row_off=row_off, nctx=nctx, bpb=bpb)
        x_all = _mlp(x_mid, mod, row(g_norm2[l]), w_mlp1_b, w_mlp2_b, row(g_final), layer=l,
                     tm=tm // 2, row_off=2 * row_off, nctx=2 * nctx, bpb=2 * bpb, final_norm=last)
        x_parts = (x_all, x_all, nctx)
    return x_all.reshape(n_batch, seq, d_model)
```

```python
import functools

import jax
import jax.numpy as jnp
from jax import lax
from jax.experimental import pallas as pl
from jax.experimental.pallas import tpu as pltpu

F32 = jnp.float32
BF16 = jnp.bfloat16

N_BRANCH = 4
RW_HEAD = 64
RW_LN_EPS = 64e-5
HG_DK = 128
GLA_HEADS = 4
GLA_GATE_NORM = 16.0
LA_CHUNK = 16
RW_CHUNK = 32
RW_SUB = 16
SGU_CHUNK = 128
SGU_GROUPS = 4
EPS = 1e-6

C_RKV = 0
C_HGF = 1536
C_SGU = 2560
C_HGQ = 3584
C_HGI = 4096
C_HGG = 4608
C_GLQK = 5120
C_GLV = 5632
C_GLG = 6144
C_SMALL = 6656
N_MIX = 7168
N_GATE = 8192

VMEM_LIMIT = 58 * 1024 * 1024


def _cparams(n_axes):
    return pltpu.CompilerParams(dimension_semantics=("arbitrary",) * n_axes,
                                vmem_limit_bytes=VMEM_LIMIT)


def _pick(n_list, *dims):
    for n in n_list:
        if all(d % n == 0 for d in dims):
            return n
    raise ValueError(f"no block size in {n_list} divides {dims}")


def _row_group(i, nctx, bpb):
    return jnp.where(i < nctx, 0, 1 + (i - nctx) // bpb)


def _dot(a, b):
    return jnp.dot(a, b, preferred_element_type=F32)


def _dot_nt(a, b):
    return lax.dot_general(a, b, (((1,), (1,)), ((), ())), preferred_element_type=F32)


def _dot_tn(a, b):
    return lax.dot_general(a, b, (((0,), (0,)), ((), ())), preferred_element_type=F32)


def _split_dot(x, e):
    hi = x.astype(BF16)
    lo = (x - hi.astype(F32)).astype(BF16)
    return _dot(hi, e) + _dot(lo, e)


def _split3_dot(e, x):
    p1 = x.astype(BF16)
    r1 = x - p1.astype(F32)
    p2 = r1.astype(BF16)
    p3 = (r1 - p2.astype(F32)).astype(BF16)
    return _dot(e, p1) + _dot(e, p2) + _dot(e, p3)


def _log_sigmoid(x):
    return jnp.minimum(x, 0.0) - jnp.log(1.0 + jnp.exp(-jnp.abs(x)))


def _sigmoid(x):
    return 0.5 * jnp.tanh(0.5 * x) + 0.5


def _ada_kernel(c_ref, w_ref, b_ref, o_ref):
    c = c_ref[...]
    act = c * _sigmoid(c)
    o_ref[0] = _dot(act.astype(BF16), w_ref[0].astype(BF16)) + b_ref[0]


def _ada(c_rows, w_ada, b_ada):
    n_layers, d, n = w_ada.shape
    tn = _pick((1024, 512, 256, 128), n)
    return pl.pallas_call(
        _ada_kernel,
        grid=(n_layers, n // tn),
        in_specs=[pl.BlockSpec((8, d), lambda l, j: (0, 0)),
                  pl.BlockSpec((1, d, tn), lambda l, j: (l, 0, j)),
                  pl.BlockSpec((1, 1, tn), lambda l, j: (l, 0, j))],
        out_specs=pl.BlockSpec((1, 8, tn), lambda l, j: (l, 0, j)),
        out_shape=jax.ShapeDtypeStruct((n_layers, 8, n), F32),
        compiler_params=_cparams(2),
        name="ada_mod",
    )(c_rows, w_ada, b_ada.reshape(n_layers, 1, n))


def _norm_mod(x, g, shift, scale):
    y = x * lax.rsqrt(jnp.mean(x * x, axis=-1, keepdims=True) + EPS) * g
    return y * (1.0 + scale) + shift


def _two_part_specs(block, nctx, lat_off, col):
    return [pl.BlockSpec(block, lambda i, j: (jnp.minimum(i, nctx - 1), col(j)), pipeline_mode=pl.Buffered(1)),
            pl.BlockSpec(block, lambda i, j: (lat_off + jnp.maximum(i - nctx, 0), col(j)))]


def _inproj_kernel(xc_ref, xl_ref, mod_ref, g_ref, w_ref, og_ref, om_ref, h_ref, *, nctx, n_gate_tiles):
    i, j = pl.program_id(0), pl.program_id(1)

    @pl.when(j == 0)
    def _():
        x = jnp.where(i < nctx, xc_ref[...], xl_ref[...])
        h = _norm_mod(x, g_ref[...], mod_ref[0, 0:1, :], mod_ref[0, 1:2, :])
        h_ref[...] = h.astype(BF16)

    @pl.when(j < n_gate_tiles)
    def _():
        og_ref[...] = _dot(h_ref[...], w_ref[...]).astype(BF16)

    @pl.when(j >= n_gate_tiles)
    def _():
        om_ref[...] = _dot(h_ref[...], w_ref[...])


def _inproj(x_parts, mod, g, w, *, tm, nctx, bpb):
    xc, xl, lat_off = x_parts
    d = xc.shape[1]
    m = (nctx + (xl.shape[0] // tm - lat_off)) * tm
    tn = _pick((1024, 512), N_GATE, N_MIX)
    ng = N_GATE // tn
    return pl.pallas_call(
        functools.partial(_inproj_kernel, nctx=nctx, n_gate_tiles=ng),
        grid=(m // tm, (N_GATE + N_MIX) // tn),
        in_specs=_two_part_specs((tm, d), nctx, lat_off, lambda j: 0) + [
            pl.BlockSpec((1, 6, d), lambda i, j: (_row_group(i, nctx, bpb), 0, 0)),
            pl.BlockSpec((1, d), lambda i, j: (0, 0)),
            pl.BlockSpec((d, tn), lambda i, j: (0, j))],
        out_specs=[pl.BlockSpec((tm, tn), lambda i, j: (i, jnp.minimum(j, ng - 1))),
                   pl.BlockSpec((tm, tn), lambda i, j: (i, jnp.maximum(j - ng, 0)))],
        out_shape=[jax.ShapeDtypeStruct((m, N_GATE), BF16), jax.ShapeDtypeStruct((m, N_MIX), F32)],
        scratch_shapes=[pltpu.VMEM((tm, d), BF16)],
        compiler_params=_cparams(2),
        name="inproj",
    )(xc, xl, mod, g, w)


def _rw_prep_kernel(rkv_ref, prev_ref, next_ref, small_ref, conv_ref, w0_ref, w2_ref, a0_ref,
                    a2_ref, g2_ref, kk_ref, ka_ref, rk_ref, e_ref,
                    r_out, v_out, nkk_out, g_out, bonus_out, w_out, kka_out, km_out,
                    *, tb, nctx, ctx_bps, lat_bps):
    i = pl.program_id(0)
    c = r_out.shape[-1]
    j = jnp.where(i < nctx, i, i - nctx)
    bps = jnp.where(i < nctx, ctx_bps, lat_bps)
    first = lax.rem(j, bps) == 0
    last = lax.rem(j, bps) == bps - 1

    blk = rkv_ref[...]
    rows = lax.broadcasted_iota(jnp.int32, (tb, 1), 0)
    prev_row = jnp.where(first, 0.0, prev_ref[7:8, :])
    next_row = jnp.where(last, 0.0, next_ref[0:1, :])
    xm1 = jnp.where(rows == 0, prev_row, pltpu.roll(blk, 1, 0))
    xp1 = jnp.where(rows == tb - 1, next_row, pltpu.roll(blk, tb - 1, 0))
    conv = conv_ref[0:1, :] * xm1 + conv_ref[1:2, :] * blk + conv_ref[2:3, :] * xp1
    r = conv[:, 0:c]
    k = conv[:, c:2 * c]
    v = conv[:, 2 * c:3 * c]

    small = small_ref[...]
    wl = small[:, 0:128]
    al = small[:, 128:256]
    gl = small[:, 256:384]
    w_pre = w0_ref[...] + _dot(jnp.tanh(wl).astype(BF16), w2_ref[...])
    softplus = jnp.maximum(-w_pre, 0.0) + jnp.log(1.0 + jnp.exp(-jnp.abs(w_pre)))
    log_decay = -jnp.exp(-softplus - 0.5)
    a = _sigmoid(a0_ref[...] + _dot(al.astype(BF16), a2_ref[...]))
    g = _dot(_sigmoid(gl).astype(BF16), g2_ref[...])

    e = e_ref[...]
    kkv = k * kk_ref[...]
    kk = kkv * lax.rsqrt(_split_dot(kkv * kkv, e) + 1e-12)
    bonus = _split_dot(r * k * rk_ref[...], e) * v

    r_out[...] = r.astype(BF16)
    v_out[...] = v.astype(BF16)
    nkk_out[...] = (-kk).astype(BF16)
    g_out[...] = g
    bonus_out[...] = bonus
    for d in range(2):
        a_d = a[:, d * c:(d + 1) * c]
        w_out[d] = log_decay[:, d * c:(d + 1) * c]
        kka_out[d] = (kk * a_d).astype(BF16)
        km_out[d] = (k * (1.0 + (a_d - 1.0) * ka_ref[...])).astype(BF16)


def _rw_prep(p_all, prm, *, tb, nctx, ctx_bps, lat_bps):
    m = p_all.shape[0]
    c = prm["kk"].shape[-1]
    nblk = m // tb
    t8 = tb // 8
    full = lambda shape: pl.BlockSpec(shape, lambda i: (0,) * len(shape))
    tok = pl.BlockSpec((tb, c), lambda i: (i, 0))
    tok2 = pl.BlockSpec((2, tb, c), lambda i: (0, i, 0))
    one, one_b = jax.ShapeDtypeStruct((m, c), F32), jax.ShapeDtypeStruct((m, c), BF16)
    two, two_b = jax.ShapeDtypeStruct((2, m, c), F32), jax.ShapeDtypeStruct((2, m, c), BF16)
    return pl.pallas_call(
        functools.partial(_rw_prep_kernel, tb=tb, nctx=nctx, ctx_bps=ctx_bps, lat_bps=lat_bps),
        grid=(nblk,),
        in_specs=[pl.BlockSpec((tb, 3 * c), lambda i: (i, C_RKV // (3 * c))),
                  pl.BlockSpec((8, 3 * c), lambda i: (jnp.maximum(i * t8 - 1, 0), 0)),
                  pl.BlockSpec((8, 3 * c), lambda i: (jnp.minimum((i + 1) * t8, m // 8 - 1), 0)),
                  pl.BlockSpec((tb, 512), lambda i: (i, C_SMALL // 512)),
                  full((3, 3 * c)), full((1, 2 * c)), full((128, 2 * c)), full((1, 2 * c)),
                  full((128, 2 * c)), full((128, c)), full((1, c)), full((1, c)), full((1, c)),
                  full((c, c))],
        out_specs=[tok, tok, tok, tok, tok, tok2, tok2, tok2],
        out_shape=[one_b, one_b, one_b, one, one, two, two_b, two_b],
        compiler_params=_cparams(1),
        name="rwkv_prep",
    )(p_all, p_all, p_all, p_all, prm["conv"], prm["w0"], prm["w2"], prm["a0"], prm["a2"],
      prm["g2"], prm["kk"], prm["ka"], prm["rk"], prm["e"])


class _RwUnit:
    def __init__(self, refs, scratch, *, tb, npair, reverse):
        (self.r_ref, self.v_ref, self.a_ref, self.lw_ref, self.b_ref, self.k_ref, self.tri_ref,
         self.y_ref) = refs
        (self.s_ref, self.c_scr, self.ag_scr, self.rg_scr, self.bg_scr, self.kg_scr) = scratch[:6]
        self.slots = (scratch[6:10], scratch[10:14])
        self.reverse = reverse
        self.nch = tb // RW_CHUNK
        self.pairs = range(npair)
        L, hd = RW_CHUNK, RW_HEAD
        self.lss = [slice(p * 2 * hd, (p + 1) * 2 * hd) for p in self.pairs]
        lane = lax.broadcasted_iota(jnp.int32, (1, 2 * hd), 1)
        self.lo = (lane < hd).astype(F32)
        self.hi = 1.0 - self.lo
        ti = lax.broadcasted_iota(jnp.int32, (2 * L, 2 * hd), 0)
        sl = lax.broadcasted_iota(jnp.int32, (2 * L, 2 * hd), 1)
        sj = sl & (L - 1)
        tt = ti & (L - 1)
        earlier = (sj > tt) if reverse else (sj < tt)
        assert 4 * L == 2 * hd and L == 2 * RW_SUB
        self.aa_mask = (earlier | ((ti >= L) & (sj == tt))).astype(F32)
        self.col_idx0 = jnp.where(lax.broadcasted_iota(jnp.int32, (RW_SUB, 2 * hd), 1) < hd, 0, L)
        self.blk_mask = ((lax.broadcasted_iota(jnp.int32, (2 * hd, 2 * hd), 0) >> 6)
                         == (lax.broadcasted_iota(jnp.int32, (2 * hd, 2 * hd), 1) >> 6)).astype(F32)
        self.zeros2l = jnp.zeros((2 * L, 2 * hd), F32)
        self.zeros_sub = jnp.zeros((RW_SUB, 2 * hd), F32)
        self.blocks = (1, 0) if reverse else (0, 1)
        self.last = 0 if reverse else L - 1
        self.order = range(L - 1, -1, -1) if reverse else range(L)

    def reset(self):
        self.s_ref[...] = jnp.zeros_like(self.s_ref)

    def prologue(self):
        lw = self.lw_ref[0]
        c = _split3_dot(self.tri_ref[...], lw)
        self.c_scr[...] = c
        enc = jnp.exp(-c)
        self.ag_scr[...] = self.a_ref[...] * jnp.exp(c - lw)
        self.rg_scr[...] = self.r_ref[...] * jnp.exp(c)
        self.bg_scr[...] = self.b_ref[0] * enc
        self.kg_scr[...] = self.k_ref[0] * enc

    def rows(self, ci):
        L = RW_CHUNK
        cc = (self.nch - 1 - ci) if self.reverse else ci
        return pl.ds(pl.multiple_of(cc * L, L), L)

    def halves(self, x):
        return [x * self.lo, x * self.hi]

    def lhs_of(self, rows):
        return [jnp.concatenate([self.ag_scr[rows, ls], self.rg_scr[rows, ls]], axis=0).astype(BF16)
                for ls in self.lss]

    def prep_aa(self, rows):
        lhs = self.lhs_of(rows)
        out = []
        for p in self.pairs:
            ls = self.lss[p]
            rhs = jnp.concatenate(self.halves(self.bg_scr[rows, ls]) + self.halves(self.kg_scr[rows, ls]),
                                  axis=0)
            out.append(_dot_nt(lhs[p], rhs.astype(BF16)))
        return out

    def prep_akv(self, rows, aa_raw, slot):
        _, akv_scr, aar_scr, na_scr = slot
        L = RW_CHUNK
        aa = [x * self.aa_mask for x in aa_raw]
        for p in self.pairs:
            v = self.v_ref[rows, self.lss[p]]
            vv = jnp.concatenate([self.zeros2l] + self.halves(v), axis=0).astype(BF16)
            akv_scr[p] = _dot(aa[p][0:L].astype(BF16), vv)
            aar_scr[p] = aa[p][L:2 * L]
            na_scr[p] = aa[p][0:L]
        return aa

    def prep_cols(self, aa, slot):
        half = RW_SUB // 2
        for p in self.pairs:
            for s in range(RW_CHUNK):
                r0 = (s // RW_SUB) * RW_SUB
                lo, hi = self.live_rows(s - r0)
                slot[0][p, s, lo:hi] = jnp.take_along_axis(aa[p][r0 + lo:r0 + hi],
                                                           self.col_idx0[0:hi - lo] + s, axis=1)

    def live_rows(self, s_local):
        half = RW_SUB // 2
        if self.reverse:
            return (0, half) if s_local <= half else (0, RW_SUB)
        return (half, RW_SUB) if s_local >= half - 1 else (0, RW_SUB)

    def adv_g(self, rows):
        lhs = self.lhs_of(rows)
        return [_dot_nt(lhs[p], self.s_ref[p].astype(BF16)) for p in self.pairs]

    def adv_solve(self, g, slot):
        L, sub = RW_CHUNK, RW_SUB
        col_scr, akv_scr, _, na_scr = slot
        first, second = self.blocks
        ub = {}
        for blk in self.blocks:
            r0 = blk * sub
            ub[blk] = [g[p][r0:r0 + sub] + akv_scr[p, r0:r0 + sub] for p in self.pairs]
            if blk == second:
                for p in self.pairs:
                    done = ub[first][p]
                    z = self.zeros_sub
                    parts = [z, z, z, z]
                    parts[first], parts[2 + first] = done * self.lo, done * self.hi
                    stack = jnp.concatenate(parts + [self.zeros2l], axis=0).astype(BF16)
                    ub[blk][p] = ub[blk][p] + _dot(na_scr[p, r0:r0 + sub].astype(BF16), stack)
            steps = [s for s in self.order if s // sub == blk]
            half = sub // 2
            top = [x[0:half] for x in ub[blk]]
            bot = [x[half:] for x in ub[blk]]
            for s in steps:
                sl = s - r0
                lo, hi = self.live_rows(sl)
                for p in self.pairs:
                    row = top[p][sl:sl + 1] if sl < half else bot[p][sl - half:sl - half + 1]
                    if lo == 0:
                        top[p] = top[p] + col_scr[p, s, 0:half] * row
                    if hi == sub:
                        bot[p] = bot[p] + col_scr[p, s, half:sub] * row
            ub[blk] = [jnp.concatenate([top[p], bot[p]], axis=0) for p in self.pairs]
        return [jnp.concatenate([ub[0][p], ub[1][p]], axis=0) for p in self.pairs]

    def adv_out(self, rows, g, u, slot):
        L = RW_CHUNK
        for p in self.pairs:
            v = self.v_ref[rows, self.lss[p]]
            uv = jnp.concatenate(self.halves(u[p]) + self.halves(v), axis=0).astype(BF16)
            self.y_ref[rows, self.lss[p]] = g[p][L:2 * L] + _dot(slot[2][p].astype(BF16), uv)
        for p in self.pairs:
            ls = self.lss[p]
            cch = self.c_scr[rows, ls]
            cl = cch[self.last:self.last + 1]
            dec = jnp.exp(cl - cch)
            bk = jnp.concatenate([self.b_ref[0, rows, ls] * dec, self.k_ref[0, rows, ls] * dec], axis=0)
            upd = _dot_tn(jnp.concatenate([u[p], self.v_ref[rows, ls]], axis=0).astype(BF16),
                          bk.astype(BF16))
            self.s_ref[p] = self.s_ref[p] * jnp.exp(cl) + upd * self.blk_mask


N_RW_REFS = 8
N_RW_SCRATCH = 14


def _rw_scan_kernel(*refs, tb, npair):
    n_in = N_RW_REFS - 1
    units = []
    for d in range(2):
        ins = refs[d * n_in:(d + 1) * n_in]
        out = refs[2 * n_in + d]
        scr = refs[2 * n_in + 2 + d * N_RW_SCRATCH:2 * n_in + 2 + (d + 1) * N_RW_SCRATCH]
        units.append(_RwUnit(tuple(ins) + (out,), scr, tb=tb, npair=npair, reverse=d == 1))
    nch = tb // RW_CHUNK

    @pl.when(pl.program_id(1) == 0)
    def _():
        for un in units:
            un.reset()

    for un in units:
        un.prologue()

    def step(ci_adv, sa, ci_prep, sp):
        rows_a = [un.rows(ci_adv) for un in units]
        rows_p = [un.rows(ci_prep) for un in units]
        aa_raw = [un.prep_aa(rp) for un, rp in zip(units, rows_p)]
        g = [un.adv_g(ra) for un, ra in zip(units, rows_a)]
        aa = [un.prep_akv(rp, x, un.slots[sp]) for un, rp, x in zip(units, rows_p, aa_raw)]
        u = [un.adv_solve(x, un.slots[sa]) for un, x in zip(units, g)]
        for un, x in zip(units, aa):
            un.prep_cols(x, un.slots[sp])
        for un, ra, x, y in zip(units, rows_a, g, u):
            un.adv_out(ra, x, y, un.slots[sa])

    for un in units:
        r0 = un.rows(0)
        un.prep_cols(un.prep_akv(r0, un.prep_aa(r0), un.slots[0]), un.slots[0])

    def two_chunks(j, carry):
        c0 = 2 * j
        step(c0, 0, c0 + 1, 1)
        step(c0 + 1, 1, jnp.minimum(c0 + 2, nch - 1), 0)
        return carry

    lax.fori_loop(0, nch // 2, two_chunks, 0)


def _seq_block(d, b, i, *, n_batch, ctxb, latb):
    is_ctx = i < ctxb
    cs = jnp.where(d == 0, i, ctxb - 1 - i)
    lj = jnp.where(d == 0, i - ctxb, latb - 1 - (i - ctxb))
    return jnp.where(is_ctx, b * ctxb + cs, n_batch * ctxb + b * latb + lj)


def _rw_scan(r, v, nkk, lw, kka, km, *, tb, n_batch, ctxb, latb):
    m, c = r.shape
    npair = c // (2 * RW_HEAD)
    lanes = 2 * RW_HEAD
    in_specs, args = [], []
    for d in range(2):
        blk = _la_index(d, n_batch, ctxb, latb)
        tok = pl.BlockSpec((tb, c), lambda b, i, blk=blk: (blk(b, i), 0))
        tokd = pl.BlockSpec((1, tb, c), lambda b, i, blk=blk, d=d: (d, blk(b, i), 0))
        in_specs += [tok, tok, tok, tokd, tokd, tokd, pl.BlockSpec((tb, tb), lambda b, i: (0, 0))]
        args += [r, v, nkk, lw, kka, km, _chunk_tri(tb, d == 1, RW_CHUNK)]
    out_specs = [pl.BlockSpec((tb, c), lambda b, i, blk=_la_index(d, n_batch, ctxb, latb): (blk(b, i), 0))
                 for d in range(2)]
    buf = pltpu.VMEM((tb, c), F32)
    unit_scratch = [pltpu.VMEM((npair, lanes, lanes), F32), buf, buf, buf, buf, buf] + 2 * [
        pltpu.VMEM((npair, RW_CHUNK, RW_SUB, lanes), F32),
        pltpu.VMEM((npair, RW_CHUNK, lanes), F32),
        pltpu.VMEM((npair, RW_CHUNK, lanes), F32),
        pltpu.VMEM((npair, RW_CHUNK, lanes), F32)]
    assert len(unit_scratch) == N_RW_SCRATCH
    return pl.pallas_call(
        functools.partial(_rw_scan_kernel, tb=tb, npair=npair),
        grid=(n_batch, ctxb + latb),
        in_specs=in_specs,
        out_specs=out_specs,
        out_shape=[jax.ShapeDtypeStruct((m, c), F32)] * 2,
        scratch_shapes=unit_scratch + unit_scratch,
        compiler_params=_cparams(2),
        name="rwkv_scan",
    )(*args)


def _rw_post(y, g, bonus, lnw, lnb, e):
    inv = 1.0 / RW_HEAD
    yc = y - _split_dot(y, e) * inv
    var = _split_dot(yc * yc, e) * inv
    yn = yc * lax.rsqrt(var + RW_LN_EPS)
    return (yn * lnw + lnb + bonus) * g


class _LaUnit:
    def __init__(self, q_ref, k_ref, v_ref, b_ref, o_ref, s_ref, *, tb, nh, dk, dv, reverse, q_scale):
        self.q_ref, self.k_ref, self.v_ref, self.b_ref, self.o_ref, self.s_ref = (
            q_ref, k_ref, v_ref, b_ref, o_ref, s_ref)
        self.nch = tb // LA_CHUNK
        self.dk, self.dv, self.reverse, self.q_scale = dk, dv, reverse, q_scale
        self.lanes = 128
        self.pack = self.lanes // dk
        self.groups = range(nh // self.pack)
        self.rowi = lax.broadcasted_iota(jnp.int32, (LA_CHUNK, 1), 0)
        lane = lax.broadcasted_iota(jnp.int32, (1, self.lanes), 1)
        self.head_lanes = [(lane // dk == j).astype(F32) for j in range(self.pack)]
        if self.pack > 1:
            rows = lax.broadcasted_iota(jnp.int32, (self.pack * dv, self.lanes), 0)
            cols = lax.broadcasted_iota(jnp.int32, (self.pack * dv, self.lanes), 1)
            self.blk_mask = ((rows // dv) == (cols // dk)).astype(F32)

    def reset(self):
        self.s_ref[...] = jnp.zeros_like(self.s_ref)

    def load(self, ci):
        cc = (self.nch - 1 - ci) if self.reverse else ci
        rows = pl.ds(pl.multiple_of(cc * LA_CHUNK, LA_CHUNK), LA_CHUNK)
        pack, dv, lanes = self.pack, self.dv, self.lanes
        out = []
        for g in self.groups:
            ks = slice(g * lanes, (g + 1) * lanes)
            q = self.q_ref[rows, ks] * self.q_scale
            k = self.k_ref[rows, ks]
            v = self.v_ref[rows, g * pack * dv:(g + 1) * pack * dv]
            b = self.b_ref[rows, ks]
            o_inter = _dot_nt((q * jnp.exp(b)).astype(BF16), self.s_ref[g].astype(BF16))
            out.append((rows, q, k, v, b, [o_inter[:, j * dv:(j + 1) * dv] for j in range(pack)]))
        return out

    def intra(self, ops):
        pack, dv, half = self.pack, self.dv, LA_CHUNK // 2
        rowi = self.rowi[0:half]
        for g in self.groups:
            rows, q, k, v, b, o = ops[g]
            qh = [q[0:half], q[half:]]
            bh = [b[0:half], b[half:]]
            oh = [[oj[0:half], oj[half:]] for oj in o]
            for s in range(LA_CHUNK):
                for h in range(2):
                    lo_row, hi_row = h * half, (h + 1) * half - 1
                    if (hi_row > s) if self.reverse else (lo_row < s):
                        if (lo_row > s) if self.reverse else (hi_row < s):
                            continue
                        valid = (rowi + lo_row <= s) if self.reverse else (rowi + lo_row >= s)
                    else:
                        valid = None
                    term = (qh[h] * k[s:s + 1]) * jnp.exp(bh[h] - b[s:s + 1])
                    for j in range(pack):
                        tj = term if pack == 1 else term * self.head_lanes[j]
                        col = jnp.sum(tj, axis=-1, keepdims=True)
                        if valid is not None:
                            col = jnp.where(valid, col, 0.0)
                        oh[j][h] = oh[j][h] + col * v[s:s + 1, j * dv:(j + 1) * dv]
            for j in range(pack):
                self.o_ref[rows, (g * pack + j) * dv:(g * pack + j + 1) * dv] = jnp.concatenate(oh[j], axis=0)

    def update(self, ops):
        for g in self.groups:
            _, _, k, v, b, _ = ops[g]
            b_last = b[0:1] if self.reverse else b[LA_CHUNK - 1:LA_CHUNK]
            upd = _dot_tn(v.astype(BF16), (k * jnp.exp(b_last - b)).astype(BF16))
            if self.pack > 1:
                upd = upd * self.blk_mask
            self.s_ref[g] = self.s_ref[g] * jnp.exp(b_last) + upd


def _la_run(units, nch):
    def chunk(ci, carry):
        ops = [un.load(ci) for un in units]
        for un, x in zip(units, ops):
            un.update(x)
        for un, x in zip(units, ops):
            un.intra(x)
        return carry

    lax.fori_loop(0, nch, chunk, 0)


def _hg_kernel(qf_ref, ff_ref, vf_ref, trif_ref, qb_ref, fb_ref, vb_ref, trib_ref, gamma_ref,
               of_ref, ob_ref, sf_ref, kf_scr, bf_scr, sb_ref, kb_scr, bb_scr, *, layer, tb, nh, dk, dv):
    gam = gamma_ref[...]
    ex = jnp.exp(gam - jnp.max(gam, axis=0, keepdims=True))
    p = ex / jnp.sum(ex, axis=0, keepdims=True)
    cum = p[0:1]
    for i in range(1, layer + 1):
        cum = cum + p[i:i + 1]
    lb = cum - p[0:1]
    lo = jnp.log(lb)
    l1 = jnp.log(1.0 - lb)

    units = []
    for d, (q_ref, f_ref, v_ref, tri_ref, o_ref, s_ref, k_scr, b_scr) in enumerate((
            (qf_ref, ff_ref, vf_ref, trif_ref, of_ref, sf_ref, kf_scr, bf_scr),
            (qb_ref, fb_ref, vb_ref, trib_ref, ob_ref, sb_ref, kb_scr, bb_scr))):
        hi = l1 + _log_sigmoid(f_ref[...])
        mx = jnp.maximum(lo, hi)
        mn = jnp.minimum(lo, hi)
        log_f = mx + jnp.log(1.0 + jnp.exp(mn - mx))
        k_scr[...] = 1.0 - jnp.exp(log_f)
        b_scr[...] = _split3_dot(tri_ref[...], log_f)
        units.append(_LaUnit(q_ref, k_scr, v_ref, b_scr, o_ref, s_ref, tb=tb, nh=nh, dk=dk, dv=dv,
                             reverse=d == 1, q_scale=1.0))

    @pl.when(pl.program_id(1) == 0)
    def _():
        for un in units:
            un.reset()

    _la_run(units, tb // LA_CHUNK)


def _gla_kernel(qkf_ref, vf_ref, smallf_ref, trif_ref, qkb_ref, vb_ref, smallb_ref, trib_ref,
                gw_ref, gb_ref, of_ref, ob_ref, sf_ref, bf_scr, sb_ref, bb_scr, *, tb, nh, dk, dv):
    hk = nh * dk
    units = []
    for d, (qk_ref, v_ref, small_ref, tri_ref, o_ref, s_ref, b_scr) in enumerate((
            (qkf_ref, vf_ref, smallf_ref, trif_ref, of_ref, sf_ref, bf_scr),
            (qkb_ref, vb_ref, smallb_ref, trib_ref, ob_ref, sb_ref, bb_scr))):
        code = small_ref[:, 384:512]
        pre = _dot(code.astype(BF16), gw_ref[:, d * hk:(d + 1) * hk]) + gb_ref[:, d * hk:(d + 1) * hk]
        log_g = _log_sigmoid(pre) * (1.0 / GLA_GATE_NORM)
        b_scr[...] = _split3_dot(tri_ref[...], log_g)
        units.append(_LaUnit(qk_ref.at[:, 0:hk], qk_ref.at[:, hk:2 * hk], v_ref, b_scr, o_ref, s_ref,
                             tb=tb, nh=nh, dk=dk, dv=dv, reverse=d == 1, q_scale=dk ** -0.5))

    @pl.when(pl.program_id(1) == 0)
    def _():
        for un in units:
            un.reset()

    _la_run(units, tb // LA_CHUNK)


def _chunk_tri(tb, reverse, chunk=LA_CHUNK):
    t = jnp.arange(tb)
    same = (t[:, None] // chunk) == (t[None, :] // chunk)
    tri = (t[None, :] >= t[:, None]) if reverse else (t[None, :] <= t[:, None])
    return (same & tri).astype(BF16)


def _la_index(d, n_batch, ctxb, latb):
    return lambda b, i: _seq_block(d, b, i, n_batch=n_batch, ctxb=ctxb, latb=latb)


def _hgrn2(p_all, gamma, *, layer, tb, n_batch, ctxb, latb):
    m = p_all.shape[0]
    c = gamma.shape[-1]
    nh = c // HG_DK
    in_specs, args, out_specs = [], [], []
    for d in range(2):
        blk = _la_index(d, n_batch, ctxb, latb)
        col = lambda off, blk=blk: pl.BlockSpec((tb, c), lambda b, i: (blk(b, i), off // c))
        in_specs += [col(C_HGQ), col(C_HGF + d * c), col(C_HGI), pl.BlockSpec((tb, tb), lambda b, i: (0, 0))]
        args += [p_all, p_all, p_all, _chunk_tri(tb, d == 1)]
        out_specs.append(col(0))
    unit_scratch = [pltpu.VMEM((nh, HG_DK, HG_DK), F32), pltpu.VMEM((tb, c), F32), pltpu.VMEM((tb, c), F32)]
    return pl.pallas_call(
        functools.partial(_hg_kernel, layer=layer, tb=tb, nh=nh, dk=HG_DK, dv=HG_DK),
        grid=(n_batch, ctxb + latb),
        in_specs=in_specs + [pl.BlockSpec(gamma.shape, lambda b, i: (0, 0))],
        out_specs=out_specs,
        out_shape=[jax.ShapeDtypeStruct((m, c), F32)] * 2,
        scratch_shapes=unit_scratch + unit_scratch,
        compiler_params=_cparams(2),
        name="hgrn2_scan",
    )(*args, gamma)


def _gla(p_all, gw, gb, *, tb, n_batch, ctxb, latb):
    m = p_all.shape[0]
    hk = gw.shape[-1] // 2
    dk = hk // GLA_HEADS
    c = 2 * hk
    dv = c // GLA_HEADS
    in_specs, args, out_specs = [], [], []
    for d in range(2):
        blk = _la_index(d, n_batch, ctxb, latb)
        col = lambda off, blk=blk: pl.BlockSpec((tb, c), lambda b, i: (blk(b, i), off // c))
        in_specs += [col(C_GLQK), col(C_GLV), col(C_SMALL), pl.BlockSpec((tb, tb), lambda b, i: (0, 0))]
        args += [p_all, p_all, p_all, _chunk_tri(tb, d == 1)]
        out_specs.append(col(0))
    unit_scratch = [pltpu.VMEM((hk // 128, (128 // dk) * dv, 128), F32), pltpu.VMEM((tb, hk), F32)]
    return pl.pallas_call(
        functools.partial(_gla_kernel, tb=tb, nh=GLA_HEADS, dk=dk, dv=dv),
        grid=(n_batch, ctxb + latb),
        in_specs=in_specs + [pl.BlockSpec(gw.shape, lambda b, i: (0, 0)),
                             pl.BlockSpec(gb.shape, lambda b, i: (0, 0))],
        out_specs=out_specs,
        out_shape=[jax.ShapeDtypeStruct((m, c), F32)] * 2,
        scratch_shapes=unit_scratch + unit_scratch,
        compiler_params=_cparams(2),
        name="gla_scan",
    )(*args, gw, gb)


def _la_post(o, gate, norm_g, nh):
    dv = o.shape[-1] // nh
    outs = []
    for h in range(nh):
        oh = o[:, h * dv:(h + 1) * dv]
        outs.append(oh * lax.rsqrt(jnp.mean(oh * oh, axis=-1, keepdims=True) + EPS))
    return jnp.concatenate(outs, axis=-1) * norm_g * (gate * _sigmoid(gate))


def _gelu(x):
    return 0.5 * x * (1.0 + jnp.tanh(0.7978845608028654 * (x + 0.044715 * (x * x * x))))


def _sgu_kernel(u_ref, v_ref, lnw_ref, lnb_ref, ws_ref, bs_ref, o_ref, *, rb):
    u = _gelu(u_ref[...])
    v = _gelu(v_ref[...])
    vc = v - jnp.mean(v, axis=-1, keepdims=True)
    vn = vc * lax.rsqrt(jnp.mean(vc * vc, axis=-1, keepdims=True) + EPS)
    vn = (vn * lnw_ref[...] + lnb_ref[...]).astype(BF16)
    gw = vn.shape[-1] // SGU_GROUPS
    for n in range(rb // SGU_CHUNK):
        rs = slice(n * SGU_CHUNK, (n + 1) * SGU_CHUNK)
        for g in range(SGU_GROUPS):
            cs = slice(g * gw, (g + 1) * gw)
            s = _dot(ws_ref[g], vn[rs, cs]) + bs_ref[g]
            o_ref[rs, cs] = u[rs, cs] * s


def _sgu(p_all, lnw, lnb, ws, bs, *, rb):
    m = p_all.shape[0]
    c = lnw.shape[-1]
    return pl.pallas_call(
        functools.partial(_sgu_kernel, rb=rb),
        grid=(m // rb,),
        in_specs=[pl.BlockSpec((rb, c), lambda i: (i, C_SGU // c)),
                  pl.BlockSpec((rb, c), lambda i: (i, C_SGU // c + 1)),
                  pl.BlockSpec((1, c), lambda i: (0, 0)),
                  pl.BlockSpec((1, c), lambda i: (0, 0)),
                  pl.BlockSpec(ws.shape, lambda i: (0, 0, 0)),
                  pl.BlockSpec(bs.shape, lambda i: (0, 0, 0))],
        out_specs=pl.BlockSpec((rb, c), lambda i: (i, 0)),
        out_shape=jax.ShapeDtypeStruct((m, c), F32),
        compiler_params=_cparams(1),
        name="sgu",
    )(p_all, p_all, lnw, lnb, ws, bs)


def _merge_kernel(raf_ref, rab_ref, rg_ref, rbonus_ref, hf_ref, hb_ref, hgate_ref, gf_ref, gb_ref,
                  ggate_ref, yd_ref, lnw_ref, lnb_ref, e_ref, hnorm_ref, gnorm_ref,
                  g0_ref, g1_ref, g2_ref, g3_ref, w_ref, o_ref, y_scr, *, hg_heads):
    @pl.when(pl.program_id(1) == 0)
    def _():
        y_scr[0] = _rw_post(raf_ref[...] + rab_ref[...], rg_ref[...], rbonus_ref[...], lnw_ref[...],
                            lnb_ref[...], e_ref[...]).astype(BF16)
        y_scr[1] = _la_post(hf_ref[...] + hb_ref[...], hgate_ref[...], hnorm_ref[...], hg_heads).astype(BF16)
        y_scr[2] = _la_post(gf_ref[...] + gb_ref[...], ggate_ref[...], gnorm_ref[...], GLA_HEADS).astype(BF16)
        y_scr[3] = yd_ref[...].astype(BF16)

    gs = (g0_ref, g1_ref, g2_ref, g3_ref)
    acc = None
    for j in range(N_BRANCH):
        t = (jnp.tanh(gs[j][...].astype(F32)) + 1.0) * _dot(y_scr[j], w_ref[j])
        acc = t if acc is None else acc + t
    o_ref[...] = (0.5 * acc).astype(BF16)


def _merge(rw, hg, gla, yd, p_gate, p_mix, vecs, e, w_branch, *, layer, tm, row_off):
    m = p_mix.shape[0]
    _, _, c, d = w_branch.shape
    tn = _pick((1024, 512), d)
    mo = m - row_off * tm
    ytok = pl.BlockSpec((tm, c), lambda i, j: (i + row_off, 0))
    mix = lambda off: pl.BlockSpec((tm, c), lambda i, j: (i + row_off, off // c))
    gate = lambda b: pl.BlockSpec((tm, tn), lambda i, j: (i + row_off, b * d // tn + j))
    vec = pl.BlockSpec((1, c), lambda i, j: (0, 0))
    return pl.pallas_call(
        functools.partial(_merge_kernel, hg_heads=c // HG_DK),
        grid=(mo // tm, d // tn),
        in_specs=[ytok, ytok, ytok, ytok, ytok, ytok, mix(C_HGG), ytok, ytok, mix(C_GLG), ytok,
                  vec, vec, pl.BlockSpec((c, c), lambda i, j: (0, 0)), vec, vec,
                  gate(0), gate(1), gate(2), gate(3),
                  pl.BlockSpec((None, N_BRANCH, c, tn), lambda i, j: (layer, 0, 0, j))],
        out_specs=pl.BlockSpec((tm, tn), lambda i, j: (i, j)),
        out_shape=jax.ShapeDtypeStruct((mo, d), BF16),
        scratch_shapes=[pltpu.VMEM((N_BRANCH, tm, c), BF16)],
        compiler_params=_cparams(2),
        name="merge",
    )(*rw, hg[0], hg[1], p_mix, gla[0], gla[1], p_mix, yd, *vecs[:2], e, *vecs[2:],
      p_gate, p_gate, p_gate, p_gate, w_branch)


def _outproj_kernel(m_ref, w_ref, xc_ref, xl_ref, mod_ref, o_ref, *, nctx, row_off):
    x = jnp.where(pl.program_id(0) + row_off < nctx, xc_ref[...], xl_ref[...])
    o_ref[...] = x + mod_ref[0, 2:3, :] * _dot(m_ref[...], w_ref[...])


def _outproj(mm, w_out, x_parts, mod, *, layer, tm, row_off, nctx, bpb):
    xc, xl, lat_off = x_parts
    mo, d = mm.shape
    tn = _pick((1024, 512), d)
    xspecs = [pl.BlockSpec((tm, tn), lambda i, j: (jnp.minimum(i + row_off, nctx - 1),
                                                   jnp.where(i + row_off < nctx, j, 0))),
              pl.BlockSpec((tm, tn), lambda i, j: (lat_off + jnp.maximum(i + row_off - nctx, 0),
                                                   jnp.where(i + row_off < nctx, 0, j)))]
    return pl.pallas_call(
        functools.partial(_outproj_kernel, nctx=nctx, row_off=row_off),
        grid=(mo // tm, d // tn),
        in_specs=[pl.BlockSpec((tm, d), lambda i, j: (i, 0)),
                  pl.BlockSpec((None, d, tn), lambda i, j: (layer, 0, j))] + xspecs + [
                  pl.BlockSpec((1, 6, tn), lambda i, j: (_row_group(i + row_off, nctx, bpb), 0, j))],
        out_specs=pl.BlockSpec((tm, tn), lambda i, j: (i, j)),
        out_shape=jax.ShapeDtypeStruct((mo, d), F32),
        compiler_params=_cparams(2),
        name="outproj",
    )(mm, w_out, xc, xl, mod)


def _mlp_kernel(x_ref, mod_ref, g_ref, w1_ref, w2_ref, gf_ref, o_ref, h_ref, acc_ref, *, final_norm):
    j = pl.program_id(1)

    @pl.when(j == 0)
    def _():
        h = _norm_mod(x_ref[...], g_ref[...], mod_ref[0, 3:4, :], mod_ref[0, 4:5, :])
        h_ref[...] = h.astype(BF16)
        acc_ref[...] = jnp.zeros_like(acc_ref)

    a = jnp.maximum(_dot(h_ref[...], w1_ref[...]), 0.0)
    acc_ref[...] += _dot((a * a).astype(BF16), w2_ref[...])

    @pl.when(j == pl.num_programs(1) - 1)
    def _():
        y = x_ref[...] + mod_ref[0, 5:6, :] * acc_ref[...]
        if final_norm:
            y = y * lax.rsqrt(jnp.mean(y * y, axis=-1, keepdims=True) + EPS) * gf_ref[...]
        o_ref[...] = y


def _mlp(x_in, mod, g, w1, w2, g_final, *, layer, tm, row_off, nctx, bpb, final_norm):
    mo, d = x_in.shape
    hid = w1.shape[-1]
    th = _pick((1024, 512, 256, 128), hid)
    return pl.pallas_call(
        functools.partial(_mlp_kernel, final_norm=final_norm),
        grid=(mo // tm, hid // th),
        in_specs=[pl.BlockSpec((tm, d), lambda i, j: (i, 0)),
                  pl.BlockSpec((1, 6, d), lambda i, j: (_row_group(i + row_off, nctx, bpb), 0, 0)),
                  pl.BlockSpec((1, d), lambda i, j: (0, 0)),
                  pl.BlockSpec((None, d, th), lambda i, j: (layer, 0, j)),
                  pl.BlockSpec((None, th, d), lambda i, j: (layer, j, 0)),
                  pl.BlockSpec((1, d), lambda i, j: (0, 0))],
        out_specs=pl.BlockSpec((tm, d), lambda i, j: (i, 0)),
        out_shape=jax.ShapeDtypeStruct((mo, d), F32),
        scratch_shapes=[pltpu.VMEM((tm, d), BF16), pltpu.VMEM((tm, d), F32)],
        compiler_params=_cparams(2),
        name="mlp",
    )(x_in, mod, g, w1, w2, g_final)


def _blockdiag2(w):
    _, r, c = w.shape
    z = jnp.zeros((r, c), w.dtype)
    out = jnp.concatenate([jnp.concatenate([w[0], z], axis=1), jnp.concatenate([z, w[1]], axis=1)], axis=0)
    return jnp.pad(out, ((0, 128 - 2 * r), (0, 0))).astype(BF16)


def _permute_w_in(w, d_model):
    c = d_model // N_BRANCH
    o = [0]
    for wd in (3 * c, 64 * 2, 64 * 2, 128, c, 2 * c, c, c, c, c, 32, c, 2 * c, N_BRANCH * d_model):
        o.append(o[-1] + wd)
    seg = lambda k: w[:, o[k]:o[k + 1]]
    pad = jnp.zeros((w.shape[0], 512 - 128 * 3 - 32), w.dtype)
    parts = [0.5 * seg(13), seg(0), seg(5), seg(12), seg(4), seg(6), seg(7), seg(8), seg(9), seg(11),
             seg(1), seg(2), seg(3), seg(10), pad]
    out = jnp.concatenate(parts, axis=1).astype(BF16)
    assert out.shape[1] == N_GATE + N_MIX
    return out


def kernel(x, c, ctx, c_ctx, w_ada, b_ada, g_norm1, g_norm2, g_final, w_in, rw_conv, rw_w0, rw_w2,
           rw_a0, rw_a2, rw_g2, rw_kk, rw_ka, rw_rk, rw_ln_w, rw_ln_b, hg_gamma, hg_norm, gla_gw,
           gla_gb, gla_norm, sgu_ln_w, sgu_ln_b, sgu_w, sgu_b, w_branch, w_out, w_mlp1, w_mlp2):
    n_batch, seq, d_model = x.shape
    ctx_len = ctx.shape[1]
    depth = w_in.shape[0]
    cw = d_model // N_BRANCH
    assert cw == 512 and d_model == 2048, "column layout constants assume D_MODEL = 2048"
    m_ctx = n_batch * ctx_len

    tm = _pick((1024, 512, 256, 128), m_ctx, seq)
    tb = _pick((256, 128), ctx_len, seq)
    nctx, bpb = m_ctx // tm, seq // tm
    ctxb, latb = ctx_len // tb, seq // tb
    seqs = dict(tb=tb, n_batch=n_batch, ctxb=ctxb, latb=latb)

    x_parts = (ctx.reshape(m_ctx, d_model), x.reshape(n_batch * seq, d_model), 0)
    c_rows = jnp.concatenate([c_ctx[None, :], c, jnp.zeros((7 - n_batch, d_model), F32)], axis=0)
    mod_all = _ada(c_rows, w_ada, b_ada).reshape(depth, 8, 6, d_model)

    head_ones = (jnp.arange(cw)[:, None] // RW_HEAD == jnp.arange(cw)[None, :] // RW_HEAD).astype(BF16)
    w_branch_b, w_out_b = w_branch.astype(BF16), w_out.astype(BF16)
    w_mlp1_b, w_mlp2_b = w_mlp1.astype(BF16), w_mlp2.astype(BF16)
    row = lambda a: a.reshape(1, -1)

    for l in range(depth):
        last = l == depth - 1
        mod = mod_all[l]
        p_gate, p_all = _inproj(x_parts, mod, row(g_norm1[l]), _permute_w_in(w_in[l], d_model),
                                tm=tm, nctx=nctx, bpb=bpb)

        prm = dict(conv=rw_conv[l], w0=row(rw_w0[l]), w2=_blockdiag2(rw_w2[l]), a0=row(rw_a0[l]),
                   a2=_blockdiag2(rw_a2[l]), g2=rw_g2[l].astype(BF16), kk=row(rw_kk[l]),
                   ka=row(rw_ka[l]), rk=row(rw_rk[l]), e=head_ones)
        r, v, nkk, g, bonus, w, kka, km = _rw_prep(p_all, prm, tb=tb, nctx=m_ctx // tb,
                                                    ctx_bps=ctxb, lat_bps=latb)
        ys = _rw_scan(r, v, nkk, w, kka, km, **seqs)

        ob = _hgrn2(p_all, hg_gamma, layer=l, **seqs)

        gw = _blockdiag2(gla_gw[l])
        gb = row(gla_gb[l])
        oc = _gla(p_all, gw, gb, **seqs)

        bs = jnp.broadcast_to(sgu_b[l][:, :, None], sgu_w[l].shape)
        yd = _sgu(p_all, row(sgu_ln_w[l]), row(sgu_ln_b[l]), sgu_w[l].astype(BF16), bs,
                  rb=_pick((512, 256, 128), m_ctx, seq))

        row_off = nctx if last else 0
        vecs = (row(rw_ln_w[l]), row(rw_ln_b[l]), row(hg_norm[l]), row(gla_norm[l]))
        mm = _merge((ys[0], ys[1], g, bonus), ob, oc, yd, p_gate, p_all, vecs, head_ones,
                    w_branch_b, layer=l, tm=tm // 2, row_off=2 * row_off)
        x_mid = _outproj(mm, w_out_b, x_parts, mod, layer=l, tm=tm, row_off=row_off, nctx=nctx, bpb=bpb)
        x_all = _mlp(x_mid, mod, row(g_norm2[l]), w_mlp1_b, w_mlp2_b, row(g_final), layer=l,
                     tm=tm // 2, row_off=2 * row_off, nctx=2 * nctx, bpb=2 * bpb, final_norm=last)
        x_parts = (x_all, x_all, nctx)
    return x_all.reshape(n_batch, seq, d_model)
```

```python
import functools

import jax
import jax.numpy as jnp
from jax import lax
from jax.experimental import pallas as pl
from jax.experimental.pallas import tpu as pltpu

F32 = jnp.float32
BF16 = jnp.bfloat16

N_BRANCH = 4
RW_HEAD = 64
RW_LN_EPS = 64e-5
HG_DK = 128
GLA_HEADS = 4
GLA_GATE_NORM = 16.0
LA_CHUNK = 16
RW_CHUNK = 32
RW_SUB = 16
SGU_CHUNK = 128
SGU_GROUPS = 4
EPS = 1e-6

C_RKV = 0
C_HGF = 1536
C_SGU = 2560
C_HGQ = 3584
C_HGI = 4096
C_HGG = 4608
C_GLQK = 5120
C_GLV = 5632
C_GLG = 6144
C_SMALL = 6656
N_MIX = 7168
N_GATE = 8192

VMEM_LIMIT = 58 * 1024 * 1024


def _cparams(n_axes):
    return pltpu.CompilerParams(dimension_semantics=("arbitrary",) * n_axes,
                                vmem_limit_bytes=VMEM_LIMIT)


def _pick(n_list, *dims):
    for n in n_list:
        if all(d % n == 0 for d in dims):
            return n
    raise ValueError(f"no block size in {n_list} divides {dims}")


def _row_group(i, nctx, bpb):
    return jnp.where(i < nctx, 0, 1 + (i - nctx) // bpb)


def _dot(a, b):
    return jnp.dot(a, b, preferred_element_type=F32)


def _dot_nt(a, b):
    return lax.dot_general(a, b, (((1,), (1,)), ((), ())), preferred_element_type=F32)


def _dot_tn(a, b):
    return lax.dot_general(a, b, (((0,), (0,)), ((), ())), preferred_element_type=F32)


def _split_dot(x, e):
    hi = x.astype(BF16)
    lo = (x - hi.astype(F32)).astype(BF16)
    return _dot(hi, e) + _dot(lo, e)


def _split3_dot(e, x):
    p1 = x.astype(BF16)
    r1 = x - p1.astype(F32)
    p2 = r1.astype(BF16)
    p3 = (r1 - p2.astype(F32)).astype(BF16)
    return _dot(e, p1) + _dot(e, p2) + _dot(e, p3)


def _log_sigmoid(x):
    return jnp.minimum(x, 0.0) - jnp.log(1.0 + jnp.exp(-jnp.abs(x)))


def _sigmoid(x):
    return 0.5 * jnp.tanh(0.5 * x) + 0.5


def _ada_kernel(c_ref, w_ref, b_ref, o_ref):
    c = c_ref[...]
    act = c * _sigmoid(c)
    o_ref[0] = _dot(act.astype(BF16), w_ref[0].astype(BF16)) + b_ref[0]


def _ada(c_rows, w_ada, b_ada):
    n_layers, d, n = w_ada.shape
    tn = _pick((1024, 512, 256, 128), n)
    return pl.pallas_call(
        _ada_kernel,
        grid=(n_layers, n // tn),
        in_specs=[pl.BlockSpec((8, d), lambda l, j: (0, 0)),
                  pl.BlockSpec((1, d, tn), lambda l, j: (l, 0, j)),
                  pl.BlockSpec((1, 1, tn), lambda l, j: (l, 0, j))],
        out_specs=pl.BlockSpec((1, 8, tn), lambda l, j: (l, 0, j)),
        out_shape=jax.ShapeDtypeStruct((n_layers, 8, n), F32),
        compiler_params=_cparams(2),
        name="ada_mod",
    )(c_rows, w_ada, b_ada.reshape(n_layers, 1, n))


def _norm_mod(x, g, shift, scale):
    y = x * lax.rsqrt(jnp.mean(x * x, axis=-1, keepdims=True) + EPS) * g
    return y * (1.0 + scale) + shift


def _two_part_specs(block, nctx, lat_off, col):
    return [pl.BlockSpec(block, lambda i, j: (jnp.minimum(i, nctx - 1), col(j)), pipeline_mode=pl.Buffered(1)),
            pl.BlockSpec(block, lambda i, j: (lat_off + jnp.maximum(i - nctx, 0), col(j)))]


def _inproj_kernel(xc_ref, xl_ref, mod_ref, g_ref, w_ref, og_ref, om_ref, h_ref, *, nctx, n_gate_tiles):
    i, j = pl.program_id(0), pl.program_id(1)

    @pl.when(j == 0)
    def _():
        x = jnp.where(i < nctx, xc_ref[...], xl_ref[...])
        h = _norm_mod(x, g_ref[...], mod_ref[0, 0:1, :], mod_ref[0, 1:2, :])
        h_ref[...] = h.astype(BF16)

    @pl.when(j < n_gate_tiles)
    def _():
        og_ref[...] = _dot(h_ref[...], w_ref[...]).astype(BF16)

    @pl.when(j >= n_gate_tiles)
    def _():
        om_ref[...] = _dot(h_ref[...], w_ref[...])


def _inproj(x_parts, mod, g, w, *, tm, nctx, bpb):
    xc, xl, lat_off = x_parts
    d = xc.shape[1]
    m = (nctx + (xl.shape[0] // tm - lat_off)) * tm
    tn = _pick((1024, 512), N_GATE, N_MIX)
    ng = N_GATE // tn
    return pl.pallas_call(
        functools.partial(_inproj_kernel, nctx=nctx, n_gate_tiles=ng),
        grid=(m // tm, (N_GATE + N_MIX) // tn),
        in_specs=_two_part_specs((tm, d), nctx, lat_off, lambda j: 0) + [
            pl.BlockSpec((1, 6, d), lambda i, j: (_row_group(i, nctx, bpb), 0, 0)),
            pl.BlockSpec((1, d), lambda i, j: (0, 0)),
            pl.BlockSpec((d, tn), lambda i, j: (0, j))],
        out_specs=[pl.BlockSpec((tm, tn), lambda i, j: (i, jnp.minimum(j, ng - 1))),
                   pl.BlockSpec((tm, tn), lambda i, j: (i, jnp.maximum(j - ng, 0)))],
        out_shape=[jax.ShapeDtypeStruct((m, N_GATE), BF16), jax.ShapeDtypeStruct((m, N_MIX), F32)],
        scratch_shapes=[pltpu.VMEM((tm, d), BF16)],
        compiler_params=_cparams(2),
        name="inproj",
    )(xc, xl, mod, g, w)


def _rw_prep_kernel(rkv_ref, prev_ref, next_ref, small_ref, conv_ref, w0_ref, w2_ref, a0_ref,
                    a2_ref, g2_ref, kk_ref, ka_ref, rk_ref, e_ref,
                    r_out, v_out, nkk_out, g_out, bonus_out, w_out, kka_out, km_out,
                    *, tb, nctx, ctx_bps, lat_bps):
    i = pl.program_id(0)
    c = r_out.shape[-1]
    j = jnp.where(i < nctx, i, i - nctx)
    bps = jnp.where(i < nctx, ctx_bps, lat_bps)
    first = lax.rem(j, bps) == 0
    last = lax.rem(j, bps) == bps - 1

    blk = rkv_ref[...]
    rows = lax.broadcasted_iota(jnp.int32, (tb, 1), 0)
    prev_row = jnp.where(first, 0.0, prev_ref[7:8, :])
    next_row = jnp.where(last, 0.0, next_ref[0:1, :])
    xm1 = jnp.where(rows == 0, prev_row, pltpu.roll(blk, 1, 0))
    xp1 = jnp.where(rows == tb - 1, next_row, pltpu.roll(blk, tb - 1, 0))
    conv = conv_ref[0:1, :] * xm1 + conv_ref[1:2, :] * blk + conv_ref[2:3, :] * xp1
    r = conv[:, 0:c]
    k = conv[:, c:2 * c]
    v = conv[:, 2 * c:3 * c]

    small = small_ref[...]
    wl = small[:, 0:128]
    al = small[:, 128:256]
    gl = small[:, 256:384]
    w_pre = w0_ref[...] + _dot(jnp.tanh(wl).astype(BF16), w2_ref[...])
    softplus = jnp.maximum(-w_pre, 0.0) + jnp.log(1.0 + jnp.exp(-jnp.abs(w_pre)))
    log_decay = -jnp.exp(-softplus - 0.5)
    a = _sigmoid(a0_ref[...] + _dot(al.astype(BF16), a2_ref[...]))
    g = _dot(_sigmoid(gl).astype(BF16), g2_ref[...])

    e = e_ref[...]
    kkv = k * kk_ref[...]
    kk = kkv * lax.rsqrt(_split_dot(kkv * kkv, e) + 1e-12)
    bonus = _split_dot(r * k * rk_ref[...], e) * v

    r_out[...] = r.astype(BF16)
    v_out[...] = v.astype(BF16)
    nkk_out[...] = (-kk).astype(BF16)
    g_out[...] = g
    bonus_out[...] = bonus
    for d in range(2):
        a_d = a[:, d * c:(d + 1) * c]
        w_out[d] = log_decay[:, d * c:(d + 1) * c]
        kka_out[d] = (kk * a_d).astype(BF16)
        km_out[d] = (k * (1.0 + (a_d - 1.0) * ka_ref[...])).astype(BF16)


def _rw_prep(p_all, prm, *, tb, nctx, ctx_bps, lat_bps):
    m = p_all.shape[0]
    c = prm["kk"].shape[-1]
    nblk = m // tb
    t8 = tb // 8
    full = lambda shape: pl.BlockSpec(shape, lambda i: (0,) * len(shape))
    tok = pl.BlockSpec((tb, c), lambda i: (i, 0))
    tok2 = pl.BlockSpec((2, tb, c), lambda i: (0, i, 0))
    one, one_b = jax.ShapeDtypeStruct((m, c), F32), jax.ShapeDtypeStruct((m, c), BF16)
    two, two_b = jax.ShapeDtypeStruct((2, m, c), F32), jax.ShapeDtypeStruct((2, m, c), BF16)
    return pl.pallas_call(
        functools.partial(_rw_prep_kernel, tb=tb, nctx=nctx, ctx_bps=ctx_bps, lat_bps=lat_bps),
        grid=(nblk,),
        in_specs=[pl.BlockSpec((tb, 3 * c), lambda i: (i, C_RKV // (3 * c))),
                  pl.BlockSpec((8, 3 * c), lambda i: (jnp.maximum(i * t8 - 1, 0), 0)),
                  pl.BlockSpec((8, 3 * c), lambda i: (jnp.minimum((i + 1) * t8, m // 8 - 1), 0)),
                  pl.BlockSpec((tb, 512), lambda i: (i, C_SMALL // 512)),
                  full((3, 3 * c)), full((1, 2 * c)), full((128, 2 * c)), full((1, 2 * c)),
                  full((128, 2 * c)), full((128, c)), full((1, c)), full((1, c)), full((1, c)),
                  full((c, c))],
        out_specs=[tok, tok, tok, tok, tok, tok2, tok2, tok2],
        out_shape=[one_b, one_b, one_b, one, one, two, two_b, two_b],
        compiler_params=_cparams(1),
        name="rwkv_prep",
    )(p_all, p_all, p_all, p_all, prm["conv"], prm["w0"], prm["w2"], prm["a0"], prm["a2"],
      prm["g2"], prm["kk"], prm["ka"], prm["rk"], prm["e"])


class _RwUnit:
    def __init__(self, refs, scratch, *, tb, npair, reverse):
        (self.r_ref, self.v_ref, self.a_ref, self.lw_ref, self.b_ref, self.k_ref, self.tri_ref,
         self.y_ref) = refs
        (self.s_ref, self.c_scr, self.ag_scr, self.rg_scr, self.bg_scr, self.kg_scr) = scratch[:6]
        self.slots = (scratch[6:10], scratch[10:14])
        self.reverse = reverse
        self.nch = tb // RW_CHUNK
        self.pairs = range(npair)
        L, hd = RW_CHUNK, RW_HEAD
        self.lss = [slice(p * 2 * hd, (p + 1) * 2 * hd) for p in self.pairs]
        lane = lax.broadcasted_iota(jnp.int32, (1, 2 * hd), 1)
        self.lo = (lane < hd).astype(F32)
        self.hi = 1.0 - self.lo
        ti = lax.broadcasted_iota(jnp.int32, (2 * L, 2 * hd), 0)
        sl = lax.broadcasted_iota(jnp.int32, (2 * L, 2 * hd), 1)
        sj = sl & (L - 1)
        tt = ti & (L - 1)
        earlier = (sj > tt) if reverse else (sj < tt)
        assert 4 * L == 2 * hd and L == 2 * RW_SUB
        self.aa_mask = (earlier | ((ti >= L) & (sj == tt))).astype(F32)
        self.col_idx0 = jnp.where(lax.broadcasted_iota(jnp.int32, (RW_SUB, 2 * hd), 1) < hd, 0, L)
        self.blk_mask = ((lax.broadcasted_iota(jnp.int32, (2 * hd, 2 * hd), 0) >> 6)
                         == (lax.broadcasted_iota(jnp.int32, (2 * hd, 2 * hd), 1) >> 6)).astype(F32)
        self.zeros2l = jnp.zeros((2 * L, 2 * hd), F32)
        self.zeros_sub = jnp.zeros((RW_SUB, 2 * hd), F32)
        self.blocks = (1, 0) if reverse else (0, 1)
        self.last = 0 if reverse else L - 1
        self.order = range(L - 1, -1, -1) if reverse else range(L)

    def reset(self):
        self.s_ref[...] = jnp.zeros_like(self.s_ref)

    def prologue(self):
        lw = self.lw_ref[0]
        c = _split3_dot(self.tri_ref[...], lw)
        self.c_scr[...] = c
        enc = jnp.exp(-c)
        self.ag_scr[...] = self.a_ref[...] * jnp.exp(c - lw)
        self.rg_scr[...] = self.r_ref[...] * jnp.exp(c)
        self.bg_scr[...] = self.b_ref[0] * enc
        self.kg_scr[...] = self.k_ref[0] * enc

    def rows(self, ci):
        L = RW_CHUNK
        cc = (self.nch - 1 - ci) if self.reverse else ci
        return pl.ds(pl.multiple_of(cc * L, L), L)

    def halves(self, x):
        return [x * self.lo, x * self.hi]

    def lhs_of(self, rows):
        return [jnp.concatenate([self.ag_scr[rows, ls], self.rg_scr[rows, ls]], axis=0).astype(BF16)
                for ls in self.lss]

    def prep_aa(self, rows):
        lhs = self.lhs_of(rows)
        out = []
        for p in self.pairs:
            ls = self.lss[p]
            rhs = jnp.concatenate(self.halves(self.bg_scr[rows, ls]) + self.halves(self.kg_scr[rows, ls]),
                                  axis=0)
            out.append(_dot_nt(lhs[p], rhs.astype(BF16)))
        return out

    def prep_akv(self, rows, aa_raw, slot):
        _, akv_scr, aar_scr, na_scr = slot
        L = RW_CHUNK
        aa = [x * self.aa_mask for x in aa_raw]
        for p in self.pairs:
            v = self.v_ref[rows, self.lss[p]]
            vv = jnp.concatenate([self.zeros2l] + self.halves(v), axis=0).astype(BF16)
            akv_scr[p] = _dot(aa[p][0:L].astype(BF16), vv)
            aar_scr[p] = aa[p][L:2 * L]
            na_scr[p] = aa[p][0:L]
        return aa

    def prep_cols(self, aa, slot):
        half = RW_SUB // 2
        for p in self.pairs:
            for s in range(RW_CHUNK):
                r0 = (s // RW_SUB) * RW_SUB
                lo, hi = self.live_rows(s - r0)
                slot[0][p, s, lo:hi] = jnp.take_along_axis(aa[p][r0 + lo:r0 + hi],
                                                           self.col_idx0[0:hi - lo] + s, axis=1)

    def live_rows(self, s_local):
        half = RW_SUB // 2
        if self.reverse:
            return (0, half) if s_local <= half else (0, RW_SUB)
        return (half, RW_SUB) if s_local >= half - 1 else (0, RW_SUB)

    def adv_g(self, rows):
        lhs = self.lhs_of(rows)
        return [_dot_nt(lhs[p], self.s_ref[p].astype(BF16)) for p in self.pairs]

    def adv_solve(self, g, slot):
        L, sub = RW_CHUNK, RW_SUB
        col_scr, akv_scr, _, na_scr = slot
        first, second = self.blocks
        ub = {}
        for blk in self.blocks:
            r0 = blk * sub
            ub[blk] = [g[p][r0:r0 + sub] + akv_scr[p, r0:r0 + sub] for p in self.pairs]
            if blk == second:
                for p in self.pairs:
                    done = ub[first][p]
                    z = self.zeros_sub
                    parts = [z, z, z, z]
                    parts[first], parts[2 + first] = done * self.lo, done * self.hi
                    stack = jnp.concatenate(parts + [self.zeros2l], axis=0).astype(BF16)
                    ub[blk][p] = ub[blk][p] + _dot(na_scr[p, r0:r0 + sub].astype(BF16), stack)
            steps = [s for s in self.order if s // sub == blk]
            half = sub // 2
            top = [x[0:half] for x in ub[blk]]
            bot = [x[half:] for x in ub[blk]]
            for s in steps:
                sl = s - r0
                lo, hi = self.live_rows(sl)
                for p in self.pairs:
                    row = top[p][sl:sl + 1] if sl < half else bot[p][sl - half:sl - half + 1]
                    if lo == 0:
                        top[p] = top[p] + col_scr[p, s, 0:half] * row
                    if hi == sub:
                        bot[p] = bot[p] + col_scr[p, s, half:sub] * row
            ub[blk] = [jnp.concatenate([top[p], bot[p]], axis=0) for p in self.pairs]
        return [jnp.concatenate([ub[0][p], ub[1][p]], axis=0) for p in self.pairs]

    def adv_out(self, rows, g, u, slot):
        L = RW_CHUNK
        for p in self.pairs:
            v = self.v_ref[rows, self.lss[p]]
            uv = jnp.concatenate(self.halves(u[p]) + self.halves(v), axis=0).astype(BF16)
            self.y_ref[rows, self.lss[p]] = g[p][L:2 * L] + _dot(slot[2][p].astype(BF16), uv)
        for p in self.pairs:
            ls = self.lss[p]
            cch = self.c_scr[rows, ls]
            cl = cch[self.last:self.last + 1]
            dec = jnp.exp(cl - cch)
            bk = jnp.concatenate([self.b_ref[0, rows, ls] * dec, self.k_ref[0, rows, ls] * dec], axis=0)
            upd = _dot_tn(jnp.concatenate([u[p], self.v_ref[rows, ls]], axis=0).astype(BF16),
                          bk.astype(BF16))
            self.s_ref[p] = self.s_ref[p] * jnp.exp(cl) + upd * self.blk_mask


N_RW_REFS = 8
N_RW_SCRATCH = 14


def _rw_scan_kernel(*refs, tb, npair):
    n_in = N_RW_REFS - 1
    units = []
    for d in range(2):
        ins = refs[d * n_in:(d + 1) * n_in]
        out = refs[2 * n_in + d]
        scr = refs[2 * n_in + 2 + d * N_RW_SCRATCH:2 * n_in + 2 + (d + 1) * N_RW_SCRATCH]
        units.append(_RwUnit(tuple(ins) + (out,), scr, tb=tb, npair=npair, reverse=d == 1))
    nch = tb // RW_CHUNK

    @pl.when(pl.program_id(1) == 0)
    def _():
        for un in units:
            un.reset()

    for un in units:
        un.prologue()

    def step(ci_adv, sa, ci_prep, sp):
        rows_a = [un.rows(ci_adv) for un in units]
        rows_p = [un.rows(ci_prep) for un in units]
        aa_raw = [un.prep_aa(rp) for un, rp in zip(units, rows_p)]
        g = [un.adv_g(ra) for un, ra in zip(units, rows_a)]
        aa = [un.prep_akv(rp, x, un.slots[sp]) for un, rp, x in zip(units, rows_p, aa_raw)]
        u = [un.adv_solve(x, un.slots[sa]) for un, x in zip(units, g)]
        for un, x in zip(units, aa):
            un.prep_cols(x, un.slots[sp])
        for un, ra, x, y in zip(units, rows_a, g, u):
            un.adv_out(ra, x, y, un.slots[sa])

    for un in units:
        r0 = un.rows(0)
        un.prep_cols(un.prep_akv(r0, un.prep_aa(r0), un.slots[0]), un.slots[0])

    def two_chunks(j, carry):
        c0 = 2 * j
        step(c0, 0, c0 + 1, 1)
        step(c0 + 1, 1, jnp.minimum(c0 + 2, nch - 1), 0)
        return carry

    lax.fori_loop(0, nch // 2, two_chunks, 0)


def _seq_block(d, b, i, *, n_batch, ctxb, latb):
    is_ctx = i < ctxb
    cs = jnp.where(d == 0, i, ctxb - 1 - i)
    lj = jnp.where(d == 0, i - ctxb, latb - 1 - (i - ctxb))
    return jnp.where(is_ctx, b * ctxb + cs, n_batch * ctxb + b * latb + lj)


def _rw_scan(r, v, nkk, lw, kka, km, *, tb, n_batch, ctxb, latb):
    m, c = r.shape
    npair = c // (2 * RW_HEAD)
    lanes = 2 * RW_HEAD
    in_specs, args = [], []
    for d in range(2):
        blk = _la_index(d, n_batch, ctxb, latb)
        tok = pl.BlockSpec((tb, c), lambda b, i, blk=blk: (blk(b, i), 0))
        tokd = pl.BlockSpec((1, tb, c), lambda b, i, blk=blk, d=d: (d, blk(b, i), 0))
        in_specs += [tok, tok, tok, tokd, tokd, tokd, pl.BlockSpec((tb, tb), lambda b, i: (0, 0))]
        args += [r, v, nkk, lw, kka, km, _chunk_tri(tb, d == 1, RW_CHUNK)]
    out_specs = [pl.BlockSpec((tb, c), lambda b, i, blk=_la_index(d, n_batch, ctxb, latb): (blk(b, i), 0))
                 for d in range(2)]
    buf = pltpu.VMEM((tb, c), F32)
    unit_scratch = [pltpu.VMEM((npair, lanes, lanes), F32), buf, buf, buf, buf, buf] + 2 * [
        pltpu.VMEM((npair, RW_CHUNK, RW_SUB, lanes), F32),
        pltpu.VMEM((npair, RW_CHUNK, lanes), F32),
        pltpu.VMEM((npair, RW_CHUNK, lanes), F32),
        pltpu.VMEM((npair, RW_CHUNK, lanes), F32)]
    assert len(unit_scratch) == N_RW_SCRATCH
    return pl.pallas_call(
        functools.partial(_rw_scan_kernel, tb=tb, npair=npair),
        grid=(n_batch, ctxb + latb),
        in_specs=in_specs,
        out_specs=out_specs,
        out_shape=[jax.ShapeDtypeStruct((m, c), F32)] * 2,
        scratch_shapes=unit_scratch + unit_scratch,
        compiler_params=_cparams(2),
        name="rwkv_scan",
    )(*args)


def _rw_post(y, g, bonus, lnw, lnb, e):
    inv = 1.0 / RW_HEAD
    yc = y - _split_dot(y, e) * inv
    var = _split_dot(yc * yc, e) * inv
    yn = yc * lax.rsqrt(var + RW_LN_EPS)
    return (yn * lnw + lnb + bonus) * g


class _LaUnit:
    def __init__(self, q_ref, k_ref, v_ref, b_ref, o_ref, s_ref, *, tb, nh, dk, dv, reverse, q_scale):
        self.q_ref, self.k_ref, self.v_ref, self.b_ref, self.o_ref, self.s_ref = (
            q_ref, k_ref, v_ref, b_ref, o_ref, s_ref)
        self.nch = tb // LA_CHUNK
        self.dk, self.dv, self.reverse, self.q_scale = dk, dv, reverse, q_scale
        self.lanes = 128
        self.pack = self.lanes // dk
        self.groups = range(nh // self.pack)
        self.rowi = lax.broadcasted_iota(jnp.int32, (LA_CHUNK, 1), 0)
        lane = lax.broadcasted_iota(jnp.int32, (1, self.lanes), 1)
        self.head_lanes = [(lane // dk == j).astype(F32) for j in range(self.pack)]
        if self.pack > 1:
            rows = lax.broadcasted_iota(jnp.int32, (self.pack * dv, self.lanes), 0)
            cols = lax.broadcasted_iota(jnp.int32, (self.pack * dv, self.lanes), 1)
            self.blk_mask = ((rows // dv) == (cols // dk)).astype(F32)

    def reset(self):
        self.s_ref[...] = jnp.zeros_like(self.s_ref)

    def load(self, ci):
        cc = (self.nch - 1 - ci) if self.reverse else ci
        rows = pl.ds(pl.multiple_of(cc * LA_CHUNK, LA_CHUNK), LA_CHUNK)
        pack, dv, lanes = self.pack, self.dv, self.lanes
        out = []
        for g in self.groups:
            ks = slice(g * lanes, (g + 1) * lanes)
            q = self.q_ref[rows, ks] * self.q_scale
            k = self.k_ref[rows, ks]
            v = self.v_ref[rows, g * pack * dv:(g + 1) * pack * dv]
            b = self.b_ref[rows, ks]
            o_inter = _dot_nt((q * jnp.exp(b)).astype(BF16), self.s_ref[g].astype(BF16))
            out.append((rows, q, k, v, b, [o_inter[:, j * dv:(j + 1) * dv] for j in range(pack)]))
        return out

    def intra(self, ops):
        pack, dv, half = self.pack, self.dv, LA_CHUNK // 2
        rowi = self.rowi[0:half]
        for g in self.groups:
            rows, q, k, v, b, o = ops[g]
            qh = [q[0:half], q[half:]]
            bh = [b[0:half], b[half:]]
            oh = [[oj[0:half], oj[half:]] for oj in o]
            for s in range(LA_CHUNK):
                for h in range(2):
                    lo_row, hi_row = h * half, (h + 1) * half - 1
                    if (hi_row > s) if self.reverse else (lo_row < s):
                        if (lo_row > s) if self.reverse else (hi_row < s):
                            continue
                        valid = (rowi + lo_row <= s) if self.reverse else (rowi + lo_row >= s)
                    else:
                        valid = None
                    term = (qh[h] * k[s:s + 1]) * jnp.exp(bh[h] - b[s:s + 1])
                    for j in range(pack):
                        tj = term if pack == 1 else term * self.head_lanes[j]
                        col = jnp.sum(tj, axis=-1, keepdims=True)
                        if valid is not None:
                            col = jnp.where(valid, col, 0.0)
                        oh[j][h] = oh[j][h] + col * v[s:s + 1, j * dv:(j + 1) * dv]
            for j in range(pack):
                self.o_ref[rows, (g * pack + j) * dv:(g * pack + j + 1) * dv] = jnp.concatenate(oh[j], axis=0)

    def update(self, ops):
        for g in self.groups:
            _, _, k, v, b, _ = ops[g]
            b_last = b[0:1] if self.reverse else b[LA_CHUNK - 1:LA_CHUNK]
            upd = _dot_tn(v.astype(BF16), (k * jnp.exp(b_last - b)).astype(BF16))
            if self.pack > 1:
                upd = upd * self.blk_mask
            self.s_ref[g] = self.s_ref[g] * jnp.exp(b_last) + upd


def _la_run(units, nch):
    def chunk(ci, carry):
        ops = [un.load(ci) for un in units]
        for un, x in zip(units, ops):
            un.update(x)
        for un, x in zip(units, ops):
            un.intra(x)
        return carry

    lax.fori_loop(0, nch, chunk, 0)


def _hg_kernel(qf_ref, ff_ref, vf_ref, trif_ref, qb_ref, fb_ref, vb_ref, trib_ref, gamma_ref,
               of_ref, ob_ref, sf_ref, kf_scr, bf_scr, sb_ref, kb_scr, bb_scr, *, layer, tb, nh, dk, dv):
    gam = gamma_ref[...]
    ex = jnp.exp(gam - jnp.max(gam, axis=0, keepdims=True))
    p = ex / jnp.sum(ex, axis=0, keepdims=True)
    cum = p[0:1]
    for i in range(1, layer + 1):
        cum = cum + p[i:i + 1]
    lb = cum - p[0:1]
    lo = jnp.log(lb)
    l1 = jnp.log(1.0 - lb)

    units = []
    for d, (q_ref, f_ref, v_ref, tri_ref, o_ref, s_ref, k_scr, b_scr) in enumerate((
            (qf_ref, ff_ref, vf_ref, trif_ref, of_ref, sf_ref, kf_scr, bf_scr),
            (qb_ref, fb_ref, vb_ref, trib_ref, ob_ref, sb_ref, kb_scr, bb_scr))):
        hi = l1 + _log_sigmoid(f_ref[...])
        mx = jnp.maximum(lo, hi)
        mn = jnp.minimum(lo, hi)
        log_f = mx + jnp.log(1.0 + jnp.exp(mn - mx))
        k_scr[...] = 1.0 - jnp.exp(log_f)
        b_scr[...] = _split3_dot(tri_ref[...], log_f)
        units.append(_LaUnit(q_ref, k_scr, v_ref, b_scr, o_ref, s_ref, tb=tb, nh=nh, dk=dk, dv=dv,
                             reverse=d == 1, q_scale=1.0))

    @pl.when(pl.program_id(1) == 0)
    def _():
        for un in units:
            un.reset()

    _la_run(units, tb // LA_CHUNK)


def _gla_kernel(qkf_ref, vf_ref, smallf_ref, trif_ref, qkb_ref, vb_ref, smallb_ref, trib_ref,
                gw_ref, gb_ref, of_ref, ob_ref, sf_ref, bf_scr, sb_ref, bb_scr, *, tb, nh, dk, dv):
    hk = nh * dk
    units = []
    for d, (qk_ref, v_ref, small_ref, tri_ref, o_ref, s_ref, b_scr) in enumerate((
            (qkf_ref, vf_ref, smallf_ref, trif_ref, of_ref, sf_ref, bf_scr),
            (qkb_ref, vb_ref, smallb_ref, trib_ref, ob_ref, sb_ref, bb_scr))):
        code = small_ref[:, 384:512]
        pre = _dot(code.astype(BF16), gw_ref[:, d * hk:(d + 1) * hk]) + gb_ref[:, d * hk:(d + 1) * hk]
        log_g = _log_sigmoid(pre) * (1.0 / GLA_GATE_NORM)
        b_scr[...] = _split3_dot(tri_ref[...], log_g)
        units.append(_LaUnit(qk_ref.at[:, 0:hk], qk_ref.at[:, hk:2 * hk], v_ref, b_scr, o_ref, s_ref,
                             tb=tb, nh=nh, dk=dk, dv=dv, reverse=d == 1, q_scale=dk ** -0.5))

    @pl.when(pl.program_id(1) == 0)
    def _():
        for un in units:
            un.reset()

    _la_run(units, tb // LA_CHUNK)


def _chunk_tri(tb, reverse, chunk=LA_CHUNK):
    t = jnp.arange(tb)
    same = (t[:, None] // chunk) == (t[None, :] // chunk)
    tri = (t[None, :] >= t[:, None]) if reverse else (t[None, :] <= t[:, None])
    return (same & tri).astype(BF16)


def _la_index(d, n_batch, ctxb, latb):
    return lambda b, i: _seq_block(d, b, i, n_batch=n_batch, ctxb=ctxb, latb=latb)


def _hgrn2(p_all, gamma, *, layer, tb, n_batch, ctxb, latb):
    m = p_all.shape[0]
    c = gamma.shape[-1]
    nh = c // HG_DK
    in_specs, args, out_specs = [], [], []
    for d in range(2):
        blk = _la_index(d, n_batch, ctxb, latb)
        col = lambda off, blk=blk: pl.BlockSpec((tb, c), lambda b, i: (blk(b, i), off // c))
        in_specs += [col(C_HGQ), col(C_HGF + d * c), col(C_HGI), pl.BlockSpec((tb, tb), lambda b, i: (0, 0))]
        args += [p_all, p_all, p_all, _chunk_tri(tb, d == 1)]
        out_specs.append(col(0))
    unit_scratch = [pltpu.VMEM((nh, HG_DK, HG_DK), F32), pltpu.VMEM((tb, c), F32), pltpu.VMEM((tb, c), F32)]
    return pl.pallas_call(
        functools.partial(_hg_kernel, layer=layer, tb=tb, nh=nh, dk=HG_DK, dv=HG_DK),
        grid=(n_batch, ctxb + latb),
        in_specs=in_specs + [pl.BlockSpec(gamma.shape, lambda b, i: (0, 0))],
        out_specs=out_specs,
        out_shape=[jax.ShapeDtypeStruct((m, c), F32)] * 2,
        scratch_shapes=unit_scratch + unit_scratch,
        compiler_params=_cparams(2),
        name="hgrn2_scan",
    )(*args, gamma)


def _gla(p_all, gw, gb, *, tb, n_batch, ctxb, latb):
    m = p_all.shape[0]
    hk = gw.shape[-1] // 2
    dk = hk // GLA_HEADS
    c = 2 * hk
    dv = c // GLA_HEADS
    in_specs, args, out_specs = [], [], []
    for d in range(2):
        blk = _la_index(d, n_batch, ctxb, latb)
        col = lambda off, blk=blk: pl.BlockSpec((tb, c), lambda b, i: (blk(b, i), off // c))
        in_specs += [col(C_GLQK), col(C_GLV), col(C_SMALL), pl.BlockSpec((tb, tb), lambda b, i: (0, 0))]
        args += [p_all, p_all, p_all, _chunk_tri(tb, d == 1)]
        out_specs.append(col(0))
    unit_scratch = [pltpu.VMEM((hk // 128, (128 // dk) * dv, 128), F32), pltpu.VMEM((tb, hk), F32)]
    return pl.pallas_call(
        functools.partial(_gla_kernel, tb=tb, nh=GLA_HEADS, dk=dk, dv=dv),
        grid=(n_batch, ctxb + latb),
        in_specs=in_specs + [pl.BlockSpec(gw.shape, lambda b, i: (0, 0)),
                             pl.BlockSpec(gb.shape, lambda b, i: (0, 0))],
        out_specs=out_specs,
        out_shape=[jax.ShapeDtypeStruct((m, c), F32)] * 2,
        scratch_shapes=unit_scratch + unit_scratch,
        compiler_params=_cparams(2),
        name="gla_scan",
    )(*args, gw, gb)


def _la_post(o, gate, norm_g, nh):
    dv = o.shape[-1] // nh
    outs = []
    for h in range(nh):
        oh = o[:, h * dv:(h + 1) * dv]
        outs.append(oh * lax.rsqrt(jnp.mean(oh * oh, axis=-1, keepdims=True) + EPS))
    return jnp.concatenate(outs, axis=-1) * norm_g * (gate * _sigmoid(gate))


def _gelu(x):
    return 0.5 * x * (1.0 + jnp.tanh(0.7978845608028654 * (x + 0.044715 * (x * x * x))))


def _sgu_kernel(u_ref, v_ref, lnw_ref, lnb_ref, ws_ref, bs_ref, o_ref, *, rb):
    u = _gelu(u_ref[...])
    v = _gelu(v_ref[...])
    vc = v - jnp.mean(v, axis=-1, keepdims=True)
    vn = vc * lax.rsqrt(jnp.mean(vc * vc, axis=-1, keepdims=True) + EPS)
    vn = (vn * lnw_ref[...] + lnb_ref[...]).astype(BF16)
    gw = vn.shape[-1] // SGU_GROUPS
    for n in range(rb // SGU_CHUNK):
        rs = slice(n * SGU_CHUNK, (n + 1) * SGU_CHUNK)
        for g in range(SGU_GROUPS):
            cs = slice(g * gw, (g + 1) * gw)
            s = _dot(ws_ref[g], vn[rs, cs]) + bs_ref[g]
            o_ref[rs, cs] = u[rs, cs] * s


def _sgu(p_all, lnw, lnb, ws, bs, *, rb):
    m = p_all.shape[0]
    c = lnw.shape[-1]
    return pl.pallas_call(
        functools.partial(_sgu_kernel, rb=rb),
        grid=(m // rb,),
        in_specs=[pl.BlockSpec((rb, c), lambda i: (i, C_SGU // c)),
                  pl.BlockSpec((rb, c), lambda i: (i, C_SGU // c + 1)),
                  pl.BlockSpec((1, c), lambda i: (0, 0)),
                  pl.BlockSpec((1, c), lambda i: (0, 0)),
                  pl.BlockSpec(ws.shape, lambda i: (0, 0, 0)),
                  pl.BlockSpec(bs.shape, lambda i: (0, 0, 0))],
        out_specs=pl.BlockSpec((rb, c), lambda i: (i, 0)),
        out_shape=jax.ShapeDtypeStruct((m, c), F32),
        compiler_params=_cparams(1),
        name="sgu",
    )(p_all, p_all, lnw, lnb, ws, bs)


def _merge_kernel(raf_ref, rab_ref, rg_ref, rbonus_ref, hf_ref, hb_ref, hgate_ref, gf_ref, gb_ref,
                  ggate_ref, yd_ref, lnw_ref, lnb_ref, e_ref, hnorm_ref, gnorm_ref,
                  g0_ref, g1_ref, g2_ref, g3_ref, w_ref, o_ref, y_scr, *, hg_heads):
    @pl.when(pl.program_id(1) == 0)
    def _():
        y_scr[0] = _rw_post(raf_ref[...] + rab_ref[...], rg_ref[...], rbonus_ref[...], lnw_ref[...],
                            lnb_ref[...], e_ref[...]).astype(BF16)
        y_scr[1] = _la_post(hf_ref[...] + hb_ref[...], hgate_ref[...], hnorm_ref[...], hg_heads).astype(BF16)
        y_scr[2] = _la_post(gf_ref[...] + gb_ref[...], ggate_ref[...], gnorm_ref[...], GLA_HEADS).astype(BF16)
        y_scr[3] = yd_ref[...].astype(BF16)

    gs = (g0_ref, g1_ref, g2_ref, g3_ref)
    acc = None
    for j in range(N_BRANCH):
        t = _sigmoid(gs[j][...].astype(F32)) * _dot(y_scr[j], w_ref[j])
        acc = t if acc is None else acc + t
    o_ref[...] = acc.astype(BF16)


def _merge(rw, hg, gla, yd, p_gate, p_mix, vecs, e, w_branch, *, layer, tm, row_off):
    m = p_mix.shape[0]
    _, _, c, d = w_branch.shape
    tn = _pick((1024, 512), d)
    mo = m - row_off * tm
    ytok = pl.BlockSpec((tm, c), lambda i, j: (i + row_off, 0))
    mix = lambda off: pl.BlockSpec((tm, c), lambda i, j: (i + row_off, off // c))
    gate = lambda b: pl.BlockSpec((tm, tn), lambda i, j: (i + row_off, b * d // tn + j))
    vec = pl.BlockSpec((1, c), lambda i, j: (0, 0))
    return pl.pallas_call(
        functools.partial(_merge_kernel, hg_heads=c // HG_DK),
        grid=(mo // tm, d // tn),
        in_specs=[ytok, ytok, ytok, ytok, ytok, ytok, mix(C_HGG), ytok, ytok, mix(C_GLG), ytok,
                  vec, vec, pl.BlockSpec((c, c), lambda i, j: (0, 0)), vec, vec,
                  gate(0), gate(1), gate(2), gate(3),
                  pl.BlockSpec((None, N_BRANCH, c, tn), lambda i, j: (layer, 0, 0, j))],
        out_specs=pl.BlockSpec((tm, tn), lambda i, j: (i, j)),
        out_shape=jax.ShapeDtypeStruct((mo, d), BF16),
        scratch_shapes=[pltpu.VMEM((N_BRANCH, tm, c), BF16)],
        compiler_params=_cparams(2),
        name="merge",
    )(*rw, hg[0], hg[1], p_mix, gla[0], gla[1], p_mix, yd, *vecs[:2], e, *vecs[2:],
      p_gate, p_gate, p_gate, p_gate, w_branch)


def _outproj_kernel(m_ref, w_ref, xc_ref, xl_ref, mod_ref, o_ref, *, nctx, row_off):
    x = jnp.where(pl.program_id(0) + row_off < nctx, xc_ref[...], xl_ref[...])
    o_ref[...] = x + mod_ref[0, 2:3, :] * _dot(m_ref[...], w_ref[...])


def _outproj(mm, w_out, x_parts, mod, *, layer, tm, row_off, nctx, bpb):
    xc, xl, lat_off = x_parts
    mo, d = mm.shape
    tn = _pick((1024, 512), d)
    xspecs = [pl.BlockSpec((tm, tn), lambda i, j: (jnp.minimum(i + row_off, nctx - 1),
                                                   jnp.where(i + row_off < nctx, j, 0))),
              pl.BlockSpec((tm, tn), lambda i, j: (lat_off + jnp.maximum(i + row_off - nctx, 0),
                                                   jnp.where(i + row_off < nctx, 0, j)))]
    return pl.pallas_call(
        functools.partial(_outproj_kernel, nctx=nctx, row_off=row_off),
        grid=(mo // tm, d // tn),
        in_specs=[pl.BlockSpec((tm, d), lambda i, j: (i, 0)),
                  pl.BlockSpec((None, d, tn), lambda i, j: (layer, 0, j))] + xspecs + [
                  pl.BlockSpec((1, 6, tn), lambda i, j: (_row_group(i + row_off, nctx, bpb), 0, j))],
        out_specs=pl.BlockSpec((tm, tn), lambda i, j: (i, j)),
        out_shape=jax.ShapeDtypeStruct((mo, d), F32),
        compiler_params=_cparams(2),
        name="outproj",
    )(mm, w_out, xc, xl, mod)


def _mlp_kernel(x_ref, mod_ref, g_ref, w1_ref, w2_ref, gf_ref, o_ref, h_ref, acc_ref, *, final_norm):
    j = pl.program_id(1)

    @pl.when(j == 0)
    def _():
        h = _norm_mod(x_ref[...], g_ref[...], mod_ref[0, 3:4, :], mod_ref[0, 4:5, :])
        h_ref[...] = h.astype(BF16)
        acc_ref[...] = jnp.zeros_like(acc_ref)

    a = jnp.maximum(_dot(h_ref[...], w1_ref[...]), 0.0)
    acc_ref[...] += _dot((a * a).astype(BF16), w2_ref[...])

    @pl.when(j == pl.num_programs(1) - 1)
    def _():
        y = x_ref[...] + mod_ref[0, 5:6, :] * acc_ref[...]
        if final_norm:
            y = y * lax.rsqrt(jnp.mean(y * y, axis=-1, keepdims=True) + EPS) * gf_ref[...]
        o_ref[...] = y


def _mlp(x_in, mod, g, w1, w2, g_final, *, layer, tm, row_off, nctx, bpb, final_norm):
    mo, d = x_in.shape
    hid = w1.shape[-1]
    th = _pick((1024, 512, 256, 128), hid)
    return pl.pallas_call(
        functools.partial(_mlp_kernel, final_norm=final_norm),
        grid=(mo // tm, hid // th),
        in_specs=[pl.BlockSpec((tm, d), lambda i, j: (i, 0)),
                  pl.BlockSpec((1, 6, d), lambda i, j: (_row_group(i + row_off, nctx, bpb), 0, 0)),
                  pl.BlockSpec((1, d), lambda i, j: (0, 0)),
                  pl.BlockSpec((None, d, th), lambda i, j: (layer, 0, j)),
                  pl.BlockSpec((None, th, d), lambda i, j: (layer, j, 0)),
                  pl.BlockSpec((1, d), lambda i, j: (0, 0))],
        out_specs=pl.BlockSpec((tm, d), lambda i, j: (i, 0)),
        out_shape=jax.ShapeDtypeStruct((mo, d), F32),
        scratch_shapes=[pltpu.VMEM((tm, d), BF16), pltpu.VMEM((tm, d), F32)],
        compiler_params=_cparams(2),
        name="mlp",
    )(x_in, mod, g, w1, w2, g_final)


def _blockdiag2(w):
    _, r, c = w.shape
    z = jnp.zeros((r, c), w.dtype)
    out = jnp.concatenate([jnp.concatenate([w[0], z], axis=1), jnp.concatenate([z, w[1]], axis=1)], axis=0)
    return jnp.pad(out, ((0, 128 - 2 * r), (0, 0))).astype(BF16)


def _permute_w_in(w, d_model):
    c = d_model // N_BRANCH
    o = [0]
    for wd in (3 * c, 64 * 2, 64 * 2, 128, c, 2 * c, c, c, c, c, 32, c, 2 * c, N_BRANCH * d_model):
        o.append(o[-1] + wd)
    seg = lambda k: w[:, o[k]:o[k + 1]]
    pad = jnp.zeros((w.shape[0], 512 - 128 * 3 - 32), w.dtype)
    parts = [seg(13), seg(0), seg(5), seg(12), seg(4), seg(6), seg(7), seg(8), seg(9), seg(11),
             seg(1), seg(2), seg(3), seg(10), pad]
    out = jnp.concatenate(parts, axis=1).astype(BF16)
    assert out.shape[1] == N_GATE + N_MIX
    return out


def kernel(x, c, ctx, c_ctx, w_ada, b_ada, g_norm1, g_norm2, g_final, w_in, rw_conv, rw_w0, rw_w2,
           rw_a0, rw_a2, rw_g2, rw_kk, rw_ka, rw_rk, rw_ln_w, rw_ln_b, hg_gamma, hg_norm, gla_gw,
           gla_gb, gla_norm, sgu_ln_w, sgu_ln_b, sgu_w, sgu_b, w_branch, w_out, w_mlp1, w_mlp2):
    n_batch, seq, d_model = x.shape
    ctx_len = ctx.shape[1]
    depth = w_in.shape[0]
    cw = d_model // N_BRANCH
    assert cw == 512 and d_model == 2048, "column layout constants assume D_MODEL = 2048"
    m_ctx = n_batch * ctx_len

    tm = _pick((1024, 512, 256, 128), m_ctx, seq)
    tb = _pick((256, 128), ctx_len, seq)
    nctx, bpb = m_ctx // tm, seq // tm
    ctxb, latb = ctx_len // tb, seq // tb
    seqs = dict(tb=tb, n_batch=n_batch, ctxb=ctxb, latb=latb)

    x_parts = (ctx.reshape(m_ctx, d_model), x.reshape(n_batch * seq, d_model), 0)
    c_rows = jnp.concatenate([c_ctx[None, :], c, jnp.zeros((7 - n_batch, d_model), F32)], axis=0)
    mod_all = _ada(c_rows, w_ada, b_ada).reshape(depth, 8, 6, d_model)

    head_ones = (jnp.arange(cw)[:, None] // RW_HEAD == jnp.arange(cw)[None, :] // RW_HEAD).astype(BF16)
    w_branch_b, w_out_b = w_branch.astype(BF16), w_out.astype(BF16)
    w_mlp1_b, w_mlp2_b = w_mlp1.astype(BF16), w_mlp2.astype(BF16)
    row = lambda a: a.reshape(1, -1)

    for l in range(depth):
        last = l == depth - 1
        mod = mod_all[l]
        p_gate, p_all = _inproj(x_parts, mod, row(g_norm1[l]), _permute_w_in(w_in[l], d_model),
                                tm=tm, nctx=nctx, bpb=bpb)

        prm = dict(conv=rw_conv[l], w0=row(rw_w0[l]), w2=_blockdiag2(rw_w2[l]), a0=row(rw_a0[l]),
                   a2=_blockdiag2(rw_a2[l]), g2=rw_g2[l].astype(BF16), kk=row(rw_kk[l]),
                   ka=row(rw_ka[l]), rk=row(rw_rk[l]), e=head_ones)
        r, v, nkk, g, bonus, w, kka, km = _rw_prep(p_all, prm, tb=tb, nctx=m_ctx // tb,
                                                    ctx_bps=ctxb, lat_bps=latb)
        ys = _rw_scan(r, v, nkk, w, kka, km, **seqs)

        ob = _hgrn2(p_all, hg_gamma, layer=l, **seqs)

        gw = _blockdiag2(gla_gw[l])
        gb = row(gla_gb[l])
        oc = _gla(p_all, gw, gb, **seqs)

        bs = jnp.broadcast_to(sgu_b[l][:, :, None], sgu_w[l].shape)
        yd = _sgu(p_all, row(sgu_ln_w[l]), row(sgu_ln_b[l]), sgu_w[l].astype(BF16), bs,
                  rb=_pick((512, 256, 128), m_ctx, seq))

        row_off = nctx if last else 0
        vecs = (row(rw_ln_w[l]), row(rw_ln_b[l]), row(hg_norm[l]), row(gla_norm[l]))
        mm = _merge((ys[0], ys[1], g, bonus), ob, oc, yd, p_gate, p_all, vecs, head_ones,
                    w_branch_b, layer=l, tm=tm // 2, row_off=2 * row_off)
        x_mid = _outproj(mm, w_out_b, x_parts, mod, layer=l, tm=tm, row_off=row_off, nctx=nctx, bpb=bpb)
        x_all = _mlp(x_mid, mod, row(g_norm2[l]), w_mlp1_b, w_mlp2_b, row(g_final), layer=l,
                     tm=tm // 2, row_off=2 * row_off, nctx=2 * nctx, bpb=2 * bpb, final_norm=last)
        x_parts = (x_all, x_all, nctx)
    return x_all.reshape(n_batch, seq, d_model)
```

```python
import functools

import jax
import jax.numpy as jnp
from jax import lax
from jax.experimental import pallas as pl
from jax.experimental.pallas import tpu as pltpu

F32 = jnp.float32
BF16 = jnp.bfloat16

N_BRANCH = 4
RW_HEAD = 64
RW_LN_EPS = 64e-5
HG_DK = 128
GLA_HEADS = 4
GLA_GATE_NORM = 16.0
LA_CHUNK = 16
RW_CHUNK = 32
RW_SUB = 16
SGU_CHUNK = 128
SGU_GROUPS = 4
EPS = 1e-6

C_RKV = 0
C_HGF = 1536
C_SGU = 2560
C_HGQ = 3584
C_HGI = 4096
C_HGG = 4608
C_GLQK = 5120
C_GLV = 5632
C_GLG = 6144
C_SMALL = 6656
N_MIX = 7168
N_GATE = 8192

VMEM_LIMIT = 58 * 1024 * 1024


def _cparams(n_axes):
    return pltpu.CompilerParams(dimension_semantics=("arbitrary",) * n_axes,
                                vmem_limit_bytes=VMEM_LIMIT)


def _pick(n_list, *dims):
    for n in n_list:
        if all(d % n == 0 for d in dims):
            return n
    raise ValueError(f"no block size in {n_list} divides {dims}")


def _row_group(i, nctx, bpb):
    return jnp.where(i < nctx, 0, 1 + (i - nctx) // bpb)


def _dot(a, b):
    return jnp.dot(a, b, preferred_element_type=F32)


def _dot_nt(a, b):
    return lax.dot_general(a, b, (((1,), (1,)), ((), ())), preferred_element_type=F32)


def _dot_tn(a, b):
    return lax.dot_general(a, b, (((0,), (0,)), ((), ())), preferred_element_type=F32)


def _split_dot(x, e):
    hi = x.astype(BF16)
    lo = (x - hi.astype(F32)).astype(BF16)
    return _dot(hi, e) + _dot(lo, e)


def _split3_dot(e, x):
    p1 = x.astype(BF16)
    r1 = x - p1.astype(F32)
    p2 = r1.astype(BF16)
    p3 = (r1 - p2.astype(F32)).astype(BF16)
    return _dot(e, p1) + _dot(e, p2) + _dot(e, p3)


def _log_sigmoid(x):
    return jnp.minimum(x, 0.0) - jnp.log(1.0 + jnp.exp(-jnp.abs(x)))


def _sigmoid(x):
    return 0.5 * jnp.tanh(0.5 * x) + 0.5


def _ada_kernel(c_ref, w_ref, b_ref, o_ref):
    c = c_ref[...]
    act = c * _sigmoid(c)
    o_ref[0] = _dot(act.astype(BF16), w_ref[0].astype(BF16)) + b_ref[0]


def _ada(c_rows, w_ada, b_ada):
    n_layers, d, n = w_ada.shape
    tn = _pick((1024, 512, 256, 128), n)
    return pl.pallas_call(
        _ada_kernel,
        grid=(n_layers, n // tn),
        in_specs=[pl.BlockSpec((8, d), lambda l, j: (0, 0)),
                  pl.BlockSpec((1, d, tn), lambda l, j: (l, 0, j)),
                  pl.BlockSpec((1, 1, tn), lambda l, j: (l, 0, j))],
        out_specs=pl.BlockSpec((1, 8, tn), lambda l, j: (l, 0, j)),
        out_shape=jax.ShapeDtypeStruct((n_layers, 8, n), F32),
        compiler_params=_cparams(2),
        name="ada_mod",
    )(c_rows, w_ada, b_ada.reshape(n_layers, 1, n))


def _norm_mod(x, g, shift, scale):
    y = x * lax.rsqrt(jnp.mean(x * x, axis=-1, keepdims=True) + EPS) * g
    return y * (1.0 + scale) + shift


def _two_part_specs(block, nctx, lat_off):
    return [pl.BlockSpec(block, lambda i, j: (jnp.minimum(i, nctx - 1), 0), pipeline_mode=pl.Buffered(1)),
            pl.BlockSpec(block, lambda i, j: (lat_off + jnp.maximum(i - nctx, 0), 0))]


def _inproj_kernel(xc_ref, xl_ref, mod_ref, g_ref, w_ref, og_ref, om_ref, h_ref, *, nctx, n_gate_tiles):
    i, j = pl.program_id(0), pl.program_id(1)

    @pl.when(j == 0)
    def _():
        x = jnp.where(i < nctx, xc_ref[...], xl_ref[...])
        h = _norm_mod(x, g_ref[...], mod_ref[0, 0:1, :], mod_ref[0, 1:2, :])
        h_ref[...] = h.astype(BF16)

    @pl.when(j < n_gate_tiles)
    def _():
        og_ref[...] = _dot(h_ref[...], w_ref[...]).astype(BF16)

    @pl.when(j >= n_gate_tiles)
    def _():
        om_ref[...] = _dot(h_ref[...], w_ref[...])


def _inproj(x_parts, mod, g, w, *, tm, nctx, bpb):
    xc, xl, lat_off = x_parts
    d = xc.shape[1]
    m = (nctx + (xl.shape[0] // tm - lat_off)) * tm
    tn = _pick((1024, 512), N_GATE, N_MIX)
    ng = N_GATE // tn
    return pl.pallas_call(
        functools.partial(_inproj_kernel, nctx=nctx, n_gate_tiles=ng),
        grid=(m // tm, (N_GATE + N_MIX) // tn),
        in_specs=_two_part_specs((tm, d), nctx, lat_off) + [
            pl.BlockSpec((1, 6, d), lambda i, j: (_row_group(i, nctx, bpb), 0, 0)),
            pl.BlockSpec((1, d), lambda i, j: (0, 0)),
            pl.BlockSpec((d, tn), lambda i, j: (0, j))],
        out_specs=[pl.BlockSpec((tm, tn), lambda i, j: (i, jnp.minimum(j, ng - 1))),
                   pl.BlockSpec((tm, tn), lambda i, j: (i, jnp.maximum(j - ng, 0)))],
        out_shape=[jax.ShapeDtypeStruct((m, N_GATE), BF16), jax.ShapeDtypeStruct((m, N_MIX), F32)],
        scratch_shapes=[pltpu.VMEM((tm, d), BF16)],
        compiler_params=_cparams(2),
        name="inproj",
    )(xc, xl, mod, g, w)


def _rw_prep_kernel(rkv_ref, prev_ref, next_ref, small_ref, conv_ref, w0_ref, w2_ref, a0_ref,
                    a2_ref, g2_ref, kk_ref, ka_ref, rk_ref, e_ref,
                    r_out, v_out, nkk_out, g_out, bonus_out, w_out, kka_out, km_out,
                    *, tb, nctx, ctx_bps, lat_bps):
    i = pl.program_id(0)
    c = r_out.shape[-1]
    j = jnp.where(i < nctx, i, i - nctx)
    bps = jnp.where(i < nctx, ctx_bps, lat_bps)
    first = lax.rem(j, bps) == 0
    last = lax.rem(j, bps) == bps - 1

    blk = rkv_ref[...]
    rows = lax.broadcasted_iota(jnp.int32, (tb, 1), 0)
    prev_row = jnp.where(first, 0.0, prev_ref[7:8, :])
    next_row = jnp.where(last, 0.0, next_ref[0:1, :])
    xm1 = jnp.where(rows == 0, prev_row, pltpu.roll(blk, 1, 0))
    xp1 = jnp.where(rows == tb - 1, next_row, pltpu.roll(blk, tb - 1, 0))
    conv = conv_ref[0:1, :] * xm1 + conv_ref[1:2, :] * blk + conv_ref[2:3, :] * xp1
    r = conv[:, 0:c]
    k = conv[:, c:2 * c]
    v = conv[:, 2 * c:3 * c]

    small = small_ref[...]
    wl = small[:, 0:128]
    al = small[:, 128:256]
    gl = small[:, 256:384]
    w_pre = w0_ref[...] + _dot(jnp.tanh(wl).astype(BF16), w2_ref[...])
    softplus = jnp.maximum(-w_pre, 0.0) + jnp.log(1.0 + jnp.exp(-jnp.abs(w_pre)))
    log_decay = -jnp.exp(-softplus - 0.5)
    a = _sigmoid(a0_ref[...] + _dot(al.astype(BF16), a2_ref[...]))
    g = _dot(_sigmoid(gl).astype(BF16), g2_ref[...])

    e = e_ref[...]
    kkv = k * kk_ref[...]
    kk = kkv * lax.rsqrt(_split_dot(kkv * kkv, e) + 1e-12)
    bonus = _split_dot(r * k * rk_ref[...], e) * v

    r_out[...] = r.astype(BF16)
    v_out[...] = v.astype(BF16)
    nkk_out[...] = (-kk).astype(BF16)
    g_out[...] = g
    bonus_out[...] = bonus
    for d in range(2):
        a_d = a[:, d * c:(d + 1) * c]
        w_out[d] = log_decay[:, d * c:(d + 1) * c]
        kka_out[d] = (kk * a_d).astype(BF16)
        km_out[d] = (k * (1.0 + (a_d - 1.0) * ka_ref[...])).astype(BF16)


def _rw_prep(p_all, prm, *, tb, nctx, ctx_bps, lat_bps):
    m = p_all.shape[0]
    c = prm["kk"].shape[-1]
    nblk = m // tb
    t8 = tb // 8
    full = lambda shape: pl.BlockSpec(shape, lambda i: (0,) * len(shape))
    tok = pl.BlockSpec((tb, c), lambda i: (i, 0))
    tok2 = pl.BlockSpec((2, tb, c), lambda i: (0, i, 0))
    one, one_b = jax.ShapeDtypeStruct((m, c), F32), jax.ShapeDtypeStruct((m, c), BF16)
    two, two_b = jax.ShapeDtypeStruct((2, m, c), F32), jax.ShapeDtypeStruct((2, m, c), BF16)
    return pl.pallas_call(
        functools.partial(_rw_prep_kernel, tb=tb, nctx=nctx, ctx_bps=ctx_bps, lat_bps=lat_bps),
        grid=(nblk,),
        in_specs=[pl.BlockSpec((tb, 3 * c), lambda i: (i, C_RKV // (3 * c))),
                  pl.BlockSpec((8, 3 * c), lambda i: (jnp.maximum(i * t8 - 1, 0), 0)),
                  pl.BlockSpec((8, 3 * c), lambda i: (jnp.minimum((i + 1) * t8, m // 8 - 1), 0)),
                  pl.BlockSpec((tb, 512), lambda i: (i, C_SMALL // 512)),
                  full((3, 3 * c)), full((1, 2 * c)), full((128, 2 * c)), full((1, 2 * c)),
                  full((128, 2 * c)), full((128, c)), full((1, c)), full((1, c)), full((1, c)),
                  full((c, c))],
        out_specs=[tok, tok, tok, tok, tok, tok2, tok2, tok2],
        out_shape=[one_b, one_b, one_b, one, one, two, two_b, two_b],
        compiler_params=_cparams(1),
        name="rwkv_prep",
    )(p_all, p_all, p_all, p_all, prm["conv"], prm["w0"], prm["w2"], prm["a0"], prm["a2"],
      prm["g2"], prm["kk"], prm["ka"], prm["rk"], prm["e"])


class _RwUnit:
    def __init__(self, refs, scratch, *, tb, npair, reverse):
        (self.r_ref, self.v_ref, self.a_ref, self.lw_ref, self.b_ref, self.k_ref, self.tri_ref,
         self.y_ref) = refs
        (self.s_ref, self.c_scr, self.ag_scr, self.rg_scr, self.bg_scr, self.kg_scr) = scratch[:6]
        self.slots = (scratch[6:10], scratch[10:14])
        self.reverse = reverse
        self.nch = tb // RW_CHUNK
        self.pairs = range(npair)
        L, hd = RW_CHUNK, RW_HEAD
        self.lss = [slice(p * 2 * hd, (p + 1) * 2 * hd) for p in self.pairs]
        lane = lax.broadcasted_iota(jnp.int32, (1, 2 * hd), 1)
        self.lo = (lane < hd).astype(F32)
        self.hi = 1.0 - self.lo
        ti = lax.broadcasted_iota(jnp.int32, (2 * L, 2 * hd), 0)
        sl = lax.broadcasted_iota(jnp.int32, (2 * L, 2 * hd), 1)
        sj = sl & (L - 1)
        tt = ti & (L - 1)
        earlier = (sj > tt) if reverse else (sj < tt)
        assert 4 * L == 2 * hd and L == 2 * RW_SUB
        self.aa_mask = (earlier | ((ti >= L) & (sj == tt))).astype(F32)
        self.col_idx0 = jnp.where(lax.broadcasted_iota(jnp.int32, (RW_SUB, 2 * hd), 1) < hd, 0, L)
        self.blk_mask = ((lax.broadcasted_iota(jnp.int32, (2 * hd, 2 * hd), 0) >> 6)
                         == (lax.broadcasted_iota(jnp.int32, (2 * hd, 2 * hd), 1) >> 6)).astype(F32)
        self.zeros2l = jnp.zeros((2 * L, 2 * hd), F32)
        self.zeros_sub = jnp.zeros((RW_SUB, 2 * hd), F32)
        self.blocks = (1, 0) if reverse else (0, 1)
        self.last = 0 if reverse else L - 1
        self.order = range(L - 1, -1, -1) if reverse else range(L)

    def reset(self):
        self.s_ref[...] = jnp.zeros_like(self.s_ref)

    def prologue(self):
        lw = self.lw_ref[0]
        c = _split3_dot(self.tri_ref[...], lw)
        self.c_scr[...] = c
        enc = jnp.exp(-c)
        self.ag_scr[...] = self.a_ref[...] * jnp.exp(c - lw)
        self.rg_scr[...] = self.r_ref[...] * jnp.exp(c)
        self.bg_scr[...] = self.b_ref[0] * enc
        self.kg_scr[...] = self.k_ref[0] * enc

    def rows(self, ci):
        L = RW_CHUNK
        cc = (self.nch - 1 - ci) if self.reverse else ci
        return pl.ds(pl.multiple_of(cc * L, L), L)

    def halves(self, x):
        return [x * self.lo, x * self.hi]

    def lhs_of(self, rows):
        return [jnp.concatenate([self.ag_scr[rows, ls], self.rg_scr[rows, ls]], axis=0).astype(BF16)
                for ls in self.lss]

    def prep_aa(self, rows):
        lhs = self.lhs_of(rows)
        out = []
        for p in self.pairs:
            ls = self.lss[p]
            rhs = jnp.concatenate(self.halves(self.bg_scr[rows, ls]) + self.halves(self.kg_scr[rows, ls]),
                                  axis=0)
            out.append(_dot_nt(lhs[p], rhs.astype(BF16)))
        return out

    def prep_akv(self, rows, aa_raw, slot):
        _, akv_scr, aar_scr, na_scr = slot
        L = RW_CHUNK
        aa = [x * self.aa_mask for x in aa_raw]
        for p in self.pairs:
            v = self.v_ref[rows, self.lss[p]]
            vv = jnp.concatenate([self.zeros2l] + self.halves(v), axis=0).astype(BF16)
            akv_scr[p] = _dot(aa[p][0:L].astype(BF16), vv)
            aar_scr[p] = aa[p][L:2 * L]
            na_scr[p] = aa[p][0:L]
        return aa

    def prep_cols(self, aa, slot):
        half = RW_SUB // 2
        for p in self.pairs:
            for s in range(RW_CHUNK):
                r0 = (s // RW_SUB) * RW_SUB
                lo, hi = self.live_rows(s - r0)
                slot[0][p, s, lo:hi] = jnp.take_along_axis(aa[p][r0 + lo:r0 + hi],
                                                           self.col_idx0[0:hi - lo] + s, axis=1)

    def live_rows(self, s_local):
        half = RW_SUB // 2
        if self.reverse:
            return (0, half) if s_local <= half else (0, RW_SUB)
        return (half, RW_SUB) if s_local >= half - 1 else (0, RW_SUB)

    def adv_g(self, rows):
        lhs = self.lhs_of(rows)
        return [_dot_nt(lhs[p], self.s_ref[p].astype(BF16)) for p in self.pairs]

    def adv_solve(self, g, slot):
        L, sub = RW_CHUNK, RW_SUB
        col_scr, akv_scr, _, na_scr = slot
        first, second = self.blocks
        ub = {}
        for blk in self.blocks:
            r0 = blk * sub
            ub[blk] = [g[p][r0:r0 + sub] + akv_scr[p, r0:r0 + sub] for p in self.pairs]
            if blk == second:
                for p in self.pairs:
                    done = ub[first][p]
                    z = self.zeros_sub
                    parts = [z, z, z, z]
                    parts[first], parts[2 + first] = done * self.lo, done * self.hi
                    stack = jnp.concatenate(parts + [self.zeros2l], axis=0).astype(BF16)
                    ub[blk][p] = ub[blk][p] + _dot(na_scr[p, r0:r0 + sub].astype(BF16), stack)
            steps = [s for s in self.order if s // sub == blk]
            half = sub // 2
            top = [x[0:half] for x in ub[blk]]
            bot = [x[half:] for x in ub[blk]]
            for s in steps:
                sl = s - r0
                lo, hi = self.live_rows(sl)
                for p in self.pairs:
                    row = top[p][sl:sl + 1] if sl < half else bot[p][sl - half:sl - half + 1]
                    if lo == 0:
                        top[p] = top[p] + col_scr[p, s, 0:half] * row
                    if hi == sub:
                        bot[p] = bot[p] + col_scr[p, s, half:sub] * row
            ub[blk] = [jnp.concatenate([top[p], bot[p]], axis=0) for p in self.pairs]
        return [jnp.concatenate([ub[0][p], ub[1][p]], axis=0) for p in self.pairs]

    def adv_out(self, rows, g, u, slot):
        L = RW_CHUNK
        for p in self.pairs:
            v = self.v_ref[rows, self.lss[p]]
            uv = jnp.concatenate(self.halves(u[p]) + self.halves(v), axis=0).astype(BF16)
            self.y_ref[rows, self.lss[p]] = (g[p][L:2 * L] + _dot(slot[2][p].astype(BF16), uv)).astype(BF16)
        for p in self.pairs:
            ls = self.lss[p]
            cch = self.c_scr[rows, ls]
            cl = cch[self.last:self.last + 1]
            dec = jnp.exp(cl - cch)
            bk = jnp.concatenate([self.b_ref[0, rows, ls] * dec, self.k_ref[0, rows, ls] * dec], axis=0)
            upd = _dot_tn(jnp.concatenate([u[p], self.v_ref[rows, ls]], axis=0).astype(BF16),
                          bk.astype(BF16))
            self.s_ref[p] = self.s_ref[p] * jnp.exp(cl) + upd * self.blk_mask


N_RW_REFS = 8
N_RW_SCRATCH = 14


def _rw_scan_kernel(*refs, tb, npair):
    n_in = N_RW_REFS - 1
    units = []
    for d in range(2):
        ins = refs[d * n_in:(d + 1) * n_in]
        out = refs[2 * n_in + d]
        scr = refs[2 * n_in + 2 + d * N_RW_SCRATCH:2 * n_in + 2 + (d + 1) * N_RW_SCRATCH]
        units.append(_RwUnit(tuple(ins) + (out,), scr, tb=tb, npair=npair, reverse=d == 1))
    nch = tb // RW_CHUNK

    @pl.when(pl.program_id(1) == 0)
    def _():
        for un in units:
            un.reset()

    for un in units:
        un.prologue()

    def step(ci_adv, sa, ci_prep, sp):
        rows_a = [un.rows(ci_adv) for un in units]
        rows_p = [un.rows(ci_prep) for un in units]
        aa_raw = [un.prep_aa(rp) for un, rp in zip(units, rows_p)]
        g = [un.adv_g(ra) for un, ra in zip(units, rows_a)]
        aa = [un.prep_akv(rp, x, un.slots[sp]) for un, rp, x in zip(units, rows_p, aa_raw)]
        u = [un.adv_solve(x, un.slots[sa]) for un, x in zip(units, g)]
        for un, x in zip(units, aa):
            un.prep_cols(x, un.slots[sp])
        for un, ra, x, y in zip(units, rows_a, g, u):
            un.adv_out(ra, x, y, un.slots[sa])

    for un in units:
        r0 = un.rows(0)
        un.prep_cols(un.prep_akv(r0, un.prep_aa(r0), un.slots[0]), un.slots[0])

    def two_chunks(j, carry):
        c0 = 2 * j
        step(c0, 0, c0 + 1, 1)
        step(c0 + 1, 1, jnp.minimum(c0 + 2, nch - 1), 0)
        return carry

    lax.fori_loop(0, nch // 2, two_chunks, 0)


def _seq_block(d, b, i, *, n_batch, ctxb, latb):
    is_ctx = i < ctxb
    cs = jnp.where(d == 0, i, ctxb - 1 - i)
    lj = jnp.where(d == 0, i - ctxb, latb - 1 - (i - ctxb))
    return jnp.where(is_ctx, b * ctxb + cs, n_batch * ctxb + b * latb + lj)


def _rw_scan(r, v, nkk, lw, kka, km, *, tb, n_batch, ctxb, latb):
    m, c = r.shape
    npair = c // (2 * RW_HEAD)
    lanes = 2 * RW_HEAD
    in_specs, args = [], []
    for d in range(2):
        blk = _la_index(d, n_batch, ctxb, latb)
        tok = pl.BlockSpec((tb, c), lambda b, i, blk=blk: (blk(b, i), 0))
        tokd = pl.BlockSpec((1, tb, c), lambda b, i, blk=blk, d=d: (d, blk(b, i), 0))
        in_specs += [tok, tok, tok, tokd, tokd, tokd, pl.BlockSpec((tb, tb), lambda b, i: (0, 0))]
        args += [r, v, nkk, lw, kka, km, _chunk_tri(tb, d == 1, RW_CHUNK)]
    out_specs = [pl.BlockSpec((tb, c), lambda b, i, blk=_la_index(d, n_batch, ctxb, latb): (blk(b, i), 0))
                 for d in range(2)]
    buf = pltpu.VMEM((tb, c), F32)
    unit_scratch = [pltpu.VMEM((npair, lanes, lanes), F32), buf, buf, buf, buf, buf] + 2 * [
        pltpu.VMEM((npair, RW_CHUNK, RW_SUB, lanes), F32),
        pltpu.VMEM((npair, RW_CHUNK, lanes), F32),
        pltpu.VMEM((npair, RW_CHUNK, lanes), F32),
        pltpu.VMEM((npair, RW_CHUNK, lanes), F32)]
    assert len(unit_scratch) == N_RW_SCRATCH
    return pl.pallas_call(
        functools.partial(_rw_scan_kernel, tb=tb, npair=npair),
        grid=(n_batch, ctxb + latb),
        in_specs=in_specs,
        out_specs=out_specs,
        out_shape=[jax.ShapeDtypeStruct((m, c), BF16)] * 2,
        scratch_shapes=unit_scratch + unit_scratch,
        compiler_params=_cparams(2),
        name="rwkv_scan",
    )(*args)


def _rw_post(y, g, bonus, lnw, lnb, e):
    inv = 1.0 / RW_HEAD
    yc = y - _split_dot(y, e) * inv
    var = _split_dot(yc * yc, e) * inv
    yn = yc * lax.rsqrt(var + RW_LN_EPS)
    return (yn * lnw + lnb + bonus) * g


class _LaUnit:
    def __init__(self, q_ref, k_ref, v_ref, b_ref, o_ref, s_ref, *, tb, nh, dk, dv, reverse, q_scale):
        self.q_ref, self.k_ref, self.v_ref, self.b_ref, self.o_ref, self.s_ref = (
            q_ref, k_ref, v_ref, b_ref, o_ref, s_ref)
        self.nch = tb // LA_CHUNK
        self.dk, self.dv, self.reverse, self.q_scale = dk, dv, reverse, q_scale
        self.lanes = 128
        self.pack = self.lanes // dk
        self.groups = range(nh // self.pack)
        self.rowi = lax.broadcasted_iota(jnp.int32, (LA_CHUNK, 1), 0)
        lane = lax.broadcasted_iota(jnp.int32, (1, self.lanes), 1)
        self.head_lanes = [(lane // dk == j).astype(F32) for j in range(self.pack)]
        if self.pack > 1:
            rows = lax.broadcasted_iota(jnp.int32, (self.pack * dv, self.lanes), 0)
            cols = lax.broadcasted_iota(jnp.int32, (self.pack * dv, self.lanes), 1)
            self.blk_mask = ((rows // dv) == (cols // dk)).astype(F32)

    def reset(self):
        self.s_ref[...] = jnp.zeros_like(self.s_ref)

    def load(self, ci):
        cc = (self.nch - 1 - ci) if self.reverse else ci
        rows = pl.ds(pl.multiple_of(cc * LA_CHUNK, LA_CHUNK), LA_CHUNK)
        pack, dv, lanes = self.pack, self.dv, self.lanes
        out = []
        for g in self.groups:
            ks = slice(g * lanes, (g + 1) * lanes)
            q = self.q_ref[rows, ks] * self.q_scale
            k = self.k_ref[rows, ks]
            v = self.v_ref[rows, g * pack * dv:(g + 1) * pack * dv]
            b = self.b_ref[rows, ks]
            o_inter = _dot_nt((q * jnp.exp(b)).astype(BF16), self.s_ref[g].astype(BF16))
            out.append((rows, q, k, v, b, [o_inter[:, j * dv:(j + 1) * dv] for j in range(pack)]))
        return out

    def intra(self, ops):
        pack, dv, half = self.pack, self.dv, LA_CHUNK // 2
        rowi = self.rowi[0:half]
        for g in self.groups:
            rows, q, k, v, b, o = ops[g]
            qh = [q[0:half], q[half:]]
            bh = [b[0:half], b[half:]]
            oh = [[oj[0:half], oj[half:]] for oj in o]
            for s in range(LA_CHUNK):
                for h in range(2):
                    lo_row, hi_row = h * half, (h + 1) * half - 1
                    if (hi_row > s) if self.reverse else (lo_row < s):
                        if (lo_row > s) if self.reverse else (hi_row < s):
                            continue
                        valid = (rowi + lo_row <= s) if self.reverse else (rowi + lo_row >= s)
                    else:
                        valid = None
                    term = (qh[h] * k[s:s + 1]) * jnp.exp(bh[h] - b[s:s + 1])
                    for j in range(pack):
                        tj = term if pack == 1 else term * self.head_lanes[j]
                        col = jnp.sum(tj, axis=-1, keepdims=True)
                        if valid is not None:
                            col = jnp.where(valid, col, 0.0)
                        oh[j][h] = oh[j][h] + col * v[s:s + 1, j * dv:(j + 1) * dv]
            for j in range(pack):
                self.o_ref[rows, (g * pack + j) * dv:(g * pack + j + 1) * dv] = jnp.concatenate(
                    oh[j], axis=0).astype(BF16)

    def update(self, ops):
        for g in self.groups:
            _, _, k, v, b, _ = ops[g]
            b_last = b[0:1] if self.reverse else b[LA_CHUNK - 1:LA_CHUNK]
            upd = _dot_tn(v.astype(BF16), (k * jnp.exp(b_last - b)).astype(BF16))
            if self.pack > 1:
                upd = upd * self.blk_mask
            self.s_ref[g] = self.s_ref[g] * jnp.exp(b_last) + upd


def _la_run(units, nch):
    def chunk(ci, carry):
        ops = [un.load(ci) for un in units]
        for un, x in zip(units, ops):
            un.update(x)
        for un, x in zip(units, ops):
            un.intra(x)
        return carry

    lax.fori_loop(0, nch, chunk, 0)


def _hg_kernel(qf_ref, ff_ref, vf_ref, trif_ref, qb_ref, fb_ref, vb_ref, trib_ref, gamma_ref,
               of_ref, ob_ref, sf_ref, kf_scr, bf_scr, sb_ref, kb_scr, bb_scr, *, layer, tb, nh, dk, dv):
    gam = gamma_ref[...]
    ex = jnp.exp(gam - jnp.max(gam, axis=0, keepdims=True))
    p = ex / jnp.sum(ex, axis=0, keepdims=True)
    cum = p[0:1]
    for i in range(1, layer + 1):
        cum = cum + p[i:i + 1]
    lb = cum - p[0:1]
    lo = jnp.log(lb)
    l1 = jnp.log(1.0 - lb)

    units = []
    for d, (q_ref, f_ref, v_ref, tri_ref, o_ref, s_ref, k_scr, b_scr) in enumerate((
            (qf_ref, ff_ref, vf_ref, trif_ref, of_ref, sf_ref, kf_scr, bf_scr),
            (qb_ref, fb_ref, vb_ref, trib_ref, ob_ref, sb_ref, kb_scr, bb_scr))):
        hi = l1 + _log_sigmoid(f_ref[...])
        mx = jnp.maximum(lo, hi)
        mn = jnp.minimum(lo, hi)
        log_f = mx + jnp.log(1.0 + jnp.exp(mn - mx))
        k_scr[...] = 1.0 - jnp.exp(log_f)
        b_scr[...] = _split3_dot(tri_ref[...], log_f)
        units.append(_LaUnit(q_ref, k_scr, v_ref, b_scr, o_ref, s_ref, tb=tb, nh=nh, dk=dk, dv=dv,
                             reverse=d == 1, q_scale=1.0))

    @pl.when(pl.program_id(1) == 0)
    def _():
        for un in units:
            un.reset()

    _la_run(units, tb // LA_CHUNK)


def _gla_kernel(qkf_ref, vf_ref, smallf_ref, trif_ref, qkb_ref, vb_ref, smallb_ref, trib_ref,
                gw_ref, gb_ref, of_ref, ob_ref, sf_ref, bf_scr, sb_ref, bb_scr, *, tb, nh, dk, dv):
    hk = nh * dk
    units = []
    for d, (qk_ref, v_ref, small_ref, tri_ref, o_ref, s_ref, b_scr) in enumerate((
            (qkf_ref, vf_ref, smallf_ref, trif_ref, of_ref, sf_ref, bf_scr),
            (qkb_ref, vb_ref, smallb_ref, trib_ref, ob_ref, sb_ref, bb_scr))):
        code = small_ref[:, 384:512]
        pre = _dot(code.astype(BF16), gw_ref[:, d * hk:(d + 1) * hk]) + gb_ref[:, d * hk:(d + 1) * hk]
        log_g = _log_sigmoid(pre) * (1.0 / GLA_GATE_NORM)
        b_scr[...] = _split3_dot(tri_ref[...], log_g)
        units.append(_LaUnit(qk_ref.at[:, 0:hk], qk_ref.at[:, hk:2 * hk], v_ref, b_scr, o_ref, s_ref,
                             tb=tb, nh=nh, dk=dk, dv=dv, reverse=d == 1, q_scale=dk ** -0.5))

    @pl.when(pl.program_id(1) == 0)
    def _():
        for un in units:
            un.reset()

    _la_run(units, tb // LA_CHUNK)


def _chunk_tri(tb, reverse, chunk=LA_CHUNK):
    t = jnp.arange(tb)
    same = (t[:, None] // chunk) == (t[None, :] // chunk)
    tri = (t[None, :] >= t[:, None]) if reverse else (t[None, :] <= t[:, None])
    return (same & tri).astype(BF16)


def _la_index(d, n_batch, ctxb, latb):
    return lambda b, i: _seq_block(d, b, i, n_batch=n_batch, ctxb=ctxb, latb=latb)


def _hgrn2(p_all, gamma, *, layer, tb, n_batch, ctxb, latb):
    m = p_all.shape[0]
    c = gamma.shape[-1]
    nh = c // HG_DK
    in_specs, args, out_specs = [], [], []
    for d in range(2):
        blk = _la_index(d, n_batch, ctxb, latb)
        col = lambda off, blk=blk: pl.BlockSpec((tb, c), lambda b, i: (blk(b, i), off // c))
        in_specs += [col(C_HGQ), col(C_HGF + d * c), col(C_HGI), pl.BlockSpec((tb, tb), lambda b, i: (0, 0))]
        args += [p_all, p_all, p_all, _chunk_tri(tb, d == 1)]
        out_specs.append(col(0))
    unit_scratch = [pltpu.VMEM((nh, HG_DK, HG_DK), F32), pltpu.VMEM((tb, c), F32), pltpu.VMEM((tb, c), F32)]
    return pl.pallas_call(
        functools.partial(_hg_kernel, layer=layer, tb=tb, nh=nh, dk=HG_DK, dv=HG_DK),
        grid=(n_batch, ctxb + latb),
        in_specs=in_specs + [pl.BlockSpec(gamma.shape, lambda b, i: (0, 0))],
        out_specs=out_specs,
        out_shape=[jax.ShapeDtypeStruct((m, c), BF16)] * 2,
        scratch_shapes=unit_scratch + unit_scratch,
        compiler_params=_cparams(2),
        name="hgrn2_scan",
    )(*args, gamma)


def _gla(p_all, gw, gb, *, tb, n_batch, ctxb, latb):
    m = p_all.shape[0]
    hk = gw.shape[-1] // 2
    dk = hk // GLA_HEADS
    c = 2 * hk
    dv = c // GLA_HEADS
    in_specs, args, out_specs = [], [], []
    for d in range(2):
        blk = _la_index(d, n_batch, ctxb, latb)
        col = lambda off, blk=blk: pl.BlockSpec((tb, c), lambda b, i: (blk(b, i), off // c))
        in_specs += [col(C_GLQK), col(C_GLV), col(C_SMALL), pl.BlockSpec((tb, tb), lambda b, i: (0, 0))]
        args += [p_all, p_all, p_all, _chunk_tri(tb, d == 1)]
        out_specs.append(col(0))
    unit_scratch = [pltpu.VMEM((hk // 128, (128 // dk) * dv, 128), F32), pltpu.VMEM((tb, hk), F32)]
    return pl.pallas_call(
        functools.partial(_gla_kernel, tb=tb, nh=GLA_HEADS, dk=dk, dv=dv),
        grid=(n_batch, ctxb + latb),
        in_specs=in_specs + [pl.BlockSpec(gw.shape, lambda b, i: (0, 0)),
                             pl.BlockSpec(gb.shape, lambda b, i: (0, 0))],
        out_specs=out_specs,
        out_shape=[jax.ShapeDtypeStruct((m, c), BF16)] * 2,
        scratch_shapes=unit_scratch + unit_scratch,
        compiler_params=_cparams(2),
        name="gla_scan",
    )(*args, gw, gb)


def _la_post(o, gate, norm_g, nh):
    dv = o.shape[-1] // nh
    outs = []
    for h in range(nh):
        oh = o[:, h * dv:(h + 1) * dv]
        outs.append(oh * lax.rsqrt(jnp.mean(oh * oh, axis=-1, keepdims=True) + EPS))
    return jnp.concatenate(outs, axis=-1) * norm_g * (gate * _sigmoid(gate))


def _gelu(x):
    return 0.5 * x * (1.0 + jnp.tanh(0.7978845608028654 * (x + 0.044715 * (x * x * x))))


def _sgu_kernel(u_ref, v_ref, lnw_ref, lnb_ref, ws_ref, bs_ref, o_ref, *, rb):
    u = _gelu(u_ref[...])
    v = _gelu(v_ref[...])
    vc = v - jnp.mean(v, axis=-1, keepdims=True)
    vn = vc * lax.rsqrt(jnp.mean(vc * vc, axis=-1, keepdims=True) + EPS)
    vn = (vn * lnw_ref[...] + lnb_ref[...]).astype(BF16)
    gw = vn.shape[-1] // SGU_GROUPS
    for n in range(rb // SGU_CHUNK):
        rs = slice(n * SGU_CHUNK, (n + 1) * SGU_CHUNK)
        for g in range(SGU_GROUPS):
            cs = slice(g * gw, (g + 1) * gw)
            s = _dot(ws_ref[g], vn[rs, cs]) + bs_ref[g]
            o_ref[rs, cs] = u[rs, cs] * s


def _sgu(p_all, lnw, lnb, ws, bs, *, rb):
    m = p_all.shape[0]
    c = lnw.shape[-1]
    return pl.pallas_call(
        functools.partial(_sgu_kernel, rb=rb),
        grid=(m // rb,),
        in_specs=[pl.BlockSpec((rb, c), lambda i: (i, C_SGU // c)),
                  pl.BlockSpec((rb, c), lambda i: (i, C_SGU // c + 1)),
                  pl.BlockSpec((1, c), lambda i: (0, 0)),
                  pl.BlockSpec((1, c), lambda i: (0, 0)),
                  pl.BlockSpec(ws.shape, lambda i: (0, 0, 0)),
                  pl.BlockSpec(bs.shape, lambda i: (0, 0, 0))],
        out_specs=pl.BlockSpec((rb, c), lambda i: (i, 0)),
        out_shape=jax.ShapeDtypeStruct((m, c), F32),
        compiler_params=_cparams(1),
        name="sgu",
    )(p_all, p_all, lnw, lnb, ws, bs)


def _merge_kernel(raf_ref, rab_ref, rg_ref, rbonus_ref, hf_ref, hb_ref, hgate_ref, gf_ref, gb_ref,
                  ggate_ref, yd_ref, lnw_ref, lnb_ref, e_ref, hnorm_ref, gnorm_ref,
                  g0_ref, g1_ref, g2_ref, g3_ref, w_ref, o_ref, y_scr, *, hg_heads):
    @pl.when(pl.program_id(1) == 0)
    def _():
        both = lambda f_ref, b_ref: f_ref[...].astype(F32) + b_ref[...].astype(F32)
        y_scr[0] = _rw_post(both(raf_ref, rab_ref), rg_ref[...], rbonus_ref[...], lnw_ref[...],
                            lnb_ref[...], e_ref[...]).astype(BF16)
        y_scr[1] = _la_post(both(hf_ref, hb_ref), hgate_ref[...], hnorm_ref[...], hg_heads).astype(BF16)
        y_scr[2] = _la_post(both(gf_ref, gb_ref), ggate_ref[...], gnorm_ref[...], GLA_HEADS).astype(BF16)
        y_scr[3] = yd_ref[...].astype(BF16)

    gs = (g0_ref, g1_ref, g2_ref, g3_ref)
    acc = None
    for j in range(N_BRANCH):
        t = _sigmoid(gs[j][...].astype(F32)) * _dot(y_scr[j], w_ref[j])
        acc = t if acc is None else acc + t
    o_ref[...] = acc.astype(BF16)


def _merge(rw, hg, gla, yd, p_gate, p_mix, vecs, e, w_branch, *, layer, tm, row_off):
    m = p_mix.shape[0]
    _, _, c, d = w_branch.shape
    tn = _pick((1024, 512), d)
    mo = m - row_off * tm
    ytok = pl.BlockSpec((tm, c), lambda i, j: (i + row_off, 0))
    mix = lambda off: pl.BlockSpec((tm, c), lambda i, j: (i + row_off, off // c))
    gate = lambda b: pl.BlockSpec((tm, tn), lambda i, j: (i + row_off, b * d // tn + j))
    vec = pl.BlockSpec((1, c), lambda i, j: (0, 0))
    return pl.pallas_call(
        functools.partial(_merge_kernel, hg_heads=c // HG_DK),
        grid=(mo // tm, d // tn),
        in_specs=[ytok, ytok, ytok, ytok, ytok, ytok, mix(C_HGG), ytok, ytok, mix(C_GLG), ytok,
                  vec, vec, pl.BlockSpec((c, c), lambda i, j: (0, 0)), vec, vec,
                  gate(0), gate(1), gate(2), gate(3),
                  pl.BlockSpec((None, N_BRANCH, c, tn), lambda i, j: (layer, 0, 0, j))],
        out_specs=pl.BlockSpec((tm, tn), lambda i, j: (i, j)),
        out_shape=jax.ShapeDtypeStruct((mo, d), BF16),
        scratch_shapes=[pltpu.VMEM((N_BRANCH, tm, c), BF16)],
        compiler_params=_cparams(2),
        name="merge",
    )(*rw, hg[0], hg[1], p_mix, gla[0], gla[1], p_mix, yd, *vecs[:2], e, *vecs[2:],
      p_gate, p_gate, p_gate, p_gate, w_branch)


def _outproj_kernel(m_ref, w_ref, xc_ref, xl_ref, mod_ref, o_ref, *, nctx, row_off):
    x = jnp.where(pl.program_id(0) + row_off < nctx, xc_ref[...], xl_ref[...])
    o_ref[...] = x + mod_ref[0, 2:3, :] * _dot(m_ref[...], w_ref[...])


def _outproj(mm, w_out, x_parts, mod, *, layer, tm, row_off, nctx, bpb):
    xc, xl, lat_off = x_parts
    mo, d = mm.shape
    tn = _pick((1024, 512), d)
    xspecs = [pl.BlockSpec((tm, tn), lambda i, j: (jnp.minimum(i + row_off, nctx - 1),
                                                   jnp.where(i + row_off < nctx, j, 0))),
              pl.BlockSpec((tm, tn), lambda i, j: (lat_off + jnp.maximum(i + row_off - nctx, 0),
                                                   jnp.where(i + row_off < nctx, 0, j)))]
    return pl.pallas_call(
        functools.partial(_outproj_kernel, nctx=nctx, row_off=row_off),
        grid=(mo // tm, d // tn),
        in_specs=[pl.BlockSpec((tm, d), lambda i, j: (i, 0)),
                  pl.BlockSpec((None, d, tn), lambda i, j: (layer, 0, j))] + xspecs + [
                  pl.BlockSpec((1, 6, tn), lambda i, j: (_row_group(i + row_off, nctx, bpb), 0, j))],
        out_specs=pl.BlockSpec((tm, tn), lambda i, j: (i, j)),
        out_shape=jax.ShapeDtypeStruct((mo, d), F32),
        compiler_params=_cparams(2),
        name="outproj",
    )(mm, w_out, xc, xl, mod)


def _mlp_kernel(x_ref, mod_ref, g_ref, w1_ref, w2_ref, gf_ref, o_ref, h_ref, acc_ref, *, final_norm):
    j = pl.program_id(1)

    @pl.when(j == 0)
    def _():
        h = _norm_mod(x_ref[...], g_ref[...], mod_ref[0, 3:4, :], mod_ref[0, 4:5, :])
        h_ref[...] = h.astype(BF16)
        acc_ref[...] = jnp.zeros_like(acc_ref)

    a = jnp.maximum(_dot(h_ref[...], w1_ref[...]), 0.0)
    acc_ref[...] += _dot((a * a).astype(BF16), w2_ref[...])

    @pl.when(j == pl.num_programs(1) - 1)
    def _():
        y = x_ref[...] + mod_ref[0, 5:6, :] * acc_ref[...]
        if final_norm:
            y = y * lax.rsqrt(jnp.mean(y * y, axis=-1, keepdims=True) + EPS) * gf_ref[...]
        o_ref[...] = y


def _mlp(x_in, mod, g, w1, w2, g_final, *, layer, tm, row_off, nctx, bpb, final_norm):
    mo, d = x_in.shape
    hid = w1.shape[-1]
    th = _pick((1024, 512, 256, 128), hid)
    return pl.pallas_call(
        functools.partial(_mlp_kernel, final_norm=final_norm),
        grid=(mo // tm, hid // th),
        in_specs=[pl.BlockSpec((tm, d), lambda i, j: (i, 0)),
                  pl.BlockSpec((1, 6, d), lambda i, j: (_row_group(i + row_off, nctx, bpb), 0, 0)),
                  pl.BlockSpec((1, d), lambda i, j: (0, 0)),
                  pl.BlockSpec((None, d, th), lambda i, j: (layer, 0, j)),
                  pl.BlockSpec((None, th, d), lambda i, j: (layer, j, 0)),
                  pl.BlockSpec((1, d), lambda i, j: (0, 0))],
        out_specs=pl.BlockSpec((tm, d), lambda i, j: (i, 0)),
        out_shape=jax.ShapeDtypeStruct((mo, d), F32),
        scratch_shapes=[pltpu.VMEM((tm, d), BF16), pltpu.VMEM((tm, d), F32)],
        compiler_params=_cparams(2),
        name="mlp",
    )(x_in, mod, g, w1, w2, g_final)


def _blockdiag2(w):
    _, r, c = w.shape
    z = jnp.zeros((r, c), w.dtype)
    out = jnp.concatenate([jnp.concatenate([w[0], z], axis=1), jnp.concatenate([z, w[1]], axis=1)], axis=0)
    return jnp.pad(out, ((0, 128 - 2 * r), (0, 0))).astype(BF16)


def _permute_w_in(w, d_model):
    c = d_model // N_BRANCH
    o = [0]
    for wd in (3 * c, 64 * 2, 64 * 2, 128, c, 2 * c, c, c, c, c, 32, c, 2 * c, N_BRANCH * d_model):
        o.append(o[-1] + wd)
    seg = lambda k: w[:, o[k]:o[k + 1]]
    pad = jnp.zeros((w.shape[0], 512 - 128 * 3 - 32), w.dtype)
    parts = [seg(13), seg(0), seg(5), seg(12), seg(4), seg(6), seg(7), seg(8), seg(9), seg(11),
             seg(1), seg(2), seg(3), seg(10), pad]
    out = jnp.concatenate(parts, axis=1).astype(BF16)
    assert out.shape[1] == N_GATE + N_MIX
    return out


def kernel(x, c, ctx, c_ctx, w_ada, b_ada, g_norm1, g_norm2, g_final, w_in, rw_conv, rw_w0, rw_w2,
           rw_a0, rw_a2, rw_g2, rw_kk, rw_ka, rw_rk, rw_ln_w, rw_ln_b, hg_gamma, hg_norm, gla_gw,
           gla_gb, gla_norm, sgu_ln_w, sgu_ln_b, sgu_w, sgu_b, w_branch, w_out, w_mlp1, w_mlp2):
    n_batch, seq, d_model = x.shape
    ctx_len = ctx.shape[1]
    depth = w_in.shape[0]
    cw = d_model // N_BRANCH
    assert cw == 512 and d_model == 2048, "column layout constants assume D_MODEL = 2048"
    m_ctx = n_batch * ctx_len

    tm = _pick((1024, 512, 256, 128), m_ctx, seq)
    tb = _pick((256, 128), ctx_len, seq)
    nctx, bpb = m_ctx // tm, seq // tm
    ctxb, latb = ctx_len // tb, seq // tb
    seqs = dict(tb=tb, n_batch=n_batch, ctxb=ctxb, latb=latb)

    x_parts = (ctx.reshape(m_ctx, d_model), x.reshape(n_batch * seq, d_model), 0)
    c_rows = jnp.concatenate([c_ctx[None, :], c, jnp.zeros((7 - n_batch, d_model), F32)], axis=0)
    mod_all = _ada(c_rows, w_ada, b_ada).reshape(depth, 8, 6, d_model)

    head_ones = (jnp.arange(cw)[:, None] // RW_HEAD == jnp.arange(cw)[None, :] // RW_HEAD).astype(BF16)
    w_branch_b, w_out_b = w_branch.astype(BF16), w_out.astype(BF16)
    w_mlp1_b, w_mlp2_b = w_mlp1.astype(BF16), w_mlp2.astype(BF16)
    row = lambda a: a.reshape(1, -1)

    for l in range(depth):
        last = l == depth - 1
        mod = mod_all[l]
        p_gate, p_all = _inproj(x_parts, mod, row(g_norm1[l]), _permute_w_in(w_in[l], d_model),
                                tm=tm, nctx=nctx, bpb=bpb)

        prm = dict(conv=rw_conv[l], w0=row(rw_w0[l]), w2=_blockdiag2(rw_w2[l]), a0=row(rw_a0[l]),
                   a2=_blockdiag2(rw_a2[l]), g2=rw_g2[l].astype(BF16), kk=row(rw_kk[l]),
                   ka=row(rw_ka[l]), rk=row(rw_rk[l]), e=head_ones)
        r, v, nkk, g, bonus, w, kka, km = _rw_prep(p_all, prm, tb=tb, nctx=m_ctx // tb,
                                                    ctx_bps=ctxb, lat_bps=latb)
        ys = _rw_scan(r, v, nkk, w, kka, km, **seqs)

        ob = _hgrn2(p_all, hg_gamma, layer=l, **seqs)

        gw = _blockdiag2(gla_gw[l])
        gb = row(gla_gb[l])
        oc = _gla(p_all, gw, gb, **seqs)

        bs = jnp.broadcast_to(sgu_b[l][:, :, None], sgu_w[l].shape)
        yd = _sgu(p_all, row(sgu_ln_w[l]), row(sgu_ln_b[l]), sgu_w[l].astype(BF16), bs,
                  rb=_pick((512, 256, 128), m_ctx, seq))

        row_off = nctx if last else 0
        vecs = (row(rw_ln_w[l]), row(rw_ln_b[l]), row(hg_norm[l]), row(gla_norm[l]))
        mm = _merge((ys[0], ys[1], g, bonus), ob, oc, yd, p_gate, p_all, vecs, head_ones,
                    w_branch_b, layer=l, tm=tm // 2, row_off=2 * row_off)
        x_mid = _outproj(mm, w_out_b, x_parts, mod, layer=l, tm=tm, row_off=row_off, nctx=nctx, bpb=bpb)
        x_all = _mlp(x_mid, mod, row(g_norm2[l]), w_mlp1_b, w_mlp2_b, row(g_final), layer=l,
                     tm=tm // 2, row_off=2 * row_off, nctx=2 * nctx, bpb=2 * bpb, final_norm=last)
        x_parts = (x_all, x_all, nctx)
    return x_all.reshape(n_batch, seq, d_model)
```

```python
import functools

import jax
import jax.numpy as jnp
from jax import lax
from jax.experimental import pallas as pl
from jax.experimental.pallas import tpu as pltpu

F32 = jnp.float32
BF16 = jnp.bfloat16

N_BRANCH = 4
RW_HEAD = 64
RW_LN_EPS = 64e-5
HG_DK = 128
GLA_HEADS = 4
GLA_GATE_NORM = 16.0
LA_CHUNK = 16
RW_CHUNK = 32
RW_SUB = 16
SGU_CHUNK = 128
SGU_GROUPS = 4
EPS = 1e-6

C_RKV = 0
C_HGF = 1536
C_SGU = 2560
C_HGQ = 3584
C_HGI = 4096
C_HGG = 4608
C_GLQK = 5120
C_GLV = 5632
C_GLG = 6144
C_SMALL = 6656
N_MIX = 7168
N_GATE = 8192

VMEM_LIMIT = 58 * 1024 * 1024


def _cparams(n_axes):
    return pltpu.CompilerParams(dimension_semantics=("arbitrary",) * n_axes,
                                vmem_limit_bytes=VMEM_LIMIT)


def _pick(n_list, *dims):
    for n in n_list:
        if all(d % n == 0 for d in dims):
            return n
    raise ValueError(f"no block size in {n_list} divides {dims}")


def _row_group(i, nctx, bpb):
    return jnp.where(i < nctx, 0, 1 + (i - nctx) // bpb)


def _dot(a, b):
    return jnp.dot(a, b, preferred_element_type=F32)


def _dot_nt(a, b):
    return lax.dot_general(a, b, (((1,), (1,)), ((), ())), preferred_element_type=F32)


def _dot_tn(a, b):
    return lax.dot_general(a, b, (((0,), (0,)), ((), ())), preferred_element_type=F32)


def _split_dot(x, e):
    hi = x.astype(BF16)
    lo = (x - hi.astype(F32)).astype(BF16)
    return _dot(hi, e) + _dot(lo, e)


def _split3_dot(e, x):
    p1 = x.astype(BF16)
    r1 = x - p1.astype(F32)
    p2 = r1.astype(BF16)
    p3 = (r1 - p2.astype(F32)).astype(BF16)
    return _dot(e, p1) + _dot(e, p2) + _dot(e, p3)


def _log_sigmoid(x):
    return jnp.minimum(x, 0.0) - jnp.log(1.0 + jnp.exp(-jnp.abs(x)))


def _sigmoid(x):
    return 0.5 * jnp.tanh(0.5 * x) + 0.5


def _ada_kernel(c_ref, w_ref, b_ref, o_ref):
    c = c_ref[...]
    act = c * _sigmoid(c)
    part = _dot(act.astype(BF16), w_ref[0].astype(BF16))

    @pl.when(pl.program_id(1) == 0)
    def _():
        o_ref[0] = part + b_ref[0]

    @pl.when(pl.program_id(1) > 0)
    def _():
        o_ref[0] += part


def _ada(c_rows, w_ada, b_ada):
    n_layers, d, n = w_ada.shape
    tk = _pick((256, 128), d)
    return pl.pallas_call(
        _ada_kernel,
        grid=(n_layers, d // tk),
        in_specs=[pl.BlockSpec((8, tk), lambda l, k: (0, k)),
                  pl.BlockSpec((1, tk, n), lambda l, k: (l, k, 0)),
                  pl.BlockSpec((1, 1, n), lambda l, k: (l, 0, 0))],
        out_specs=pl.BlockSpec((1, 8, n), lambda l, k: (l, 0, 0)),
        out_shape=jax.ShapeDtypeStruct((n_layers, 8, n), F32),
        compiler_params=_cparams(2),
        name="ada_mod",
    )(c_rows, w_ada, b_ada.reshape(n_layers, 1, n))


def _norm_mod(x, g, shift, scale):
    y = x * lax.rsqrt(jnp.mean(x * x, axis=-1, keepdims=True) + EPS) * g
    return y * (1.0 + scale) + shift


def _two_part_specs(block, nctx, lat_off):
    return [pl.BlockSpec(block, lambda i, j: (jnp.minimum(i, nctx - 1), 0), pipeline_mode=pl.Buffered(1)),
            pl.BlockSpec(block, lambda i, j: (lat_off + jnp.maximum(i - nctx, 0), 0))]


def _inproj_kernel(xc_ref, xl_ref, mod_ref, g_ref, w_ref, og_ref, om_ref, h_ref, *, nctx, n_gate_tiles):
    i, j = pl.program_id(0), pl.program_id(1)

    @pl.when(j == 0)
    def _():
        x = jnp.where(i < nctx, xc_ref[...], xl_ref[...])
        h = _norm_mod(x, g_ref[...], mod_ref[0, 0:1, :], mod_ref[0, 1:2, :])
        h_ref[...] = h.astype(BF16)

    @pl.when(j < n_gate_tiles)
    def _():
        og_ref[...] = _dot(h_ref[...], w_ref[...]).astype(BF16)

    @pl.when(j >= n_gate_tiles)
    def _():
        om_ref[...] = _dot(h_ref[...], w_ref[...])


def _inproj(x_parts, mod, g, w, *, tm, nctx, bpb):
    xc, xl, lat_off = x_parts
    d = xc.shape[1]
    m = (nctx + (xl.shape[0] // tm - lat_off)) * tm
    tn = _pick((1024, 512), N_GATE, N_MIX)
    ng = N_GATE // tn
    return pl.pallas_call(
        functools.partial(_inproj_kernel, nctx=nctx, n_gate_tiles=ng),
        grid=(m // tm, (N_GATE + N_MIX) // tn),
        in_specs=_two_part_specs((tm, d), nctx, lat_off) + [
            pl.BlockSpec((1, 6, d), lambda i, j: (_row_group(i, nctx, bpb), 0, 0)),
            pl.BlockSpec((1, d), lambda i, j: (0, 0)),
            pl.BlockSpec((d, tn), lambda i, j: (0, j))],
        out_specs=[pl.BlockSpec((tm, tn), lambda i, j: (i, jnp.minimum(j, ng - 1))),
                   pl.BlockSpec((tm, tn), lambda i, j: (i, jnp.maximum(j - ng, 0)))],
        out_shape=[jax.ShapeDtypeStruct((m, N_GATE), BF16), jax.ShapeDtypeStruct((m, N_MIX), F32)],
        scratch_shapes=[pltpu.VMEM((tm, d), BF16)],
        compiler_params=_cparams(2),
        name="inproj",
    )(xc, xl, mod, g, w)


def _rw_prep_kernel(rkv_ref, prev_ref, next_ref, small_ref, conv_ref, w0_ref, w2_ref, a0_ref,
                    a2_ref, g2_ref, kk_ref, ka_ref, rk_ref, e_ref,
                    r_out, v_out, nkk_out, g_out, bonus_out, w_out, kka_out, km_out,
                    *, tb, nctx, ctx_bps, lat_bps):
    i = pl.program_id(0)
    c = r_out.shape[-1]
    j = jnp.where(i < nctx, i, i - nctx)
    bps = jnp.where(i < nctx, ctx_bps, lat_bps)
    first = lax.rem(j, bps) == 0
    last = lax.rem(j, bps) == bps - 1

    blk = rkv_ref[...]
    rows = lax.broadcasted_iota(jnp.int32, (tb, 1), 0)
    prev_row = jnp.where(first, 0.0, prev_ref[7:8, :])
    next_row = jnp.where(last, 0.0, next_ref[0:1, :])
    xm1 = jnp.where(rows == 0, prev_row, pltpu.roll(blk, 1, 0))
    xp1 = jnp.where(rows == tb - 1, next_row, pltpu.roll(blk, tb - 1, 0))
    conv = conv_ref[0:1, :] * xm1 + conv_ref[1:2, :] * blk + conv_ref[2:3, :] * xp1
    r = conv[:, 0:c]
    k = conv[:, c:2 * c]
    v = conv[:, 2 * c:3 * c]

    small = small_ref[...]
    wl = small[:, 0:128]
    al = small[:, 128:256]
    gl = small[:, 256:384]
    w_pre = w0_ref[...] + _dot(jnp.tanh(wl).astype(BF16), w2_ref[...])
    softplus = jnp.maximum(-w_pre, 0.0) + jnp.log(1.0 + jnp.exp(-jnp.abs(w_pre)))
    log_decay = -jnp.exp(-softplus - 0.5)
    a = _sigmoid(a0_ref[...] + _dot(al.astype(BF16), a2_ref[...]))
    g = _dot(_sigmoid(gl).astype(BF16), g2_ref[...])

    e = e_ref[...]
    kkv = k * kk_ref[...]
    kk = kkv * lax.rsqrt(_split_dot(kkv * kkv, e) + 1e-12)
    bonus = _split_dot(r * k * rk_ref[...], e) * v

    r_out[...] = r.astype(BF16)
    v_out[...] = v.astype(BF16)
    nkk_out[...] = (-kk).astype(BF16)
    g_out[...] = g
    bonus_out[...] = bonus
    for d in range(2):
        a_d = a[:, d * c:(d + 1) * c]
        w_out[d] = log_decay[:, d * c:(d + 1) * c]
        kka_out[d] = (kk * a_d).astype(BF16)
        km_out[d] = (k * (1.0 + (a_d - 1.0) * ka_ref[...])).astype(BF16)


def _rw_prep(p_all, prm, *, tb, nctx, ctx_bps, lat_bps):
    m = p_all.shape[0]
    c = prm["kk"].shape[-1]
    nblk = m // tb
    t8 = tb // 8
    full = lambda shape: pl.BlockSpec(shape, lambda i: (0,) * len(shape))
    tok = pl.BlockSpec((tb, c), lambda i: (i, 0))
    tok2 = pl.BlockSpec((2, tb, c), lambda i: (0, i, 0))
    one, one_b = jax.ShapeDtypeStruct((m, c), F32), jax.ShapeDtypeStruct((m, c), BF16)
    two, two_b = jax.ShapeDtypeStruct((2, m, c), F32), jax.ShapeDtypeStruct((2, m, c), BF16)
    return pl.pallas_call(
        functools.partial(_rw_prep_kernel, tb=tb, nctx=nctx, ctx_bps=ctx_bps, lat_bps=lat_bps),
        grid=(nblk,),
        in_specs=[pl.BlockSpec((tb, 3 * c), lambda i: (i, C_RKV // (3 * c))),
                  pl.BlockSpec((8, 3 * c), lambda i: (jnp.maximum(i * t8 - 1, 0), 0)),
                  pl.BlockSpec((8, 3 * c), lambda i: (jnp.minimum((i + 1) * t8, m // 8 - 1), 0)),
                  pl.BlockSpec((tb, 512), lambda i: (i, C_SMALL // 512)),
                  full((3, 3 * c)), full((1, 2 * c)), full((128, 2 * c)), full((1, 2 * c)),
                  full((128, 2 * c)), full((128, c)), full((1, c)), full((1, c)), full((1, c)),
                  full((c, c))],
        out_specs=[tok, tok, tok, tok, tok, tok2, tok2, tok2],
        out_shape=[one_b, one_b, one_b, one, one, two, two_b, two_b],
        compiler_params=_cparams(1),
        name="rwkv_prep",
    )(p_all, p_all, p_all, p_all, prm["conv"], prm["w0"], prm["w2"], prm["a0"], prm["a2"],
      prm["g2"], prm["kk"], prm["ka"], prm["rk"], prm["e"])


class _RwUnit:
    def __init__(self, refs, scratch, *, tb, npair, reverse):
        (self.r_ref, self.v_ref, self.a_ref, self.lw_ref, self.b_ref, self.k_ref, self.tri_ref,
         self.y_ref) = refs
        (self.s_ref, self.c_scr, self.ag_scr, self.rg_scr, self.bg_scr, self.kg_scr) = scratch[:6]
        self.slots = (scratch[6:10], scratch[10:14])
        self.reverse = reverse
        self.nch = tb // RW_CHUNK
        self.pairs = range(npair)
        L, hd = RW_CHUNK, RW_HEAD
        self.lss = [slice(p * 2 * hd, (p + 1) * 2 * hd) for p in self.pairs]
        lane = lax.broadcasted_iota(jnp.int32, (1, 2 * hd), 1)
        self.lo = (lane < hd).astype(F32)
        self.hi = 1.0 - self.lo
        ti = lax.broadcasted_iota(jnp.int32, (2 * L, 2 * hd), 0)
        sl = lax.broadcasted_iota(jnp.int32, (2 * L, 2 * hd), 1)
        sj = sl & (L - 1)
        tt = ti & (L - 1)
        earlier = (sj > tt) if reverse else (sj < tt)
        assert 4 * L == 2 * hd and L == 2 * RW_SUB
        self.aa_mask = (earlier | ((ti >= L) & (sj == tt))).astype(F32)
        self.col_idx0 = jnp.where(lax.broadcasted_iota(jnp.int32, (RW_SUB, 2 * hd), 1) < hd, 0, L)
        self.blk_mask = ((lax.broadcasted_iota(jnp.int32, (2 * hd, 2 * hd), 0) >> 6)
                         == (lax.broadcasted_iota(jnp.int32, (2 * hd, 2 * hd), 1) >> 6)).astype(F32)
        self.zeros2l = jnp.zeros((2 * L, 2 * hd), F32)
        self.zeros_sub = jnp.zeros((RW_SUB, 2 * hd), F32)
        self.blocks = (1, 0) if reverse else (0, 1)
        self.last = 0 if reverse else L - 1
        self.order = range(L - 1, -1, -1) if reverse else range(L)

    def reset(self):
        self.s_ref[...] = jnp.zeros_like(self.s_ref)

    def prologue(self):
        lw = self.lw_ref[0]
        c = _split3_dot(self.tri_ref[...], lw)
        self.c_scr[...] = c
        enc = jnp.exp(-c)
        self.ag_scr[...] = self.a_ref[...] * jnp.exp(c - lw)
        self.rg_scr[...] = self.r_ref[...] * jnp.exp(c)
        self.bg_scr[...] = self.b_ref[0] * enc
        self.kg_scr[...] = self.k_ref[0] * enc

    def rows(self, ci):
        L = RW_CHUNK
        cc = (self.nch - 1 - ci) if self.reverse else ci
        return pl.ds(pl.multiple_of(cc * L, L), L)

    def halves(self, x):
        return [x * self.lo, x * self.hi]

    def lhs_of(self, rows):
        return [jnp.concatenate([self.ag_scr[rows, ls], self.rg_scr[rows, ls]], axis=0).astype(BF16)
                for ls in self.lss]

    def prep_aa(self, rows):
        lhs = self.lhs_of(rows)
        out = []
        for p in self.pairs:
            ls = self.lss[p]
            rhs = jnp.concatenate(self.halves(self.bg_scr[rows, ls]) + self.halves(self.kg_scr[rows, ls]),
                                  axis=0)
            out.append(_dot_nt(lhs[p], rhs.astype(BF16)))
        return out

    def prep_akv(self, rows, aa_raw, slot):
        _, akv_scr, aar_scr, na_scr = slot
        L = RW_CHUNK
        aa = [x * self.aa_mask for x in aa_raw]
        for p in self.pairs:
            v = self.v_ref[rows, self.lss[p]]
            vv = jnp.concatenate([self.zeros2l] + self.halves(v), axis=0).astype(BF16)
            akv_scr[p] = _dot(aa[p][0:L].astype(BF16), vv)
            aar_scr[p] = aa[p][L:2 * L]
            na_scr[p] = aa[p][0:L]
        return aa

    def prep_cols(self, aa, slot):
        half = RW_SUB // 2
        for p in self.pairs:
            for s in range(RW_CHUNK):
                r0 = (s // RW_SUB) * RW_SUB
                lo, hi = self.live_rows(s - r0)
                slot[0][p, s, lo:hi] = jnp.take_along_axis(aa[p][r0 + lo:r0 + hi],
                                                           self.col_idx0[0:hi - lo] + s, axis=1)

    def live_rows(self, s_local):
        half = RW_SUB // 2
        if self.reverse:
            return (0, half) if s_local <= half else (0, RW_SUB)
        return (half, RW_SUB) if s_local >= half - 1 else (0, RW_SUB)

    def adv_g(self, rows):
        lhs = self.lhs_of(rows)
        return [_dot_nt(lhs[p], self.s_ref[p].astype(BF16)) for p in self.pairs]

    def adv_solve(self, g, slot):
        L, sub = RW_CHUNK, RW_SUB
        col_scr, akv_scr, _, na_scr = slot
        first, second = self.blocks
        ub = {}
        for blk in self.blocks:
            r0 = blk * sub
            ub[blk] = [g[p][r0:r0 + sub] + akv_scr[p, r0:r0 + sub] for p in self.pairs]
            if blk == second:
                for p in self.pairs:
                    done = ub[first][p]
                    z = self.zeros_sub
                    parts = [z, z, z, z]
                    parts[first], parts[2 + first] = done * self.lo, done * self.hi
                    stack = jnp.concatenate(parts + [self.zeros2l], axis=0).astype(BF16)
                    ub[blk][p] = ub[blk][p] + _dot(na_scr[p, r0:r0 + sub].astype(BF16), stack)
            steps = [s for s in self.order if s // sub == blk]
            half = sub // 2
            top = [x[0:half] for x in ub[blk]]
            bot = [x[half:] for x in ub[blk]]
            for s in steps:
                sl = s - r0
                lo, hi = self.live_rows(sl)
                for p in self.pairs:
                    row = top[p][sl:sl + 1] if sl < half else bot[p][sl - half:sl - half + 1]
                    if lo == 0:
                        top[p] = top[p] + col_scr[p, s, 0:half] * row
                    if hi == sub:
                        bot[p] = bot[p] + col_scr[p, s, half:sub] * row
            ub[blk] = [jnp.concatenate([top[p], bot[p]], axis=0) for p in self.pairs]
        return [jnp.concatenate([ub[0][p], ub[1][p]], axis=0) for p in self.pairs]

    def adv_out(self, rows, g, u, slot):
        L = RW_CHUNK
        for p in self.pairs:
            v = self.v_ref[rows, self.lss[p]]
            uv = jnp.concatenate(self.halves(u[p]) + self.halves(v), axis=0).astype(BF16)
            self.y_ref[rows, self.lss[p]] = (g[p][L:2 * L] + _dot(slot[2][p].astype(BF16), uv)).astype(BF16)
        for p in self.pairs:
            ls = self.lss[p]
            cch = self.c_scr[rows, ls]
            cl = cch[self.last:self.last + 1]
            dec = jnp.exp(cl - cch)
            bk = jnp.concatenate([self.b_ref[0, rows, ls] * dec, self.k_ref[0, rows, ls] * dec], axis=0)
            upd = _dot_tn(jnp.concatenate([u[p], self.v_ref[rows, ls]], axis=0).astype(BF16),
                          bk.astype(BF16))
            self.s_ref[p] = self.s_ref[p] * jnp.exp(cl) + upd * self.blk_mask


N_RW_REFS = 8
N_RW_SCRATCH = 14


def _rw_scan_kernel(*refs, tb, npair):
    n_in = N_RW_REFS - 1
    units = []
    for d in range(2):
        ins = refs[d * n_in:(d + 1) * n_in]
        out = refs[2 * n_in + d]
        scr = refs[2 * n_in + 2 + d * N_RW_SCRATCH:2 * n_in + 2 + (d + 1) * N_RW_SCRATCH]
        units.append(_RwUnit(tuple(ins) + (out,), scr, tb=tb, npair=npair, reverse=d == 1))
    nch = tb // RW_CHUNK

    @pl.when(pl.program_id(1) == 0)
    def _():
        for un in units:
            un.reset()

    for un in units:
        un.prologue()

    def step(ci_adv, sa, ci_prep, sp):
        rows_a = [un.rows(ci_adv) for un in units]
        rows_p = [un.rows(ci_prep) for un in units]
        aa_raw = [un.prep_aa(rp) for un, rp in zip(units, rows_p)]
        g = [un.adv_g(ra) for un, ra in zip(units, rows_a)]
        aa = [un.prep_akv(rp, x, un.slots[sp]) for un, rp, x in zip(units, rows_p, aa_raw)]
        u = [un.adv_solve(x, un.slots[sa]) for un, x in zip(units, g)]
        for un, x in zip(units, aa):
            un.prep_cols(x, un.slots[sp])
        for un, ra, x, y in zip(units, rows_a, g, u):
            un.adv_out(ra, x, y, un.slots[sa])

    for un in units:
        r0 = un.rows(0)
        un.prep_cols(un.prep_akv(r0, un.prep_aa(r0), un.slots[0]), un.slots[0])

    def two_chunks(j, carry):
        c0 = 2 * j
        step(c0, 0, c0 + 1, 1)
        step(c0 + 1, 1, jnp.minimum(c0 + 2, nch - 1), 0)
        return carry

    lax.fori_loop(0, nch // 2, two_chunks, 0)


def _seq_block(d, b, i, *, n_batch, ctxb, latb):
    is_ctx = i < ctxb
    cs = jnp.where(d == 0, i, ctxb - 1 - i)
    lj = jnp.where(d == 0, i - ctxb, latb - 1 - (i - ctxb))
    return jnp.where(is_ctx, b * ctxb + cs, n_batch * ctxb + b * latb + lj)


def _rw_scan(r, v, nkk, lw, kka, km, *, tb, n_batch, ctxb, latb):
    m, c = r.shape
    npair = c // (2 * RW_HEAD)
    lanes = 2 * RW_HEAD
    in_specs, args = [], []
    for d in range(2):
        blk = _la_index(d, n_batch, ctxb, latb)
        tok = pl.BlockSpec((tb, c), lambda b, i, blk=blk: (blk(b, i), 0))
        tokd = pl.BlockSpec((1, tb, c), lambda b, i, blk=blk, d=d: (d, blk(b, i), 0))
        in_specs += [tok, tok, tok, tokd, tokd, tokd, pl.BlockSpec((tb, tb), lambda b, i: (0, 0))]
        args += [r, v, nkk, lw, kka, km, _chunk_tri(tb, d == 1, RW_CHUNK)]
    out_specs = [pl.BlockSpec((tb, c), lambda b, i, blk=_la_index(d, n_batch, ctxb, latb): (blk(b, i), 0))
                 for d in range(2)]
    buf = pltpu.VMEM((tb, c), F32)
    unit_scratch = [pltpu.VMEM((npair, lanes, lanes), F32), buf, buf, buf, buf, buf] + 2 * [
        pltpu.VMEM((npair, RW_CHUNK, RW_SUB, lanes), F32),
        pltpu.VMEM((npair, RW_CHUNK, lanes), F32),
        pltpu.VMEM((npair, RW_CHUNK, lanes), F32),
        pltpu.VMEM((npair, RW_CHUNK, lanes), F32)]
    assert len(unit_scratch) == N_RW_SCRATCH
    return pl.pallas_call(
        functools.partial(_rw_scan_kernel, tb=tb, npair=npair),
        grid=(n_batch, ctxb + latb),
        in_specs=in_specs,
        out_specs=out_specs,
        out_shape=[jax.ShapeDtypeStruct((m, c), BF16)] * 2,
        scratch_shapes=unit_scratch + unit_scratch,
        compiler_params=_cparams(2),
        name="rwkv_scan",
    )(*args)


def _rw_post(y, g, bonus, lnw, lnb, e):
    inv = 1.0 / RW_HEAD
    yc = y - _split_dot(y, e) * inv
    var = _split_dot(yc * yc, e) * inv
    yn = yc * lax.rsqrt(var + RW_LN_EPS)
    return (yn * lnw + lnb + bonus) * g


class _LaUnit:
    def __init__(self, q_ref, k_ref, v_ref, b_ref, o_ref, s_ref, *, tb, nh, dk, dv, reverse, q_scale):
        self.q_ref, self.k_ref, self.v_ref, self.b_ref, self.o_ref, self.s_ref = (
            q_ref, k_ref, v_ref, b_ref, o_ref, s_ref)
        self.nch = tb // LA_CHUNK
        self.dk, self.dv, self.reverse, self.q_scale = dk, dv, reverse, q_scale
        self.lanes = 128
        self.pack = self.lanes // dk
        self.groups = range(nh // self.pack)
        self.rowi = lax.broadcasted_iota(jnp.int32, (LA_CHUNK, 1), 0)
        lane = lax.broadcasted_iota(jnp.int32, (1, self.lanes), 1)
        self.head_lanes = [(lane // dk == j).astype(F32) for j in range(self.pack)]
        if self.pack > 1:
            rows = lax.broadcasted_iota(jnp.int32, (self.pack * dv, self.lanes), 0)
            cols = lax.broadcasted_iota(jnp.int32, (self.pack * dv, self.lanes), 1)
            self.blk_mask = ((rows // dv) == (cols // dk)).astype(F32)

    def reset(self):
        self.s_ref[...] = jnp.zeros_like(self.s_ref)

    def load(self, ci):
        cc = (self.nch - 1 - ci) if self.reverse else ci
        rows = pl.ds(pl.multiple_of(cc * LA_CHUNK, LA_CHUNK), LA_CHUNK)
        pack, dv, lanes = self.pack, self.dv, self.lanes
        out = []
        for g in self.groups:
            ks = slice(g * lanes, (g + 1) * lanes)
            q = self.q_ref[rows, ks] * self.q_scale
            k = self.k_ref[rows, ks]
            v = self.v_ref[rows, g * pack * dv:(g + 1) * pack * dv]
            b = self.b_ref[rows, ks]
            o_inter = _dot_nt((q * jnp.exp(b)).astype(BF16), self.s_ref[g].astype(BF16))
            out.append((rows, q, k, v, b, [o_inter[:, j * dv:(j + 1) * dv] for j in range(pack)]))
        return out

    def intra(self, ops):
        pack, dv, half = self.pack, self.dv, LA_CHUNK // 2
        rowi = self.rowi[0:half]
        for g in self.groups:
            rows, q, k, v, b, o = ops[g]
            qh = [q[0:half], q[half:]]
            bh = [b[0:half], b[half:]]
            oh = [[oj[0:half], oj[half:]] for oj in o]
            for s in range(LA_CHUNK):
                for h in range(2):
                    lo_row, hi_row = h * half, (h + 1) * half - 1
                    if (hi_row > s) if self.reverse else (lo_row < s):
                        if (lo_row > s) if self.reverse else (hi_row < s):
                            continue
                        valid = (rowi + lo_row <= s) if self.reverse else (rowi + lo_row >= s)
                    else:
                        valid = None
                    term = (qh[h] * k[s:s + 1]) * jnp.exp(bh[h] - b[s:s + 1])
                    for j in range(pack):
                        tj = term if pack == 1 else term * self.head_lanes[j]
                        col = jnp.sum(tj, axis=-1, keepdims=True)
                        if valid is not None:
                            col = jnp.where(valid, col, 0.0)
                        oh[j][h] = oh[j][h] + col * v[s:s + 1, j * dv:(j + 1) * dv]
            for j in range(pack):
                self.o_ref[rows, (g * pack + j) * dv:(g * pack + j + 1) * dv] = jnp.concatenate(
                    oh[j], axis=0).astype(BF16)

    def update(self, ops):
        for g in self.groups:
            _, _, k, v, b, _ = ops[g]
            b_last = b[0:1] if self.reverse else b[LA_CHUNK - 1:LA_CHUNK]
            upd = _dot_tn(v.astype(BF16), (k * jnp.exp(b_last - b)).astype(BF16))
            if self.pack > 1:
                upd = upd * self.blk_mask
            self.s_ref[g] = self.s_ref[g] * jnp.exp(b_last) + upd


def _la_run(units, nch):
    def chunk(ci, carry):
        ops = [un.load(ci) for un in units]
        for un, x in zip(units, ops):
            un.update(x)
        for un, x in zip(units, ops):
            un.intra(x)
        return carry

    lax.fori_loop(0, nch, chunk, 0)


def _hg_kernel(qf_ref, ff_ref, vf_ref, trif_ref, qb_ref, fb_ref, vb_ref, trib_ref, gamma_ref,
               of_ref, ob_ref, sf_ref, kf_scr, bf_scr, sb_ref, kb_scr, bb_scr, *, layer, tb, nh, dk, dv):
    gam = gamma_ref[...]
    ex = jnp.exp(gam - jnp.max(gam, axis=0, keepdims=True))
    p = ex / jnp.sum(ex, axis=0, keepdims=True)
    cum = p[0:1]
    for i in range(1, layer + 1):
        cum = cum + p[i:i + 1]
    lb = cum - p[0:1]
    lo = jnp.log(lb)
    l1 = jnp.log(1.0 - lb)

    units = []
    for d, (q_ref, f_ref, v_ref, tri_ref, o_ref, s_ref, k_scr, b_scr) in enumerate((
            (qf_ref, ff_ref, vf_ref, trif_ref, of_ref, sf_ref, kf_scr, bf_scr),
            (qb_ref, fb_ref, vb_ref, trib_ref, ob_ref, sb_ref, kb_scr, bb_scr))):
        hi = l1 + _log_sigmoid(f_ref[...])
        mx = jnp.maximum(lo, hi)
        mn = jnp.minimum(lo, hi)
        log_f = mx + jnp.log(1.0 + jnp.exp(mn - mx))
        k_scr[...] = 1.0 - jnp.exp(log_f)
        b_scr[...] = _split3_dot(tri_ref[...], log_f)
        units.append(_LaUnit(q_ref, k_scr, v_ref, b_scr, o_ref, s_ref, tb=tb, nh=nh, dk=dk, dv=dv,
                             reverse=d == 1, q_scale=1.0))

    @pl.when(pl.program_id(1) == 0)
    def _():
        for un in units:
            un.reset()

    _la_run(units, tb // LA_CHUNK)


def _gla_kernel(qkf_ref, vf_ref, smallf_ref, trif_ref, qkb_ref, vb_ref, smallb_ref, trib_ref,
                gw_ref, gb_ref, of_ref, ob_ref, sf_ref, bf_scr, sb_ref, bb_scr, *, tb, nh, dk, dv):
    hk = nh * dk
    units = []
    for d, (qk_ref, v_ref, small_ref, tri_ref, o_ref, s_ref, b_scr) in enumerate((
            (qkf_ref, vf_ref, smallf_ref, trif_ref, of_ref, sf_ref, bf_scr),
            (qkb_ref, vb_ref, smallb_ref, trib_ref, ob_ref, sb_ref, bb_scr))):
        code = small_ref[:, 384:512]
        pre = _dot(code.astype(BF16), gw_ref[:, d * hk:(d + 1) * hk]) + gb_ref[:, d * hk:(d + 1) * hk]
        log_g = _log_sigmoid(pre) * (1.0 / GLA_GATE_NORM)
        b_scr[...] = _split3_dot(tri_ref[...], log_g)
        units.append(_LaUnit(qk_ref.at[:, 0:hk], qk_ref.at[:, hk:2 * hk], v_ref, b_scr, o_ref, s_ref,
                             tb=tb, nh=nh, dk=dk, dv=dv, reverse=d == 1, q_scale=dk ** -0.5))

    @pl.when(pl.program_id(1) == 0)
    def _():
        for un in units:
            un.reset()

    _la_run(units, tb // LA_CHUNK)


def _chunk_tri(tb, reverse, chunk=LA_CHUNK):
    t = jnp.arange(tb)
    same = (t[:, None] // chunk) == (t[None, :] // chunk)
    tri = (t[None, :] >= t[:, None]) if reverse else (t[None, :] <= t[:, None])
    return (same & tri).astype(BF16)


def _la_index(d, n_batch, ctxb, latb):
    return lambda b, i: _seq_block(d, b, i, n_batch=n_batch, ctxb=ctxb, latb=latb)


def _hgrn2(p_all, gamma, *, layer, tb, n_batch, ctxb, latb):
    m = p_all.shape[0]
    c = gamma.shape[-1]
    nh = c // HG_DK
    in_specs, args, out_specs = [], [], []
    for d in range(2):
        blk = _la_index(d, n_batch, ctxb, latb)
        col = lambda off, blk=blk: pl.BlockSpec((tb, c), lambda b, i: (blk(b, i), off // c))
        in_specs += [col(C_HGQ), col(C_HGF + d * c), col(C_HGI), pl.BlockSpec((tb, tb), lambda b, i: (0, 0))]
        args += [p_all, p_all, p_all, _chunk_tri(tb, d == 1)]
        out_specs.append(col(0))
    unit_scratch = [pltpu.VMEM((nh, HG_DK, HG_DK), F32), pltpu.VMEM((tb, c), F32), pltpu.VMEM((tb, c), F32)]
    return pl.pallas_call(
        functools.partial(_hg_kernel, layer=layer, tb=tb, nh=nh, dk=HG_DK, dv=HG_DK),
        grid=(n_batch, ctxb + latb),
        in_specs=in_specs + [pl.BlockSpec(gamma.shape, lambda b, i: (0, 0))],
        out_specs=out_specs,
        out_shape=[jax.ShapeDtypeStruct((m, c), BF16)] * 2,
        scratch_shapes=unit_scratch + unit_scratch,
        compiler_params=_cparams(2),
        name="hgrn2_scan",
    )(*args, gamma)


def _gla(p_all, gw, gb, *, tb, n_batch, ctxb, latb):
    m = p_all.shape[0]
    hk = gw.shape[-1] // 2
    dk = hk // GLA_HEADS
    c = 2 * hk
    dv = c // GLA_HEADS
    in_specs, args, out_specs = [], [], []
    for d in range(2):
        blk = _la_index(d, n_batch, ctxb, latb)
        col = lambda off, blk=blk: pl.BlockSpec((tb, c), lambda b, i: (blk(b, i), off // c))
        in_specs += [col(C_GLQK), col(C_GLV), col(C_SMALL), pl.BlockSpec((tb, tb), lambda b, i: (0, 0))]
        args += [p_all, p_all, p_all, _chunk_tri(tb, d == 1)]
        out_specs.append(col(0))
    unit_scratch = [pltpu.VMEM((hk // 128, (128 // dk) * dv, 128), F32), pltpu.VMEM((tb, hk), F32)]
    return pl.pallas_call(
        functools.partial(_gla_kernel, tb=tb, nh=GLA_HEADS, dk=dk, dv=dv),
        grid=(n_batch, ctxb + latb),
        in_specs=in_specs + [pl.BlockSpec(gw.shape, lambda b, i: (0, 0)),
                             pl.BlockSpec(gb.shape, lambda b, i: (0, 0))],
        out_specs=out_specs,
        out_shape=[jax.ShapeDtypeStruct((m, c), BF16)] * 2,
        scratch_shapes=unit_scratch + unit_scratch,
        compiler_params=_cparams(2),
        name="gla_scan",
    )(*args, gw, gb)


def _la_post(o, gate, norm_g, nh):
    dv = o.shape[-1] // nh
    outs = []
    for h in range(nh):
        oh = o[:, h * dv:(h + 1) * dv]
        outs.append(oh * lax.rsqrt(jnp.mean(oh * oh, axis=-1, keepdims=True) + EPS))
    return jnp.concatenate(outs, axis=-1) * norm_g * (gate * _sigmoid(gate))


def _gelu(x):
    return 0.5 * x * (1.0 + jnp.tanh(0.7978845608028654 * (x + 0.044715 * (x * x * x))))


def _sgu_kernel(u_ref, v_ref, lnw_ref, lnb_ref, ws_ref, bs_ref, o_ref, *, rb):
    u = _gelu(u_ref[...])
    v = _gelu(v_ref[...])
    vc = v - jnp.mean(v, axis=-1, keepdims=True)
    vn = vc * lax.rsqrt(jnp.mean(vc * vc, axis=-1, keepdims=True) + EPS)
    vn = (vn * lnw_ref[...] + lnb_ref[...]).astype(BF16)
    gw = vn.shape[-1] // SGU_GROUPS
    for n in range(rb // SGU_CHUNK):
        rs = slice(n * SGU_CHUNK, (n + 1) * SGU_CHUNK)
        for g in range(SGU_GROUPS):
            cs = slice(g * gw, (g + 1) * gw)
            s = _dot(ws_ref[g], vn[rs, cs]) + bs_ref[g]
            o_ref[rs, cs] = u[rs, cs] * s


def _sgu(p_all, lnw, lnb, ws, bs, *, rb):
    m = p_all.shape[0]
    c = lnw.shape[-1]
    return pl.pallas_call(
        functools.partial(_sgu_kernel, rb=rb),
        grid=(m // rb,),
        in_specs=[pl.BlockSpec((rb, c), lambda i: (i, C_SGU // c)),
                  pl.BlockSpec((rb, c), lambda i: (i, C_SGU // c + 1)),
                  pl.BlockSpec((1, c), lambda i: (0, 0)),
                  pl.BlockSpec((1, c), lambda i: (0, 0)),
                  pl.BlockSpec(ws.shape, lambda i: (0, 0, 0)),
                  pl.BlockSpec(bs.shape, lambda i: (0, 0, 0))],
        out_specs=pl.BlockSpec((rb, c), lambda i: (i, 0)),
        out_shape=jax.ShapeDtypeStruct((m, c), F32),
        compiler_params=_cparams(1),
        name="sgu",
    )(p_all, p_all, lnw, lnb, ws, bs)


def _merge_kernel(raf_ref, rab_ref, rg_ref, rbonus_ref, hf_ref, hb_ref, hgate_ref, gf_ref, gb_ref,
                  ggate_ref, yd_ref, lnw_ref, lnb_ref, e_ref, hnorm_ref, gnorm_ref,
                  g0_ref, g1_ref, g2_ref, g3_ref, w_ref, o_ref, y_scr, *, hg_heads):
    @pl.when(pl.program_id(1) == 0)
    def _():
        both = lambda f_ref, b_ref: f_ref[...].astype(F32) + b_ref[...].astype(F32)
        y_scr[0] = _rw_post(both(raf_ref, rab_ref), rg_ref[...], rbonus_ref[...], lnw_ref[...],
                            lnb_ref[...], e_ref[...]).astype(BF16)
        y_scr[1] = _la_post(both(hf_ref, hb_ref), hgate_ref[...], hnorm_ref[...], hg_heads).astype(BF16)
        y_scr[2] = _la_post(both(gf_ref, gb_ref), ggate_ref[...], gnorm_ref[...], GLA_HEADS).astype(BF16)
        y_scr[3] = yd_ref[...].astype(BF16)

    gs = (g0_ref, g1_ref, g2_ref, g3_ref)
    acc = None
    for j in range(N_BRANCH):
        t = _sigmoid(gs[j][...].astype(F32)) * _dot(y_scr[j], w_ref[j])
        acc = t if acc is None else acc + t
    o_ref[...] = acc.astype(BF16)


def _merge(rw, hg, gla, yd, p_gate, p_mix, vecs, e, w_branch, *, layer, tm, row_off):
    m = p_mix.shape[0]
    _, _, c, d = w_branch.shape
    tn = _pick((1024, 512), d)
    mo = m - row_off * tm
    ytok = pl.BlockSpec((tm, c), lambda i, j: (i + row_off, 0))
    mix = lambda off: pl.BlockSpec((tm, c), lambda i, j: (i + row_off, off // c))
    gate = lambda b: pl.BlockSpec((tm, tn), lambda i, j: (i + row_off, b * d // tn + j))
    vec = pl.BlockSpec((1, c), lambda i, j: (0, 0))
    return pl.pallas_call(
        functools.partial(_merge_kernel, hg_heads=c // HG_DK),
        grid=(mo // tm, d // tn),
        in_specs=[ytok, ytok, ytok, ytok, ytok, ytok, mix(C_HGG), ytok, ytok, mix(C_GLG), ytok,
                  vec, vec, pl.BlockSpec((c, c), lambda i, j: (0, 0)), vec, vec,
                  gate(0), gate(1), gate(2), gate(3),
                  pl.BlockSpec((None, N_BRANCH, c, tn), lambda i, j: (layer, 0, 0, j))],
        out_specs=pl.BlockSpec((tm, tn), lambda i, j: (i, j)),
        out_shape=jax.ShapeDtypeStruct((mo, d), BF16),
        scratch_shapes=[pltpu.VMEM((N_BRANCH, tm, c), BF16)],
        compiler_params=_cparams(2),
        name="merge",
    )(*rw, hg[0], hg[1], p_mix, gla[0], gla[1], p_mix, yd, *vecs[:2], e, *vecs[2:],
      p_gate, p_gate, p_gate, p_gate, w_branch)


def _outproj_kernel(m_ref, w_ref, xc_ref, xl_ref, mod_ref, o_ref, *, nctx, row_off):
    x = jnp.where(pl.program_id(0) + row_off < nctx, xc_ref[...], xl_ref[...])
    o_ref[...] = x + mod_ref[0, 2:3, :] * _dot(m_ref[...], w_ref[...])


def _outproj(mm, w_out, x_parts, mod, *, layer, tm, row_off, nctx, bpb):
    xc, xl, lat_off = x_parts
    mo, d = mm.shape
    tn = _pick((1024, 512), d)
    xspecs = [pl.BlockSpec((tm, tn), lambda i, j: (jnp.minimum(i + row_off, nctx - 1),
                                                   jnp.where(i + row_off < nctx, j, 0))),
              pl.BlockSpec((tm, tn), lambda i, j: (lat_off + jnp.maximum(i + row_off - nctx, 0),
                                                   jnp.where(i + row_off < nctx, 0, j)))]
    return pl.pallas_call(
        functools.partial(_outproj_kernel, nctx=nctx, row_off=row_off),
        grid=(mo // tm, d // tn),
        in_specs=[pl.BlockSpec((tm, d), lambda i, j: (i, 0)),
                  pl.BlockSpec((None, d, tn), lambda i, j: (layer, 0, j))] + xspecs + [
                  pl.BlockSpec((1, 6, tn), lambda i, j: (_row_group(i + row_off, nctx, bpb), 0, j))],
        out_specs=pl.BlockSpec((tm, tn), lambda i, j: (i, j)),
        out_shape=jax.ShapeDtypeStruct((mo, d), F32),
        compiler_params=_cparams(2),
        name="outproj",
    )(mm, w_out, xc, xl, mod)


def _mlp_kernel(x_ref, mod_ref, g_ref, w1_ref, w2_ref, gf_ref, o_ref, h_ref, acc_ref, *, final_norm):
    j = pl.program_id(1)

    @pl.when(j == 0)
    def _():
        h = _norm_mod(x_ref[...], g_ref[...], mod_ref[0, 3:4, :], mod_ref[0, 4:5, :])
        h_ref[...] = h.astype(BF16)
        acc_ref[...] = jnp.zeros_like(acc_ref)

    a = jnp.maximum(_dot(h_ref[...], w1_ref[...]), 0.0)
    acc_ref[...] += _dot((a * a).astype(BF16), w2_ref[...])

    @pl.when(j == pl.num_programs(1) - 1)
    def _():
        y = x_ref[...] + mod_ref[0, 5:6, :] * acc_ref[...]
        if final_norm:
            y = y * lax.rsqrt(jnp.mean(y * y, axis=-1, keepdims=True) + EPS) * gf_ref[...]
        o_ref[...] = y


def _mlp(x_in, mod, g, w1, w2, g_final, *, layer, tm, row_off, nctx, bpb, final_norm):
    mo, d = x_in.shape
    hid = w1.shape[-1]
    th = _pick((1024, 512, 256, 128), hid)
    return pl.pallas_call(
        functools.partial(_mlp_kernel, final_norm=final_norm),
        grid=(mo // tm, hid // th),
        in_specs=[pl.BlockSpec((tm, d), lambda i, j: (i, 0)),
                  pl.BlockSpec((1, 6, d), lambda i, j: (_row_group(i + row_off, nctx, bpb), 0, 0)),
                  pl.BlockSpec((1, d), lambda i, j: (0, 0)),
                  pl.BlockSpec((None, d, th), lambda i, j: (layer, 0, j)),
                  pl.BlockSpec((None, th, d), lambda i, j: (layer, j, 0)),
                  pl.BlockSpec((1, d), lambda i, j: (0, 0))],
        out_specs=pl.BlockSpec((tm, d), lambda i, j: (i, 0)),
        out_shape=jax.ShapeDtypeStruct((mo, d), F32),
        scratch_shapes=[pltpu.VMEM((tm, d), BF16), pltpu.VMEM((tm, d), F32)],
        compiler_params=_cparams(2),
        name="mlp",
    )(x_in, mod, g, w1, w2, g_final)


def _blockdiag2(w):
    _, r, c = w.shape
    z = jnp.zeros((r, c), w.dtype)
    out = jnp.concatenate([jnp.concatenate([w[0], z], axis=1), jnp.concatenate([z, w[1]], axis=1)], axis=0)
    return jnp.pad(out, ((0, 128 - 2 * r), (0, 0))).astype(BF16)


def _permute_w_in(w, d_model):
    c = d_model // N_BRANCH
    o = [0]
    for wd in (3 * c, 64 * 2, 64 * 2, 128, c, 2 * c, c, c, c, c, 32, c, 2 * c, N_BRANCH * d_model):
        o.append(o[-1] + wd)
    seg = lambda k: w[:, o[k]:o[k + 1]]
    pad = jnp.zeros((w.shape[0], 512 - 128 * 3 - 32), w.dtype)
    parts = [seg(13), seg(0), seg(5), seg(12), seg(4), seg(6), seg(7), seg(8), seg(9), seg(11),
             seg(1), seg(2), seg(3), seg(10), pad]
    out = jnp.concatenate(parts, axis=1).astype(BF16)
    assert out.shape[1] == N_GATE + N_MIX
    return out


def kernel(x, c, ctx, c_ctx, w_ada, b_ada, g_norm1, g_norm2, g_final, w_in, rw_conv, rw_w0, rw_w2,
           rw_a0, rw_a2, rw_g2, rw_kk, rw_ka, rw_rk, rw_ln_w, rw_ln_b, hg_gamma, hg_norm, gla_gw,
           gla_gb, gla_norm, sgu_ln_w, sgu_ln_b, sgu_w, sgu_b, w_branch, w_out, w_mlp1, w_mlp2):
    n_batch, seq, d_model = x.shape
    ctx_len = ctx.shape[1]
    depth = w_in.shape[0]
    cw = d_model // N_BRANCH
    assert cw == 512 and d_model == 2048, "column layout constants assume D_MODEL = 2048"
    m_ctx = n_batch * ctx_len

    tm = _pick((1024, 512, 256, 128), m_ctx, seq)
    tb = _pick((256, 128), ctx_len, seq)
    nctx, bpb = m_ctx // tm, seq // tm
    ctxb, latb = ctx_len // tb, seq // tb
    seqs = dict(tb=tb, n_batch=n_batch, ctxb=ctxb, latb=latb)

    x_parts = (ctx.reshape(m_ctx, d_model), x.reshape(n_batch * seq, d_model), 0)
    c_rows = jnp.concatenate([c_ctx[None, :], c, jnp.zeros((7 - n_batch, d_model), F32)], axis=0)
    mod_all = _ada(c_rows, w_ada, b_ada).reshape(depth, 8, 6, d_model)

    head_ones = (jnp.arange(cw)[:, None] // RW_HEAD == jnp.arange(cw)[None, :] // RW_HEAD).astype(BF16)
    w_branch_b, w_out_b = w_branch.astype(BF16), w_out.astype(BF16)
    w_mlp1_b, w_mlp2_b = w_mlp1.astype(BF16), w_mlp2.astype(BF16)
    row = lambda a: a.reshape(1, -1)

    for l in range(depth):
        last = l == depth - 1
        mod = mod_all[l]
        p_gate, p_all = _inproj(x_parts, mod, row(g_norm1[l]), _permute_w_in(w_in[l], d_model),
                                tm=tm, nctx=nctx, bpb=bpb)

        prm = dict(conv=rw_conv[l], w0=row(rw_w0[l]), w2=_blockdiag2(rw_w2[l]), a0=row(rw_a0[l]),
                   a2=_blockdiag2(rw_a2[l]), g2=rw_g2[l].astype(BF16), kk=row(rw_kk[l]),
                   ka=row(rw_ka[l]), rk=row(rw_rk[l]), e=head_ones)
        r, v, nkk, g, bonus, w, kka, km = _rw_prep(p_all, prm, tb=tb, nctx=m_ctx // tb,
                                                    ctx_bps=ctxb, lat_bps=latb)
        ys = _rw_scan(r, v, nkk, w, kka, km, **seqs)

        ob = _hgrn2(p_all, hg_gamma, layer=l, **seqs)

        gw = _blockdiag2(gla_gw[l])
        gb = row(gla_gb[l])
        oc = _gla(p_all, gw, gb, **seqs)

        bs = jnp.broadcast_to(sgu_b[l][:, :, None], sgu_w[l].shape)
        yd = _sgu(p_all, row(sgu_ln_w[l]), row(sgu_ln_b[l]), sgu_w[l].astype(BF16), bs,
                  rb=_pick((512, 256, 128), m_ctx, seq))

        row_off = nctx if last else 0
        vecs = (row(rw_ln_w[l]), row(rw_ln_b[l]), row(hg_norm[l]), row(gla_norm[l]))
        mm = _merge((ys[0], ys[1], g, bonus), ob, oc, yd, p_gate, p_all, vecs, head_ones,
                    w_branch_b, layer=l, tm=tm // 2, row_off=2 * row_off)
        x_mid = _outproj(mm, w_out_b, x_parts, mod, layer=l, tm=tm, row_off=row_off, nctx=nctx, bpb=bpb)
        x_all = _mlp(x_mid, mod, row(g_norm2[l]), w_mlp1_b, w_mlp2_b, row(g_final), layer=l,
                     tm=tm // 2, row_off=2 * row_off, nctx=2 * nctx, bpb=2 * bpb, final_norm=last)
        x_parts = (x_all, x_all, nctx)
    return x_all.reshape(n_batch, seq, d_model)
```

```python
import functools

import jax
import jax.numpy as jnp
from jax import lax
from jax.experimental import pallas as pl
from jax.experimental.pallas import tpu as pltpu

F32 = jnp.float32
BF16 = jnp.bfloat16

N_BRANCH = 4
RW_HEAD = 64
RW_LN_EPS = 64e-5
HG_DK = 128
GLA_HEADS = 4
GLA_GATE_NORM = 16.0
LA_CHUNK = 16
RW_CHUNK = 32
RW_SUB = 16
SGU_CHUNK = 128
SGU_GROUPS = 4
EPS = 1e-6
HALO = 16

C_RKV = 0
C_HGF = 1536
C_SGU = 2560
C_HGQ = 3584
C_HGI = 4096
C_HGG = 4608
C_GLQK = 5120
C_GLV = 5632
C_GLG = 6144
C_SMALL = 6656
N_MIX = 7168
N_GATE = 8192

VMEM_LIMIT = 58 * 1024 * 1024


def _cparams(n_axes):
    return pltpu.CompilerParams(dimension_semantics=("arbitrary",) * n_axes,
                                vmem_limit_bytes=VMEM_LIMIT)


def _pick(n_list, *dims):
    for n in n_list:
        if all(d % n == 0 for d in dims):
            return n
    raise ValueError(f"no block size in {n_list} divides {dims}")


def _row_group(i, nctx, bpb):
    return jnp.where(i < nctx, 0, 1 + (i - nctx) // bpb)


def _dot(a, b):
    return jnp.dot(a, b, preferred_element_type=F32)


def _dot_nt(a, b):
    return lax.dot_general(a, b, (((1,), (1,)), ((), ())), preferred_element_type=F32)


def _dot_tn(a, b):
    return lax.dot_general(a, b, (((0,), (0,)), ((), ())), preferred_element_type=F32)


def _split_dot(x, e):
    hi = x.astype(BF16)
    lo = (x - hi.astype(F32)).astype(BF16)
    return _dot(hi, e) + _dot(lo, e)


def _split3_dot(e, x):
    p1 = x.astype(BF16)
    r1 = x - p1.astype(F32)
    p2 = r1.astype(BF16)
    p3 = (r1 - p2.astype(F32)).astype(BF16)
    return _dot(e, p1) + _dot(e, p2) + _dot(e, p3)


def _log_sigmoid(x):
    return jnp.minimum(x, 0.0) - jnp.log(1.0 + jnp.exp(-jnp.abs(x)))


def _sigmoid(x):
    return 0.5 * jnp.tanh(0.5 * x) + 0.5


def _ada_kernel(c_ref, w_ref, b_ref, o_ref):
    c = c_ref[...]
    act = c * _sigmoid(c)
    part = _dot(act.astype(BF16), w_ref[0].astype(BF16))

    @pl.when(pl.program_id(1) == 0)
    def _():
        o_ref[0] = part + b_ref[0]

    @pl.when(pl.program_id(1) > 0)
    def _():
        o_ref[0] += part


def _ada(c_rows, w_ada, b_ada):
    n_layers, d, n = w_ada.shape
    tk = _pick((256, 128), d)
    return pl.pallas_call(
        _ada_kernel,
        grid=(n_layers, d // tk),
        in_specs=[pl.BlockSpec((8, tk), lambda l, k: (0, k)),
                  pl.BlockSpec((1, tk, n), lambda l, k: (l, k, 0)),
                  pl.BlockSpec((1, 1, n), lambda l, k: (l, 0, 0))],
        out_specs=pl.BlockSpec((1, 8, n), lambda l, k: (l, 0, 0)),
        out_shape=jax.ShapeDtypeStruct((n_layers, 8, n), F32),
        compiler_params=_cparams(2),
        name="ada_mod",
    )(c_rows, w_ada, b_ada.reshape(n_layers, 1, n))


def _norm_mod(x, g, shift, scale):
    y = x * lax.rsqrt(jnp.mean(x * x, axis=-1, keepdims=True) + EPS) * g
    return y * (1.0 + scale) + shift


def _two_part_specs(block, nctx, lat_off):
    return [pl.BlockSpec(block, lambda i, j: (jnp.minimum(i, nctx - 1), 0), pipeline_mode=pl.Buffered(1)),
            pl.BlockSpec(block, lambda i, j: (lat_off + jnp.maximum(i - nctx, 0), 0))]


def _inproj_kernel(xc_ref, xl_ref, mod_ref, g_ref, w_ref, og_ref, om_ref, h_ref, *, nctx, n_gate_tiles):
    i, j = pl.program_id(0), pl.program_id(1)

    @pl.when(j == 0)
    def _():
        x = jnp.where(i < nctx, xc_ref[...], xl_ref[...])
        h = _norm_mod(x, g_ref[...], mod_ref[0, 0:1, :], mod_ref[0, 1:2, :])
        h_ref[...] = h.astype(BF16)

    @pl.when(j < n_gate_tiles)
    def _():
        og_ref[...] = _dot(h_ref[...], w_ref[...]).astype(BF16)

    @pl.when(j >= n_gate_tiles)
    def _():
        om_ref[...] = _dot(h_ref[...], w_ref[...]).astype(BF16)


def _inproj(x_parts, mod, g, w, *, tm, nctx, bpb):
    xc, xl, lat_off = x_parts
    d = xc.shape[1]
    m = (nctx + (xl.shape[0] // tm - lat_off)) * tm
    tn = _pick((1024, 512), N_GATE, N_MIX)
    ng = N_GATE // tn
    return pl.pallas_call(
        functools.partial(_inproj_kernel, nctx=nctx, n_gate_tiles=ng),
        grid=(m // tm, (N_GATE + N_MIX) // tn),
        in_specs=_two_part_specs((tm, d), nctx, lat_off) + [
            pl.BlockSpec((1, 6, d), lambda i, j: (_row_group(i, nctx, bpb), 0, 0)),
            pl.BlockSpec((1, d), lambda i, j: (0, 0)),
            pl.BlockSpec((d, tn), lambda i, j: (0, j))],
        out_specs=[pl.BlockSpec((tm, tn), lambda i, j: (i, jnp.minimum(j, ng - 1))),
                   pl.BlockSpec((tm, tn), lambda i, j: (i, jnp.maximum(j - ng, 0)))],
        out_shape=[jax.ShapeDtypeStruct((m, N_GATE), BF16), jax.ShapeDtypeStruct((m, N_MIX), BF16)],
        scratch_shapes=[pltpu.VMEM((tm, d), BF16)],
        compiler_params=_cparams(2),
        name="inproj",
    )(xc, xl, mod, g, w)


def _rw_prep_kernel(rkv_ref, prev_ref, next_ref, small_ref, conv_ref, w0_ref, w2_ref, a0_ref,
                    a2_ref, g2_ref, kk_ref, ka_ref, rk_ref, e_ref,
                    r_out, v_out, nkk_out, g_out, bonus_out, w_out, kka_out, km_out,
                    *, tb, nctx, ctx_bps, lat_bps):
    i = pl.program_id(0)
    c = r_out.shape[-1]
    j = jnp.where(i < nctx, i, i - nctx)
    bps = jnp.where(i < nctx, ctx_bps, lat_bps)
    first = lax.rem(j, bps) == 0
    last = lax.rem(j, bps) == bps - 1

    blk = rkv_ref[...].astype(F32)
    rows = lax.broadcasted_iota(jnp.int32, (tb, 1), 0)
    prev_row = jnp.where(first, 0.0, prev_ref[HALO - 1:HALO, :].astype(F32))
    next_row = jnp.where(last, 0.0, next_ref[0:1, :].astype(F32))
    xm1 = jnp.where(rows == 0, prev_row, pltpu.roll(blk, 1, 0))
    xp1 = jnp.where(rows == tb - 1, next_row, pltpu.roll(blk, tb - 1, 0))
    conv = conv_ref[0:1, :] * xm1 + conv_ref[1:2, :] * blk + conv_ref[2:3, :] * xp1
    r = conv[:, 0:c]
    k = conv[:, c:2 * c]
    v = conv[:, 2 * c:3 * c]

    small = small_ref[...].astype(F32)
    wl = small[:, 0:128]
    al = small[:, 128:256]
    gl = small[:, 256:384]
    w_pre = w0_ref[...] + _dot(jnp.tanh(wl).astype(BF16), w2_ref[...])
    softplus = jnp.maximum(-w_pre, 0.0) + jnp.log(1.0 + jnp.exp(-jnp.abs(w_pre)))
    log_decay = -jnp.exp(-softplus - 0.5)
    a = _sigmoid(a0_ref[...] + _dot(al.astype(BF16), a2_ref[...]))
    g = _dot(_sigmoid(gl).astype(BF16), g2_ref[...])

    e = e_ref[...]
    kkv = k * kk_ref[...]
    kk = kkv * lax.rsqrt(_split_dot(kkv * kkv, e) + 1e-12)
    bonus = _split_dot(r * k * rk_ref[...], e) * v

    r_out[...] = r.astype(BF16)
    v_out[...] = v.astype(BF16)
    nkk_out[...] = (-kk).astype(BF16)
    g_out[...] = g
    bonus_out[...] = bonus
    for d in range(2):
        a_d = a[:, d * c:(d + 1) * c]
        w_out[d] = log_decay[:, d * c:(d + 1) * c]
        kka_out[d] = (kk * a_d).astype(BF16)
        km_out[d] = (k * (1.0 + (a_d - 1.0) * ka_ref[...])).astype(BF16)


def _rw_prep(p_all, prm, *, tb, nctx, ctx_bps, lat_bps):
    m = p_all.shape[0]
    c = prm["kk"].shape[-1]
    nblk = m // tb
    th = tb // HALO
    full = lambda shape: pl.BlockSpec(shape, lambda i: (0,) * len(shape))
    tok = pl.BlockSpec((tb, c), lambda i: (i, 0))
    tok2 = pl.BlockSpec((2, tb, c), lambda i: (0, i, 0))
    one, one_b = jax.ShapeDtypeStruct((m, c), F32), jax.ShapeDtypeStruct((m, c), BF16)
    two, two_b = jax.ShapeDtypeStruct((2, m, c), F32), jax.ShapeDtypeStruct((2, m, c), BF16)
    return pl.pallas_call(
        functools.partial(_rw_prep_kernel, tb=tb, nctx=nctx, ctx_bps=ctx_bps, lat_bps=lat_bps),
        grid=(nblk,),
        in_specs=[pl.BlockSpec((tb, 3 * c), lambda i: (i, C_RKV // (3 * c))),
                  pl.BlockSpec((HALO, 3 * c), lambda i: (jnp.maximum(i * th - 1, 0), 0)),
                  pl.BlockSpec((HALO, 3 * c), lambda i: (jnp.minimum((i + 1) * th, m // HALO - 1), 0)),
                  pl.BlockSpec((tb, 512), lambda i: (i, C_SMALL // 512)),
                  full((3, 3 * c)), full((1, 2 * c)), full((128, 2 * c)), full((1, 2 * c)),
                  full((128, 2 * c)), full((128, c)), full((1, c)), full((1, c)), full((1, c)),
                  full((c, c))],
        out_specs=[tok, tok, tok, tok, tok, tok2, tok2, tok2],
        out_shape=[one_b, one_b, one_b, one, one, two, two_b, two_b],
        compiler_params=_cparams(1),
        name="rwkv_prep",
    )(p_all, p_all, p_all, p_all, prm["conv"], prm["w0"], prm["w2"], prm["a0"], prm["a2"],
      prm["g2"], prm["kk"], prm["ka"], prm["rk"], prm["e"])


class _RwUnit:
    def __init__(self, refs, scratch, *, tb, npair, reverse):
        (self.r_ref, self.v_ref, self.a_ref, self.lw_ref, self.b_ref, self.k_ref, self.tri_ref,
         self.y_ref) = refs
        (self.s_ref, self.c_scr, self.ag_scr, self.rg_scr, self.bg_scr, self.kg_scr) = scratch[:6]
        self.slots = (scratch[6:10], scratch[10:14])
        self.reverse = reverse
        self.nch = tb // RW_CHUNK
        self.pairs = range(npair)
        L, hd = RW_CHUNK, RW_HEAD
        self.lss = [slice(p * 2 * hd, (p + 1) * 2 * hd) for p in self.pairs]
        lane = lax.broadcasted_iota(jnp.int32, (1, 2 * hd), 1)
        self.lo = (lane < hd).astype(F32)
        self.hi = 1.0 - self.lo
        ti = lax.broadcasted_iota(jnp.int32, (2 * L, 2 * hd), 0)
        sl = lax.broadcasted_iota(jnp.int32, (2 * L, 2 * hd), 1)
        sj = sl & (L - 1)
        tt = ti & (L - 1)
        earlier = (sj > tt) if reverse else (sj < tt)
        assert 4 * L == 2 * hd and L == 2 * RW_SUB
        self.aa_mask = (earlier | ((ti >= L) & (sj == tt))).astype(F32)
        self.col_idx0 = jnp.where(lax.broadcasted_iota(jnp.int32, (RW_SUB, 2 * hd), 1) < hd, 0, L)
        self.blk_mask = ((lax.broadcasted_iota(jnp.int32, (2 * hd, 2 * hd), 0) >> 6)
                         == (lax.broadcasted_iota(jnp.int32, (2 * hd, 2 * hd), 1) >> 6)).astype(F32)
        self.zeros2l = jnp.zeros((2 * L, 2 * hd), F32)
        self.zeros_sub = jnp.zeros((RW_SUB, 2 * hd), F32)
        self.blocks = (1, 0) if reverse else (0, 1)
        self.last = 0 if reverse else L - 1
        self.order = range(L - 1, -1, -1) if reverse else range(L)

    def reset(self):
        self.s_ref[...] = jnp.zeros_like(self.s_ref)

    def prologue(self):
        lw = self.lw_ref[0]
        c = _split3_dot(self.tri_ref[...], lw)
        self.c_scr[...] = c
        enc = jnp.exp(-c)
        self.ag_scr[...] = self.a_ref[...] * jnp.exp(c - lw)
        self.rg_scr[...] = self.r_ref[...] * jnp.exp(c)
        self.bg_scr[...] = self.b_ref[0] * enc
        self.kg_scr[...] = self.k_ref[0] * enc

    def rows(self, ci):
        L = RW_CHUNK
        cc = (self.nch - 1 - ci) if self.reverse else ci
        return pl.ds(pl.multiple_of(cc * L, L), L)

    def halves(self, x):
        return [x * self.lo, x * self.hi]

    def lhs_of(self, rows):
        return [jnp.concatenate([self.ag_scr[rows, ls], self.rg_scr[rows, ls]], axis=0).astype(BF16)
                for ls in self.lss]

    def prep_aa(self, rows):
        lhs = self.lhs_of(rows)
        out = []
        for p in self.pairs:
            ls = self.lss[p]
            rhs = jnp.concatenate(self.halves(self.bg_scr[rows, ls]) + self.halves(self.kg_scr[rows, ls]),
                                  axis=0)
            out.append(_dot_nt(lhs[p], rhs.astype(BF16)))
        return out

    def prep_akv(self, rows, aa_raw, slot):
        _, akv_scr, aar_scr, na_scr = slot
        L = RW_CHUNK
        aa = [x * self.aa_mask for x in aa_raw]
        for p in self.pairs:
            v = self.v_ref[rows, self.lss[p]]
            vv = jnp.concatenate([self.zeros2l] + self.halves(v), axis=0).astype(BF16)
            akv_scr[p] = _dot(aa[p][0:L].astype(BF16), vv)
            aar_scr[p] = aa[p][L:2 * L]
            na_scr[p] = aa[p][0:L]
        return aa

    def prep_cols(self, aa, slot):
        half = RW_SUB // 2
        for p in self.pairs:
            for s in range(RW_CHUNK):
                r0 = (s // RW_SUB) * RW_SUB
                lo, hi = self.live_rows(s - r0)
                slot[0][p, s, lo:hi] = jnp.take_along_axis(aa[p][r0 + lo:r0 + hi],
                                                           self.col_idx0[0:hi - lo] + s, axis=1)

    def live_rows(self, s_local):
        half = RW_SUB // 2
        if self.reverse:
            return (0, half) if s_local <= half else (0, RW_SUB)
        return (half, RW_SUB) if s_local >= half - 1 else (0, RW_SUB)

    def adv_g(self, rows):
        lhs = self.lhs_of(rows)
        return [_dot_nt(lhs[p], self.s_ref[p].astype(BF16)) for p in self.pairs]

    def adv_solve(self, g, slot):
        L, sub = RW_CHUNK, RW_SUB
        col_scr, akv_scr, _, na_scr = slot
        first, second = self.blocks
        ub = {}
        for blk in self.blocks:
            r0 = blk * sub
            ub[blk] = [g[p][r0:r0 + sub] + akv_scr[p, r0:r0 + sub] for p in self.pairs]
            if blk == second:
                for p in self.pairs:
                    done = ub[first][p]
                    z = self.zeros_sub
                    parts = [z, z, z, z]
                    parts[first], parts[2 + first] = done * self.lo, done * self.hi
                    stack = jnp.concatenate(parts + [self.zeros2l], axis=0).astype(BF16)
                    ub[blk][p] = ub[blk][p] + _dot(na_scr[p, r0:r0 + sub].astype(BF16), stack)
            steps = [s for s in self.order if s // sub == blk]
            half = sub // 2
            top = [x[0:half] for x in ub[blk]]
            bot = [x[half:] for x in ub[blk]]
            for s in steps:
                sl = s - r0
                lo, hi = self.live_rows(sl)
                for p in self.pairs:
                    row = top[p][sl:sl + 1] if sl < half else bot[p][sl - half:sl - half + 1]
                    if lo == 0:
                        top[p] = top[p] + col_scr[p, s, 0:half] * row
                    if hi == sub:
                        bot[p] = bot[p] + col_scr[p, s, half:sub] * row
            ub[blk] = [jnp.concatenate([top[p], bot[p]], axis=0) for p in self.pairs]
        return [jnp.concatenate([ub[0][p], ub[1][p]], axis=0) for p in self.pairs]

    def adv_out(self, rows, g, u, slot):
        L = RW_CHUNK
        for p in self.pairs:
            v = self.v_ref[rows, self.lss[p]]
            uv = jnp.concatenate(self.halves(u[p]) + self.halves(v), axis=0).astype(BF16)
            self.y_ref[rows, self.lss[p]] = (g[p][L:2 * L] + _dot(slot[2][p].astype(BF16), uv)).astype(BF16)
        for p in self.pairs:
            ls = self.lss[p]
            cch = self.c_scr[rows, ls]
            cl = cch[self.last:self.last + 1]
            dec = jnp.exp(cl - cch)
            bk = jnp.concatenate([self.b_ref[0, rows, ls] * dec, self.k_ref[0, rows, ls] * dec], axis=0)
            upd = _dot_tn(jnp.concatenate([u[p], self.v_ref[rows, ls]], axis=0).astype(BF16),
                          bk.astype(BF16))
            self.s_ref[p] = self.s_ref[p] * jnp.exp(cl) + upd * self.blk_mask


N_RW_REFS = 8
N_RW_SCRATCH = 14


def _rw_scan_kernel(*refs, tb, npair):
    n_in = N_RW_REFS - 1
    units = []
    for d in range(2):
        ins = refs[d * n_in:(d + 1) * n_in]
        out = refs[2 * n_in + d]
        scr = refs[2 * n_in + 2 + d * N_RW_SCRATCH:2 * n_in + 2 + (d + 1) * N_RW_SCRATCH]
        units.append(_RwUnit(tuple(ins) + (out,), scr, tb=tb, npair=npair, reverse=d == 1))
    nch = tb // RW_CHUNK

    @pl.when(pl.program_id(1) == 0)
    def _():
        for un in units:
            un.reset()

    for un in units:
        un.prologue()

    def step(ci_adv, sa, ci_prep, sp):
        rows_a = [un.rows(ci_adv) for un in units]
        rows_p = [un.rows(ci_prep) for un in units]
        aa_raw = [un.prep_aa(rp) for un, rp in zip(units, rows_p)]
        g = [un.adv_g(ra) for un, ra in zip(units, rows_a)]
        aa = [un.prep_akv(rp, x, un.slots[sp]) for un, rp, x in zip(units, rows_p, aa_raw)]
        u = [un.adv_solve(x, un.slots[sa]) for un, x in zip(units, g)]
        for un, x in zip(units, aa):
            un.prep_cols(x, un.slots[sp])
        for un, ra, x, y in zip(units, rows_a, g, u):
            un.adv_out(ra, x, y, un.slots[sa])

    for un in units:
        r0 = un.rows(0)
        un.prep_cols(un.prep_akv(r0, un.prep_aa(r0), un.slots[0]), un.slots[0])

    def two_chunks(j, carry):
        c0 = 2 * j
        step(c0, 0, c0 + 1, 1)
        step(c0 + 1, 1, jnp.minimum(c0 + 2, nch - 1), 0)
        return carry

    lax.fori_loop(0, nch // 2, two_chunks, 0)


def _seq_block(d, b, i, *, n_batch, ctxb, latb):
    is_ctx = i < ctxb
    cs = jnp.where(d == 0, i, ctxb - 1 - i)
    lj = jnp.where(d == 0, i - ctxb, latb - 1 - (i - ctxb))
    return jnp.where(is_ctx, b * ctxb + cs, n_batch * ctxb + b * latb + lj)


def _rw_scan(r, v, nkk, lw, kka, km, *, tb, n_batch, ctxb, latb):
    m, c = r.shape
    npair = c // (2 * RW_HEAD)
    lanes = 2 * RW_HEAD
    in_specs, args = [], []
    for d in range(2):
        blk = _la_index(d, n_batch, ctxb, latb)
        tok = pl.BlockSpec((tb, c), lambda b, i, blk=blk: (blk(b, i), 0))
        tokd = pl.BlockSpec((1, tb, c), lambda b, i, blk=blk, d=d: (d, blk(b, i), 0))
        in_specs += [tok, tok, tok, tokd, tokd, tokd, pl.BlockSpec((tb, tb), lambda b, i: (0, 0))]
        args += [r, v, nkk, lw, kka, km, _chunk_tri(tb, d == 1, RW_CHUNK)]
    out_specs = [pl.BlockSpec((tb, c), lambda b, i, blk=_la_index(d, n_batch, ctxb, latb): (blk(b, i), 0))
                 for d in range(2)]
    buf = pltpu.VMEM((tb, c), F32)
    unit_scratch = [pltpu.VMEM((npair, lanes, lanes), F32), buf, buf, buf, buf, buf] + 2 * [
        pltpu.VMEM((npair, RW_CHUNK, RW_SUB, lanes), F32),
        pltpu.VMEM((npair, RW_CHUNK, lanes), F32),
        pltpu.VMEM((npair, RW_CHUNK, lanes), F32),
        pltpu.VMEM((npair, RW_CHUNK, lanes), F32)]
    assert len(unit_scratch) == N_RW_SCRATCH
    return pl.pallas_call(
        functools.partial(_rw_scan_kernel, tb=tb, npair=npair),
        grid=(n_batch, ctxb + latb),
        in_specs=in_specs,
        out_specs=out_specs,
        out_shape=[jax.ShapeDtypeStruct((m, c), BF16)] * 2,
        scratch_shapes=unit_scratch + unit_scratch,
        compiler_params=_cparams(2),
        name="rwkv_scan",
    )(*args)


def _rw_post(y, g, bonus, lnw, lnb, e):
    inv = 1.0 / RW_HEAD
    yc = y - _split_dot(y, e) * inv
    var = _split_dot(yc * yc, e) * inv
    yn = yc * lax.rsqrt(var + RW_LN_EPS)
    return (yn * lnw + lnb + bonus) * g


class _LaUnit:
    def __init__(self, q_ref, k_ref, v_ref, b_ref, o_ref, s_ref, *, tb, nh, dk, dv, reverse, q_scale):
        self.q_ref, self.k_ref, self.v_ref, self.b_ref, self.o_ref, self.s_ref = (
            q_ref, k_ref, v_ref, b_ref, o_ref, s_ref)
        self.nch = tb // LA_CHUNK
        self.dk, self.dv, self.reverse, self.q_scale = dk, dv, reverse, q_scale
        self.lanes = 128
        self.pack = self.lanes // dk
        self.groups = range(nh // self.pack)
        self.rowi = lax.broadcasted_iota(jnp.int32, (LA_CHUNK, 1), 0)
        lane = lax.broadcasted_iota(jnp.int32, (1, self.lanes), 1)
        self.head_lanes = [(lane // dk == j).astype(F32) for j in range(self.pack)]
        if self.pack > 1:
            rows = lax.broadcasted_iota(jnp.int32, (self.pack * dv, self.lanes), 0)
            cols = lax.broadcasted_iota(jnp.int32, (self.pack * dv, self.lanes), 1)
            self.blk_mask = ((rows // dv) == (cols // dk)).astype(F32)

    def reset(self):
        self.s_ref[...] = jnp.zeros_like(self.s_ref)

    def load(self, ci):
        cc = (self.nch - 1 - ci) if self.reverse else ci
        rows = pl.ds(pl.multiple_of(cc * LA_CHUNK, LA_CHUNK), LA_CHUNK)
        pack, dv, lanes = self.pack, self.dv, self.lanes
        out = []
        for g in self.groups:
            ks = slice(g * lanes, (g + 1) * lanes)
            q = self.q_ref[rows, ks].astype(F32) * self.q_scale
            k = self.k_ref[rows, ks].astype(F32)
            v = self.v_ref[rows, g * pack * dv:(g + 1) * pack * dv].astype(F32)
            b = self.b_ref[rows, ks]
            o_inter = _dot_nt((q * jnp.exp(b)).astype(BF16), self.s_ref[g].astype(BF16))
            out.append((rows, q, k, v, b, [o_inter[:, j * dv:(j + 1) * dv] for j in range(pack)]))
        return out

    def intra(self, ops):
        pack, dv, half = self.pack, self.dv, LA_CHUNK // 2
        rowi = self.rowi[0:half]
        for g in self.groups:
            rows, q, k, v, b, o = ops[g]
            qh = [q[0:half], q[half:]]
            bh = [b[0:half], b[half:]]
            oh = [[oj[0:half], oj[half:]] for oj in o]
            for s in range(LA_CHUNK):
                for h in range(2):
                    lo_row, hi_row = h * half, (h + 1) * half - 1
                    if (hi_row > s) if self.reverse else (lo_row < s):
                        if (lo_row > s) if self.reverse else (hi_row < s):
                            continue
                        valid = (rowi + lo_row <= s) if self.reverse else (rowi + lo_row >= s)
                    else:
                        valid = None
                    term = (qh[h] * k[s:s + 1]) * jnp.exp(bh[h] - b[s:s + 1])
                    for j in range(pack):
                        tj = term if pack == 1 else term * self.head_lanes[j]
                        col = jnp.sum(tj, axis=-1, keepdims=True)
                        if valid is not None:
                            col = jnp.where(valid, col, 0.0)
                        oh[j][h] = oh[j][h] + col * v[s:s + 1, j * dv:(j + 1) * dv]
            for j in range(pack):
                self.o_ref[rows, (g * pack + j) * dv:(g * pack + j + 1) * dv] = jnp.concatenate(
                    oh[j], axis=0).astype(BF16)

    def update(self, ops):
        for g in self.groups:
            _, _, k, v, b, _ = ops[g]
            b_last = b[0:1] if self.reverse else b[LA_CHUNK - 1:LA_CHUNK]
            upd = _dot_tn(v.astype(BF16), (k * jnp.exp(b_last - b)).astype(BF16))
            if self.pack > 1:
                upd = upd * self.blk_mask
            self.s_ref[g] = self.s_ref[g] * jnp.exp(b_last) + upd


def _la_run(units, nch):
    def chunk(ci, carry):
        ops = [un.load(ci) for un in units]
        for un, x in zip(units, ops):
            un.update(x)
        for un, x in zip(units, ops):
            un.intra(x)
        return carry

    lax.fori_loop(0, nch, chunk, 0)


def _hg_kernel(qf_ref, ff_ref, vf_ref, trif_ref, qb_ref, fb_ref, vb_ref, trib_ref, gamma_ref,
               of_ref, ob_ref, sf_ref, kf_scr, bf_scr, sb_ref, kb_scr, bb_scr, *, layer, tb, nh, dk, dv):
    gam = gamma_ref[...]
    ex = jnp.exp(gam - jnp.max(gam, axis=0, keepdims=True))
    p = ex / jnp.sum(ex, axis=0, keepdims=True)
    cum = p[0:1]
    for i in range(1, layer + 1):
        cum = cum + p[i:i + 1]
    lb = cum - p[0:1]
    lo = jnp.log(lb)
    l1 = jnp.log(1.0 - lb)

    units = []
    for d, (q_ref, f_ref, v_ref, tri_ref, o_ref, s_ref, k_scr, b_scr) in enumerate((
            (qf_ref, ff_ref, vf_ref, trif_ref, of_ref, sf_ref, kf_scr, bf_scr),
            (qb_ref, fb_ref, vb_ref, trib_ref, ob_ref, sb_ref, kb_scr, bb_scr))):
        hi = l1 + _log_sigmoid(f_ref[...].astype(F32))
        mx = jnp.maximum(lo, hi)
        mn = jnp.minimum(lo, hi)
        log_f = mx + jnp.log(1.0 + jnp.exp(mn - mx))
        k_scr[...] = 1.0 - jnp.exp(log_f)
        b_scr[...] = _split3_dot(tri_ref[...], log_f)
        units.append(_LaUnit(q_ref, k_scr, v_ref, b_scr, o_ref, s_ref, tb=tb, nh=nh, dk=dk, dv=dv,
                             reverse=d == 1, q_scale=1.0))

    @pl.when(pl.program_id(1) == 0)
    def _():
        for un in units:
            un.reset()

    _la_run(units, tb // LA_CHUNK)


def _gla_kernel(qkf_ref, vf_ref, smallf_ref, trif_ref, qkb_ref, vb_ref, smallb_ref, trib_ref,
                gw_ref, gb_ref, of_ref, ob_ref, sf_ref, bf_scr, sb_ref, bb_scr, *, tb, nh, dk, dv):
    hk = nh * dk
    units = []
    for d, (qk_ref, v_ref, small_ref, tri_ref, o_ref, s_ref, b_scr) in enumerate((
            (qkf_ref, vf_ref, smallf_ref, trif_ref, of_ref, sf_ref, bf_scr),
            (qkb_ref, vb_ref, smallb_ref, trib_ref, ob_ref, sb_ref, bb_scr))):
        code = small_ref[:, 384:512]
        pre = _dot(code.astype(BF16), gw_ref[:, d * hk:(d + 1) * hk]) + gb_ref[:, d * hk:(d + 1) * hk]
        log_g = _log_sigmoid(pre) * (1.0 / GLA_GATE_NORM)
        b_scr[...] = _split3_dot(tri_ref[...], log_g)
        units.append(_LaUnit(qk_ref.at[:, 0:hk], qk_ref.at[:, hk:2 * hk], v_ref, b_scr, o_ref, s_ref,
                             tb=tb, nh=nh, dk=dk, dv=dv, reverse=d == 1, q_scale=dk ** -0.5))

    @pl.when(pl.program_id(1) == 0)
    def _():
        for un in units:
            un.reset()

    _la_run(units, tb // LA_CHUNK)


def _chunk_tri(tb, reverse, chunk=LA_CHUNK):
    t = jnp.arange(tb)
    same = (t[:, None] // chunk) == (t[None, :] // chunk)
    tri = (t[None, :] >= t[:, None]) if reverse else (t[None, :] <= t[:, None])
    return (same & tri).astype(BF16)


def _la_index(d, n_batch, ctxb, latb):
    return lambda b, i: _seq_block(d, b, i, n_batch=n_batch, ctxb=ctxb, latb=latb)


def _hgrn2(p_all, gamma, *, layer, tb, n_batch, ctxb, latb):
    m = p_all.shape[0]
    c = gamma.shape[-1]
    nh = c // HG_DK
    in_specs, args, out_specs = [], [], []
    for d in range(2):
        blk = _la_index(d, n_batch, ctxb, latb)
        col = lambda off, blk=blk: pl.BlockSpec((tb, c), lambda b, i: (blk(b, i), off // c))
        in_specs += [col(C_HGQ), col(C_HGF + d * c), col(C_HGI), pl.BlockSpec((tb, tb), lambda b, i: (0, 0))]
        args += [p_all, p_all, p_all, _chunk_tri(tb, d == 1)]
        out_specs.append(col(0))
    unit_scratch = [pltpu.VMEM((nh, HG_DK, HG_DK), F32), pltpu.VMEM((tb, c), F32), pltpu.VMEM((tb, c), F32)]
    return pl.pallas_call(
        functools.partial(_hg_kernel, layer=layer, tb=tb, nh=nh, dk=HG_DK, dv=HG_DK),
        grid=(n_batch, ctxb + latb),
        in_specs=in_specs + [pl.BlockSpec(gamma.shape, lambda b, i: (0, 0))],
        out_specs=out_specs,
        out_shape=[jax.ShapeDtypeStruct((m, c), BF16)] * 2,
        scratch_shapes=unit_scratch + unit_scratch,
        compiler_params=_cparams(2),
        name="hgrn2_scan",
    )(*args, gamma)


def _gla(p_all, gw, gb, *, tb, n_batch, ctxb, latb):
    m = p_all.shape[0]
    hk = gw.shape[-1] // 2
    dk = hk // GLA_HEADS
    c = 2 * hk
    dv = c // GLA_HEADS
    in_specs, args, out_specs = [], [], []
    for d in range(2):
        blk = _la_index(d, n_batch, ctxb, latb)
        col = lambda off, blk=blk: pl.BlockSpec((tb, c), lambda b, i: (blk(b, i), off // c))
        in_specs += [col(C_GLQK), col(C_GLV), col(C_SMALL), pl.BlockSpec((tb, tb), lambda b, i: (0, 0))]
        args += [p_all, p_all, p_all, _chunk_tri(tb, d == 1)]
        out_specs.append(col(0))
    unit_scratch = [pltpu.VMEM((hk // 128, (128 // dk) * dv, 128), F32), pltpu.VMEM((tb, hk), F32)]
    return pl.pallas_call(
        functools.partial(_gla_kernel, tb=tb, nh=GLA_HEADS, dk=dk, dv=dv),
        grid=(n_batch, ctxb + latb),
        in_specs=in_specs + [pl.BlockSpec(gw.shape, lambda b, i: (0, 0)),
                             pl.BlockSpec(gb.shape, lambda b, i: (0, 0))],
        out_specs=out_specs,
        out_shape=[jax.ShapeDtypeStruct((m, c), BF16)] * 2,
        scratch_shapes=unit_scratch + unit_scratch,
        compiler_params=_cparams(2),
        name="gla_scan",
    )(*args, gw, gb)


def _la_post(o, gate, norm_g, nh):
    dv = o.shape[-1] // nh
    outs = []
    for h in range(nh):
        oh = o[:, h * dv:(h + 1) * dv]
        outs.append(oh * lax.rsqrt(jnp.mean(oh * oh, axis=-1, keepdims=True) + EPS))
    return jnp.concatenate(outs, axis=-1) * norm_g * (gate * _sigmoid(gate))


def _gelu(x):
    return 0.5 * x * (1.0 + jnp.tanh(0.7978845608028654 * (x + 0.044715 * (x * x * x))))


def _sgu_kernel(u_ref, v_ref, lnw_ref, lnb_ref, ws_ref, bs_ref, o_ref, *, rb):
    u = _gelu(u_ref[...].astype(F32))
    v = _gelu(v_ref[...].astype(F32))
    vc = v - jnp.mean(v, axis=-1, keepdims=True)
    vn = vc * lax.rsqrt(jnp.mean(vc * vc, axis=-1, keepdims=True) + EPS)
    vn = (vn * lnw_ref[...] + lnb_ref[...]).astype(BF16)
    gw = vn.shape[-1] // SGU_GROUPS
    for n in range(rb // SGU_CHUNK):
        rs = slice(n * SGU_CHUNK, (n + 1) * SGU_CHUNK)
        for g in range(SGU_GROUPS):
            cs = slice(g * gw, (g + 1) * gw)
            s = _dot(ws_ref[g], vn[rs, cs]) + bs_ref[g]
            o_ref[rs, cs] = u[rs, cs] * s


def _sgu(p_all, lnw, lnb, ws, bs, *, rb):
    m = p_all.shape[0]
    c = lnw.shape[-1]
    return pl.pallas_call(
        functools.partial(_sgu_kernel, rb=rb),
        grid=(m // rb,),
        in_specs=[pl.BlockSpec((rb, c), lambda i: (i, C_SGU // c)),
                  pl.BlockSpec((rb, c), lambda i: (i, C_SGU // c + 1)),
                  pl.BlockSpec((1, c), lambda i: (0, 0)),
                  pl.BlockSpec((1, c), lambda i: (0, 0)),
                  pl.BlockSpec(ws.shape, lambda i: (0, 0, 0)),
                  pl.BlockSpec(bs.shape, lambda i: (0, 0, 0))],
        out_specs=pl.BlockSpec((rb, c), lambda i: (i, 0)),
        out_shape=jax.ShapeDtypeStruct((m, c), F32),
        compiler_params=_cparams(1),
        name="sgu",
    )(p_all, p_all, lnw, lnb, ws, bs)


def _merge_kernel(raf_ref, rab_ref, rg_ref, rbonus_ref, hf_ref, hb_ref, hgate_ref, gf_ref, gb_ref,
                  ggate_ref, yd_ref, lnw_ref, lnb_ref, e_ref, hnorm_ref, gnorm_ref,
                  g0_ref, g1_ref, g2_ref, g3_ref, w_ref, o_ref, y_scr, *, hg_heads):
    @pl.when(pl.program_id(1) == 0)
    def _():
        both = lambda f_ref, b_ref: f_ref[...].astype(F32) + b_ref[...].astype(F32)
        y_scr[0] = _rw_post(both(raf_ref, rab_ref), rg_ref[...], rbonus_ref[...], lnw_ref[...],
                            lnb_ref[...], e_ref[...]).astype(BF16)
        y_scr[1] = _la_post(both(hf_ref, hb_ref), hgate_ref[...].astype(F32), hnorm_ref[...],
                            hg_heads).astype(BF16)
        y_scr[2] = _la_post(both(gf_ref, gb_ref), ggate_ref[...].astype(F32), gnorm_ref[...],
                            GLA_HEADS).astype(BF16)
        y_scr[3] = yd_ref[...].astype(BF16)

    gs = (g0_ref, g1_ref, g2_ref, g3_ref)
    acc = None
    for j in range(N_BRANCH):
        t = _sigmoid(gs[j][...].astype(F32)) * _dot(y_scr[j], w_ref[j])
        acc = t if acc is None else acc + t
    o_ref[...] = acc.astype(BF16)


def _merge(rw, hg, gla, yd, p_gate, p_mix, vecs, e, w_branch, *, layer, tm, row_off):
    m = p_mix.shape[0]
    _, _, c, d = w_branch.shape
    tn = _pick((1024, 512), d)
    mo = m - row_off * tm
    ytok = pl.BlockSpec((tm, c), lambda i, j: (i + row_off, 0))
    mix = lambda off: pl.BlockSpec((tm, c), lambda i, j: (i + row_off, off // c))
    gate = lambda b: pl.BlockSpec((tm, tn), lambda i, j: (i + row_off, b * d // tn + j))
    vec = pl.BlockSpec((1, c), lambda i, j: (0, 0))
    return pl.pallas_call(
        functools.partial(_merge_kernel, hg_heads=c // HG_DK),
        grid=(mo // tm, d // tn),
        in_specs=[ytok, ytok, ytok, ytok, ytok, ytok, mix(C_HGG), ytok, ytok, mix(C_GLG), ytok,
                  vec, vec, pl.BlockSpec((c, c), lambda i, j: (0, 0)), vec, vec,
                  gate(0), gate(1), gate(2), gate(3),
                  pl.BlockSpec((None, N_BRANCH, c, tn), lambda i, j: (layer, 0, 0, j))],
        out_specs=pl.BlockSpec((tm, tn), lambda i, j: (i, j)),
        out_shape=jax.ShapeDtypeStruct((mo, d), BF16),
        scratch_shapes=[pltpu.VMEM((N_BRANCH, tm, c), BF16)],
        compiler_params=_cparams(2),
        name="merge",
    )(*rw, hg[0], hg[1], p_mix, gla[0], gla[1], p_mix, yd, *vecs[:2], e, *vecs[2:],
      p_gate, p_gate, p_gate, p_gate, w_branch)


def _outproj_kernel(m_ref, w_ref, xc_ref, xl_ref, mod_ref, o_ref, *, nctx, row_off):
    x = jnp.where(pl.program_id(0) + row_off < nctx, xc_ref[...], xl_ref[...])
    o_ref[...] = x + mod_ref[0, 2:3, :] * _dot(m_ref[...], w_ref[...])


def _outproj(mm, w_out, x_parts, mod, *, layer, tm, row_off, nctx, bpb):
    xc, xl, lat_off = x_parts
    mo, d = mm.shape
    tn = _pick((1024, 512), d)
    xspecs = [pl.BlockSpec((tm, tn), lambda i, j: (jnp.minimum(i + row_off, nctx - 1),
                                                   jnp.where(i + row_off < nctx, j, 0))),
              pl.BlockSpec((tm, tn), lambda i, j: (lat_off + jnp.maximum(i + row_off - nctx, 0),
                                                   jnp.where(i + row_off < nctx, 0, j)))]
    return pl.pallas_call(
        functools.partial(_outproj_kernel, nctx=nctx, row_off=row_off),
        grid=(mo // tm, d // tn),
        in_specs=[pl.BlockSpec((tm, d), lambda i, j: (i, 0)),
                  pl.BlockSpec((None, d, tn), lambda i, j: (layer, 0, j))] + xspecs + [
                  pl.BlockSpec((1, 6, tn), lambda i, j: (_row_group(i + row_off, nctx, bpb), 0, j))],
        out_specs=pl.BlockSpec((tm, tn), lambda i, j: (i, j)),
        out_shape=jax.ShapeDtypeStruct((mo, d), F32),
        compiler_params=_cparams(2),
        name="outproj",
    )(mm, w_out, xc, xl, mod)


def _mlp_kernel(x_ref, mod_ref, g_ref, w1_ref, w2_ref, gf_ref, o_ref, h_ref, acc_ref, *, final_norm):
    j = pl.program_id(1)

    @pl.when(j == 0)
    def _():
        h = _norm_mod(x_ref[...], g_ref[...], mod_ref[0, 3:4, :], mod_ref[0, 4:5, :])
        h_ref[...] = h.astype(BF16)
        acc_ref[...] = jnp.zeros_like(acc_ref)

    a = jnp.maximum(_dot(h_ref[...], w1_ref[...]), 0.0)
    acc_ref[...] += _dot((a * a).astype(BF16), w2_ref[...])

    @pl.when(j == pl.num_programs(1) - 1)
    def _():
        y = x_ref[...] + mod_ref[0, 5:6, :] * acc_ref[...]
        if final_norm:
            y = y * lax.rsqrt(jnp.mean(y * y, axis=-1, keepdims=True) + EPS) * gf_ref[...]
        o_ref[...] = y


def _mlp(x_in, mod, g, w1, w2, g_final, *, layer, tm, row_off, nctx, bpb, final_norm):
    mo, d = x_in.shape
    hid = w1.shape[-1]
    th = _pick((1024, 512, 256, 128), hid)
    return pl.pallas_call(
        functools.partial(_mlp_kernel, final_norm=final_norm),
        grid=(mo // tm, hid // th),
        in_specs=[pl.BlockSpec((tm, d), lambda i, j: (i, 0)),
                  pl.BlockSpec((1, 6, d), lambda i, j: (_row_group(i + row_off, nctx, bpb), 0, 0)),
                  pl.BlockSpec((1, d), lambda i, j: (0, 0)),
                  pl.BlockSpec((None, d, th), lambda i, j: (layer, 0, j)),
                  pl.BlockSpec((None, th, d), lambda i, j: (layer, j, 0)),
                  pl.BlockSpec((1, d), lambda i, j: (0, 0))],
        out_specs=pl.BlockSpec((tm, d), lambda i, j: (i, 0)),
        out_shape=jax.ShapeDtypeStruct((mo, d), F32),
        scratch_shapes=[pltpu.VMEM((tm, d), BF16), pltpu.VMEM((tm, d), F32)],
        compiler_params=_cparams(2),
        name="mlp",
    )(x_in, mod, g, w1, w2, g_final)


def _blockdiag2(w):
    _, r, c = w.shape
    z = jnp.zeros((r, c), w.dtype)
    out = jnp.concatenate([jnp.concatenate([w[0], z], axis=1), jnp.concatenate([z, w[1]], axis=1)], axis=0)
    return jnp.pad(out, ((0, 128 - 2 * r), (0, 0))).astype(BF16)


def _permute_w_in(w, d_model):
    c = d_model // N_BRANCH
    o = [0]
    for wd in (3 * c, 64 * 2, 64 * 2, 128, c, 2 * c, c, c, c, c, 32, c, 2 * c, N_BRANCH * d_model):
        o.append(o[-1] + wd)
    seg = lambda k: w[:, o[k]:o[k + 1]]
    pad = jnp.zeros((w.shape[0], 512 - 128 * 3 - 32), w.dtype)
    parts = [seg(13), seg(0), seg(5), seg(12), seg(4), seg(6), seg(7), seg(8), seg(9), seg(11),
             seg(1), seg(2), seg(3), seg(10), pad]
    out = jnp.concatenate(parts, axis=1).astype(BF16)
    assert out.shape[1] == N_GATE + N_MIX
    return out


def kernel(x, c, ctx, c_ctx, w_ada, b_ada, g_norm1, g_norm2, g_final, w_in, rw_conv, rw_w0, rw_w2,
           rw_a0, rw_a2, rw_g2, rw_kk, rw_ka, rw_rk, rw_ln_w, rw_ln_b, hg_gamma, hg_norm, gla_gw,
           gla_gb, gla_norm, sgu_ln_w, sgu_ln_b, sgu_w, sgu_b, w_branch, w_out, w_mlp1, w_mlp2):
    n_batch, seq, d_model = x.shape
    ctx_len = ctx.shape[1]
    depth = w_in.shape[0]
    cw = d_model // N_BRANCH
    assert cw == 512 and d_model == 2048, "column layout constants assume D_MODEL = 2048"
    m_ctx = n_batch * ctx_len

    tm = _pick((1024, 512, 256, 128), m_ctx, seq)
    tb = _pick((256, 128), ctx_len, seq)
    nctx, bpb = m_ctx // tm, seq // tm
    ctxb, latb = ctx_len // tb, seq // tb
    seqs = dict(tb=tb, n_batch=n_batch, ctxb=ctxb, latb=latb)

    x_parts = (ctx.reshape(m_ctx, d_model), x.reshape(n_batch * seq, d_model), 0)
    c_rows = jnp.concatenate([c_ctx[None, :], c, jnp.zeros((7 - n_batch, d_model), F32)], axis=0)
    mod_all = _ada(c_rows, w_ada, b_ada).reshape(depth, 8, 6, d_model)

    head_ones = (jnp.arange(cw)[:, None] // RW_HEAD == jnp.arange(cw)[None, :] // RW_HEAD).astype(BF16)
    w_branch_b, w_out_b = w_branch.astype(BF16), w_out.astype(BF16)
    w_mlp1_b, w_mlp2_b = w_mlp1.astype(BF16), w_mlp2.astype(BF16)
    row = lambda a: a.reshape(1, -1)

    for l in range(depth):
        last = l == depth - 1
        mod = mod_all[l]
        p_gate, p_all = _inproj(x_parts, mod, row(g_norm1[l]), _permute_w_in(w_in[l], d_model),
                                tm=tm, nctx=nctx, bpb=bpb)

        prm = dict(conv=rw_conv[l], w0=row(rw_w0[l]), w2=_blockdiag2(rw_w2[l]), a0=row(rw_a0[l]),
                   a2=_blockdiag2(rw_a2[l]), g2=rw_g2[l].astype(BF16), kk=row(rw_kk[l]),
                   ka=row(rw_ka[l]), rk=row(rw_rk[l]), e=head_ones)
        r, v, nkk, g, bonus, w, kka, km = _rw_prep(p_all, prm, tb=tb, nctx=m_ctx // tb,
                                                    ctx_bps=ctxb, lat_bps=latb)
        ys = _rw_scan(r, v, nkk, w, kka, km, **seqs)

        ob = _hgrn2(p_all, hg_gamma, layer=l, **seqs)

        gw = _blockdiag2(gla_gw[l])
        gb = row(gla_gb[l])
        oc = _gla(p_all, gw, gb, **seqs)

        bs = jnp.broadcast_to(sgu_b[l][:, :, None], sgu_w[l].shape)
        yd = _sgu(p_all, row(sgu_ln_w[l]), row(sgu_ln_b[l]), sgu_w[l].astype(BF16), bs,
                  rb=_pick((512, 256, 128), m_ctx, seq))

        row_off = nctx if last else 0
        vecs = (row(rw_ln_w[l]), row(rw_ln_b[l]), row(hg_norm[l]), row(gla_norm[l]))
        mm = _merge((ys[0], ys[1], g, bonus), ob, oc, yd, p_gate, p_all, vecs, head_ones,
                    w_branch_b, layer=l, tm=tm // 2, row_off=2 * row_off)
        x_mid = _outproj(mm, w_out_b, x_parts, mod, layer=l, tm=tm, row_off=row_off, nctx=nctx, bpb=bpb)
        x_all = _mlp(x_mid, mod, row(g_norm2[l]), w_mlp1_b, w_mlp2_b, row(g_final), layer=l,
                     tm=tm // 2, row_off=2 * row_off, nctx=2 * nctx, bpb=2 * bpb, final_norm=last)
        x_parts = (x_all, x_all, nctx)
    return x_all.reshape(n_batch, seq, d_model)
```

```python
import functools

import jax
import jax.numpy as jnp
from jax import lax
from jax.experimental import pallas as pl
from jax.experimental.pallas import tpu as pltpu

F32 = jnp.float32
BF16 = jnp.bfloat16

N_BRANCH = 4
RW_HEAD = 64
RW_LN_EPS = 64e-5
HG_DK = 128
GLA_HEADS = 4
GLA_GATE_NORM = 16.0
LA_CHUNK = 16
RW_CHUNK = 32
RW_SUB = 16
SGU_CHUNK = 128
SGU_GROUPS = 4
EPS = 1e-6
HALO = 16

C_RKV = 0
C_HGF = 1536
C_SGU = 2560
C_HGQ = 3584
C_HGI = 4096
C_HGG = 4608
C_GLQK = 5120
C_GLV = 5632
C_GLG = 6144
C_SMALL = 6656
N_MIX = 7168
N_GATE = 8192

VMEM_LIMIT = 58 * 1024 * 1024


def _cparams(n_axes):
    return pltpu.CompilerParams(dimension_semantics=("arbitrary",) * n_axes,
                                vmem_limit_bytes=VMEM_LIMIT)


def _pick(n_list, *dims):
    for n in n_list:
        if all(d % n == 0 for d in dims):
            return n
    raise ValueError(f"no block size in {n_list} divides {dims}")


def _row_group(i, nctx, bpb):
    return jnp.where(i < nctx, 0, 1 + (i - nctx) // bpb)


def _dot(a, b):
    return jnp.dot(a, b, preferred_element_type=F32)


def _dot_nt(a, b):
    return lax.dot_general(a, b, (((1,), (1,)), ((), ())), preferred_element_type=F32)


def _dot_tn(a, b):
    return lax.dot_general(a, b, (((0,), (0,)), ((), ())), preferred_element_type=F32)


def _split_dot(x, e):
    hi = x.astype(BF16)
    lo = (x - hi.astype(F32)).astype(BF16)
    return _dot(hi, e) + _dot(lo, e)


def _split3_dot(e, x):
    p1 = x.astype(BF16)
    r1 = x - p1.astype(F32)
    p2 = r1.astype(BF16)
    p3 = (r1 - p2.astype(F32)).astype(BF16)
    return _dot(e, p1) + _dot(e, p2) + _dot(e, p3)


def _log_sigmoid(x):
    return jnp.minimum(x, 0.0) - jnp.log(1.0 + jnp.exp(-jnp.abs(x)))


def _sigmoid(x):
    return 0.5 * jnp.tanh(0.5 * x) + 0.5


def _ada_kernel(c_ref, w_ref, b_ref, o_ref):
    c = c_ref[...]
    act = c * _sigmoid(c)
    part = _dot(act.astype(BF16), w_ref[0].astype(BF16))

    @pl.when(pl.program_id(1) == 0)
    def _():
        o_ref[0] = part + b_ref[0]

    @pl.when(pl.program_id(1) > 0)
    def _():
        o_ref[0] += part


def _ada(c_rows, w_ada, b_ada):
    n_layers, d, n = w_ada.shape
    tk = _pick((256, 128), d)
    return pl.pallas_call(
        _ada_kernel,
        grid=(n_layers, d // tk),
        in_specs=[pl.BlockSpec((8, tk), lambda l, k: (0, k)),
                  pl.BlockSpec((1, tk, n), lambda l, k: (l, k, 0)),
                  pl.BlockSpec((1, 1, n), lambda l, k: (l, 0, 0))],
        out_specs=pl.BlockSpec((1, 8, n), lambda l, k: (l, 0, 0)),
        out_shape=jax.ShapeDtypeStruct((n_layers, 8, n), F32),
        compiler_params=_cparams(2),
        name="ada_mod",
    )(c_rows, w_ada, b_ada.reshape(n_layers, 1, n))


def _norm_mod(x, g, shift, scale):
    y = x * lax.rsqrt(jnp.mean(x * x, axis=-1, keepdims=True) + EPS) * g
    return y * (1.0 + scale) + shift


def _two_part_specs(block, nctx, lat_off):
    return [pl.BlockSpec(block, lambda i, j: (jnp.minimum(i, nctx - 1), 0), pipeline_mode=pl.Buffered(1)),
            pl.BlockSpec(block, lambda i, j: (lat_off + jnp.maximum(i - nctx, 0), 0))]


def _inproj_kernel(xc_ref, xl_ref, mod_ref, g_ref, w_ref, og_ref, om_ref, h_ref, *, nctx, n_gate_tiles):
    i, j = pl.program_id(0), pl.program_id(1)

    @pl.when(j == 0)
    def _():
        x = jnp.where(i < nctx, xc_ref[...], xl_ref[...])
        h = _norm_mod(x, g_ref[...], mod_ref[0, 0:1, :], mod_ref[0, 1:2, :])
        h_ref[...] = h.astype(BF16)

    @pl.when(j < n_gate_tiles)
    def _():
        og_ref[...] = _dot(h_ref[...], w_ref[...]).astype(BF16)

    @pl.when(j >= n_gate_tiles)
    def _():
        om_ref[...] = _dot(h_ref[...], w_ref[...]).astype(BF16)


def _inproj(x_parts, mod, g, w, *, tm, nctx, bpb):
    xc, xl, lat_off = x_parts
    d = xc.shape[1]
    m = (nctx + (xl.shape[0] // tm - lat_off)) * tm
    tn = _pick((1024, 512), N_GATE, N_MIX)
    ng = N_GATE // tn
    return pl.pallas_call(
        functools.partial(_inproj_kernel, nctx=nctx, n_gate_tiles=ng),
        grid=(m // tm, (N_GATE + N_MIX) // tn),
        in_specs=_two_part_specs((tm, d), nctx, lat_off) + [
            pl.BlockSpec((1, 6, d), lambda i, j: (_row_group(i, nctx, bpb), 0, 0)),
            pl.BlockSpec((1, d), lambda i, j: (0, 0)),
            pl.BlockSpec((d, tn), lambda i, j: (0, j))],
        out_specs=[pl.BlockSpec((tm, tn), lambda i, j: (i, jnp.minimum(j, ng - 1))),
                   pl.BlockSpec((tm, tn), lambda i, j: (i, jnp.maximum(j - ng, 0)))],
        out_shape=[jax.ShapeDtypeStruct((m, N_GATE), BF16), jax.ShapeDtypeStruct((m, N_MIX), BF16)],
        scratch_shapes=[pltpu.VMEM((tm, d), BF16)],
        compiler_params=_cparams(2),
        name="inproj",
    )(xc, xl, mod, g, w)


def _rw_prep_kernel(rkv_ref, prev_ref, next_ref, small_ref, conv_ref, w0_ref, w2_ref, a0_ref,
                    a2_ref, g2_ref, kk_ref, ka_ref, rk_ref, e_ref,
                    r_out, v_out, nkk_out, g_out, bonus_out, w_out, kka_out, km_out,
                    *, tb, nctx, ctx_bps, lat_bps):
    i = pl.program_id(0)
    c = r_out.shape[-1]
    j = jnp.where(i < nctx, i, i - nctx)
    bps = jnp.where(i < nctx, ctx_bps, lat_bps)
    first = lax.rem(j, bps) == 0
    last = lax.rem(j, bps) == bps - 1

    blk = rkv_ref[...].astype(F32)
    rows = lax.broadcasted_iota(jnp.int32, (tb, 1), 0)
    prev_row = jnp.where(first, 0.0, prev_ref[HALO - 1:HALO, :].astype(F32))
    next_row = jnp.where(last, 0.0, next_ref[0:1, :].astype(F32))
    xm1 = jnp.where(rows == 0, prev_row, pltpu.roll(blk, 1, 0))
    xp1 = jnp.where(rows == tb - 1, next_row, pltpu.roll(blk, tb - 1, 0))
    conv = conv_ref[0:1, :] * xm1 + conv_ref[1:2, :] * blk + conv_ref[2:3, :] * xp1
    r = conv[:, 0:c]
    k = conv[:, c:2 * c]
    v = conv[:, 2 * c:3 * c]

    small = small_ref[...].astype(F32)
    wl = small[:, 0:128]
    al = small[:, 128:256]
    gl = small[:, 256:384]
    w_pre = w0_ref[...] + _dot(jnp.tanh(wl).astype(BF16), w2_ref[...])
    softplus = jnp.maximum(-w_pre, 0.0) + jnp.log(1.0 + jnp.exp(-jnp.abs(w_pre)))
    log_decay = -jnp.exp(-softplus - 0.5)
    a = _sigmoid(a0_ref[...] + _dot(al.astype(BF16), a2_ref[...]))
    g = _dot(_sigmoid(gl).astype(BF16), g2_ref[...])

    e = e_ref[...]
    kkv = k * kk_ref[...]
    kk = kkv * lax.rsqrt(_split_dot(kkv * kkv, e) + 1e-12)
    bonus = _split_dot(r * k * rk_ref[...], e) * v

    r_out[...] = r.astype(BF16)
    v_out[...] = v.astype(BF16)
    nkk_out[...] = (-kk).astype(BF16)
    g_out[...] = g
    bonus_out[...] = bonus
    for d in range(2):
        a_d = a[:, d * c:(d + 1) * c]
        w_out[d] = log_decay[:, d * c:(d + 1) * c]
        kka_out[d] = (kk * a_d).astype(BF16)
        km_out[d] = (k * (1.0 + (a_d - 1.0) * ka_ref[...])).astype(BF16)


def _rw_prep(p_all, prm, *, tb, nctx, ctx_bps, lat_bps):
    m = p_all.shape[0]
    c = prm["kk"].shape[-1]
    nblk = m // tb
    th = tb // HALO
    full = lambda shape: pl.BlockSpec(shape, lambda i: (0,) * len(shape))
    tok = pl.BlockSpec((tb, c), lambda i: (i, 0))
    tok2 = pl.BlockSpec((2, tb, c), lambda i: (0, i, 0))
    one, one_b = jax.ShapeDtypeStruct((m, c), F32), jax.ShapeDtypeStruct((m, c), BF16)
    two, two_b = jax.ShapeDtypeStruct((2, m, c), F32), jax.ShapeDtypeStruct((2, m, c), BF16)
    return pl.pallas_call(
        functools.partial(_rw_prep_kernel, tb=tb, nctx=nctx, ctx_bps=ctx_bps, lat_bps=lat_bps),
        grid=(nblk,),
        in_specs=[pl.BlockSpec((tb, 3 * c), lambda i: (i, C_RKV // (3 * c))),
                  pl.BlockSpec((HALO, 3 * c), lambda i: (jnp.maximum(i * th - 1, 0), 0)),
                  pl.BlockSpec((HALO, 3 * c), lambda i: (jnp.minimum((i + 1) * th, m // HALO - 1), 0)),
                  pl.BlockSpec((tb, 512), lambda i: (i, C_SMALL // 512)),
                  full((3, 3 * c)), full((1, 2 * c)), full((128, 2 * c)), full((1, 2 * c)),
                  full((128, 2 * c)), full((128, c)), full((1, c)), full((1, c)), full((1, c)),
                  full((c, c))],
        out_specs=[tok, tok, tok, tok, tok, tok2, tok2, tok2],
        out_shape=[one_b, one_b, one_b, one, one, two, two_b, two_b],
        compiler_params=_cparams(1),
        name="rwkv_prep",
    )(p_all, p_all, p_all, p_all, prm["conv"], prm["w0"], prm["w2"], prm["a0"], prm["a2"],
      prm["g2"], prm["kk"], prm["ka"], prm["rk"], prm["e"])


class _RwUnit:
    def __init__(self, refs, scratch, *, tb, npair, reverse):
        (self.r_ref, self.v_ref, self.a_ref, self.lw_ref, self.b_ref, self.k_ref, self.tri_ref,
         self.y_ref) = refs
        (self.s_ref, self.c_scr, self.ag_scr, self.rg_scr, self.bg_scr, self.kg_scr) = scratch[:6]
        self.slots = (scratch[6:10], scratch[10:14])
        self.reverse = reverse
        self.nch = tb // RW_CHUNK
        self.pairs = range(npair)
        L, hd = RW_CHUNK, RW_HEAD
        self.lss = [slice(p * 2 * hd, (p + 1) * 2 * hd) for p in self.pairs]
        lane = lax.broadcasted_iota(jnp.int32, (1, 2 * hd), 1)
        self.lo = (lane < hd).astype(F32)
        self.hi = 1.0 - self.lo
        ti = lax.broadcasted_iota(jnp.int32, (2 * L, 2 * hd), 0)
        sl = lax.broadcasted_iota(jnp.int32, (2 * L, 2 * hd), 1)
        sj = sl & (L - 1)
        tt = ti & (L - 1)
        earlier = (sj > tt) if reverse else (sj < tt)
        assert 4 * L == 2 * hd and L == 2 * RW_SUB
        self.aa_mask = (earlier | ((ti >= L) & (sj == tt))).astype(F32)
        self.col_idx0 = jnp.where(lax.broadcasted_iota(jnp.int32, (RW_SUB, 2 * hd), 1) < hd, 0, L)
        self.blk_mask = ((lax.broadcasted_iota(jnp.int32, (2 * hd, 2 * hd), 0) >> 6)
                         == (lax.broadcasted_iota(jnp.int32, (2 * hd, 2 * hd), 1) >> 6)).astype(F32)
        self.zeros2l = jnp.zeros((2 * L, 2 * hd), F32)
        self.zeros_sub = jnp.zeros((RW_SUB, 2 * hd), F32)
        self.blocks = (1, 0) if reverse else (0, 1)
        self.last = 0 if reverse else L - 1
        self.order = range(L - 1, -1, -1) if reverse else range(L)

    def reset(self):
        self.s_ref[...] = jnp.zeros_like(self.s_ref)

    def prologue(self):
        lw = self.lw_ref[0]
        c = _split3_dot(self.tri_ref[...], lw)
        self.c_scr[...] = c
        enc = jnp.exp(-c)
        self.ag_scr[...] = self.a_ref[...] * jnp.exp(c - lw)
        self.rg_scr[...] = self.r_ref[...] * jnp.exp(c)
        self.bg_scr[...] = self.b_ref[0] * enc
        self.kg_scr[...] = self.k_ref[0] * enc

    def rows(self, ci):
        L = RW_CHUNK
        cc = (self.nch - 1 - ci) if self.reverse else ci
        return pl.ds(pl.multiple_of(cc * L, L), L)

    def halves(self, x):
        return [x * self.lo, x * self.hi]

    def lhs_of(self, rows):
        return [jnp.concatenate([self.ag_scr[rows, ls], self.rg_scr[rows, ls]], axis=0).astype(BF16)
                for ls in self.lss]

    def prep_aa(self, rows):
        lhs = self.lhs_of(rows)
        out = []
        for p in self.pairs:
            ls = self.lss[p]
            rhs = jnp.concatenate(self.halves(self.bg_scr[rows, ls]) + self.halves(self.kg_scr[rows, ls]),
                                  axis=0)
            out.append(_dot_nt(lhs[p], rhs.astype(BF16)))
        return out

    def prep_akv(self, rows, aa_raw, slot):
        _, akv_scr, aar_scr, na_scr = slot
        L = RW_CHUNK
        aa = [x * self.aa_mask for x in aa_raw]
        for p in self.pairs:
            v = self.v_ref[rows, self.lss[p]]
            vv = jnp.concatenate([self.zeros2l] + self.halves(v), axis=0).astype(BF16)
            akv_scr[p] = _dot(aa[p][0:L].astype(BF16), vv)
            aar_scr[p] = aa[p][L:2 * L]
            na_scr[p] = aa[p][0:L]
        return aa

    def prep_cols(self, aa, slot):
        half = RW_SUB // 2
        for p in self.pairs:
            for s in range(RW_CHUNK):
                r0 = (s // RW_SUB) * RW_SUB
                lo, hi = self.live_rows(s - r0)
                slot[0][p, s, lo:hi] = jnp.take_along_axis(aa[p][r0 + lo:r0 + hi],
                                                           self.col_idx0[0:hi - lo] + s, axis=1)

    def live_rows(self, s_local):
        half = RW_SUB // 2
        if self.reverse:
            return (0, half) if s_local <= half else (0, RW_SUB)
        return (half, RW_SUB) if s_local >= half - 1 else (0, RW_SUB)

    def adv_g(self, rows):
        lhs = self.lhs_of(rows)
        return [_dot_nt(lhs[p], self.s_ref[p].astype(BF16)) for p in self.pairs]

    def adv_solve(self, g, slot):
        L, sub = RW_CHUNK, RW_SUB
        col_scr, akv_scr, _, na_scr = slot
        first, second = self.blocks
        ub = {}
        for blk in self.blocks:
            r0 = blk * sub
            ub[blk] = [g[p][r0:r0 + sub] + akv_scr[p, r0:r0 + sub] for p in self.pairs]
            if blk == second:
                for p in self.pairs:
                    done = ub[first][p]
                    z = self.zeros_sub
                    parts = [z, z, z, z]
                    parts[first], parts[2 + first] = done * self.lo, done * self.hi
                    stack = jnp.concatenate(parts + [self.zeros2l], axis=0).astype(BF16)
                    ub[blk][p] = ub[blk][p] + _dot(na_scr[p, r0:r0 + sub].astype(BF16), stack)
            steps = [s for s in self.order if s // sub == blk]
            half = sub // 2
            top = [x[0:half] for x in ub[blk]]
            bot = [x[half:] for x in ub[blk]]
            for s in steps:
                sl = s - r0
                lo, hi = self.live_rows(sl)
                for p in self.pairs:
                    row = top[p][sl:sl + 1] if sl < half else bot[p][sl - half:sl - half + 1]
                    if lo == 0:
                        top[p] = top[p] + col_scr[p, s, 0:half] * row
                    if hi == sub:
                        bot[p] = bot[p] + col_scr[p, s, half:sub] * row
            ub[blk] = [jnp.concatenate([top[p], bot[p]], axis=0) for p in self.pairs]
        return [jnp.concatenate([ub[0][p], ub[1][p]], axis=0) for p in self.pairs]

    def adv_out(self, rows, g, u, slot):
        L = RW_CHUNK
        for p in self.pairs:
            v = self.v_ref[rows, self.lss[p]]
            uv = jnp.concatenate(self.halves(u[p]) + self.halves(v), axis=0).astype(BF16)
            self.y_ref[rows, self.lss[p]] = (g[p][L:2 * L] + _dot(slot[2][p].astype(BF16), uv)).astype(BF16)
        for p in self.pairs:
            ls = self.lss[p]
            cch = self.c_scr[rows, ls]
            cl = cch[self.last:self.last + 1]
            dec = jnp.exp(cl - cch)
            bk = jnp.concatenate([self.b_ref[0, rows, ls] * dec, self.k_ref[0, rows, ls] * dec], axis=0)
            upd = _dot_tn(jnp.concatenate([u[p], self.v_ref[rows, ls]], axis=0).astype(BF16),
                          bk.astype(BF16))
            self.s_ref[p] = self.s_ref[p] * jnp.exp(cl) + upd * self.blk_mask


N_RW_REFS = 8
N_RW_SCRATCH = 14


def _rw_scan_kernel(*refs, tb, npair):
    n_in = N_RW_REFS - 1
    units = []
    for d in range(2):
        ins = refs[d * n_in:(d + 1) * n_in]
        out = refs[2 * n_in + d]
        scr = refs[2 * n_in + 2 + d * N_RW_SCRATCH:2 * n_in + 2 + (d + 1) * N_RW_SCRATCH]
        units.append(_RwUnit(tuple(ins) + (out,), scr, tb=tb, npair=npair, reverse=d == 1))
    nch = tb // RW_CHUNK

    @pl.when(pl.program_id(1) == 0)
    def _():
        for un in units:
            un.reset()

    for un in units:
        un.prologue()

    def step(ci_adv, sa, ci_prep, sp):
        rows_a = [un.rows(ci_adv) for un in units]
        rows_p = [un.rows(ci_prep) for un in units]
        aa_raw = [un.prep_aa(rp) for un, rp in zip(units, rows_p)]
        g = [un.adv_g(ra) for un, ra in zip(units, rows_a)]
        aa = [un.prep_akv(rp, x, un.slots[sp]) for un, rp, x in zip(units, rows_p, aa_raw)]
        u = [un.adv_solve(x, un.slots[sa]) for un, x in zip(units, g)]
        for un, x in zip(units, aa):
            un.prep_cols(x, un.slots[sp])
        for un, ra, x, y in zip(units, rows_a, g, u):
            un.adv_out(ra, x, y, un.slots[sa])

    for un in units:
        r0 = un.rows(0)
        un.prep_cols(un.prep_akv(r0, un.prep_aa(r0), un.slots[0]), un.slots[0])

    def two_chunks(j, carry):
        c0 = 2 * j
        step(c0, 0, c0 + 1, 1)
        step(c0 + 1, 1, jnp.minimum(c0 + 2, nch - 1), 0)
        return carry

    lax.fori_loop(0, nch // 2, two_chunks, 0)


def _seq_block(d, b, i, *, n_batch, ctxb, latb):
    is_ctx = i < ctxb
    cs = jnp.where(d == 0, i, ctxb - 1 - i)
    lj = jnp.where(d == 0, i - ctxb, latb - 1 - (i - ctxb))
    return jnp.where(is_ctx, b * ctxb + cs, n_batch * ctxb + b * latb + lj)


def _rw_scan(r, v, nkk, lw, kka, km, *, tb, n_batch, ctxb, latb):
    m, c = r.shape
    npair = c // (2 * RW_HEAD)
    lanes = 2 * RW_HEAD
    in_specs, args = [], []
    for d in range(2):
        blk = _la_index(d, n_batch, ctxb, latb)
        tok = pl.BlockSpec((tb, c), lambda b, i, blk=blk: (blk(b, i), 0))
        tokd = pl.BlockSpec((1, tb, c), lambda b, i, blk=blk, d=d: (d, blk(b, i), 0))
        in_specs += [tok, tok, tok, tokd, tokd, tokd, pl.BlockSpec((tb, tb), lambda b, i: (0, 0))]
        args += [r, v, nkk, lw, kka, km, _chunk_tri(tb, d == 1, RW_CHUNK)]
    out_specs = [pl.BlockSpec((tb, c), lambda b, i, blk=_la_index(d, n_batch, ctxb, latb): (blk(b, i), 0))
                 for d in range(2)]
    buf = pltpu.VMEM((tb, c), F32)
    unit_scratch = [pltpu.VMEM((npair, lanes, lanes), F32), buf, buf, buf, buf, buf] + 2 * [
        pltpu.VMEM((npair, RW_CHUNK, RW_SUB, lanes), F32),
        pltpu.VMEM((npair, RW_CHUNK, lanes), F32),
        pltpu.VMEM((npair, RW_CHUNK, lanes), F32),
        pltpu.VMEM((npair, RW_CHUNK, lanes), F32)]
    assert len(unit_scratch) == N_RW_SCRATCH
    return pl.pallas_call(
        functools.partial(_rw_scan_kernel, tb=tb, npair=npair),
        grid=(n_batch, ctxb + latb),
        in_specs=in_specs,
        out_specs=out_specs,
        out_shape=[jax.ShapeDtypeStruct((m, c), BF16)] * 2,
        scratch_shapes=unit_scratch + unit_scratch,
        compiler_params=_cparams(2),
        name="rwkv_scan",
    )(*args)


def _rw_post(y, g, bonus, lnw, lnb, e):
    inv = 1.0 / RW_HEAD
    yc = y - _split_dot(y, e) * inv
    var = _split_dot(yc * yc, e) * inv
    yn = yc * lax.rsqrt(var + RW_LN_EPS)
    return (yn * lnw + lnb + bonus) * g


class _LaUnit:
    def __init__(self, q_ref, k_ref, v_ref, b_ref, o_ref, s_ref, *, tb, nh, dk, dv, reverse, q_scale):
        self.q_ref, self.k_ref, self.v_ref, self.b_ref, self.o_ref, self.s_ref = (
            q_ref, k_ref, v_ref, b_ref, o_ref, s_ref)
        self.nch = tb // LA_CHUNK
        self.dk, self.dv, self.reverse, self.q_scale = dk, dv, reverse, q_scale
        self.lanes = 128
        self.pack = self.lanes // dk
        self.groups = range(nh // self.pack)
        self.rowi = lax.broadcasted_iota(jnp.int32, (LA_CHUNK, 1), 0)
        lane = lax.broadcasted_iota(jnp.int32, (1, self.lanes), 1)
        self.head_lanes = [(lane // dk == j).astype(F32) for j in range(self.pack)]
        if self.pack > 1:
            rows = lax.broadcasted_iota(jnp.int32, (self.pack * dv, self.lanes), 0)
            cols = lax.broadcasted_iota(jnp.int32, (self.pack * dv, self.lanes), 1)
            self.blk_mask = ((rows // dv) == (cols // dk)).astype(F32)

    def reset(self):
        self.s_ref[...] = jnp.zeros_like(self.s_ref)

    def load(self, ci):
        cc = (self.nch - 1 - ci) if self.reverse else ci
        rows = pl.ds(pl.multiple_of(cc * LA_CHUNK, LA_CHUNK), LA_CHUNK)
        pack, dv, lanes = self.pack, self.dv, self.lanes
        out = []
        for g in self.groups:
            ks = slice(g * lanes, (g + 1) * lanes)
            q = self.q_ref[rows, ks].astype(F32) * self.q_scale
            k = self.k_ref[rows, ks].astype(F32)
            v = self.v_ref[rows, g * pack * dv:(g + 1) * pack * dv].astype(F32)
            b = self.b_ref[rows, ks]
            o_inter = _dot_nt((q * jnp.exp(b)).astype(BF16), self.s_ref[g].astype(BF16))
            out.append((rows, q, k, v, b, [o_inter[:, j * dv:(j + 1) * dv] for j in range(pack)]))
        return out

    def intra(self, ops):
        pack, dv, half = self.pack, self.dv, LA_CHUNK // 2
        rowi = self.rowi[0:half]
        for g in self.groups:
            rows, q, k, v, b, o = ops[g]
            qh = [q[0:half], q[half:]]
            bh = [b[0:half], b[half:]]
            oh = [[oj[0:half], oj[half:]] for oj in o]
            for s in range(LA_CHUNK):
                for h in range(2):
                    lo_row, hi_row = h * half, (h + 1) * half - 1
                    if (hi_row > s) if self.reverse else (lo_row < s):
                        if (lo_row > s) if self.reverse else (hi_row < s):
                            continue
                        valid = (rowi + lo_row <= s) if self.reverse else (rowi + lo_row >= s)
                    else:
                        valid = None
                    term = (qh[h] * k[s:s + 1]) * jnp.exp(bh[h] - b[s:s + 1])
                    for j in range(pack):
                        tj = term if pack == 1 else term * self.head_lanes[j]
                        col = jnp.sum(tj, axis=-1, keepdims=True)
                        if valid is not None:
                            col = jnp.where(valid, col, 0.0)
                        oh[j][h] = oh[j][h] + col * v[s:s + 1, j * dv:(j + 1) * dv]
            for j in range(pack):
                self.o_ref[rows, (g * pack + j) * dv:(g * pack + j + 1) * dv] = jnp.concatenate(
                    oh[j], axis=0).astype(BF16)

    def update(self, ops):
        for g in self.groups:
            _, _, k, v, b, _ = ops[g]
            b_last = b[0:1] if self.reverse else b[LA_CHUNK - 1:LA_CHUNK]
            upd = _dot_tn(v.astype(BF16), (k * jnp.exp(b_last - b)).astype(BF16))
            if self.pack > 1:
                upd = upd * self.blk_mask
            self.s_ref[g] = self.s_ref[g] * jnp.exp(b_last) + upd


def _la_run(units, nch):
    def chunk(ci, carry):
        ops = [un.load(ci) for un in units]
        for un, x in zip(units, ops):
            un.update(x)
        for un, x in zip(units, ops):
            un.intra(x)
        return carry

    lax.fori_loop(0, nch, chunk, 0)


def _hg_units(ins, gamma_ref, outs, scr, *, layer, tb, nh, dk, dv):
    gam = gamma_ref[...]
    ex = jnp.exp(gam - jnp.max(gam, axis=0, keepdims=True))
    p = ex / jnp.sum(ex, axis=0, keepdims=True)
    cum = p[0:1]
    for i in range(1, layer + 1):
        cum = cum + p[i:i + 1]
    lb = cum - p[0:1]
    lo = jnp.log(lb)
    l1 = jnp.log(1.0 - lb)

    units = []
    for d in range(2):
        q_ref, f_ref, v_ref, tri_ref = ins[4 * d:4 * d + 4]
        s_ref, k_scr, b_scr = scr[3 * d:3 * d + 3]
        hi = l1 + _log_sigmoid(f_ref[...].astype(F32))
        mx = jnp.maximum(lo, hi)
        mn = jnp.minimum(lo, hi)
        log_f = mx + jnp.log(1.0 + jnp.exp(mn - mx))
        k_scr[...] = 1.0 - jnp.exp(log_f)
        b_scr[...] = _split3_dot(tri_ref[...], log_f)
        units.append(_LaUnit(q_ref, k_scr, v_ref, b_scr, outs[d], s_ref, tb=tb, nh=nh, dk=dk, dv=dv,
                             reverse=d == 1, q_scale=1.0))
    return units


def _gla_units(ins, gw_ref, gb_ref, outs, scr, *, tb, nh, dk, dv):
    hk = nh * dk
    units = []
    for d in range(2):
        qk_ref, v_ref, small_ref, tri_ref = ins[4 * d:4 * d + 4]
        s_ref, b_scr = scr[2 * d:2 * d + 2]
        code = small_ref[:, 384:512]
        pre = _dot(code.astype(BF16), gw_ref[:, d * hk:(d + 1) * hk]) + gb_ref[:, d * hk:(d + 1) * hk]
        log_g = _log_sigmoid(pre) * (1.0 / GLA_GATE_NORM)
        b_scr[...] = _split3_dot(tri_ref[...], log_g)
        units.append(_LaUnit(qk_ref.at[:, 0:hk], qk_ref.at[:, hk:2 * hk], v_ref, b_scr, outs[d], s_ref,
                             tb=tb, nh=nh, dk=dk, dv=dv, reverse=d == 1, q_scale=dk ** -0.5))
    return units


def _la_kernel(*refs, layer, tb, hg_heads, gla_dk, gla_dv):
    hg_in, (gamma_ref,), gla_in, (gw_ref, gb_ref) = refs[0:8], refs[8:9], refs[9:17], refs[17:19]
    hg_out, gla_out, hg_scr, gla_scr = refs[19:21], refs[21:23], refs[23:29], refs[29:33]
    units = (_hg_units(hg_in, gamma_ref, hg_out, hg_scr, layer=layer, tb=tb, nh=hg_heads, dk=HG_DK, dv=HG_DK)
             + _gla_units(gla_in, gw_ref, gb_ref, gla_out, gla_scr, tb=tb, nh=GLA_HEADS, dk=gla_dk, dv=gla_dv))

    @pl.when(pl.program_id(1) == 0)
    def _():
        for un in units:
            un.reset()

    _la_run(units, tb // LA_CHUNK)


def _chunk_tri(tb, reverse, chunk=LA_CHUNK):
    t = jnp.arange(tb)
    same = (t[:, None] // chunk) == (t[None, :] // chunk)
    tri = (t[None, :] >= t[:, None]) if reverse else (t[None, :] <= t[:, None])
    return (same & tri).astype(BF16)


def _la_index(d, n_batch, ctxb, latb):
    return lambda b, i: _seq_block(d, b, i, n_batch=n_batch, ctxb=ctxb, latb=latb)


def _la_mixers(p_all, gamma, gw, gb, *, layer, tb, n_batch, ctxb, latb):
    m = p_all.shape[0]
    c = gamma.shape[-1]
    hk = gw.shape[-1] // 2
    gla_dk, gla_dv = hk // GLA_HEADS, c // GLA_HEADS
    tri = pl.BlockSpec((tb, tb), lambda b, i: (0, 0))
    whole = lambda a: pl.BlockSpec(a.shape, lambda b, i: (0,) * a.ndim)
    blks = [_la_index(d, n_batch, ctxb, latb) for d in range(2)]
    col = lambda d, off: pl.BlockSpec((tb, c), lambda b, i, blk=blks[d]: (blk(b, i), off // c))
    in_specs, args = [], []
    for d in range(2):
        in_specs += [col(d, C_HGQ), col(d, C_HGF + d * c), col(d, C_HGI), tri]
        args += [p_all, p_all, p_all, _chunk_tri(tb, d == 1)]
    in_specs.append(whole(gamma))
    args.append(gamma)
    for d in range(2):
        in_specs += [col(d, C_GLQK), col(d, C_GLV), col(d, C_SMALL), tri]
        args += [p_all, p_all, p_all, _chunk_tri(tb, d == 1)]
    in_specs += [whole(gw), whole(gb)]
    args += [gw, gb]
    buf = pltpu.VMEM((tb, c), F32)
    hg_scratch = [pltpu.VMEM((c // HG_DK, HG_DK, HG_DK), F32), buf, buf]
    gla_scratch = [pltpu.VMEM((hk // 128, (128 // gla_dk) * gla_dv, 128), F32), pltpu.VMEM((tb, hk), F32)]
    outs = pl.pallas_call(
        functools.partial(_la_kernel, layer=layer, tb=tb, hg_heads=c // HG_DK, gla_dk=gla_dk, gla_dv=gla_dv),
        grid=(n_batch, ctxb + latb),
        in_specs=in_specs,
        out_specs=[col(0, 0), col(1, 0)] * 2,
        out_shape=[jax.ShapeDtypeStruct((m, c), BF16)] * 4,
        scratch_shapes=2 * hg_scratch + 2 * gla_scratch,
        compiler_params=_cparams(2),
        name="la_scan",
    )(*args)
    return outs[0:2], outs[2:4]


def _la_post(o, gate, norm_g, nh):
    dv = o.shape[-1] // nh
    outs = []
    for h in range(nh):
        oh = o[:, h * dv:(h + 1) * dv]
        outs.append(oh * lax.rsqrt(jnp.mean(oh * oh, axis=-1, keepdims=True) + EPS))
    return jnp.concatenate(outs, axis=-1) * norm_g * (gate * _sigmoid(gate))


def _gelu(x):
    return 0.5 * x * (1.0 + jnp.tanh(0.7978845608028654 * (x + 0.044715 * (x * x * x))))


def _sgu_kernel(u_ref, v_ref, lnw_ref, lnb_ref, ws_ref, bs_ref, o_ref, *, rb):
    u = _gelu(u_ref[...].astype(F32))
    v = _gelu(v_ref[...].astype(F32))
    vc = v - jnp.mean(v, axis=-1, keepdims=True)
    vn = vc * lax.rsqrt(jnp.mean(vc * vc, axis=-1, keepdims=True) + EPS)
    vn = (vn * lnw_ref[...] + lnb_ref[...]).astype(BF16)
    gw = vn.shape[-1] // SGU_GROUPS
    for n in range(rb // SGU_CHUNK):
        rs = slice(n * SGU_CHUNK, (n + 1) * SGU_CHUNK)
        for g in range(SGU_GROUPS):
            cs = slice(g * gw, (g + 1) * gw)
            s = _dot(ws_ref[g], vn[rs, cs]) + bs_ref[g]
            o_ref[rs, cs] = u[rs, cs] * s


def _sgu(p_all, lnw, lnb, ws, bs, *, rb):
    m = p_all.shape[0]
    c = lnw.shape[-1]
    return pl.pallas_call(
        functools.partial(_sgu_kernel, rb=rb),
        grid=(m // rb,),
        in_specs=[pl.BlockSpec((rb, c), lambda i: (i, C_SGU // c)),
                  pl.BlockSpec((rb, c), lambda i: (i, C_SGU // c + 1)),
                  pl.BlockSpec((1, c), lambda i: (0, 0)),
                  pl.BlockSpec((1, c), lambda i: (0, 0)),
                  pl.BlockSpec(ws.shape, lambda i: (0, 0, 0)),
                  pl.BlockSpec(bs.shape, lambda i: (0, 0, 0))],
        out_specs=pl.BlockSpec((rb, c), lambda i: (i, 0)),
        out_shape=jax.ShapeDtypeStruct((m, c), F32),
        compiler_params=_cparams(1),
        name="sgu",
    )(p_all, p_all, lnw, lnb, ws, bs)


def _merge_kernel(raf_ref, rab_ref, rg_ref, rbonus_ref, hf_ref, hb_ref, hgate_ref, gf_ref, gb_ref,
                  ggate_ref, yd_ref, lnw_ref, lnb_ref, e_ref, hnorm_ref, gnorm_ref,
                  g0_ref, g1_ref, g2_ref, g3_ref, w_ref, o_ref, y_scr, *, hg_heads):
    @pl.when(pl.program_id(1) == 0)
    def _():
        both = lambda f_ref, b_ref: f_ref[...].astype(F32) + b_ref[...].astype(F32)
        y_scr[0] = _rw_post(both(raf_ref, rab_ref), rg_ref[...], rbonus_ref[...], lnw_ref[...],
                            lnb_ref[...], e_ref[...]).astype(BF16)
        y_scr[1] = _la_post(both(hf_ref, hb_ref), hgate_ref[...].astype(F32), hnorm_ref[...],
                            hg_heads).astype(BF16)
        y_scr[2] = _la_post(both(gf_ref, gb_ref), ggate_ref[...].astype(F32), gnorm_ref[...],
                            GLA_HEADS).astype(BF16)
        y_scr[3] = yd_ref[...].astype(BF16)

    gs = (g0_ref, g1_ref, g2_ref, g3_ref)
    acc = None
    for j in range(N_BRANCH):
        t = _sigmoid(gs[j][...].astype(F32)) * _dot(y_scr[j], w_ref[j])
        acc = t if acc is None else acc + t
    o_ref[...] = acc.astype(BF16)


def _merge(rw, hg, gla, yd, p_gate, p_mix, vecs, e, w_branch, *, layer, tm, row_off):
    m = p_mix.shape[0]
    _, _, c, d = w_branch.shape
    tn = _pick((1024, 512), d)
    mo = m - row_off * tm
    ytok = pl.BlockSpec((tm, c), lambda i, j: (i + row_off, 0))
    mix = lambda off: pl.BlockSpec((tm, c), lambda i, j: (i + row_off, off // c))
    gate = lambda b: pl.BlockSpec((tm, tn), lambda i, j: (i + row_off, b * d // tn + j))
    vec = pl.BlockSpec((1, c), lambda i, j: (0, 0))
    return pl.pallas_call(
        functools.partial(_merge_kernel, hg_heads=c // HG_DK),
        grid=(mo // tm, d // tn),
        in_specs=[ytok, ytok, ytok, ytok, ytok, ytok, mix(C_HGG), ytok, ytok, mix(C_GLG), ytok,
                  vec, vec, pl.BlockSpec((c, c), lambda i, j: (0, 0)), vec, vec,
                  gate(0), gate(1), gate(2), gate(3),
                  pl.BlockSpec((None, N_BRANCH, c, tn), lambda i, j: (layer, 0, 0, j))],
        out_specs=pl.BlockSpec((tm, tn), lambda i, j: (i, j)),
        out_shape=jax.ShapeDtypeStruct((mo, d), BF16),
        scratch_shapes=[pltpu.VMEM((N_BRANCH, tm, c), BF16)],
        compiler_params=_cparams(2),
        name="merge",
    )(*rw, hg[0], hg[1], p_mix, gla[0], gla[1], p_mix, yd, *vecs[:2], e, *vecs[2:],
      p_gate, p_gate, p_gate, p_gate, w_branch)


def _outproj_kernel(m_ref, w_ref, xc_ref, xl_ref, mod_ref, o_ref, *, nctx, row_off):
    x = jnp.where(pl.program_id(0) + row_off < nctx, xc_ref[...], xl_ref[...])
    o_ref[...] = x + mod_ref[0, 2:3, :] * _dot(m_ref[...], w_ref[...])


def _outproj(mm, w_out, x_parts, mod, *, layer, tm, row_off, nctx, bpb):
    xc, xl, lat_off = x_parts
    mo, d = mm.shape
    tn = _pick((1024, 512), d)
    xspecs = [pl.BlockSpec((tm, tn), lambda i, j: (jnp.minimum(i + row_off, nctx - 1),
                                                   jnp.where(i + row_off < nctx, j, 0))),
              pl.BlockSpec((tm, tn), lambda i, j: (lat_off + jnp.maximum(i + row_off - nctx, 0),
                                                   jnp.where(i + row_off < nctx, 0, j)))]
    return pl.pallas_call(
        functools.partial(_outproj_kernel, nctx=nctx, row_off=row_off),
        grid=(mo // tm, d // tn),
        in_specs=[pl.BlockSpec((tm, d), lambda i, j: (i, 0)),
                  pl.BlockSpec((None, d, tn), lambda i, j: (layer, 0, j))] + xspecs + [
                  pl.BlockSpec((1, 6, tn), lambda i, j: (_row_group(i + row_off, nctx, bpb), 0, j))],
        out_specs=pl.BlockSpec((tm, tn), lambda i, j: (i, j)),
        out_shape=jax.ShapeDtypeStruct((mo, d), F32),
        compiler_params=_cparams(2),
        name="outproj",
    )(mm, w_out, xc, xl, mod)


def _mlp_kernel(x_ref, mod_ref, g_ref, w1_ref, w2_ref, gf_ref, o_ref, h_ref, acc_ref, *, final_norm):
    j = pl.program_id(1)

    @pl.when(j == 0)
    def _():
        h = _norm_mod(x_ref[...], g_ref[...], mod_ref[0, 3:4, :], mod_ref[0, 4:5, :])
        h_ref[...] = h.astype(BF16)
        acc_ref[...] = jnp.zeros_like(acc_ref)

    a = jnp.maximum(_dot(h_ref[...], w1_ref[...]), 0.0)
    acc_ref[...] += _dot((a * a).astype(BF16), w2_ref[...])

    @pl.when(j == pl.num_programs(1) - 1)
    def _():
        y = x_ref[...] + mod_ref[0, 5:6, :] * acc_ref[...]
        if final_norm:
            y = y * lax.rsqrt(jnp.mean(y * y, axis=-1, keepdims=True) + EPS) * gf_ref[...]
        o_ref[...] = y


def _mlp(x_in, mod, g, w1, w2, g_final, *, layer, tm, row_off, nctx, bpb, final_norm):
    mo, d = x_in.shape
    hid = w1.shape[-1]
    th = _pick((1024, 512, 256, 128), hid)
    return pl.pallas_call(
        functools.partial(_mlp_kernel, final_norm=final_norm),
        grid=(mo // tm, hid // th),
        in_specs=[pl.BlockSpec((tm, d), lambda i, j: (i, 0)),
                  pl.BlockSpec((1, 6, d), lambda i, j: (_row_group(i + row_off, nctx, bpb), 0, 0)),
                  pl.BlockSpec((1, d), lambda i, j: (0, 0)),
                  pl.BlockSpec((None, d, th), lambda i, j: (layer, 0, j)),
                  pl.BlockSpec((None, th, d), lambda i, j: (layer, j, 0)),
                  pl.BlockSpec((1, d), lambda i, j: (0, 0))],
        out_specs=pl.BlockSpec((tm, d), lambda i, j: (i, 0)),
        out_shape=jax.ShapeDtypeStruct((mo, d), F32),
        scratch_shapes=[pltpu.VMEM((tm, d), BF16), pltpu.VMEM((tm, d), F32)],
        compiler_params=_cparams(2),
        name="mlp",
    )(x_in, mod, g, w1, w2, g_final)


def _blockdiag2(w):
    _, r, c = w.shape
    z = jnp.zeros((r, c), w.dtype)
    out = jnp.concatenate([jnp.concatenate([w[0], z], axis=1), jnp.concatenate([z, w[1]], axis=1)], axis=0)
    return jnp.pad(out, ((0, 128 - 2 * r), (0, 0))).astype(BF16)


def _permute_w_in(w, d_model):
    c = d_model // N_BRANCH
    o = [0]
    for wd in (3 * c, 64 * 2, 64 * 2, 128, c, 2 * c, c, c, c, c, 32, c, 2 * c, N_BRANCH * d_model):
        o.append(o[-1] + wd)
    seg = lambda k: w[:, o[k]:o[k + 1]]
    pad = jnp.zeros((w.shape[0], 512 - 128 * 3 - 32), w.dtype)
    parts = [seg(13), seg(0), seg(5), seg(12), seg(4), seg(6), seg(7), seg(8), seg(9), seg(11),
             seg(1), seg(2), seg(3), seg(10), pad]
    out = jnp.concatenate(parts, axis=1).astype(BF16)
    assert out.shape[1] == N_GATE + N_MIX
    return out


def kernel(x, c, ctx, c_ctx, w_ada, b_ada, g_norm1, g_norm2, g_final, w_in, rw_conv, rw_w0, rw_w2,
           rw_a0, rw_a2, rw_g2, rw_kk, rw_ka, rw_rk, rw_ln_w, rw_ln_b, hg_gamma, hg_norm, gla_gw,
           gla_gb, gla_norm, sgu_ln_w, sgu_ln_b, sgu_w, sgu_b, w_branch, w_out, w_mlp1, w_mlp2):
    n_batch, seq, d_model = x.shape
    ctx_len = ctx.shape[1]
    depth = w_in.shape[0]
    cw = d_model // N_BRANCH
    assert cw == 512 and d_model == 2048, "column layout constants assume D_MODEL = 2048"
    m_ctx = n_batch * ctx_len

    tm = _pick((1024, 512, 256, 128), m_ctx, seq)
    tb = _pick((256, 128), ctx_len, seq)
    nctx, bpb = m_ctx // tm, seq // tm
    ctxb, latb = ctx_len // tb, seq // tb
    seqs = dict(tb=tb, n_batch=n_batch, ctxb=ctxb, latb=latb)

    x_parts = (ctx.reshape(m_ctx, d_model), x.reshape(n_batch * seq, d_model), 0)
    c_rows = jnp.concatenate([c_ctx[None, :], c, jnp.zeros((7 - n_batch, d_model), F32)], axis=0)
    mod_all = _ada(c_rows, w_ada, b_ada).reshape(depth, 8, 6, d_model)

    head_ones = (jnp.arange(cw)[:, None] // RW_HEAD == jnp.arange(cw)[None, :] // RW_HEAD).astype(BF16)
    w_branch_b, w_out_b = w_branch.astype(BF16), w_out.astype(BF16)
    w_mlp1_b, w_mlp2_b = w_mlp1.astype(BF16), w_mlp2.astype(BF16)
    row = lambda a: a.reshape(1, -1)

    for l in range(depth):
        last = l == depth - 1
        mod = mod_all[l]
        p_gate, p_all = _inproj(x_parts, mod, row(g_norm1[l]), _permute_w_in(w_in[l], d_model),
                                tm=tm, nctx=nctx, bpb=bpb)

        prm = dict(conv=rw_conv[l], w0=row(rw_w0[l]), w2=_blockdiag2(rw_w2[l]), a0=row(rw_a0[l]),
                   a2=_blockdiag2(rw_a2[l]), g2=rw_g2[l].astype(BF16), kk=row(rw_kk[l]),
                   ka=row(rw_ka[l]), rk=row(rw_rk[l]), e=head_ones)
        r, v, nkk, g, bonus, w, kka, km = _rw_prep(p_all, prm, tb=tb, nctx=m_ctx // tb,
                                                    ctx_bps=ctxb, lat_bps=latb)
        ys = _rw_scan(r, v, nkk, w, kka, km, **seqs)

        ob, oc = _la_mixers(p_all, hg_gamma, _blockdiag2(gla_gw[l]), row(gla_gb[l]), layer=l, **seqs)

        bs = jnp.broadcast_to(sgu_b[l][:, :, None], sgu_w[l].shape)
        yd = _sgu(p_all, row(sgu_ln_w[l]), row(sgu_ln_b[l]), sgu_w[l].astype(BF16), bs,
                  rb=_pick((512, 256, 128), m_ctx, seq))

        row_off = nctx if last else 0
        vecs = (row(rw_ln_w[l]), row(rw_ln_b[l]), row(hg_norm[l]), row(gla_norm[l]))
        mm = _merge((ys[0], ys[1], g, bonus), ob, oc, yd, p_gate, p_all, vecs, head_ones,
                    w_branch_b, layer=l, tm=tm // 2, row_off=2 * row_off)
        x_mid = _outproj(mm, w_out_b, x_parts, mod, layer=l, tm=tm, row_off=row_off, nctx=nctx, bpb=bpb)
        x_all = _mlp(x_mid, mod, row(g_norm2[l]), w_mlp1_b, w_mlp2_b, row(g_final), layer=l,
                     tm=tm // 2, row_off=2 * row_off, nctx=2 * nctx, bpb=2 * bpb, final_norm=last)
        x_parts = (x_all, x_all, nctx)
    return x_all.reshape(n_batch, seq, d_model)
```

```python
import functools

import jax
import jax.numpy as jnp
from jax import lax
from jax.experimental import pallas as pl
from jax.experimental.pallas import tpu as pltpu

F32 = jnp.float32
BF16 = jnp.bfloat16

N_BRANCH = 4
RW_HEAD = 64
RW_LN_EPS = 64e-5
HG_DK = 128
GLA_HEADS = 4
GLA_GATE_NORM = 16.0
LA_CHUNK = 16
RW_CHUNK = 32
RW_SUB = 16
SGU_CHUNK = 128
SGU_GROUPS = 4
EPS = 1e-6
HALO = 16

C_RKV = 0
C_HGF = 1536
C_SGU = 2560
C_HGQ = 3584
C_HGI = 4096
C_HGG = 4608
C_GLQK = 5120
C_GLV = 5632
C_GLG = 6144
C_SMALL = 6656
N_MIX = 7168
N_GATE = 8192

VMEM_LIMIT = 58 * 1024 * 1024


def _cparams(n_axes):
    return pltpu.CompilerParams(dimension_semantics=("arbitrary",) * n_axes,
                                vmem_limit_bytes=VMEM_LIMIT)


def _pick(n_list, *dims):
    for n in n_list:
        if all(d % n == 0 for d in dims):
            return n
    raise ValueError(f"no block size in {n_list} divides {dims}")


def _row_group(i, nctx, bpb):
    return jnp.where(i < nctx, 0, 1 + (i - nctx) // bpb)


def _dot(a, b):
    return jnp.dot(a, b, preferred_element_type=F32)


def _dot_nt(a, b):
    return lax.dot_general(a, b, (((1,), (1,)), ((), ())), preferred_element_type=F32)


def _dot_tn(a, b):
    return lax.dot_general(a, b, (((0,), (0,)), ((), ())), preferred_element_type=F32)


def _split_dot(x, e):
    hi = x.astype(BF16)
    lo = (x - hi.astype(F32)).astype(BF16)
    return _dot(hi, e) + _dot(lo, e)


def _split3_dot(e, x):
    p1 = x.astype(BF16)
    r1 = x - p1.astype(F32)
    p2 = r1.astype(BF16)
    p3 = (r1 - p2.astype(F32)).astype(BF16)
    return _dot(e, p1) + _dot(e, p2) + _dot(e, p3)


def _log_sigmoid(x):
    return jnp.minimum(x, 0.0) - jnp.log(1.0 + jnp.exp(-jnp.abs(x)))


def _sigmoid(x):
    return 0.5 * jnp.tanh(0.5 * x) + 0.5


def _ada_kernel(c_ref, w_ref, b_ref, o_ref):
    c = c_ref[...]
    act = c * _sigmoid(c)
    part = _dot(act.astype(BF16), w_ref[0].astype(BF16))

    @pl.when(pl.program_id(1) == 0)
    def _():
        o_ref[0] = part + b_ref[0]

    @pl.when(pl.program_id(1) > 0)
    def _():
        o_ref[0] += part


def _ada(c_rows, w_ada, b_ada):
    n_layers, d, n = w_ada.shape
    tk = _pick((256, 128), d)
    return pl.pallas_call(
        _ada_kernel,
        grid=(n_layers, d // tk),
        in_specs=[pl.BlockSpec((8, tk), lambda l, k: (0, k)),
                  pl.BlockSpec((1, tk, n), lambda l, k: (l, k, 0)),
                  pl.BlockSpec((1, 1, n), lambda l, k: (l, 0, 0))],
        out_specs=pl.BlockSpec((1, 8, n), lambda l, k: (l, 0, 0)),
        out_shape=jax.ShapeDtypeStruct((n_layers, 8, n), F32),
        compiler_params=_cparams(2),
        name="ada_mod",
    )(c_rows, w_ada, b_ada.reshape(n_layers, 1, n))


def _norm_mod(x, g, shift, scale):
    y = x * lax.rsqrt(jnp.mean(x * x, axis=-1, keepdims=True) + EPS) * g
    return y * (1.0 + scale) + shift


def _two_part_specs(block, nctx, lat_off):
    return [pl.BlockSpec(block, lambda i, j: (jnp.minimum(i, nctx - 1), 0), pipeline_mode=pl.Buffered(1)),
            pl.BlockSpec(block, lambda i, j: (lat_off + jnp.maximum(i - nctx, 0), 0))]


def _inproj_kernel(xc_ref, xl_ref, mod_ref, g_ref, w_ref, og_ref, om_ref, h_ref, *, nctx, n_gate_tiles):
    i, j = pl.program_id(0), pl.program_id(1)

    @pl.when(j == 0)
    def _():
        x = jnp.where(i < nctx, xc_ref[...], xl_ref[...])
        h = _norm_mod(x, g_ref[...], mod_ref[0, 0:1, :], mod_ref[0, 1:2, :])
        h_ref[...] = h.astype(BF16)

    @pl.when(j < n_gate_tiles)
    def _():
        og_ref[...] = _dot(h_ref[...], w_ref[...]).astype(BF16)

    @pl.when(j >= n_gate_tiles)
    def _():
        om_ref[...] = _dot(h_ref[...], w_ref[...]).astype(BF16)


def _inproj(x_parts, mod, g, w, *, tm, nctx, bpb):
    xc, xl, lat_off = x_parts
    d = xc.shape[1]
    m = (nctx + (xl.shape[0] // tm - lat_off)) * tm
    tn = _pick((1024, 512), N_GATE, N_MIX)
    ng = N_GATE // tn
    return pl.pallas_call(
        functools.partial(_inproj_kernel, nctx=nctx, n_gate_tiles=ng),
        grid=(m // tm, (N_GATE + N_MIX) // tn),
        in_specs=_two_part_specs((tm, d), nctx, lat_off) + [
            pl.BlockSpec((1, 6, d), lambda i, j: (_row_group(i, nctx, bpb), 0, 0)),
            pl.BlockSpec((1, d), lambda i, j: (0, 0)),
            pl.BlockSpec((d, tn), lambda i, j: (0, j))],
        out_specs=[pl.BlockSpec((tm, tn), lambda i, j: (i, jnp.minimum(j, ng - 1))),
                   pl.BlockSpec((tm, tn), lambda i, j: (i, jnp.maximum(j - ng, 0)))],
        out_shape=[jax.ShapeDtypeStruct((m, N_GATE), BF16), jax.ShapeDtypeStruct((m, N_MIX), BF16)],
        scratch_shapes=[pltpu.VMEM((tm, d), BF16)],
        compiler_params=_cparams(2),
        name="inproj",
    )(xc, xl, mod, g, w)


def _rw_prep_kernel(rkv_ref, prev_ref, next_ref, small_ref, conv_ref, w0_ref, w2_ref, a0_ref,
                    a2_ref, g2_ref, kk_ref, ka_ref, rk_ref, e_ref,
                    r_out, v_out, nkk_out, g_out, bonus_out, w_out, kka_out, km_out,
                    *, tb, nctx, ctx_bps, lat_bps):
    i = pl.program_id(0)
    c = r_out.shape[-1]
    j = jnp.where(i < nctx, i, i - nctx)
    bps = jnp.where(i < nctx, ctx_bps, lat_bps)
    first = lax.rem(j, bps) == 0
    last = lax.rem(j, bps) == bps - 1

    blk = rkv_ref[...].astype(F32)
    rows = lax.broadcasted_iota(jnp.int32, (tb, 1), 0)
    prev_row = jnp.where(first, 0.0, prev_ref[HALO - 1:HALO, :].astype(F32))
    next_row = jnp.where(last, 0.0, next_ref[0:1, :].astype(F32))
    xm1 = jnp.where(rows == 0, prev_row, pltpu.roll(blk, 1, 0))
    xp1 = jnp.where(rows == tb - 1, next_row, pltpu.roll(blk, tb - 1, 0))
    conv = conv_ref[0:1, :] * xm1 + conv_ref[1:2, :] * blk + conv_ref[2:3, :] * xp1
    r = conv[:, 0:c]
    k = conv[:, c:2 * c]
    v = conv[:, 2 * c:3 * c]

    small = small_ref[...].astype(F32)
    wl = small[:, 0:128]
    al = small[:, 128:256]
    gl = small[:, 256:384]
    w_pre = w0_ref[...] + _dot(jnp.tanh(wl).astype(BF16), w2_ref[...])
    softplus = jnp.maximum(-w_pre, 0.0) + jnp.log(1.0 + jnp.exp(-jnp.abs(w_pre)))
    log_decay = -jnp.exp(-softplus - 0.5)
    a = _sigmoid(a0_ref[...] + _dot(al.astype(BF16), a2_ref[...]))
    g = _dot(_sigmoid(gl).astype(BF16), g2_ref[...])

    e = e_ref[...]
    kkv = k * kk_ref[...]
    kk = kkv * lax.rsqrt(_split_dot(kkv * kkv, e) + 1e-12)
    bonus = _split_dot(r * k * rk_ref[...], e) * v

    r_out[...] = r.astype(BF16)
    v_out[...] = v.astype(BF16)
    nkk_out[...] = (-kk).astype(BF16)
    g_out[...] = g
    bonus_out[...] = bonus
    for d in range(2):
        a_d = a[:, d * c:(d + 1) * c]
        w_out[d] = log_decay[:, d * c:(d + 1) * c]
        kka_out[d] = (kk * a_d).astype(BF16)
        km_out[d] = (k * (1.0 + (a_d - 1.0) * ka_ref[...])).astype(BF16)


def _rw_prep(p_all, prm, *, tb, nctx, ctx_bps, lat_bps):
    m = p_all.shape[0]
    c = prm["kk"].shape[-1]
    nblk = m // tb
    th = tb // HALO
    full = lambda shape: pl.BlockSpec(shape, lambda i: (0,) * len(shape))
    tok = pl.BlockSpec((tb, c), lambda i: (i, 0))
    tok2 = pl.BlockSpec((2, tb, c), lambda i: (0, i, 0))
    one, one_b = jax.ShapeDtypeStruct((m, c), F32), jax.ShapeDtypeStruct((m, c), BF16)
    two, two_b = jax.ShapeDtypeStruct((2, m, c), F32), jax.ShapeDtypeStruct((2, m, c), BF16)
    return pl.pallas_call(
        functools.partial(_rw_prep_kernel, tb=tb, nctx=nctx, ctx_bps=ctx_bps, lat_bps=lat_bps),
        grid=(nblk,),
        in_specs=[pl.BlockSpec((tb, 3 * c), lambda i: (i, C_RKV // (3 * c))),
                  pl.BlockSpec((HALO, 3 * c), lambda i: (jnp.maximum(i * th - 1, 0), 0)),
                  pl.BlockSpec((HALO, 3 * c), lambda i: (jnp.minimum((i + 1) * th, m // HALO - 1), 0)),
                  pl.BlockSpec((tb, 512), lambda i: (i, C_SMALL // 512)),
                  full((3, 3 * c)), full((1, 2 * c)), full((128, 2 * c)), full((1, 2 * c)),
                  full((128, 2 * c)), full((128, c)), full((1, c)), full((1, c)), full((1, c)),
                  full((c, c))],
        out_specs=[tok, tok, tok, tok, tok, tok2, tok2, tok2],
        out_shape=[one_b, one_b, one_b, one, one, two, two_b, two_b],
        compiler_params=_cparams(1),
        name="rwkv_prep",
    )(p_all, p_all, p_all, p_all, prm["conv"], prm["w0"], prm["w2"], prm["a0"], prm["a2"],
      prm["g2"], prm["kk"], prm["ka"], prm["rk"], prm["e"])


class _RwUnit:
    def __init__(self, refs, scratch, *, tb, npair, reverse):
        (self.r_ref, self.v_ref, self.a_ref, self.lw_ref, self.b_ref, self.k_ref, self.tri_ref,
         self.y_ref) = refs
        (self.s_ref, self.c_scr, self.ag_scr, self.rg_scr, self.bg_scr, self.kg_scr) = scratch[:6]
        self.slots = (scratch[6:10], scratch[10:14])
        self.reverse = reverse
        self.nch = tb // RW_CHUNK
        self.pairs = range(npair)
        L, hd = RW_CHUNK, RW_HEAD
        self.lss = [slice(p * 2 * hd, (p + 1) * 2 * hd) for p in self.pairs]
        lane = lax.broadcasted_iota(jnp.int32, (1, 2 * hd), 1)
        self.lo = (lane < hd).astype(F32)
        self.hi = 1.0 - self.lo
        ti = lax.broadcasted_iota(jnp.int32, (2 * L, 2 * hd), 0)
        sl = lax.broadcasted_iota(jnp.int32, (2 * L, 2 * hd), 1)
        sj = sl & (L - 1)
        tt = ti & (L - 1)
        earlier = (sj > tt) if reverse else (sj < tt)
        assert 4 * L == 2 * hd and L == 2 * RW_SUB
        self.aa_mask = (earlier | ((ti >= L) & (sj == tt))).astype(F32)
        self.col_idx0 = jnp.where(lax.broadcasted_iota(jnp.int32, (RW_SUB, 2 * hd), 1) < hd, 0, L)
        self.blk_mask = ((lax.broadcasted_iota(jnp.int32, (2 * hd, 2 * hd), 0) >> 6)
                         == (lax.broadcasted_iota(jnp.int32, (2 * hd, 2 * hd), 1) >> 6)).astype(F32)
        self.zeros2l = jnp.zeros((2 * L, 2 * hd), F32)
        self.zeros_sub = jnp.zeros((RW_SUB, 2 * hd), F32)
        self.blocks = (1, 0) if reverse else (0, 1)
        self.last = 0 if reverse else L - 1
        self.order = range(L - 1, -1, -1) if reverse else range(L)

    def reset(self):
        self.s_ref[...] = jnp.zeros_like(self.s_ref)

    def prologue(self):
        lw = self.lw_ref[0]
        c = _split3_dot(self.tri_ref[...], lw)
        self.c_scr[...] = c
        enc = jnp.exp(-c)
        self.ag_scr[...] = self.a_ref[...] * jnp.exp(c - lw)
        self.rg_scr[...] = self.r_ref[...] * jnp.exp(c)
        self.bg_scr[...] = self.b_ref[0] * enc
        self.kg_scr[...] = self.k_ref[0] * enc

    def rows(self, ci):
        L = RW_CHUNK
        cc = (self.nch - 1 - ci) if self.reverse else ci
        return pl.ds(pl.multiple_of(cc * L, L), L)

    def halves(self, x):
        return [x * self.lo, x * self.hi]

    def lhs_of(self, rows):
        return [jnp.concatenate([self.ag_scr[rows, ls], self.rg_scr[rows, ls]], axis=0).astype(BF16)
                for ls in self.lss]

    def prep_aa(self, rows):
        lhs = self.lhs_of(rows)
        out = []
        for p in self.pairs:
            ls = self.lss[p]
            rhs = jnp.concatenate(self.halves(self.bg_scr[rows, ls]) + self.halves(self.kg_scr[rows, ls]),
                                  axis=0)
            out.append(_dot_nt(lhs[p], rhs.astype(BF16)))
        return out

    def prep_akv(self, rows, aa_raw, slot):
        _, akv_scr, aar_scr, na_scr = slot
        L = RW_CHUNK
        aa = [x * self.aa_mask for x in aa_raw]
        for p in self.pairs:
            v = self.v_ref[rows, self.lss[p]]
            vv = jnp.concatenate([self.zeros2l] + self.halves(v), axis=0).astype(BF16)
            akv_scr[p] = _dot(aa[p][0:L].astype(BF16), vv)
            aar_scr[p] = aa[p][L:2 * L]
            na_scr[p] = aa[p][0:L]
        return aa

    def prep_cols(self, aa, slot):
        half = RW_SUB // 2
        for p in self.pairs:
            for s in range(RW_CHUNK):
                r0 = (s // RW_SUB) * RW_SUB
                lo, hi = self.live_rows(s - r0)
                slot[0][p, s, lo:hi] = jnp.take_along_axis(aa[p][r0 + lo:r0 + hi],
                                                           self.col_idx0[0:hi - lo] + s, axis=1)

    def live_rows(self, s_local):
        half = RW_SUB // 2
        if self.reverse:
            return (0, half) if s_local <= half else (0, RW_SUB)
        return (half, RW_SUB) if s_local >= half - 1 else (0, RW_SUB)

    def adv_g(self, rows):
        lhs = self.lhs_of(rows)
        return [_dot_nt(lhs[p], self.s_ref[p].astype(BF16)) for p in self.pairs]

    def adv_solve(self, g, slot):
        L, sub = RW_CHUNK, RW_SUB
        col_scr, akv_scr, _, na_scr = slot
        first, second = self.blocks
        ub = {}
        for blk in self.blocks:
            r0 = blk * sub
            ub[blk] = [g[p][r0:r0 + sub] + akv_scr[p, r0:r0 + sub] for p in self.pairs]
            if blk == second:
                for p in self.pairs:
                    done = ub[first][p]
                    z = self.zeros_sub
                    parts = [z, z, z, z]
                    parts[first], parts[2 + first] = done * self.lo, done * self.hi
                    stack = jnp.concatenate(parts + [self.zeros2l], axis=0).astype(BF16)
                    ub[blk][p] = ub[blk][p] + _dot(na_scr[p, r0:r0 + sub].astype(BF16), stack)
            steps = [s for s in self.order if s // sub == blk]
            half = sub // 2
            top = [x[0:half] for x in ub[blk]]
            bot = [x[half:] for x in ub[blk]]
            for s in steps:
                sl = s - r0
                lo, hi = self.live_rows(sl)
                for p in self.pairs:
                    row = top[p][sl:sl + 1] if sl < half else bot[p][sl - half:sl - half + 1]
                    if lo == 0:
                        top[p] = top[p] + col_scr[p, s, 0:half] * row
                    if hi == sub:
                        bot[p] = bot[p] + col_scr[p, s, half:sub] * row
            ub[blk] = [jnp.concatenate([top[p], bot[p]], axis=0) for p in self.pairs]
        return [jnp.concatenate([ub[0][p], ub[1][p]], axis=0) for p in self.pairs]

    def adv_out(self, rows, g, u, slot):
        L = RW_CHUNK
        for p in self.pairs:
            v = self.v_ref[rows, self.lss[p]]
            uv = jnp.concatenate(self.halves(u[p]) + self.halves(v), axis=0).astype(BF16)
            self.y_ref[rows, self.lss[p]] = (g[p][L:2 * L] + _dot(slot[2][p].astype(BF16), uv)).astype(BF16)
        for p in self.pairs:
            ls = self.lss[p]
            cch = self.c_scr[rows, ls]
            cl = cch[self.last:self.last + 1]
            dec = jnp.exp(cl - cch)
            bk = jnp.concatenate([self.b_ref[0, rows, ls] * dec, self.k_ref[0, rows, ls] * dec], axis=0)
            upd = _dot_tn(jnp.concatenate([u[p], self.v_ref[rows, ls]], axis=0).astype(BF16),
                          bk.astype(BF16))
            self.s_ref[p] = self.s_ref[p] * jnp.exp(cl) + upd * self.blk_mask


N_RW_REFS = 8
N_RW_SCRATCH = 14


def _rw_scan_kernel(*refs, tb, npair, n_units):
    n_in = N_RW_REFS - 1
    units = []
    for d in range(n_units):
        ins = refs[d * n_in:(d + 1) * n_in]
        out = refs[n_units * (n_in + 1) + d]
        base = n_units * (n_in + 2)
        scr = refs[base + d * N_RW_SCRATCH:base + (d + 1) * N_RW_SCRATCH]
        units.append(_RwUnit(tuple(ins) + (out,), scr, tb=tb, npair=npair, reverse=d % 2 == 1))
    nch = tb // RW_CHUNK

    @pl.when(pl.program_id(1) == 0)
    def _():
        for un in units:
            un.reset()

    for un in units:
        un.prologue()

    def step(ci_adv, sa, ci_prep, sp):
        rows_a = [un.rows(ci_adv) for un in units]
        rows_p = [un.rows(ci_prep) for un in units]
        aa_raw = [un.prep_aa(rp) for un, rp in zip(units, rows_p)]
        g = [un.adv_g(ra) for un, ra in zip(units, rows_a)]
        aa = [un.prep_akv(rp, x, un.slots[sp]) for un, rp, x in zip(units, rows_p, aa_raw)]
        u = [un.adv_solve(x, un.slots[sa]) for un, x in zip(units, g)]
        for un, x in zip(units, aa):
            un.prep_cols(x, un.slots[sp])
        for un, ra, x, y in zip(units, rows_a, g, u):
            un.adv_out(ra, x, y, un.slots[sa])

    for un in units:
        r0 = un.rows(0)
        un.prep_cols(un.prep_akv(r0, un.prep_aa(r0), un.slots[0]), un.slots[0])

    def two_chunks(j, carry):
        c0 = 2 * j
        step(c0, 0, c0 + 1, 1)
        step(c0 + 1, 1, jnp.minimum(c0 + 2, nch - 1), 0)
        return carry

    lax.fori_loop(0, nch // 2, two_chunks, 0)


def _seq_block(d, b, i, *, n_batch, ctxb, latb):
    is_ctx = i < ctxb
    cs = jnp.where(d == 0, i, ctxb - 1 - i)
    lj = jnp.where(d == 0, i - ctxb, latb - 1 - (i - ctxb))
    return jnp.where(is_ctx, b * ctxb + cs, n_batch * ctxb + b * latb + lj)


def _rw_scan(r, v, nkk, lw, kka, km, *, tb, n_batch, ctxb, latb):
    m, c = r.shape
    npair = c // (2 * RW_HEAD)
    lanes = 2 * RW_HEAD
    nq = 2 if n_batch % 2 == 0 else 1
    in_specs, args, out_specs = [], [], []
    for q in range(nq):
        for d in range(2):
            blk0 = _la_index(d, n_batch, ctxb, latb)
            blk = lambda g, i, blk0=blk0, q=q: blk0(nq * g + q, i)
            tok = pl.BlockSpec((tb, c), lambda g, i, blk=blk: (blk(g, i), 0))
            tokd = pl.BlockSpec((1, tb, c), lambda g, i, blk=blk, d=d: (d, blk(g, i), 0))
            in_specs += [tok, tok, tok, tokd, tokd, tokd, pl.BlockSpec((tb, tb), lambda g, i: (0, 0))]
            args += [r, v, nkk, lw, kka, km, _chunk_tri(tb, d == 1, RW_CHUNK)]
            out_specs.append(tok)
    n_units = 2 * nq
    n_real = len(args)
    in_specs += [pl.BlockSpec(memory_space=pl.ANY)] * n_units
    args += [jnp.zeros((m, c), BF16) for _ in range(n_units)]
    buf = pltpu.VMEM((tb, c), F32)
    unit_scratch = [pltpu.VMEM((npair, lanes, lanes), F32), buf, buf, buf, buf, buf] + 2 * [
        pltpu.VMEM((npair, RW_CHUNK, RW_SUB, lanes), F32),
        pltpu.VMEM((npair, RW_CHUNK, lanes), F32),
        pltpu.VMEM((npair, RW_CHUNK, lanes), F32),
        pltpu.VMEM((npair, RW_CHUNK, lanes), F32)]
    assert len(unit_scratch) == N_RW_SCRATCH
    outs = pl.pallas_call(
        functools.partial(_rw_scan_kernel, tb=tb, npair=npair, n_units=n_units),
        grid=(n_batch // nq, ctxb + latb),
        in_specs=in_specs,
        out_specs=out_specs,
        out_shape=[jax.ShapeDtypeStruct((m, c), BF16)] * n_units,
        input_output_aliases={n_real + u: u for u in range(n_units)},
        scratch_shapes=unit_scratch * n_units,
        compiler_params=_cparams(2),
        name="rwkv_scan",
    )(*args)
    return outs


def _rw_post(y, g, bonus, lnw, lnb, e):
    inv = 1.0 / RW_HEAD
    yc = y - _split_dot(y, e) * inv
    var = _split_dot(yc * yc, e) * inv
    yn = yc * lax.rsqrt(var + RW_LN_EPS)
    return (yn * lnw + lnb + bonus) * g


class _LaUnit:
    def __init__(self, q_ref, k_ref, v_ref, b_ref, o_ref, s_ref, *, tb, nh, dk, dv, reverse, q_scale):
        self.q_ref, self.k_ref, self.v_ref, self.b_ref, self.o_ref, self.s_ref = (
            q_ref, k_ref, v_ref, b_ref, o_ref, s_ref)
        self.nch = tb // LA_CHUNK
        self.dk, self.dv, self.reverse, self.q_scale = dk, dv, reverse, q_scale
        self.lanes = 128
        self.pack = self.lanes // dk
        self.groups = range(nh // self.pack)
        self.rowi = lax.broadcasted_iota(jnp.int32, (LA_CHUNK, 1), 0)
        lane = lax.broadcasted_iota(jnp.int32, (1, self.lanes), 1)
        self.head_lanes = [(lane // dk == j).astype(F32) for j in range(self.pack)]
        if self.pack > 1:
            rows = lax.broadcasted_iota(jnp.int32, (self.pack * dv, self.lanes), 0)
            cols = lax.broadcasted_iota(jnp.int32, (self.pack * dv, self.lanes), 1)
            self.blk_mask = ((rows // dv) == (cols // dk)).astype(F32)

    def reset(self):
        self.s_ref[...] = jnp.zeros_like(self.s_ref)

    def load(self, ci):
        cc = (self.nch - 1 - ci) if self.reverse else ci
        rows = pl.ds(pl.multiple_of(cc * LA_CHUNK, LA_CHUNK), LA_CHUNK)
        pack, dv, lanes = self.pack, self.dv, self.lanes
        out = []
        for g in self.groups:
            ks = slice(g * lanes, (g + 1) * lanes)
            q = self.q_ref[rows, ks].astype(F32) * self.q_scale
            k = self.k_ref[rows, ks].astype(F32)
            v = self.v_ref[rows, g * pack * dv:(g + 1) * pack * dv].astype(F32)
            b = self.b_ref[rows, ks]
            o_inter = _dot_nt((q * jnp.exp(b)).astype(BF16), self.s_ref[g].astype(BF16))
            out.append((rows, q, k, v, b, [o_inter[:, j * dv:(j + 1) * dv] for j in range(pack)]))
        return out

    def intra(self, ops):
        pack, dv, half = self.pack, self.dv, LA_CHUNK // 2
        rowi = self.rowi[0:half]
        for g in self.groups:
            rows, q, k, v, b, o = ops[g]
            qh = [q[0:half], q[half:]]
            bh = [b[0:half], b[half:]]
            oh = [[oj[0:half], oj[half:]] for oj in o]
            for s in range(LA_CHUNK):
                for h in range(2):
                    lo_row, hi_row = h * half, (h + 1) * half - 1
                    if (hi_row > s) if self.reverse else (lo_row < s):
                        if (lo_row > s) if self.reverse else (hi_row < s):
                            continue
                        valid = (rowi + lo_row <= s) if self.reverse else (rowi + lo_row >= s)
                    else:
                        valid = None
                    term = (qh[h] * k[s:s + 1]) * jnp.exp(bh[h] - b[s:s + 1])
                    for j in range(pack):
                        tj = term if pack == 1 else term * self.head_lanes[j]
                        col = jnp.sum(tj, axis=-1, keepdims=True)
                        if valid is not None:
                            col = jnp.where(valid, col, 0.0)
                        oh[j][h] = oh[j][h] + col * v[s:s + 1, j * dv:(j + 1) * dv]
            for j in range(pack):
                self.o_ref[rows, (g * pack + j) * dv:(g * pack + j + 1) * dv] = jnp.concatenate(
                    oh[j], axis=0).astype(BF16)

    def update(self, ops):
        for g in self.groups:
            _, _, k, v, b, _ = ops[g]
            b_last = b[0:1] if self.reverse else b[LA_CHUNK - 1:LA_CHUNK]
            upd = _dot_tn(v.astype(BF16), (k * jnp.exp(b_last - b)).astype(BF16))
            if self.pack > 1:
                upd = upd * self.blk_mask
            self.s_ref[g] = self.s_ref[g] * jnp.exp(b_last) + upd


def _la_run(units, nch):
    def chunk(ci, carry):
        ops = [un.load(ci) for un in units]
        for un, x in zip(units, ops):
            un.update(x)
        for un, x in zip(units, ops):
            un.intra(x)
        return carry

    lax.fori_loop(0, nch, chunk, 0)


def _hg_units(ins, gamma_ref, outs, scr, *, layer, tb, nh, dk, dv):
    gam = gamma_ref[...]
    ex = jnp.exp(gam - jnp.max(gam, axis=0, keepdims=True))
    p = ex / jnp.sum(ex, axis=0, keepdims=True)
    cum = p[0:1]
    for i in range(1, layer + 1):
        cum = cum + p[i:i + 1]
    lb = cum - p[0:1]
    lo = jnp.log(lb)
    l1 = jnp.log(1.0 - lb)

    units = []
    for d in range(2):
        q_ref, f_ref, v_ref, tri_ref = ins[4 * d:4 * d + 4]
        s_ref, k_scr, b_scr = scr[3 * d:3 * d + 3]
        hi = l1 + _log_sigmoid(f_ref[...].astype(F32))
        mx = jnp.maximum(lo, hi)
        mn = jnp.minimum(lo, hi)
        log_f = mx + jnp.log(1.0 + jnp.exp(mn - mx))
        k_scr[...] = 1.0 - jnp.exp(log_f)
        b_scr[...] = _split3_dot(tri_ref[...], log_f)
        units.append(_LaUnit(q_ref, k_scr, v_ref, b_scr, outs[d], s_ref, tb=tb, nh=nh, dk=dk, dv=dv,
                             reverse=d == 1, q_scale=1.0))
    return units


def _gla_units(ins, gw_ref, gb_ref, outs, scr, *, tb, nh, dk, dv):
    hk = nh * dk
    units = []
    for d in range(2):
        qk_ref, v_ref, small_ref, tri_ref = ins[4 * d:4 * d + 4]
        s_ref, b_scr = scr[2 * d:2 * d + 2]
        code = small_ref[:, 384:512]
        pre = _dot(code.astype(BF16), gw_ref[:, d * hk:(d + 1) * hk]) + gb_ref[:, d * hk:(d + 1) * hk]
        log_g = _log_sigmoid(pre) * (1.0 / GLA_GATE_NORM)
        b_scr[...] = _split3_dot(tri_ref[...], log_g)
        units.append(_LaUnit(qk_ref.at[:, 0:hk], qk_ref.at[:, hk:2 * hk], v_ref, b_scr, outs[d], s_ref,
                             tb=tb, nh=nh, dk=dk, dv=dv, reverse=d == 1, q_scale=dk ** -0.5))
    return units


def _la_kernel(*refs, layer, tb, hg_heads, gla_dk, gla_dv):
    hg_in, (gamma_ref,), gla_in, (gw_ref, gb_ref) = refs[0:8], refs[8:9], refs[9:17], refs[17:19]
    hg_out, gla_out, hg_scr, gla_scr = refs[19:21], refs[21:23], refs[23:29], refs[29:33]
    units = (_hg_units(hg_in, gamma_ref, hg_out, hg_scr, layer=layer, tb=tb, nh=hg_heads, dk=HG_DK, dv=HG_DK)
             + _gla_units(gla_in, gw_ref, gb_ref, gla_out, gla_scr, tb=tb, nh=GLA_HEADS, dk=gla_dk, dv=gla_dv))

    @pl.when(pl.program_id(1) == 0)
    def _():
        for un in units:
            un.reset()

    _la_run(units, tb // LA_CHUNK)


def _chunk_tri(tb, reverse, chunk=LA_CHUNK):
    t = jnp.arange(tb)
    same = (t[:, None] // chunk) == (t[None, :] // chunk)
    tri = (t[None, :] >= t[:, None]) if reverse else (t[None, :] <= t[:, None])
    return (same & tri).astype(BF16)


def _la_index(d, n_batch, ctxb, latb):
    return lambda b, i: _seq_block(d, b, i, n_batch=n_batch, ctxb=ctxb, latb=latb)


def _la_mixers(p_all, gamma, gw, gb, *, layer, tb, n_batch, ctxb, latb):
    m = p_all.shape[0]
    c = gamma.shape[-1]
    hk = gw.shape[-1] // 2
    gla_dk, gla_dv = hk // GLA_HEADS, c // GLA_HEADS
    tri = pl.BlockSpec((tb, tb), lambda b, i: (0, 0))
    whole = lambda a: pl.BlockSpec(a.shape, lambda b, i: (0,) * a.ndim)
    blks = [_la_index(d, n_batch, ctxb, latb) for d in range(2)]
    col = lambda d, off: pl.BlockSpec((tb, c), lambda b, i, blk=blks[d]: (blk(b, i), off // c))
    in_specs, args = [], []
    for d in range(2):
        in_specs += [col(d, C_HGQ), col(d, C_HGF + d * c), col(d, C_HGI), tri]
        args += [p_all, p_all, p_all, _chunk_tri(tb, d == 1)]
    in_specs.append(whole(gamma))
    args.append(gamma)
    for d in range(2):
        in_specs += [col(d, C_GLQK), col(d, C_GLV), col(d, C_SMALL), tri]
        args += [p_all, p_all, p_all, _chunk_tri(tb, d == 1)]
    in_specs += [whole(gw), whole(gb)]
    args += [gw, gb]
    buf = pltpu.VMEM((tb, c), F32)
    hg_scratch = [pltpu.VMEM((c // HG_DK, HG_DK, HG_DK), F32), buf, buf]
    gla_scratch = [pltpu.VMEM((hk // 128, (128 // gla_dk) * gla_dv, 128), F32), pltpu.VMEM((tb, hk), F32)]
    outs = pl.pallas_call(
        functools.partial(_la_kernel, layer=layer, tb=tb, hg_heads=c // HG_DK, gla_dk=gla_dk, gla_dv=gla_dv),
        grid=(n_batch, ctxb + latb),
        in_specs=in_specs,
        out_specs=[col(0, 0), col(1, 0)] * 2,
        out_shape=[jax.ShapeDtypeStruct((m, c), BF16)] * 4,
        scratch_shapes=2 * hg_scratch + 2 * gla_scratch,
        compiler_params=_cparams(2),
        name="la_scan",
    )(*args)
    return outs[0:2], outs[2:4]


def _la_post(o, gate, norm_g, nh):
    dv = o.shape[-1] // nh
    outs = []
    for h in range(nh):
        oh = o[:, h * dv:(h + 1) * dv]
        outs.append(oh * lax.rsqrt(jnp.mean(oh * oh, axis=-1, keepdims=True) + EPS))
    return jnp.concatenate(outs, axis=-1) * norm_g * (gate * _sigmoid(gate))


def _gelu(x):
    return 0.5 * x * (1.0 + jnp.tanh(0.7978845608028654 * (x + 0.044715 * (x * x * x))))


def _sgu_kernel(u_ref, v_ref, lnw_ref, lnb_ref, ws_ref, bs_ref, o_ref, *, rb):
    u = _gelu(u_ref[...].astype(F32))
    v = _gelu(v_ref[...].astype(F32))
    vc = v - jnp.mean(v, axis=-1, keepdims=True)
    vn = vc * lax.rsqrt(jnp.mean(vc * vc, axis=-1, keepdims=True) + EPS)
    vn = (vn * lnw_ref[...] + lnb_ref[...]).astype(BF16)
    gw = vn.shape[-1] // SGU_GROUPS
    for n in range(rb // SGU_CHUNK):
        rs = slice(n * SGU_CHUNK, (n + 1) * SGU_CHUNK)
        for g in range(SGU_GROUPS):
            cs = slice(g * gw, (g + 1) * gw)
            s = _dot(ws_ref[g], vn[rs, cs]) + bs_ref[g]
            o_ref[rs, cs] = u[rs, cs] * s


def _sgu(p_all, lnw, lnb, ws, bs, *, rb):
    m = p_all.shape[0]
    c = lnw.shape[-1]
    return pl.pallas_call(
        functools.partial(_sgu_kernel, rb=rb),
        grid=(m // rb,),
        in_specs=[pl.BlockSpec((rb, c), lambda i: (i, C_SGU // c)),
                  pl.BlockSpec((rb, c), lambda i: (i, C_SGU // c + 1)),
                  pl.BlockSpec((1, c), lambda i: (0, 0)),
                  pl.BlockSpec((1, c), lambda i: (0, 0)),
                  pl.BlockSpec(ws.shape, lambda i: (0, 0, 0)),
                  pl.BlockSpec(bs.shape, lambda i: (0, 0, 0))],
        out_specs=pl.BlockSpec((rb, c), lambda i: (i, 0)),
        out_shape=jax.ShapeDtypeStruct((m, c), F32),
        compiler_params=_cparams(1),
        name="sgu",
    )(p_all, p_all, lnw, lnb, ws, bs)


def _merge_kernel(*refs, hg_heads, n_rw):
    rw_refs = refs[:n_rw]
    (rg_ref, rbonus_ref, hf_ref, hb_ref, hgate_ref, gf_ref, gb_ref, ggate_ref, yd_ref, lnw_ref, lnb_ref,
     e_ref, hnorm_ref, gnorm_ref, g0_ref, g1_ref, g2_ref, g3_ref, w_ref, o_ref, y_scr) = refs[n_rw:]

    @pl.when(pl.program_id(1) == 0)
    def _():
        both = lambda f_ref, b_ref: f_ref[...].astype(F32) + b_ref[...].astype(F32)
        y_rw = rw_refs[0][...].astype(F32)
        for ref in rw_refs[1:]:
            y_rw = y_rw + ref[...].astype(F32)
        y_scr[0] = _rw_post(y_rw, rg_ref[...], rbonus_ref[...], lnw_ref[...], lnb_ref[...],
                            e_ref[...]).astype(BF16)
        y_scr[1] = _la_post(both(hf_ref, hb_ref), hgate_ref[...].astype(F32), hnorm_ref[...],
                            hg_heads).astype(BF16)
        y_scr[2] = _la_post(both(gf_ref, gb_ref), ggate_ref[...].astype(F32), gnorm_ref[...],
                            GLA_HEADS).astype(BF16)
        y_scr[3] = yd_ref[...].astype(BF16)

    gs = (g0_ref, g1_ref, g2_ref, g3_ref)
    acc = None
    for j in range(N_BRANCH):
        t = _sigmoid(gs[j][...].astype(F32)) * _dot(y_scr[j], w_ref[j])
        acc = t if acc is None else acc + t
    o_ref[...] = acc.astype(BF16)


def _merge(rw, hg, gla, yd, p_gate, p_mix, vecs, e, w_branch, *, layer, tm, row_off):
    m = p_mix.shape[0]
    _, _, c, d = w_branch.shape
    tn = _pick((1024, 512), d)
    mo = m - row_off * tm
    ytok = pl.BlockSpec((tm, c), lambda i, j: (i + row_off, 0))
    mix = lambda off: pl.BlockSpec((tm, c), lambda i, j: (i + row_off, off // c))
    gate = lambda b: pl.BlockSpec((tm, tn), lambda i, j: (i + row_off, b * d // tn + j))
    vec = pl.BlockSpec((1, c), lambda i, j: (0, 0))
    return pl.pallas_call(
        functools.partial(_merge_kernel, hg_heads=c // HG_DK, n_rw=len(rw) - 2),
        grid=(mo // tm, d // tn),
        in_specs=[ytok] * len(rw) + [ytok, ytok, mix(C_HGG), ytok, ytok, mix(C_GLG), ytok,
                  vec, vec, pl.BlockSpec((c, c), lambda i, j: (0, 0)), vec, vec,
                  gate(0), gate(1), gate(2), gate(3),
                  pl.BlockSpec((None, N_BRANCH, c, tn), lambda i, j: (layer, 0, 0, j))],
        out_specs=pl.BlockSpec((tm, tn), lambda i, j: (i, j)),
        out_shape=jax.ShapeDtypeStruct((mo, d), BF16),
        scratch_shapes=[pltpu.VMEM((N_BRANCH, tm, c), BF16)],
        compiler_params=_cparams(2),
        name="merge",
    )(*rw, hg[0], hg[1], p_mix, gla[0], gla[1], p_mix, yd, *vecs[:2], e, *vecs[2:],
      p_gate, p_gate, p_gate, p_gate, w_branch)


def _outproj_kernel(m_ref, w_ref, xc_ref, xl_ref, mod_ref, o_ref, *, nctx, row_off):
    x = jnp.where(pl.program_id(0) + row_off < nctx, xc_ref[...], xl_ref[...])
    o_ref[...] = x + mod_ref[0, 2:3, :] * _dot(m_ref[...], w_ref[...])


def _outproj(mm, w_out, x_parts, mod, *, layer, tm, row_off, nctx, bpb):
    xc, xl, lat_off = x_parts
    mo, d = mm.shape
    tn = _pick((1024, 512), d)
    xspecs = [pl.BlockSpec((tm, tn), lambda i, j: (jnp.minimum(i + row_off, nctx - 1),
                                                   jnp.where(i + row_off < nctx, j, 0))),
              pl.BlockSpec((tm, tn), lambda i, j: (lat_off + jnp.maximum(i + row_off - nctx, 0),
                                                   jnp.where(i + row_off < nctx, 0, j)))]
    return pl.pallas_call(
        functools.partial(_outproj_kernel, nctx=nctx, row_off=row_off),
        grid=(mo // tm, d // tn),
        in_specs=[pl.BlockSpec((tm, d), lambda i, j: (i, 0)),
                  pl.BlockSpec((None, d, tn), lambda i, j: (layer, 0, j))] + xspecs + [
                  pl.BlockSpec((1, 6, tn), lambda i, j: (_row_group(i + row_off, nctx, bpb), 0, j))],
        out_specs=pl.BlockSpec((tm, tn), lambda i, j: (i, j)),
        out_shape=jax.ShapeDtypeStruct((mo, d), F32),
        compiler_params=_cparams(2),
        name="outproj",
    )(mm, w_out, xc, xl, mod)


def _mlp_kernel(x_ref, mod_ref, g_ref, w1_ref, w2_ref, gf_ref, o_ref, h_ref, acc_ref, *, final_norm):
    j = pl.program_id(1)

    @pl.when(j == 0)
    def _():
        h = _norm_mod(x_ref[...], g_ref[...], mod_ref[0, 3:4, :], mod_ref[0, 4:5, :])
        h_ref[...] = h.astype(BF16)
        acc_ref[...] = jnp.zeros_like(acc_ref)

    a = jnp.maximum(_dot(h_ref[...], w1_ref[...]), 0.0)
    acc_ref[...] += _dot((a * a).astype(BF16), w2_ref[...])

    @pl.when(j == pl.num_programs(1) - 1)
    def _():
        y = x_ref[...] + mod_ref[0, 5:6, :] * acc_ref[...]
        if final_norm:
            y = y * lax.rsqrt(jnp.mean(y * y, axis=-1, keepdims=True) + EPS) * gf_ref[...]
        o_ref[...] = y


def _mlp(x_in, mod, g, w1, w2, g_final, *, layer, tm, row_off, nctx, bpb, final_norm):
    mo, d = x_in.shape
    hid = w1.shape[-1]
    th = _pick((1024, 512, 256, 128), hid)
    return pl.pallas_call(
        functools.partial(_mlp_kernel, final_norm=final_norm),
        grid=(mo // tm, hid // th),
        in_specs=[pl.BlockSpec((tm, d), lambda i, j: (i, 0)),
                  pl.BlockSpec((1, 6, d), lambda i, j: (_row_group(i + row_off, nctx, bpb), 0, 0)),
                  pl.BlockSpec((1, d), lambda i, j: (0, 0)),
                  pl.BlockSpec((None, d, th), lambda i, j: (layer, 0, j)),
                  pl.BlockSpec((None, th, d), lambda i, j: (layer, j, 0)),
                  pl.BlockSpec((1, d), lambda i, j: (0, 0))],
        out_specs=pl.BlockSpec((tm, d), lambda i, j: (i, 0)),
        out_shape=jax.ShapeDtypeStruct((mo, d), F32),
        scratch_shapes=[pltpu.VMEM((tm, d), BF16), pltpu.VMEM((tm, d), F32)],
        compiler_params=_cparams(2),
        name="mlp",
    )(x_in, mod, g, w1, w2, g_final)


def _blockdiag2(w):
    _, r, c = w.shape
    z = jnp.zeros((r, c), w.dtype)
    out = jnp.concatenate([jnp.concatenate([w[0], z], axis=1), jnp.concatenate([z, w[1]], axis=1)], axis=0)
    return jnp.pad(out, ((0, 128 - 2 * r), (0, 0))).astype(BF16)


def _permute_w_in(w, d_model):
    c = d_model // N_BRANCH
    o = [0]
    for wd in (3 * c, 64 * 2, 64 * 2, 128, c, 2 * c, c, c, c, c, 32, c, 2 * c, N_BRANCH * d_model):
        o.append(o[-1] + wd)
    seg = lambda k: w[:, o[k]:o[k + 1]]
    pad = jnp.zeros((w.shape[0], 512 - 128 * 3 - 32), w.dtype)
    parts = [seg(13), seg(0), seg(5), seg(12), seg(4), seg(6), seg(7), seg(8), seg(9), seg(11),
             seg(1), seg(2), seg(3), seg(10), pad]
    out = jnp.concatenate(parts, axis=1).astype(BF16)
    assert out.shape[1] == N_GATE + N_MIX
    return out


def kernel(x, c, ctx, c_ctx, w_ada, b_ada, g_norm1, g_norm2, g_final, w_in, rw_conv, rw_w0, rw_w2,
           rw_a0, rw_a2, rw_g2, rw_kk, rw_ka, rw_rk, rw_ln_w, rw_ln_b, hg_gamma, hg_norm, gla_gw,
           gla_gb, gla_norm, sgu_ln_w, sgu_ln_b, sgu_w, sgu_b, w_branch, w_out, w_mlp1, w_mlp2):
    n_batch, seq, d_model = x.shape
    ctx_len = ctx.shape[1]
    depth = w_in.shape[0]
    cw = d_model // N_BRANCH
    assert cw == 512 and d_model == 2048, "column layout constants assume D_MODEL = 2048"
    m_ctx = n_batch * ctx_len

    tm = _pick((1024, 512, 256, 128), m_ctx, seq)
    tb = _pick((256, 128), ctx_len, seq)
    nctx, bpb = m_ctx // tm, seq // tm
    ctxb, latb = ctx_len // tb, seq // tb
    seqs = dict(tb=tb, n_batch=n_batch, ctxb=ctxb, latb=latb)

    x_parts = (ctx.reshape(m_ctx, d_model), x.reshape(n_batch * seq, d_model), 0)
    c_rows = jnp.concatenate([c_ctx[None, :], c, jnp.zeros((7 - n_batch, d_model), F32)], axis=0)
    mod_all = _ada(c_rows, w_ada, b_ada).reshape(depth, 8, 6, d_model)

    head_ones = (jnp.arange(cw)[:, None] // RW_HEAD == jnp.arange(cw)[None, :] // RW_HEAD).astype(BF16)
    w_branch_b, w_out_b = w_branch.astype(BF16), w_out.astype(BF16)
    w_mlp1_b, w_mlp2_b = w_mlp1.astype(BF16), w_mlp2.astype(BF16)
    row = lambda a: a.reshape(1, -1)

    for l in range(depth):
        last = l == depth - 1
        mod = mod_all[l]
        p_gate, p_all = _inproj(x_parts, mod, row(g_norm1[l]), _permute_w_in(w_in[l], d_model),
                                tm=tm, nctx=nctx, bpb=bpb)

        prm = dict(conv=rw_conv[l], w0=row(rw_w0[l]), w2=_blockdiag2(rw_w2[l]), a0=row(rw_a0[l]),
                   a2=_blockdiag2(rw_a2[l]), g2=rw_g2[l].astype(BF16), kk=row(rw_kk[l]),
                   ka=row(rw_ka[l]), rk=row(rw_rk[l]), e=head_ones)
        r, v, nkk, g, bonus, w, kka, km = _rw_prep(p_all, prm, tb=tb, nctx=m_ctx // tb,
                                                    ctx_bps=ctxb, lat_bps=latb)
        ys = _rw_scan(r, v, nkk, w, kka, km, **seqs)

        ob, oc = _la_mixers(p_all, hg_gamma, _blockdiag2(gla_gw[l]), row(gla_gb[l]), layer=l, **seqs)

        bs = jnp.broadcast_to(sgu_b[l][:, :, None], sgu_w[l].shape)
        yd = _sgu(p_all, row(sgu_ln_w[l]), row(sgu_ln_b[l]), sgu_w[l].astype(BF16), bs,
                  rb=_pick((512, 256, 128), m_ctx, seq))

        row_off = nctx if last else 0
        vecs = (row(rw_ln_w[l]), row(rw_ln_b[l]), row(hg_norm[l]), row(gla_norm[l]))
        mm = _merge((*ys, g, bonus), ob, oc, yd, p_gate, p_all, vecs, head_ones,
                    w_branch_b, layer=l, tm=tm // 2, row_off=2 * row_off)
        x_mid = _outproj(mm, w_out_b, x_parts, mod, layer=l, tm=tm, row_off=row_off, nctx=nctx, bpb=bpb)
        x_all = _mlp(x_mid, mod, row(g_norm2[l]), w_mlp1_b, w_mlp2_b, row(g_final), layer=l,
                     tm=tm // 2, row_off=2 * row_off, nctx=2 * nctx, bpb=2 * bpb, final_norm=last)
        x_parts = (x_all, x_all, nctx)
    return x_all.reshape(n_batch, seq, d_model)
```
